```python
import math
import jax, jax.numpy as jnp
from jax import lax
import numpy as np

D_MODEL = 2048
BATCH = 16
SEQ = 256
DEPTH = 4
DEC_BATCH = 2
DEC_SEQ = 4096
PAST_LEN = 256

GRID_W = 64
DA_HEADS = D_MODEL // 512
DA_QK = 64
DA_V = 2 * DA_QK
HG_HEADS = D_MODEL // 512
HG_DK = 128
HG_DV = 128
HG_CHUNK = 16
SW_HEADS = D_MODEL // 256
SW_KV = SW_HEADS // 4
SW_GROUP = SW_HEADS // SW_KV
SW_HD = 128
SW_WINDOW = 128
SW_BLOCK = 128
Q_BLOCK = 128
D_MIX = DA_HEADS * DA_V + HG_HEADS * HG_DV + SW_HEADS * SW_HD
IN_WIDTHS = (DA_HEADS * 2 * DA_QK, DA_HEADS * 2 * DA_QK, DA_HEADS * DA_V,
             HG_HEADS * HG_DK, HG_HEADS * HG_DK, HG_HEADS * HG_DK, HG_HEADS * HG_DV, HG_HEADS * HG_DV,
             SW_HEADS * SW_HD, SW_KV * SW_HD, SW_KV * SW_HD)
D_IN = sum(IN_WIDTHS)
D_FF = ((8 * D_MODEL // 3 + 255) // 256) * 256
DEEPNORM_ALPHA = (2 * DEPTH) ** 0.25
DEEPNORM_BETA = (8 * DEPTH) ** -0.25
ROPE_BASE = 10000.0
LN_EPS = 1e-5
RMS_EPS = 1e-6
NEG_INF = -1e30
LB_FLOOR = 1e-30

kernel_name = 'hymba_diff_hgrn2_swa_flow_trunk'


def _layer_norm(x, g, b):
    xf = x.astype(jnp.float32)
    mu = jnp.mean(xf, axis=-1, keepdims=True)
    var = jnp.mean(jnp.square(xf - mu), axis=-1, keepdims=True)
    return ((xf - mu) * lax.rsqrt(var + LN_EPS)).astype(x.dtype) * g + b


def _rms_norm(x, g):
    xf = x.astype(jnp.float32)
    return (xf * lax.rsqrt(jnp.mean(xf * xf, axis=-1, keepdims=True) + RMS_EPS)).astype(x.dtype) * g


def _rope_1d(x, pos):
    h = x.shape[-1] // 2
    inv = ROPE_BASE ** (-jnp.arange(h, dtype=jnp.float32) / h)
    ang = pos.astype(jnp.float32)[:, None] * inv[None, :]
    cos = jnp.cos(ang).astype(x.dtype)
    sin = jnp.sin(ang).astype(x.dtype)
    x1, x2 = x[..., :h], x[..., h:]
    return jnp.concatenate([x1 * cos - x2 * sin, x2 * cos + x1 * sin], axis=-1)


def _rope_2d(x, row, col):
    half = x.shape[-1] // 2
    return jnp.concatenate([_rope_1d(x[..., :half], row), _rope_1d(x[..., half:], col)], axis=-1)


def _adaln(cond, w_mod, b_mod):
    m = (jax.nn.silu(cond) @ w_mod + b_mod)[:, None, :]
    return jnp.split(m, 6, axis=-1)


def _project(h, w_in, lb):
    b, n, _ = h.shape
    idx = np.cumsum(IN_WIDTHS)[:-1].tolist()
    aq, ak, av, bq, bff, bfb, bi, bg, cq, ck, cv = jnp.split(h @ w_in, idx, axis=-1)

    def heads(a, nh):
        return a.reshape(b, n, nh, -1).transpose(0, 2, 1, 3)

    def dual(a):
        return a.reshape(b, n, DA_HEADS, 2, DA_QK).transpose(0, 2, 3, 1, 4)

    dq, dk, dv = dual(aq), dual(ak), heads(av, DA_HEADS)
    hq = jax.nn.silu(heads(bq, HG_HEADS))
    z = jnp.stack([heads(bff, HG_HEADS), heads(bfb, HG_HEADS)], axis=0).astype(jnp.float32)
    lb = lb.reshape(2, 1, HG_HEADS, 1, HG_DK)
    hlogf = jnp.logaddexp(jnp.log(jnp.maximum(lb, LB_FLOOR)), jnp.log1p(-lb) + jax.nn.log_sigmoid(z))
    hk = (1.0 - lb) * jax.nn.sigmoid(-z)
    hv, hg = heads(bi, HG_HEADS), heads(bg, HG_HEADS)
    sq = heads(cq, SW_HEADS).reshape(b, SW_KV, SW_GROUP, n, SW_HD)
    sk, sv = heads(ck, SW_KV), heads(cv, SW_KV)
    return dq, dk, dv, hq, hk, hlogf, hv, hg, sq, sk, sv


def _diff_lambda(lam_p, layer):
    lam_init = 0.8 - 0.6 * math.exp(-0.3 * layer)
    lp = lam_p.astype(jnp.float32)
    lam = jnp.exp(jnp.sum(lp[0] * lp[1])) - jnp.exp(jnp.sum(lp[2] * lp[3])) + lam_init
    return lam, lam_init


def _diff_attention(q, k, v, lam):
    b, h, _, nq, dh = q.shape
    nb = nq // Q_BLOCK
    qb = jnp.moveaxis(q.reshape(b, h, 2, nb, Q_BLOCK, dh), 3, 0)
    scale = dh ** -0.5

    def block(qi):
        s = jnp.einsum('bhmqd,bhmkd->bhmqk', qi, k).astype(jnp.float32) * scale
        p = jax.nn.softmax(s, axis=-1)
        w = p[:, :, 0] - lam * p[:, :, 1]
        return jnp.einsum('bhqk,bhkv->bhqv', w.astype(v.dtype), v)

    o = lax.map(block, qb)
    return jnp.moveaxis(o, 0, 2).reshape(b, h, nq, v.shape[-1])


def _sink_attention(q, k, v, sink):
    b, kv, g, nq, d = q.shape
    nb = nq // Q_BLOCK
    qb = jnp.moveaxis(q.reshape(b, kv, g, nb, Q_BLOCK, d), 3, 0)
    scale = d ** -0.5

    def block(qi):
        s = jnp.einsum('bkgqd,bksd->bkgqs', qi, k).astype(jnp.float32) * scale
        s_sink = jnp.broadcast_to(sink.astype(jnp.float32)[None, :, :, None, None], s.shape[:-1] + (1,))
        p = jax.nn.softmax(jnp.concatenate([s, s_sink], axis=-1), axis=-1)[..., :-1]
        return jnp.einsum('bkgqs,bksd->bkgqd', p.astype(v.dtype), v)

    o = lax.map(block, qb)
    return jnp.moveaxis(o, 0, 3).reshape(b, kv, g, nq, d)


def _banded_sink_attention(q, k, v, kc, vc, sink):
    b, kv, g, n, d = q.shape
    nb = n // SW_BLOCK
    scale = d ** -0.5

    def band(a):
        pad = jnp.pad(a, ((0, 0), (0, 0), (SW_BLOCK, SW_BLOCK), (0, 0))).reshape(b, kv, nb + 2, SW_BLOCK, d)
        return jnp.concatenate([pad[:, :, 0:nb], pad[:, :, 1:nb + 1], pad[:, :, 2:nb + 2]], axis=3)

    kb, vb = band(k), band(v)
    qb = q.reshape(b, kv, g, nb, SW_BLOCK, d)
    s_band = jnp.einsum('bkgnqd,bknsd->bkgnqs', qb, kb).astype(jnp.float32) * scale
    qpos = jnp.arange(n).reshape(nb, SW_BLOCK)
    kpos = (jnp.arange(nb)[:, None] - 1) * SW_BLOCK + jnp.arange(3 * SW_BLOCK)[None, :]
    valid = ((jnp.abs(qpos[:, :, None] - kpos[:, None, :]) <= SW_WINDOW)
             & (kpos >= 0)[:, None, :] & (kpos < n)[:, None, :])
    s_band = jnp.where(valid, s_band, NEG_INF)
    s_ctx = jnp.einsum('bkgnqd,bksd->bkgnqs', qb, kc).astype(jnp.float32) * scale
    s_sink = jnp.broadcast_to(sink.astype(jnp.float32)[None, :, :, None, None, None], s_band.shape[:-1] + (1,))
    p = jax.nn.softmax(jnp.concatenate([s_band, s_ctx, s_sink], axis=-1), axis=-1)
    pb = p[..., :3 * SW_BLOCK].astype(v.dtype)
    pc = p[..., 3 * SW_BLOCK:-1].astype(v.dtype)
    o = jnp.einsum('bkgnqs,bknsd->bkgnqd', pb, vb) + jnp.einsum('bkgnqs,bksd->bkgnqd', pc, vc)
    return o.reshape(b, kv, g, n, d)


def _gla_scan(q, k, v, logf, s0):
    b, h, n, _ = q.shape
    dv = v.shape[-1]
    nc = n // HG_CHUNK

    def chunks(a):
        return jnp.moveaxis(a.astype(jnp.float32).reshape(b, h, nc, HG_CHUNK, a.shape[-1]), 2, 0)

    causal = jnp.tril(jnp.ones((HG_CHUNK, HG_CHUNK), dtype=bool))[:, :, None]

    def step(S, xs):
        qc, kc, vc, gc = xs
        cum = jnp.cumsum(gc, axis=2)
        rel = jnp.where(causal, cum[:, :, :, None, :] - cum[:, :, None, :, :], NEG_INF)
        att = jnp.einsum('bhtd,bhsd,bhtsd->bhts', qc, kc, jnp.exp(rel))
        last = cum[:, :, -1:, :]
        o = jnp.einsum('bhts,bhsv->bhtv', att, vc) + jnp.einsum('bhtd,bhdv->bhtv', qc * jnp.exp(cum), S)
        S = jnp.exp(last[:, :, 0, :])[..., None] * S + jnp.einsum('bhsd,bhsv->bhdv', kc * jnp.exp(last - cum), vc)
        return S, o

    S, o = lax.scan(step, s0.astype(jnp.float32), (chunks(q), chunks(k), chunks(v), chunks(logf)))
    return jnp.moveaxis(o, 0, 2).reshape(b, h, n, dv), S


def _bi_gla(q, k2, v, logf2, s0_f, s0_b):
    of, sf = _gla_scan(q, k2[0], v, logf2[0], s0_f)
    ob, sb = _gla_scan(jnp.flip(q, 2), jnp.flip(k2[1], 2), jnp.flip(v, 2), jnp.flip(logf2[1], 2), s0_b)
    return of + jnp.flip(ob, 2), sf, sb


def _merge(oa, ob, oc, hg, lp, lam_init):
    b, _, n, _ = oa.shape
    oa = _rms_norm(oa, lp['diff_g']) * (1.0 - lam_init)
    ob = _rms_norm(ob.astype(oa.dtype), lp['hgrn_g']) * jax.nn.silu(hg)
    oc = oc.reshape(b, SW_HEADS, n, SW_HD)

    def flat(a):
        return a.transpose(0, 2, 1, 3).reshape(b, n, -1)

    return jnp.concatenate([flat(oa), flat(ob), flat(oc)], axis=-1) @ lp['w_out']


def _post_block(x, mix, g1, sh2, sc2, g2, lp):
    x = _layer_norm(DEEPNORM_ALPHA * x + g1 * mix, lp['ln1_g'], lp['ln1_b'])
    h = x * (1.0 + sc2) + sh2
    a, u = jnp.split(h @ lp['w_gate_up'], 2, axis=-1)
    f = (jax.nn.silu(a) * u) @ lp['w_down']
    return _layer_norm(DEEPNORM_ALPHA * x + g2 * f, lp['ln2_g'], lp['ln2_b'])


def _context_layer(x, c_ctx, lp, layer):
    sh1, sc1, g1, sh2, sc2, g2 = _adaln(c_ctx[None, :], lp['w_mod'], lp['b_mod'])
    h = x * (1.0 + sc1) + sh1
    dq, dk, dv, hq, hk, hlogf, hv, hg, sq, sk, sv = _project(h, lp['w_in'], lp['lb'])
    lam, lam_init = _diff_lambda(lp['diff_lambda'], layer)
    oa = _diff_attention(dq, dk, dv, lam)
    zeros = jnp.zeros((x.shape[0], HG_HEADS, HG_DK, HG_DV), jnp.float32)
    ob, sf, sb = _bi_gla(hq, hk, hv, hlogf, zeros, zeros)
    oc = _sink_attention(sq, sk, sv, lp['sink'])
    x_new = _post_block(x, _merge(oa, ob, oc, hg, lp, lam_init), g1, sh2, sc2, g2, lp)
    b, _, _, n, _ = dk.shape
    k_pack = dk.transpose(0, 1, 3, 2, 4).reshape(b, DA_HEADS, n, 2 * DA_QK)
    st = jnp.stack([sf, sb], axis=1).astype(x.dtype)
    return x_new, k_pack, dv, sk, sv, st


def _latent_layer(x, c, ck_diff, cv_diff, ck_swa, cv_swa, st, lp, layer, row, col):
    sh1, sc1, g1, sh2, sc2, g2 = _adaln(c, lp['w_mod'], lp['b_mod'])
    h = x * (1.0 + sc1) + sh1
    dq, dk, dv, hq, hk, hlogf, hv, hg, sq, sk, sv = _project(h, lp['w_in'], lp['lb'])
    lam, lam_init = _diff_lambda(lp['diff_lambda'], layer)
    b, p_len = ck_diff.shape[0], ck_diff.shape[2]
    k_ctx = ck_diff.reshape(b, DA_HEADS, p_len, 2, DA_QK).transpose(0, 1, 3, 2, 4)
    dk_all = jnp.concatenate([_rope_2d(dk, row, col), k_ctx], axis=3)
    dv_all = jnp.concatenate([dv, cv_diff], axis=2)
    oa = _diff_attention(_rope_2d(dq, row, col), dk_all, dv_all, lam)
    ob, _, _ = _bi_gla(hq, hk, hv, hlogf, st[:, 0], st[:, 1])
    oc = _banded_sink_attention(_rope_2d(sq, row, col), _rope_2d(sk, row, col), sv, ck_swa, cv_swa, lp['sink'])
    return _post_block(x, _merge(oa, ob, oc, hg, lp, lam_init), g1, sh2, sc2, g2, lp)


def setup_inputs(seed: int = 0) -> dict:
    key = jax.random.key(seed)
    ks = jax.random.split(key, 24)

    def nrm(k, shape, scale):
        return jax.random.normal(k, shape, jnp.float32) * scale

    return {
        'x_prompt': nrm(ks[0], (BATCH, SEQ, D_MODEL), 1.0),
        'x_sample': nrm(ks[1], (DEC_BATCH, DEC_SEQ, D_MODEL), 1.0),
        'cache_diff_k': nrm(ks[2], (DEC_BATCH, DEPTH, DA_HEADS, PAST_LEN, 2 * DA_QK), 1.0),
        'cache_diff_v': nrm(ks[3], (DEC_BATCH, DEPTH, DA_HEADS, PAST_LEN, DA_V), 1.0),
        'cache_swa_k': nrm(ks[4], (DEC_BATCH, DEPTH, SW_KV, PAST_LEN, SW_HD), 1.0),
        'cache_swa_v': nrm(ks[5], (DEC_BATCH, DEPTH, SW_KV, PAST_LEN, SW_HD), 1.0),
        'state_hgrn': nrm(ks[6], (DEC_BATCH, DEPTH, 2, HG_HEADS, HG_DK, HG_DV), 0.5),
        'c': nrm(ks[7], (DEC_BATCH, D_MODEL), 1.0),
        'c_ctx': nrm(ks[8], (D_MODEL,), 1.0),
        'w_mod': nrm(ks[9], (DEPTH, D_MODEL, 6 * D_MODEL), 0.5 * D_MODEL ** -0.5),
        'b_mod': nrm(ks[10], (DEPTH, 6 * D_MODEL), 0.02),
        'w_in': nrm(ks[11], (DEPTH, D_MODEL, D_IN), D_MODEL ** -0.5),
        'w_out': nrm(ks[12], (DEPTH, D_MIX, D_MODEL), DEEPNORM_BETA * D_MIX ** -0.5),
        'diff_lambda': nrm(ks[13], (DEPTH, 4, DA_QK), 0.1),
        'diff_norm_g': 1.0 + nrm(ks[14], (DEPTH, DA_V), 0.02),
        'hgrn_lb_logits': nrm(ks[15], (DEPTH, 2, HG_HEADS * HG_DK), 0.1),
        'hgrn_norm_g': 1.0 + nrm(ks[16], (DEPTH, HG_DV), 0.02),
        'swa_sink': nrm(ks[17], (DEPTH, SW_HEADS), 0.5),
        'ln1_g': 1.0 + nrm(ks[18], (DEPTH, D_MODEL), 0.02),
        'ln1_b': nrm(ks[19], (DEPTH, D_MODEL), 0.02),
        'ln2_g': 1.0 + nrm(ks[20], (DEPTH, D_MODEL), 0.02),
        'ln2_b': nrm(ks[21], (DEPTH, D_MODEL), 0.02),
        'w_gate_up': nrm(ks[22], (DEPTH, D_MODEL, 2 * D_FF), D_MODEL ** -0.5),
        'w_down': nrm(ks[23], (DEPTH, D_FF, D_MODEL), DEEPNORM_BETA * D_FF ** -0.5),
    }


def reference(x_prompt, x_sample, cache_diff_k, cache_diff_v, cache_swa_k, cache_swa_v, state_hgrn,
              c, c_ctx, w_mod, b_mod, w_in, w_out, diff_lambda, diff_norm_g, hgrn_lb_logits, hgrn_norm_g,
              swa_sink, ln1_g, ln1_b, ln2_g, ln2_b, w_gate_up, w_down):
    lb_w = jax.nn.softmax(hgrn_lb_logits.astype(jnp.float32), axis=0)
    lb_all = jnp.cumsum(lb_w, axis=0) - lb_w[:1]
    n_lat = x_sample.shape[1]
    rows = n_lat // GRID_W
    row = jnp.repeat(jnp.arange(rows), GRID_W)
    col = jnp.tile(jnp.arange(GRID_W), rows)
    y_p, y_s = x_prompt, x_sample
    nk_d, nv_d, nk_s, nv_s, n_st = [], [], [], [], []
    for l in range(DEPTH):
        lp = {'w_mod': w_mod[l], 'b_mod': b_mod[l], 'w_in': w_in[l], 'w_out': w_out[l],
              'diff_lambda': diff_lambda[l], 'diff_g': diff_norm_g[l], 'lb': lb_all[l],
              'hgrn_g': hgrn_norm_g[l], 'sink': swa_sink[l].reshape(SW_KV, SW_GROUP),
              'ln1_g': ln1_g[l], 'ln1_b': ln1_b[l], 'ln2_g': ln2_g[l], 'ln2_b': ln2_b[l],
              'w_gate_up': w_gate_up[l], 'w_down': w_down[l]}
        y_p, kd, vd, ks_, vs_, st = _context_layer(y_p, c_ctx, lp, l)
        nk_d.append(kd)
        nv_d.append(vd)
        nk_s.append(ks_)
        nv_s.append(vs_)
        n_st.append(st)
        y_s = _latent_layer(y_s, c, cache_diff_k[:, l], cache_diff_v[:, l], cache_swa_k[:, l], cache_swa_v[:, l],
                            state_hgrn[:, l], lp, l, row, col)
    new_diff_k = jnp.stack(nk_d, axis=1)
    new_diff_v = jnp.stack(nv_d, axis=1)
    new_swa_k = jnp.stack(nk_s, axis=1)
    new_swa_v = jnp.stack(nv_s, axis=1)
    new_state_hgrn = jnp.stack(n_st, axis=1)
    return (y_p, y_s, new_diff_k, new_diff_v, new_swa_k, new_swa_v, new_state_hgrn)
```

```python
import functools
import math

import numpy as np
import jax
import jax.numpy as jnp
from jax import lax
from jax.experimental import pallas as pl
from jax.experimental.pallas import tpu as pltpu

F32 = jnp.float32
BF16 = jnp.bfloat16

GRID_W = 64
ROPE_BASE = 10000.0
LN_EPS = 1e-5
RMS_EPS = 1e-6
NEG_INF = -1e30
LB_FLOOR = 1e-30
HEAD = 128
DA_QK = 64
SW_GROUP = 4
SW_BLOCK = 128
SW_WINDOW = 128
HG_CHUNK = 64
MOD_ROWS = 8
VMEM_LIMIT = 50 * 1024 * 1024


def _cparams(*sem):
    return pltpu.CompilerParams(dimension_semantics=sem, vmem_limit_bytes=VMEM_LIMIT)


def _dot(a, b):
    return jnp.dot(a, b, preferred_element_type=F32)


def _dot_t(a, b):
    return lax.dot_general(a, b, (((1,), (1,)), ((), ())), preferred_element_type=F32)


def _silu(x):
    return x / (1.0 + jnp.exp(-x))


def _layer_norm(y, g, b):
    mu = jnp.mean(y, axis=-1, keepdims=True)
    d = y - mu
    var = jnp.mean(d * d, axis=-1, keepdims=True)
    return d * lax.rsqrt(var + LN_EPS) * g + b


def _rms_norm(o, g):
    ms = jnp.mean(o * o, axis=-1, keepdims=True)
    return o * lax.rsqrt(ms + RMS_EPS) * g


def _rope(x, c, sa, sb, w):
    return x * c + pltpu.roll(x, HEAD - w, 1) * sa + pltpu.roll(x, w, 1) * sb


def _pick_tile(n, target):
    t = min(n, target)
    while n % t or t % 128:
        t -= 128
    return t


def _lb_kernel(logit_ref, loglb_ref, log1m_ref, onem_ref):
    depth = logit_ref.shape[0]
    x = [logit_ref[l] for l in range(depth)]
    m = functools.reduce(jnp.maximum, x)
    e = [jnp.exp(xi - m) for xi in x]
    tot = functools.reduce(lambda a, b: a + b, e)
    w = [ei / tot for ei in e]
    acc = jnp.zeros_like(w[0])
    for l in range(depth):
        acc = acc + w[l]
        lb = acc - w[0]
        loglb_ref[l] = jnp.log(jnp.maximum(lb, LB_FLOOR))
        log1m_ref[l] = jnp.log1p(-lb)
        onem_ref[l] = 1.0 - lb


def _lb_params(logits):
    shp = jax.ShapeDtypeStruct(logits.shape, F32)
    return pl.pallas_call(_lb_kernel, out_shape=(shp, shp, shp), name="hgrn_lb_params")(logits)


def _mod_kernel(c_ref, w_ref, b_ref, o_ref):
    s = _silu(c_ref[...])
    s_hi = s.astype(BF16)
    s_lo = (s - s_hi.astype(F32)).astype(BF16)
    w = w_ref[...]
    w_hi = w.astype(BF16)
    w_lo = (w - w_hi.astype(F32)).astype(BF16)
    o_ref[...] = _dot(s_hi, w_hi) + _dot(s_lo, w_hi) + _dot(s_hi, w_lo) + b_ref[...]


def _modulation(cond, w_mod, b_mod):
    depth, d, d6 = w_mod.shape
    tn = _pick_tile(d6, 1024)
    return pl.pallas_call(
        _mod_kernel,
        out_shape=jax.ShapeDtypeStruct((depth, MOD_ROWS, d6), F32),
        grid=(depth, d6 // tn),
        in_specs=[pl.BlockSpec((MOD_ROWS, d), lambda l, n: (0, 0)),
                  pl.BlockSpec((None, d, tn), lambda l, n: (l, 0, n)),
                  pl.BlockSpec((None, 1, tn), lambda l, n: (l, 0, n))],
        out_specs=pl.BlockSpec((None, MOD_ROWS, tn), lambda l, n: (l, 0, n)),
        compiler_params=_cparams("parallel", "parallel"),
        name="adaln_modulation",
    )(cond, w_mod, b_mod.reshape(depth, 1, d6))


def _proj_kernel(x_ref, sh_ref, sc_ref, w_ref, o_ref, h_ref):
    @pl.when(pl.program_id(1) == 0)
    def _():
        h_ref[...] = (x_ref[...] * (1.0 + sc_ref[0]) + sh_ref[0]).astype(BF16)

    o_ref[...] = _dot(h_ref[...], w_ref[...])


def _in_proj(x, mod, mod_row, w_in, layer, tm):
    t, d = x.shape
    d_in = w_in.shape[2]
    tn = _pick_tile(d_in, 512)
    return pl.pallas_call(
        _proj_kernel,
        out_shape=jax.ShapeDtypeStruct((t, d_in), F32),
        grid=(t // tm, d_in // tn),
        in_specs=[pl.BlockSpec((tm, d), lambda m, n: (m, 0)),
                  pl.BlockSpec((1, 1, d), lambda m, n: (mod_row(m, 0), 0, 0)),
                  pl.BlockSpec((1, 1, d), lambda m, n: (mod_row(m, 1), 0, 0)),
                  pl.BlockSpec((None, d, tn), lambda m, n: (layer, 0, n))],
        out_specs=pl.BlockSpec((tm, tn), lambda m, n: (m, n)),
        scratch_shapes=[pltpu.VMEM((tm, d), BF16)],
        compiler_params=_cparams("parallel", "arbitrary"),
        name="in_proj",
    )(x, mod, mod, w_in)


def _out_proj_kernel(oa_ref, ob_ref, oc_ref, wa_ref, wb_ref, wc_ref, x_ref, g1_ref, sh2_ref, sc2_ref,
                     lng_ref, lnb_ref, x1_ref, h2_ref, *, alpha):
    mix = _dot(oa_ref[...], wa_ref[...]) + _dot(ob_ref[...], wb_ref[...]) + _dot(oc_ref[...], wc_ref[...])
    x1 = _layer_norm(alpha * x_ref[...] + g1_ref[0] * mix, lng_ref[...], lnb_ref[...])
    x1_ref[...] = x1
    h2_ref[...] = (x1 * (1.0 + sc2_ref[0]) + sh2_ref[0]).astype(BF16)


def _out_proj(oa, ob, oc, w_out, x, mod, mod_row, ln_g, ln_b, layer, tm, alpha):
    t, d = x.shape
    wa, wc = oa.shape[1], oc.shape[1]
    depth = ln_g.shape[0]
    row = lambda j: pl.BlockSpec((1, 1, d), lambda m: (mod_row(m, j), 0, 0))
    vec = pl.BlockSpec((None, 1, d), lambda m: (layer, 0, 0))
    return pl.pallas_call(
        functools.partial(_out_proj_kernel, alpha=alpha),
        out_shape=(jax.ShapeDtypeStruct((t, d), F32), jax.ShapeDtypeStruct((t, d), BF16)),
        grid=(t // tm,),
        in_specs=[pl.BlockSpec((tm, wa), lambda m: (m, 0)),
                  pl.BlockSpec((tm, wa), lambda m: (m, 0)),
                  pl.BlockSpec((tm, wc), lambda m: (m, 0)),
                  pl.BlockSpec((None, wa, d), lambda m: (layer, 0, 0)),
                  pl.BlockSpec((None, wa, d), lambda m: (layer, 1, 0)),
                  pl.BlockSpec((None, wc, d), lambda m: (layer, 1, 0)),
                  pl.BlockSpec((tm, d), lambda m: (m, 0)),
                  row(2), row(3), row(4), vec, vec],
        out_specs=(pl.BlockSpec((tm, d), lambda m: (m, 0)), pl.BlockSpec((tm, d), lambda m: (m, 0))),
        compiler_params=_cparams("parallel"),
        name="out_proj_ln",
    )(oa, ob, oc, w_out, w_out, w_out, x, mod, mod, mod, ln_g.reshape(depth, 1, d), ln_b.reshape(depth, 1, d))


def _ffn_kernel(x_ref, h_ref, wg_ref, wu_ref, wd_ref, g2_ref, lng_ref, lnb_ref, o_ref, acc_ref, *, alpha):
    f = pl.program_id(1)

    @pl.when(f == 0)
    def _():
        acc_ref[...] = jnp.zeros_like(acc_ref)

    h = h_ref[...]
    a = _dot(h, wg_ref[...])
    u = _dot(h, wu_ref[...])
    acc_ref[...] += _dot((_silu(a) * u).astype(BF16), wd_ref[...])

    @pl.when(f == pl.num_programs(1) - 1)
    def _():
        y = alpha * x_ref[...] + g2_ref[0] * acc_ref[...]
        o_ref[...] = _layer_norm(y, lng_ref[...], lnb_ref[...])


def _ffn(x1, h2, w_gu, w_down, mod, mod_row, ln_g, ln_b, layer, tm, alpha):
    t, d = x1.shape
    d_ff = w_down.shape[1]
    depth = ln_g.shape[0]
    tf = _pick_tile(d_ff, 512)
    nf = d_ff // tf
    vec = pl.BlockSpec((None, 1, d), lambda m, f: (layer, 0, 0))
    return pl.pallas_call(
        functools.partial(_ffn_kernel, alpha=alpha),
        out_shape=jax.ShapeDtypeStruct((t, d), F32),
        grid=(t // tm, nf),
        in_specs=[pl.BlockSpec((tm, d), lambda m, f: (m, 0)),
                  pl.BlockSpec((tm, d), lambda m, f: (m, 0)),
                  pl.BlockSpec((None, d, tf), lambda m, f: (layer, 0, f)),
                  pl.BlockSpec((None, d, tf), lambda m, f: (layer, 0, nf + f)),
                  pl.BlockSpec((None, tf, d), lambda m, f: (layer, f, 0)),
                  pl.BlockSpec((1, 1, d), lambda m, f: (mod_row(m, 5), 0, 0)),
                  vec, vec],
        out_specs=pl.BlockSpec((tm, d), lambda m, f: (m, 0)),
        scratch_shapes=[pltpu.VMEM((tm, d), F32)],
        compiler_params=_cparams("parallel", "arbitrary"),
        name="ffn_ln",
    )(x1, h2, w_gu, w_gu, w_down, mod, ln_g.reshape(depth, 1, d), ln_b.reshape(depth, 1, d))


def _diff_attn_kernel(*refs, rope, cached, n_self, lam_init):
    it = iter(refs)
    q_ref, k_ref, v_ref = next(it), next(it), next(it)
    if rope:
        qc_ref, qsa_ref, qsb_ref, kc_ref, ksa_ref, ksb_ref = (next(it) for _ in range(6))
    if cached:
        ck_ref, cv_ref = next(it), next(it)
    lam_ref, g_ref, o_ref, kr_ref, vr_ref = (next(it) for _ in range(5))

    @pl.when(pl.program_id(2) == 0)
    def _():
        k = k_ref[...]
        if rope:
            k = _rope(k, kc_ref[...], ksa_ref[...], ksb_ref[...], DA_QK // 4)
        kr_ref[0:n_self, :] = k.astype(BF16)
        vr_ref[0:n_self, :] = v_ref[...].astype(BF16)
        if cached:
            kr_ref[n_self:, :] = ck_ref[...].astype(BF16)
            vr_ref[n_self:, :] = cv_ref[...].astype(BF16)

    q = q_ref[...]
    if rope:
        q = _rope(q, qc_ref[...], qsa_ref[...], qsb_ref[...], DA_QK // 4)
    q = q * (DA_QK ** -0.5)
    lane = lax.broadcasted_iota(jnp.int32, q.shape, 1)
    q1 = jnp.where(lane < DA_QK, q, 0.0).astype(BF16)
    q2 = jnp.where(lane >= DA_QK, q, 0.0).astype(BF16)
    kr = kr_ref[...]

    def softmax_parts(qm):
        s = _dot_t(qm, kr)
        p = jnp.exp(s - jnp.max(s, axis=-1, keepdims=True))
        return p, jnp.sum(p, axis=-1, keepdims=True)

    p1, l1 = softmax_parts(q1)
    p2, l2 = softmax_parts(q2)
    lp = lam_ref[...]
    lam = (jnp.exp(jnp.sum(lp[0:1] * lp[1:2], axis=-1, keepdims=True))
           - jnp.exp(jnp.sum(lp[2:3] * lp[3:4], axis=-1, keepdims=True)) + lam_init)
    w = p1 * (1.0 / l1) - p2 * (lam / l2)
    o = _dot(w.astype(BF16), vr_ref[...])
    o_ref[...] = (_rms_norm(o, g_ref[...]) * (1.0 - lam_init)).astype(BF16)


def _diff_attn(p, geom, layer, diff_lambda, diff_g, rope_tabs, cache_k, cache_v):
    n_seq, n, t = geom["n_seq"], geom["n"], geom["t"]
    heads = geom["da_heads"]
    rope = rope_tabs is not None
    cached = cache_k is not None
    tq = min(n, 128 if cached else 256)
    nq = n // tq
    n_ctx = cache_k.shape[3] if cached else 0
    k_off, v_off = heads, 2 * heads
    depth = diff_g.shape[0]
    lam_init = 0.8 - 0.6 * math.exp(-0.3 * layer)

    in_specs = [pl.BlockSpec((tq, HEAD), lambda b, h, i: (b * nq + i, h)),
                pl.BlockSpec((n, HEAD), lambda b, h, i: (b, k_off + h)),
                pl.BlockSpec((n, HEAD), lambda b, h, i: (b, v_off + h))]
    args = [p, p, p]
    if rope:
        in_specs += [pl.BlockSpec((tq, HEAD), lambda b, h, i: (i, 0))] * 3
        in_specs += [pl.BlockSpec((n, HEAD), lambda b, h, i: (0, 0))] * 3
        args += list(rope_tabs) * 2
    if cached:
        spec = pl.BlockSpec((None, None, None, n_ctx, HEAD), lambda b, h, i: (b, layer, h, 0, 0))
        in_specs += [spec, spec]
        args += [cache_k, cache_v]
    in_specs += [pl.BlockSpec((None, 4, DA_QK), lambda b, h, i: (layer, 0, 0)),
                 pl.BlockSpec((None, 1, HEAD), lambda b, h, i: (layer, 0, 0))]
    args += [diff_lambda, diff_g.reshape(depth, 1, HEAD)]
    return pl.pallas_call(
        functools.partial(_diff_attn_kernel, rope=rope, cached=cached, n_self=n, lam_init=lam_init),
        out_shape=jax.ShapeDtypeStruct((t, heads * HEAD), BF16),
        grid=(n_seq, heads, nq),
        in_specs=in_specs,
        out_specs=pl.BlockSpec((tq, HEAD), lambda b, h, i: (b * nq + i, h)),
        scratch_shapes=[pltpu.VMEM((n + n_ctx, HEAD), BF16), pltpu.VMEM((n + n_ctx, HEAD), BF16)],
        compiler_params=_cparams("parallel", "parallel", "arbitrary"),
        name="diff_attention",
    )(*args)


def _scan_constants(c):
    levels = int(math.log2(c))
    t = np.arange(c)[:, None]
    s = np.arange(c)[None, :]
    mall = [(s <= t)]
    masks = []
    for j in range(levels):
        m = c >> (j + 1)
        base = (t // (2 * m)) * (2 * m)
        second = (t - base) >= m
        pref = (s >= base + m) & (s <= t)
        suff = (s > t) & (s < base + m)
        mall.append(np.where(second, pref, suff))
        sbase = (s // (2 * m)) * (2 * m)
        masks.append((sbase == base) & second & ((s - sbase) < m))
    masks.append(s == t)
    mall = np.concatenate(mall, axis=0).astype(np.float32)
    masks = np.stack(masks).astype(np.float32)
    flip = lambda a: a.reshape(-1, c, c)[:, ::-1, ::-1].reshape(a.shape)
    return (jnp.asarray(mall, BF16), jnp.asarray(flip(mall), BF16),
            jnp.asarray(masks, F32), jnp.asarray(flip(masks), F32))


def _hgrn_kernel(*refs, n, chunk, has_state, emit_state):
    it = iter(refs)
    q_ref, zf_ref, zb_ref, v_ref, hg_ref = (next(it) for _ in range(5))
    loglb_ref, log1m_ref, onem_ref, g_ref = (next(it) for _ in range(4))
    mall_refs = (next(it), next(it))
    mask_refs = (next(it), next(it))
    s0_ref = next(it) if has_state else None
    o_ref = next(it)
    st_ref = next(it) if emit_state else None
    obuf_refs = (next(it), next(it))
    s_ref = next(it)

    levels = int(math.log2(chunk))
    nchunks = n // chunk
    z_refs = (zf_ref, zb_ref)

    for d in range(2):
        s_ref[d] = s0_ref[d].T if has_state else jnp.zeros((HEAD, HEAD), F32)

    def one_chunk(d, start):
        rows = pl.ds(start, chunk)
        z = z_refs[d][rows, :]
        q = _silu(q_ref[rows, :])
        v = v_ref[rows, :].astype(BF16)
        e = jnp.exp(-jnp.abs(z))
        log_sig = jnp.minimum(z, 0.0) - jnp.log1p(e)
        sig_neg = jnp.where(z >= 0.0, e, 1.0) / (1.0 + e)
        a = loglb_ref[d:d + 1, :]
        b = log1m_ref[d:d + 1, :] + log_sig
        g = jnp.maximum(a, b) + jnp.log1p(jnp.exp(-jnp.abs(a - b)))
        k = onem_ref[d:d + 1, :] * sig_neg
        g_hi = g.astype(BF16)
        g_lo = (g - g_hi.astype(F32)).astype(BF16)
        sums = _dot(mall_refs[d][...], jnp.concatenate([g_hi, g_lo], axis=1))
        sums = sums[:, :HEAD] + sums[:, HEAD:]
        cum = sums[0:chunk]
        att = mask_refs[d][levels] * _dot_t(q.astype(BF16), k.astype(BF16))
        for j in range(levels):
            fac = jnp.exp(sums[(j + 1) * chunk:(j + 2) * chunk])
            att = att + mask_refs[d][j] * _dot_t((q * fac).astype(BF16), (k * fac).astype(BF16))
        st = s_ref[d]
        o = _dot(att.astype(BF16), v) + _dot_t((q * jnp.exp(cum)).astype(BF16), st.astype(BF16))
        obuf_refs[d][rows, :] = o
        last = cum[chunk - 1:chunk] if d == 0 else cum[0:1]
        kt = (k * jnp.exp(last - cum)).astype(BF16)
        s_ref[d] = jnp.exp(last) * st + lax.dot_general(v, kt, (((0,), (0,)), ((), ())),
                                                       preferred_element_type=F32)

    def body(i, carry):
        one_chunk(0, pl.multiple_of(i * chunk, chunk))
        one_chunk(1, pl.multiple_of((nchunks - 1 - i) * chunk, chunk))
        return carry

    lax.fori_loop(0, nchunks, body, 0)

    o = obuf_refs[0][...] + obuf_refs[1][...]
    o_ref[...] = (_rms_norm(o, g_ref[...]) * _silu(hg_ref[...])).astype(BF16)
    if emit_state:
        for d in range(2):
            st_ref[d] = s_ref[d].T


def _hgrn(p, geom, layer, lb_params, hgrn_g, consts, state):
    n_seq, n, t = geom["n_seq"], geom["n"], geom["t"]
    heads = geom["hg_heads"]
    base = 3 * geom["da_heads"]
    has_state = state is not None
    emit_state = not has_state
    depth = hgrn_g.shape[0]
    chunk = min(HG_CHUNK, n)
    col = lambda j: pl.BlockSpec((n, HEAD), lambda b, h: (b, base + j * heads + h))
    lbspec = pl.BlockSpec((None, 2, HEAD), lambda b, h: (layer, 0, h))
    const_specs = [pl.BlockSpec(c.shape, lambda b, h, nd=c.ndim: (0,) * nd) for c in consts]
    in_specs = [col(0), col(1), col(2), col(3), col(4), lbspec, lbspec, lbspec,
                pl.BlockSpec((None, 1, HEAD), lambda b, h: (layer, 0, 0))] + const_specs
    args = [p] * 5 + list(lb_params) + [hgrn_g.reshape(depth, 1, HEAD)] + list(consts)
    if has_state:
        in_specs.append(pl.BlockSpec((None, None, 2, None, HEAD, HEAD), lambda b, h: (b, layer, 0, h, 0, 0)))
        args.append(state)
    out_shape = [jax.ShapeDtypeStruct((t, heads * HEAD), BF16)]
    out_specs = [pl.BlockSpec((n, HEAD), lambda b, h: (b, h))]
    if emit_state:
        out_shape.append(jax.ShapeDtypeStruct((n_seq, 2, heads, HEAD, HEAD), F32))
        out_specs.append(pl.BlockSpec((None, 2, None, HEAD, HEAD), lambda b, h: (b, 0, h, 0, 0)))
    res = pl.pallas_call(
        functools.partial(_hgrn_kernel, n=n, chunk=chunk, has_state=has_state, emit_state=emit_state),
        out_shape=tuple(out_shape),
        grid=(n_seq, heads),
        in_specs=in_specs,
        out_specs=tuple(out_specs),
        scratch_shapes=[pltpu.VMEM((n, HEAD), F32), pltpu.VMEM((n, HEAD), F32),
                        pltpu.VMEM((2, HEAD, HEAD), F32)],
        compiler_params=_cparams("parallel", "parallel"),
        name="hgrn2_scan",
    )(*args)
    return res if emit_state else (res[0], None)


def _swa_ctx_kernel(q_ref, k_ref, v_ref, sink_ref, o_ref, *, layer):
    kv = pl.program_id(1)
    k = k_ref[...].astype(BF16)
    v = v_ref[...].astype(BF16)
    for g in range(SW_GROUP):
        q = (q_ref[:, g * HEAD:(g + 1) * HEAD] * (HEAD ** -0.5)).astype(BF16)
        sink = sink_ref[layer, kv * SW_GROUP + g]
        s = _dot_t(q, k)
        m = jnp.maximum(jnp.max(s, axis=-1, keepdims=True), sink)
        p = jnp.exp(s - m)
        l = jnp.sum(p, axis=-1, keepdims=True) + jnp.exp(sink - m)
        o_ref[:, g * HEAD:(g + 1) * HEAD] = _dot((p * (1.0 / l)).astype(BF16), v).astype(BF16)


def _swa_ctx(p, geom, layer, sink):
    n_seq, n, t = geom["n_seq"], geom["n"], geom["t"]
    kvh = geom["sw_kv"]
    qw = SW_GROUP * HEAD
    q_blk = geom["cq_off"] // qw
    k_blk = geom["ck_off"] // HEAD
    v_blk = k_blk + kvh
    return pl.pallas_call(
        functools.partial(_swa_ctx_kernel, layer=layer),
        out_shape=jax.ShapeDtypeStruct((t, kvh * qw), BF16),
        grid=(n_seq, kvh),
        in_specs=[pl.BlockSpec((n, qw), lambda b, kv: (b, q_blk + kv)),
                  pl.BlockSpec((n, HEAD), lambda b, kv: (b, k_blk + kv)),
                  pl.BlockSpec((n, HEAD), lambda b, kv: (b, v_blk + kv)),
                  pl.BlockSpec(memory_space=pltpu.SMEM)],
        out_specs=pl.BlockSpec((n, qw), lambda b, kv: (b, kv)),
        compiler_params=_cparams("parallel", "parallel"),
        name="sink_attention",
    )(p, p, p, sink)


def _swa_lat_kernel(q_ref, k_ref, v_ref, qc_ref, qsa_ref, qsb_ref, kc_ref, ksa_ref, ksb_ref, ck_ref, cv_ref,
                    sink_ref, o_ref, kr_ref, vr_ref, kctx_ref, vctx_ref, *, layer, n):
    kv = pl.program_id(1)
    qb = pl.program_id(2)
    w = HEAD // 4

    @pl.when(qb == 0)
    def _():
        zeros = jnp.zeros((SW_BLOCK, HEAD), BF16)
        k = _rope(k_ref[...], kc_ref[...], ksa_ref[...], ksb_ref[...], w)
        kr_ref[0:SW_BLOCK, :] = zeros
        kr_ref[SW_BLOCK:SW_BLOCK + n, :] = k.astype(BF16)
        kr_ref[SW_BLOCK + n:, :] = zeros
        vr_ref[0:SW_BLOCK, :] = zeros
        vr_ref[SW_BLOCK:SW_BLOCK + n, :] = v_ref[...].astype(BF16)
        vr_ref[SW_BLOCK + n:, :] = zeros
        kctx_ref[...] = ck_ref[...].astype(BF16)
        vctx_ref[...] = cv_ref[...].astype(BF16)

    band = pl.ds(pl.multiple_of(qb * SW_BLOCK, SW_BLOCK), 3 * SW_BLOCK)
    kb = kr_ref[band, :]
    vb = vr_ref[band, :]
    kc = kctx_ref[...]
    vc = vctx_ref[...]
    qpos = qb * SW_BLOCK + lax.broadcasted_iota(jnp.int32, (SW_BLOCK, 3 * SW_BLOCK), 0)
    kpos = (qb - 1) * SW_BLOCK + lax.broadcasted_iota(jnp.int32, (SW_BLOCK, 3 * SW_BLOCK), 1)
    dist = qpos - kpos
    valid = (dist <= SW_WINDOW) & (dist >= -SW_WINDOW) & (kpos >= 0) & (kpos < n)
    qc, qsa, qsb = qc_ref[...], qsa_ref[...], qsb_ref[...]
    for g in range(SW_GROUP):
        q = _rope(q_ref[:, g * HEAD:(g + 1) * HEAD], qc, qsa, qsb, w)
        q = (q * (HEAD ** -0.5)).astype(BF16)
        sink = sink_ref[layer, kv * SW_GROUP + g]
        s_band = jnp.where(valid, _dot_t(q, kb), NEG_INF)
        s_ctx = _dot_t(q, kc)
        m = jnp.maximum(jnp.maximum(jnp.max(s_band, axis=-1, keepdims=True),
                                    jnp.max(s_ctx, axis=-1, keepdims=True)), sink)
        p_band = jnp.exp(s_band - m)
        p_ctx = jnp.exp(s_ctx - m)
        l = (jnp.sum(p_band, axis=-1, keepdims=True) + jnp.sum(p_ctx, axis=-1, keepdims=True)
             + jnp.exp(sink - m))
        inv = 1.0 / l
        o = _dot((p_band * inv).astype(BF16), vb) + _dot((p_ctx * inv).astype(BF16), vc)
        o_ref[:, g * HEAD:(g + 1) * HEAD] = o.astype(BF16)


def _swa_lat(p, geom, layer, sink, rope_tabs, cache_k, cache_v):
    n_seq, n, t = geom["n_seq"], geom["n"], geom["t"]
    kvh = geom["sw_kv"]
    qw = SW_GROUP * HEAD
    q_blk = geom["cq_off"] // qw
    k_blk = geom["ck_off"] // HEAD
    v_blk = k_blk + kvh
    nb = n // SW_BLOCK
    n_ctx = cache_k.shape[3]
    cspec = pl.BlockSpec((None, None, None, n_ctx, HEAD), lambda b, kv, i: (b, layer, kv, 0, 0))
    return pl.pallas_call(
        functools.partial(_swa_lat_kernel, layer=layer, n=n),
        out_shape=jax.ShapeDtypeStruct((t, kvh * qw), BF16),
        grid=(n_seq, kvh, nb),
        in_specs=[pl.BlockSpec((SW_BLOCK, qw), lambda b, kv, i: (b * nb + i, q_blk + kv)),
                  pl.BlockSpec((n, HEAD), lambda b, kv, i: (b, k_blk + kv)),
                  pl.BlockSpec((n, HEAD), lambda b, kv, i: (b, v_blk + kv))]
                 + [pl.BlockSpec((SW_BLOCK, HEAD), lambda b, kv, i: (i, 0))] * 3
                 + [pl.BlockSpec((n, HEAD), lambda b, kv, i: (0, 0))] * 3
                 + [cspec, cspec, pl.BlockSpec(memory_space=pltpu.SMEM)],
        out_specs=pl.BlockSpec((SW_BLOCK, qw), lambda b, kv, i: (b * nb + i, kv)),
        scratch_shapes=[pltpu.VMEM((n + 2 * SW_BLOCK, HEAD), BF16), pltpu.VMEM((n + 2 * SW_BLOCK, HEAD), BF16),
                        pltpu.VMEM((n_ctx, HEAD), BF16), pltpu.VMEM((n_ctx, HEAD), BF16)],
        compiler_params=_cparams("parallel", "parallel", "arbitrary"),
        name="banded_sink_attention",
    )(p, p, p, *rope_tabs, *rope_tabs, cache_k, cache_v, sink)


def _rope_tables(n, half):
    h = half // 2
    pos = jnp.arange(n)
    inv = ROPE_BASE ** (-jnp.arange(h, dtype=F32) / h)
    zero = jnp.zeros((n, h), F32)
    c, sa, sb = [], [], []
    for axis_pos in (pos // GRID_W, pos % GRID_W):
        ang = axis_pos.astype(F32)[:, None] * inv[None, :]
        cos, sin = jnp.cos(ang), jnp.sin(ang)
        c += [cos, cos]
        sa += [-sin, zero]
        sb += [zero, sin]
    reps = HEAD // (2 * half)
    cat = lambda parts: jnp.tile(jnp.concatenate(parts, axis=1), (1, reps))
    return cat(c), cat(sa), cat(sb)


def _geometry(n_seq, n, d_model):
    da_heads = hg_heads = d_model // 512
    sw_heads = d_model // 256
    sw_kv = sw_heads // SW_GROUP
    cq_off = (3 * da_heads + 5 * hg_heads) * HEAD
    return dict(n_seq=n_seq, n=n, t=n_seq * n, da_heads=da_heads, hg_heads=hg_heads, sw_kv=sw_kv,
                cq_off=cq_off, ck_off=cq_off + sw_heads * HEAD)


def _layer(x, geom, layer, mod, mod_row, tm, wts, params, lb_params, scan_consts, tabs, caches, alpha):
    w_in, w_out, w_gu, w_down = wts
    p = _in_proj(x, mod, mod_row, w_in, layer, tm)
    if caches is None:
        oa = _diff_attn(p, geom, layer, params["diff_lambda"], params["diff_norm_g"], None, None, None)
        ob, st = _hgrn(p, geom, layer, lb_params, params["hgrn_norm_g"], scan_consts, None)
        oc = _swa_ctx(p, geom, layer, params["swa_sink"])
    else:
        ck_d, cv_d, ck_s, cv_s, state = caches
        oa = _diff_attn(p, geom, layer, params["diff_lambda"], params["diff_norm_g"], tabs[0], ck_d, cv_d)
        ob, st = _hgrn(p, geom, layer, lb_params, params["hgrn_norm_g"], scan_consts, state)
        oc = _swa_lat(p, geom, layer, params["swa_sink"], tabs[1], ck_s, cv_s)
    x1, h2 = _out_proj(oa, ob, oc, w_out, x, mod, mod_row, params["ln1_g"], params["ln1_b"], layer, tm, alpha)
    y = _ffn(x1, h2, w_gu, w_down, mod, mod_row, params["ln2_g"], params["ln2_b"], layer, tm, alpha)
    return y, p, st


def kernel(x_prompt, x_sample, cache_diff_k, cache_diff_v, cache_swa_k, cache_swa_v, state_hgrn, c, c_ctx, w_mod,
           b_mod, w_in, w_out, diff_lambda, diff_norm_g, hgrn_lb_logits, hgrn_norm_g, swa_sink, ln1_g, ln1_b, ln2_g,
           ln2_b, w_gate_up, w_down):
    batch, seq, d = x_prompt.shape
    dec_batch, dec_seq, _ = x_sample.shape
    depth = w_mod.shape[0]
    alpha = (2 * depth) ** 0.25
    geom_c = _geometry(batch, seq, d)
    geom_l = _geometry(dec_batch, dec_seq, d)
    assert 1 + dec_batch <= MOD_ROWS

    cond = jnp.zeros((MOD_ROWS, d), F32).at[0].set(c_ctx).at[1:1 + dec_batch].set(c)
    mod = _modulation(cond, w_mod, b_mod).reshape(depth * MOD_ROWS * 6, 1, d)
    lb_params = _lb_params(hgrn_lb_logits)
    params = dict(diff_lambda=diff_lambda, diff_norm_g=diff_norm_g, hgrn_norm_g=hgrn_norm_g, swa_sink=swa_sink,
                  ln1_g=ln1_g, ln1_b=ln1_b, ln2_g=ln2_g, ln2_b=ln2_b)
    wts = tuple(w.astype(BF16) for w in (w_in, w_out, w_gate_up, w_down))
    tabs = (_rope_tables(dec_seq, DA_QK // 2), _rope_tables(dec_seq, HEAD // 2))
    consts_c = _scan_constants(min(HG_CHUNK, seq))
    consts_l = _scan_constants(min(HG_CHUNK, dec_seq))

    tm_c = _pick_tile(batch * seq, 512)
    tm_l = _pick_tile(dec_seq, 512)
    y_p = x_prompt.reshape(batch * seq, d)
    y_s = x_sample.reshape(dec_batch * dec_seq, d)
    nk_d, nv_d, nk_s, nv_s, n_st = [], [], [], [], []
    heads, kvh = geom_c["da_heads"], geom_c["sw_kv"]
    for l in range(depth):
        row_c = lambda m, j, l=l: (l * MOD_ROWS) * 6 + j
        row_l = lambda m, j, l=l: (l * MOD_ROWS + 1 + (m * tm_l) // dec_seq) * 6 + j
        y_p, p_c, st = _layer(y_p, geom_c, l, mod, row_c, tm_c, wts, params, lb_params, consts_c, None, None, alpha)
        caches = (cache_diff_k, cache_diff_v, cache_swa_k, cache_swa_v, state_hgrn)
        y_s, _, _ = _layer(y_s, geom_l, l, mod, row_l, tm_l, wts, params, lb_params, consts_l, tabs, caches, alpha)

        def heads_first(off, nh):
            a = p_c[:, off:off + nh * HEAD].reshape(batch, seq, nh, HEAD)
            return a.transpose(0, 2, 1, 3)

        nk_d.append(heads_first(heads * HEAD, heads))
        nv_d.append(heads_first(2 * heads * HEAD, heads))
        nk_s.append(heads_first(geom_c["ck_off"], kvh))
        nv_s.append(heads_first(geom_c["ck_off"] + kvh * HEAD, kvh))
        n_st.append(st)
    return (y_p.reshape(batch, seq, d), y_s.reshape(dec_batch, dec_seq, d),
            jnp.stack(nk_d, axis=1), jnp.stack(nv_d, axis=1), jnp.stack(nk_s, axis=1), jnp.stack(nv_s, axis=1),
            jnp.stack(n_st, axis=1))
```

```python
import functools
import math

import numpy as np
import jax
import jax.numpy as jnp
from jax import lax
from jax.experimental import pallas as pl
from jax.experimental.pallas import tpu as pltpu

F32 = jnp.float32
BF16 = jnp.bfloat16

GRID_W = 64
ROPE_BASE = 10000.0
LN_EPS = 1e-5
RMS_EPS = 1e-6
NEG_INF = -1e30
LB_FLOOR = 1e-30
HEAD = 128
DA_QK = 64
SW_GROUP = 4
SW_BLOCK = 128
SW_WINDOW = 128
HG_CHUNK = 64
HG_UNROLL = 4
TM_PROJ = 256
TM_OUT = 512
TM_FFN = 1024
MOD_ROWS = 8
VMEM_LIMIT = 56 * 1024 * 1024


def _cparams(*sem):
    return pltpu.CompilerParams(dimension_semantics=sem, vmem_limit_bytes=VMEM_LIMIT)


def _dot(a, b):
    return jnp.dot(a, b, preferred_element_type=F32)


def _dot_t(a, b):
    return lax.dot_general(a, b, (((1,), (1,)), ((), ())), preferred_element_type=F32)


def _silu(x):
    return x / (1.0 + jnp.exp(-x))


def _layer_norm(y, g, b):
    mu = jnp.mean(y, axis=-1, keepdims=True)
    d = y - mu
    var = jnp.mean(d * d, axis=-1, keepdims=True)
    return d * lax.rsqrt(var + LN_EPS) * g + b


def _rms_norm(o, g):
    ms = jnp.mean(o * o, axis=-1, keepdims=True)
    return o * lax.rsqrt(ms + RMS_EPS) * g


def _rope(x, c, sa, sb, w):
    return x * c + pltpu.roll(x, HEAD - w, 1) * sa + pltpu.roll(x, w, 1) * sb


def _pick_tile(n, target):
    t = min(n, target)
    while n % t or t % 128:
        t -= 128
    return t


def _lb_kernel(logit_ref, loglb_ref, log1m_ref, onem_ref):
    depth = logit_ref.shape[0]
    x = [logit_ref[l] for l in range(depth)]
    m = functools.reduce(jnp.maximum, x)
    e = [jnp.exp(xi - m) for xi in x]
    tot = functools.reduce(lambda a, b: a + b, e)
    w = [ei / tot for ei in e]
    acc = jnp.zeros_like(w[0])
    for l in range(depth):
        acc = acc + w[l]
        lb = acc - w[0]
        loglb_ref[l] = jnp.log(jnp.maximum(lb, LB_FLOOR))
        log1m_ref[l] = jnp.log1p(-lb)
        onem_ref[l] = 1.0 - lb


def _lb_params(logits):
    shp = jax.ShapeDtypeStruct(logits.shape, F32)
    return pl.pallas_call(_lb_kernel, out_shape=(shp, shp, shp), name="hgrn_lb_params")(logits)


def _mod_kernel(c_ref, w_ref, b_ref, o_ref):
    s = _silu(c_ref[...])
    s_hi = s.astype(BF16)
    s_lo = (s - s_hi.astype(F32)).astype(BF16)
    w = w_ref[...]
    w_hi = w.astype(BF16)
    w_lo = (w - w_hi.astype(F32)).astype(BF16)
    o_ref[...] = _dot(s_hi, w_hi) + _dot(s_lo, w_hi) + _dot(s_hi, w_lo) + b_ref[...]


def _modulation(cond, w_mod, b_mod):
    depth, d, d6 = w_mod.shape
    tn = _pick_tile(d6, 1024)
    return pl.pallas_call(
        _mod_kernel,
        out_shape=jax.ShapeDtypeStruct((depth, MOD_ROWS, d6), F32),
        grid=(depth, d6 // tn),
        in_specs=[pl.BlockSpec((MOD_ROWS, d), lambda l, n: (0, 0)),
                  pl.BlockSpec((None, d, tn), lambda l, n: (l, 0, n)),
                  pl.BlockSpec((None, 1, tn), lambda l, n: (l, 0, n))],
        out_specs=pl.BlockSpec((None, MOD_ROWS, tn), lambda l, n: (l, 0, n)),
        compiler_params=_cparams("parallel", "parallel"),
        name="adaln_modulation",
    )(cond, w_mod, b_mod.reshape(depth, 1, d6))


def _proj_kernel(x_ref, sh_ref, sc_ref, w_ref, o_ref):
    h = (x_ref[...] * (1.0 + sc_ref[0]) + sh_ref[0]).astype(BF16)
    o_ref[...] = _dot(h, w_ref[...])


def _in_proj(x, mod, mod_row, w_in, layer, tm):
    t, d = x.shape
    d_in = w_in.shape[2]
    return pl.pallas_call(
        _proj_kernel,
        out_shape=jax.ShapeDtypeStruct((t, d_in), F32),
        grid=(t // tm,),
        in_specs=[pl.BlockSpec((tm, d), lambda m: (m, 0)),
                  pl.BlockSpec((1, 1, d), lambda m: (mod_row(m, tm, 0), 0, 0)),
                  pl.BlockSpec((1, 1, d), lambda m: (mod_row(m, tm, 1), 0, 0)),
                  pl.BlockSpec((None, d, d_in), lambda m: (layer, 0, 0), pipeline_mode=pl.Buffered(1))],
        out_specs=pl.BlockSpec((tm, d_in), lambda m: (m, 0)),
        compiler_params=_cparams("parallel"),
        name="in_proj",
    )(x, mod, mod, w_in)


def _out_proj_kernel(oa_ref, ob_ref, oc_ref, wa_ref, wb_ref, wc_ref, x_ref, g1_ref, lng_ref, lnb_ref, x1_ref, *,
                     alpha):
    mix = _dot(oa_ref[...], wa_ref[...]) + _dot(ob_ref[...], wb_ref[...]) + _dot(oc_ref[...], wc_ref[...])
    x1_ref[...] = _layer_norm(alpha * x_ref[...] + g1_ref[0] * mix, lng_ref[...], lnb_ref[...])


def _out_proj(oa, ob, oc, w_out, x, mod, mod_row, ln_g, ln_b, layer, tm, alpha):
    t, d = x.shape
    wa, wc = oa.shape[1], oc.shape[1]
    depth = ln_g.shape[0]
    vec = pl.BlockSpec((None, 1, d), lambda m: (layer, 0, 0))
    return pl.pallas_call(
        functools.partial(_out_proj_kernel, alpha=alpha),
        out_shape=jax.ShapeDtypeStruct((t, d), F32),
        grid=(t // tm,),
        in_specs=[pl.BlockSpec((tm, wa), lambda m: (m, 0)),
                  pl.BlockSpec((tm, wa), lambda m: (m, 0)),
                  pl.BlockSpec((tm, wc), lambda m: (m, 0)),
                  pl.BlockSpec((None, wa, d), lambda m: (layer, 0, 0)),
                  pl.BlockSpec((None, wa, d), lambda m: (layer, 1, 0)),
                  pl.BlockSpec((None, wc, d), lambda m: (layer, 1, 0)),
                  pl.BlockSpec((tm, d), lambda m: (m, 0)),
                  pl.BlockSpec((1, 1, d), lambda m: (mod_row(m, tm, 2), 0, 0)),
                  vec, vec],
        out_specs=pl.BlockSpec((tm, d), lambda m: (m, 0)),
        compiler_params=_cparams("parallel"),
        name="out_proj_ln",
    )(oa, ob, oc, w_out, w_out, w_out, x, mod, ln_g.reshape(depth, 1, d), ln_b.reshape(depth, 1, d))


def _ffn_kernel(x_ref, sh2_ref, sc2_ref, wg_ref, wu_ref, wd_ref, g2_ref, lng_ref, lnb_ref, o_ref, h_ref, *, alpha):
    f = pl.program_id(1)

    @pl.when(f == 0)
    def _():
        h_ref[...] = (x_ref[...] * (1.0 + sc2_ref[0]) + sh2_ref[0]).astype(BF16)
        o_ref[...] = jnp.zeros_like(o_ref)

    h = h_ref[...]
    a = _dot(h, wg_ref[...])
    u = _dot(h, wu_ref[...])
    o_ref[...] += _dot((_silu(a) * u).astype(BF16), wd_ref[...])

    @pl.when(f == pl.num_programs(1) - 1)
    def _():
        y = alpha * x_ref[...] + g2_ref[0] * o_ref[...]
        o_ref[...] = _layer_norm(y, lng_ref[...], lnb_ref[...])


def _ffn(x1, w_gu, w_down, mod, mod_row, ln_g, ln_b, layer, tm, alpha):
    t, d = x1.shape
    d_ff = w_down.shape[1]
    depth = ln_g.shape[0]
    tf = _pick_tile(d_ff, 512)
    nf = d_ff // tf
    vec = pl.BlockSpec((None, 1, d), lambda m, f: (layer, 0, 0))
    row = lambda j: pl.BlockSpec((1, 1, d), lambda m, f: (mod_row(m, tm, j), 0, 0))
    return pl.pallas_call(
        functools.partial(_ffn_kernel, alpha=alpha),
        out_shape=jax.ShapeDtypeStruct((t, d), F32),
        grid=(t // tm, nf),
        in_specs=[pl.BlockSpec((tm, d), lambda m, f: (m, 0), pipeline_mode=pl.Buffered(1)),
                  row(3), row(4),
                  pl.BlockSpec((None, d, tf), lambda m, f: (layer, 0, f)),
                  pl.BlockSpec((None, d, tf), lambda m, f: (layer, 0, nf + f)),
                  pl.BlockSpec((None, tf, d), lambda m, f: (layer, f, 0)),
                  row(5), vec, vec],
        out_specs=pl.BlockSpec((tm, d), lambda m, f: (m, 0)),
        scratch_shapes=[pltpu.VMEM((tm, d), BF16)],
        compiler_params=_cparams("parallel", "arbitrary"),
        name="ffn_ln",
    )(x1, mod, mod, w_gu, w_gu, w_down, mod, ln_g.reshape(depth, 1, d), ln_b.reshape(depth, 1, d))


def _diff_attn_kernel(*refs, rope, cached, n_self, kb, lam_init):
    it = iter(refs)
    q_ref, k_ref, v_ref = next(it), next(it), next(it)
    if rope:
        qc_ref, qsa_ref, qsb_ref, kc_ref, ksa_ref, ksb_ref = (next(it) for _ in range(6))
    if cached:
        ck_ref, cv_ref = next(it), next(it)
    lam_ref, g_ref, o_ref, kr_ref, vt_ref, s_ref, acc_ref = (next(it) for _ in range(7))
    nblk = kr_ref.shape[0]
    nself = n_self // kb
    tq = q_ref.shape[0]

    @pl.when(pl.program_id(2) == 0)
    def _():
        for j in range(nself):
            rows = slice(j * kb, (j + 1) * kb)
            k = k_ref[rows, :]
            if rope:
                k = _rope(k, kc_ref[rows, :], ksa_ref[rows, :], ksb_ref[rows, :], DA_QK // 4)
            kr_ref[j] = k.astype(BF16)
            vt_ref[j] = v_ref[rows, :].T.astype(BF16)
        if cached:
            for j in range(nblk - nself):
                rows = slice(j * kb, (j + 1) * kb)
                kr_ref[nself + j] = ck_ref[rows, :].astype(BF16)
                vt_ref[nself + j] = cv_ref[rows, :].T.astype(BF16)

    q = q_ref[...]
    if rope:
        q = _rope(q, qc_ref[...], qsa_ref[...], qsb_ref[...], DA_QK // 4)
    q = q * (DA_QK ** -0.5)
    lane = lax.broadcasted_iota(jnp.int32, q.shape, 1)
    qz = (jnp.where(lane < DA_QK, q, 0.0).astype(BF16), jnp.where(lane >= DA_QK, q, 0.0).astype(BF16))

    def fold(x, op):
        return op(x.reshape(kb // 8, 8, tq), axis=0)

    def scores(j, ms):
        kblk = kr_ref[j]
        out = []
        for mp in range(2):
            s = _dot_t(kblk, qz[mp])
            s_ref[mp, j] = s
            out.append(jnp.maximum(ms[mp], fold(s, jnp.max)))
        return tuple(out)

    neg = jnp.full((8, tq), -jnp.inf, F32)
    ms = lax.fori_loop(0, nblk, scores, (neg, neg), unroll=True)
    ms = tuple(jnp.max(m, axis=0, keepdims=True) for m in ms)

    acc_ref[...] = jnp.zeros_like(acc_ref)

    def values(j, ls):
        out = []
        for mp in range(2):
            p = jnp.exp(s_ref[mp, j] - ms[mp])
            acc_ref[mp] += _dot(vt_ref[j], p.astype(BF16))
            out.append(ls[mp] + fold(p, jnp.sum))
        return tuple(out)

    zero = jnp.zeros((8, tq), F32)
    ls = lax.fori_loop(0, nblk, values, (zero, zero), unroll=True)
    l1, l2 = (jnp.sum(l, axis=0, keepdims=True) for l in ls)
    lp = lam_ref[...]
    lam = (jnp.exp(jnp.sum(lp[0:1] * lp[1:2], axis=-1, keepdims=True))
           - jnp.exp(jnp.sum(lp[2:3] * lp[3:4], axis=-1, keepdims=True)) + lam_init)
    ot = acc_ref[0] * (1.0 / l1) - acc_ref[1] * (lam / l2)
    ms_o = jnp.mean(ot * ot, axis=0, keepdims=True)
    ot = ot * lax.rsqrt(ms_o + RMS_EPS) * g_ref[...] * (1.0 - lam_init)
    o_ref[...] = ot.T.astype(BF16)


def _diff_attn(p, geom, layer, diff_lambda, diff_g, rope_tabs, cache_k, cache_v):
    n_seq, n, t = geom["n_seq"], geom["n"], geom["t"]
    heads = geom["da_heads"]
    rope = rope_tabs is not None
    cached = cache_k is not None
    tq = min(n, 256)
    kb = min(n, 256)
    nq = n // tq
    n_ctx = cache_k.shape[3] if cached else 0
    nblk = (n + n_ctx) // kb
    assert n % kb == 0 and n_ctx % kb == 0
    k_off, v_off = heads, 2 * heads
    depth = diff_g.shape[0]
    lam_init = 0.8 - 0.6 * math.exp(-0.3 * layer)

    in_specs = [pl.BlockSpec((tq, HEAD), lambda b, h, i: (b * nq + i, h)),
                pl.BlockSpec((n, HEAD), lambda b, h, i: (b, k_off + h)),
                pl.BlockSpec((n, HEAD), lambda b, h, i: (b, v_off + h))]
    args = [p, p, p]
    if rope:
        in_specs += [pl.BlockSpec((tq, HEAD), lambda b, h, i: (i, 0))] * 3
        in_specs += [pl.BlockSpec((n, HEAD), lambda b, h, i: (0, 0))] * 3
        args += list(rope_tabs) * 2
    if cached:
        spec = pl.BlockSpec((None, None, None, n_ctx, HEAD), lambda b, h, i: (b, layer, h, 0, 0))
        in_specs += [spec, spec]
        args += [cache_k, cache_v]
    in_specs += [pl.BlockSpec((None, 4, DA_QK), lambda b, h, i: (layer, 0, 0)),
                 pl.BlockSpec((None, HEAD, 1), lambda b, h, i: (layer, 0, 0))]
    args += [diff_lambda, diff_g.reshape(depth, HEAD, 1)]
    return pl.pallas_call(
        functools.partial(_diff_attn_kernel, rope=rope, cached=cached, n_self=n, kb=kb, lam_init=lam_init),
        out_shape=jax.ShapeDtypeStruct((t, heads * HEAD), BF16),
        grid=(n_seq, heads, nq),
        in_specs=in_specs,
        out_specs=pl.BlockSpec((tq, HEAD), lambda b, h, i: (b * nq + i, h)),
        scratch_shapes=[pltpu.VMEM((nblk, kb, HEAD), BF16), pltpu.VMEM((nblk, HEAD, kb), BF16),
                        pltpu.VMEM((2, nblk, kb, tq), F32), pltpu.VMEM((2, HEAD, tq), F32)],
        compiler_params=_cparams("parallel", "parallel", "arbitrary"),
        name="diff_attention",
    )(*args)


def _scan_constants(c):
    levels = int(math.log2(c))
    t = np.arange(c)[:, None]
    s = np.arange(c)[None, :]
    mall = [(s <= t)]
    masks = []
    for j in range(levels):
        m = c >> (j + 1)
        base = (t // (2 * m)) * (2 * m)
        second = (t - base) >= m
        pref = (s >= base + m) & (s <= t)
        suff = (s > t) & (s < base + m)
        mall.append(np.where(second, pref, suff))
        sbase = (s // (2 * m)) * (2 * m)
        masks.append((sbase == base) & second & ((s - sbase) < m))
    masks.append(s == t)
    mall = np.concatenate(mall, axis=0).astype(np.float32)
    masks = np.stack(masks).astype(np.float32)
    flip = lambda a: a.reshape(-1, c, c)[:, ::-1, ::-1].reshape(a.shape)
    return (jnp.asarray(mall, BF16), jnp.asarray(flip(mall), BF16),
            jnp.asarray(masks, F32), jnp.asarray(flip(masks), F32))


def _hgrn_kernel(*refs, n, chunk, unroll, has_state, emit_state):
    it = iter(refs)
    q_ref, zf_ref, zb_ref, v_ref, hg_ref = (next(it) for _ in range(5))
    loglb_ref, log1m_ref, onem_ref, g_ref = (next(it) for _ in range(4))
    mall_refs = (next(it), next(it))
    mask_refs = (next(it), next(it))
    s0_ref = next(it) if has_state else None
    o_ref = next(it)
    st_ref = next(it) if emit_state else None
    obuf_refs = (next(it), next(it))
    s_ref = next(it)

    levels = int(math.log2(chunk))
    nchunks = n // chunk
    z_refs = (zf_ref, zb_ref)

    for d in range(2):
        s_ref[d] = s0_ref[d].T if has_state else jnp.zeros((HEAD, HEAD), F32)

    def gates(d, start):
        rows = pl.ds(start, chunk)
        z = z_refs[d][rows, :]
        q = _silu(q_ref[rows, :])
        v = v_ref[rows, :].astype(BF16)
        e = jnp.exp(-jnp.abs(z))
        log_sig = jnp.minimum(z, 0.0) - jnp.log1p(e)
        sig_neg = jnp.where(z >= 0.0, e, 1.0) / (1.0 + e)
        a = loglb_ref[d:d + 1, :]
        b = log1m_ref[d:d + 1, :] + log_sig
        g = jnp.maximum(a, b) + jnp.log1p(jnp.exp(-jnp.abs(a - b)))
        k = onem_ref[d:d + 1, :] * sig_neg
        g_hi = g.astype(BF16)
        g_lo = (g - g_hi.astype(F32)).astype(BF16)
        sums = _dot(mall_refs[d][...], jnp.concatenate([g_hi, g_lo], axis=1))
        return q, k, v, sums[:, :HEAD] + sums[:, HEAD:]

    def body(i, carry):
        chains = []
        for u in range(unroll):
            c = i * unroll + u
            chains.append((0, pl.multiple_of(c * chunk, chunk)))
            chains.append((1, pl.multiple_of((nchunks - 1 - c) * chunk, chunk)))
        work = [gates(d, start) for d, start in chains]
        atts = [mask_refs[d][levels] * _dot_t(q.astype(BF16), k.astype(BF16))
                for (d, _), (q, k, _, _) in zip(chains, work)]
        for j in range(levels):
            for ci, ((d, _), (q, k, _, sums)) in enumerate(zip(chains, work)):
                fac = jnp.exp(sums[(j + 1) * chunk:(j + 2) * chunk])
                atts[ci] = atts[ci] + mask_refs[d][j] * _dot_t((q * fac).astype(BF16), (k * fac).astype(BF16))
        intra, delta, q_in, decay = [], [], [], []
        for (d, _), (q, k, v, sums), att in zip(chains, work, atts):
            cum = sums[0:chunk]
            last = cum[chunk - 1:chunk] if d == 0 else cum[0:1]
            kt = (k * jnp.exp(last - cum)).astype(BF16)
            intra.append(_dot(att.astype(BF16), v))
            delta.append(lax.dot_general(v, kt, (((0,), (0,)), ((), ())), preferred_element_type=F32))
            q_in.append((q * jnp.exp(cum)).astype(BF16))
            decay.append(jnp.exp(last))
        st = [s_ref[0], s_ref[1]]
        for ci, (d, start) in enumerate(chains):
            obuf_refs[d][pl.ds(start, chunk), :] = intra[ci] + _dot_t(q_in[ci], st[d].astype(BF16))
            st[d] = decay[ci] * st[d] + delta[ci]
        s_ref[0] = st[0]
        s_ref[1] = st[1]
        return carry

    lax.fori_loop(0, nchunks // unroll, body, 0)

    o = obuf_refs[0][...] + obuf_refs[1][...]
    o_ref[...] = (_rms_norm(o, g_ref[...]) * _silu(hg_ref[...])).astype(BF16)
    if emit_state:
        for d in range(2):
            st_ref[d] = s_ref[d].T


def _hgrn(p, geom, layer, lb_params, hgrn_g, consts, state):
    n_seq, n, t = geom["n_seq"], geom["n"], geom["t"]
    heads = geom["hg_heads"]
    base = 3 * geom["da_heads"]
    has_state = state is not None
    emit_state = not has_state
    depth = hgrn_g.shape[0]
    chunk = min(HG_CHUNK, n)
    col = lambda j: pl.BlockSpec((n, HEAD), lambda b, h: (b, base + j * heads + h))
    lbspec = pl.BlockSpec((None, 2, HEAD), lambda b, h: (layer, 0, h))
    const_specs = [pl.BlockSpec(c.shape, lambda b, h, nd=c.ndim: (0,) * nd) for c in consts]
    in_specs = [col(0), col(1), col(2), col(3), col(4), lbspec, lbspec, lbspec,
                pl.BlockSpec((None, 1, HEAD), lambda b, h: (layer, 0, 0))] + const_specs
    args = [p] * 5 + list(lb_params) + [hgrn_g.reshape(depth, 1, HEAD)] + list(consts)
    if has_state:
        in_specs.append(pl.BlockSpec((None, None, 2, None, HEAD, HEAD), lambda b, h: (b, layer, 0, h, 0, 0)))
        args.append(state)
    out_shape = [jax.ShapeDtypeStruct((t, heads * HEAD), BF16)]
    out_specs = [pl.BlockSpec((n, HEAD), lambda b, h: (b, h))]
    if emit_state:
        out_shape.append(jax.ShapeDtypeStruct((n_seq, 2, heads, HEAD, HEAD), F32))
        out_specs.append(pl.BlockSpec((None, 2, None, HEAD, HEAD), lambda b, h: (b, 0, h, 0, 0)))
    res = pl.pallas_call(
        functools.partial(_hgrn_kernel, n=n, chunk=chunk, unroll=HG_UNROLL if (n // chunk) % HG_UNROLL == 0 else 1,
                          has_state=has_state, emit_state=emit_state),
        out_shape=tuple(out_shape),
        grid=(n_seq, heads),
        in_specs=in_specs,
        out_specs=tuple(out_specs),
        scratch_shapes=[pltpu.VMEM((n, HEAD), F32), pltpu.VMEM((n, HEAD), F32),
                        pltpu.VMEM((2, HEAD, HEAD), F32)],
        compiler_params=_cparams("parallel", "parallel"),
        name="hgrn2_scan",
    )(*args)
    return res if emit_state else (res[0], None)


def _swa_ctx_kernel(q_ref, k_ref, v_ref, sink_ref, o_ref, *, layer):
    kv = pl.program_id(1)
    k = k_ref[...].astype(BF16)
    v = v_ref[...].astype(BF16)
    for g in range(SW_GROUP):
        q = (q_ref[:, g * HEAD:(g + 1) * HEAD] * (HEAD ** -0.5)).astype(BF16)
        sink = sink_ref[layer, kv * SW_GROUP + g]
        s = _dot_t(q, k)
        m = jnp.maximum(jnp.max(s, axis=-1, keepdims=True), sink)
        p = jnp.exp(s - m)
        l = jnp.sum(p, axis=-1, keepdims=True) + jnp.exp(sink - m)
        o_ref[:, g * HEAD:(g + 1) * HEAD] = _dot((p * (1.0 / l)).astype(BF16), v).astype(BF16)


def _swa_ctx(p, geom, layer, sink):
    n_seq, n, t = geom["n_seq"], geom["n"], geom["t"]
    kvh = geom["sw_kv"]
    qw = SW_GROUP * HEAD
    q_blk = geom["cq_off"] // qw
    k_blk = geom["ck_off"] // HEAD
    v_blk = k_blk + kvh
    return pl.pallas_call(
        functools.partial(_swa_ctx_kernel, layer=layer),
        out_shape=jax.ShapeDtypeStruct((t, kvh * qw), BF16),
        grid=(n_seq, kvh),
        in_specs=[pl.BlockSpec((n, qw), lambda b, kv: (b, q_blk + kv)),
                  pl.BlockSpec((n, HEAD), lambda b, kv: (b, k_blk + kv)),
                  pl.BlockSpec((n, HEAD), lambda b, kv: (b, v_blk + kv)),
                  pl.BlockSpec(memory_space=pltpu.SMEM)],
        out_specs=pl.BlockSpec((n, qw), lambda b, kv: (b, kv)),
        compiler_params=_cparams("parallel", "parallel"),
        name="sink_attention",
    )(p, p, p, sink)


def _swa_lat_kernel(q_ref, k_ref, v_ref, qc_ref, qsa_ref, qsb_ref, kc_ref, ksa_ref, ksb_ref, ck_ref, cv_ref,
                    sink_ref, o_ref, kr_ref, vr_ref, kctx_ref, vctx_ref, *, layer, n):
    kv = pl.program_id(1)
    qb = pl.program_id(2)
    w = HEAD // 4

    @pl.when(qb == 0)
    def _():
        zeros = jnp.zeros((SW_BLOCK, HEAD), BF16)
        k = _rope(k_ref[...], kc_ref[...], ksa_ref[...], ksb_ref[...], w)
        kr_ref[0:SW_BLOCK, :] = zeros
        kr_ref[SW_BLOCK:SW_BLOCK + n, :] = k.astype(BF16)
        kr_ref[SW_BLOCK + n:, :] = zeros
        vr_ref[0:SW_BLOCK, :] = zeros
        vr_ref[SW_BLOCK:SW_BLOCK + n, :] = v_ref[...].astype(BF16)
        vr_ref[SW_BLOCK + n:, :] = zeros
        kctx_ref[...] = ck_ref[...].astype(BF16)
        vctx_ref[...] = cv_ref[...].astype(BF16)

    band = pl.ds(pl.multiple_of(qb * SW_BLOCK, SW_BLOCK), 3 * SW_BLOCK)
    kb = kr_ref[band, :]
    vb = vr_ref[band, :]
    kc = kctx_ref[...]
    vc = vctx_ref[...]
    qpos = qb * SW_BLOCK + lax.broadcasted_iota(jnp.int32, (SW_BLOCK, 3 * SW_BLOCK), 0)
    kpos = (qb - 1) * SW_BLOCK + lax.broadcasted_iota(jnp.int32, (SW_BLOCK, 3 * SW_BLOCK), 1)
    dist = qpos - kpos
    valid = (dist <= SW_WINDOW) & (dist >= -SW_WINDOW) & (kpos >= 0) & (kpos < n)
    qc, qsa, qsb = qc_ref[...], qsa_ref[...], qsb_ref[...]
    for g in range(SW_GROUP):
        q = _rope(q_ref[:, g * HEAD:(g + 1) * HEAD], qc, qsa, qsb, w)
        q = (q * (HEAD ** -0.5)).astype(BF16)
        sink = sink_ref[layer, kv * SW_GROUP + g]
        s_band = jnp.where(valid, _dot_t(q, kb), NEG_INF)
        s_ctx = _dot_t(q, kc)
        m = jnp.maximum(jnp.maximum(jnp.max(s_band, axis=-1, keepdims=True),
                                    jnp.max(s_ctx, axis=-1, keepdims=True)), sink)
        p_band = jnp.exp(s_band - m)
        p_ctx = jnp.exp(s_ctx - m)
        l = (jnp.sum(p_band, axis=-1, keepdims=True) + jnp.sum(p_ctx, axis=-1, keepdims=True)
             + jnp.exp(sink - m))
        inv = 1.0 / l
        o = _dot((p_band * inv).astype(BF16), vb) + _dot((p_ctx * inv).astype(BF16), vc)
        o_ref[:, g * HEAD:(g + 1) * HEAD] = o.astype(BF16)


def _swa_lat(p, geom, layer, sink, rope_tabs, cache_k, cache_v):
    n_seq, n, t = geom["n_seq"], geom["n"], geom["t"]
    kvh = geom["sw_kv"]
    qw = SW_GROUP * HEAD
    q_blk = geom["cq_off"] // qw
    k_blk = geom["ck_off"] // HEAD
    v_blk = k_blk + kvh
    nb = n // SW_BLOCK
    n_ctx = cache_k.shape[3]
    cspec = pl.BlockSpec((None, None, None, n_ctx, HEAD), lambda b, kv, i: (b, layer, kv, 0, 0))
    return pl.pallas_call(
        functools.partial(_swa_lat_kernel, layer=layer, n=n),
        out_shape=jax.ShapeDtypeStruct((t, kvh * qw), BF16),
        grid=(n_seq, kvh, nb),
        in_specs=[pl.BlockSpec((SW_BLOCK, qw), lambda b, kv, i: (b * nb + i, q_blk + kv)),
                  pl.BlockSpec((n, HEAD), lambda b, kv, i: (b, k_blk + kv)),
                  pl.BlockSpec((n, HEAD), lambda b, kv, i: (b, v_blk + kv))]
                 + [pl.BlockSpec((SW_BLOCK, HEAD), lambda b, kv, i: (i, 0))] * 3
                 + [pl.BlockSpec((n, HEAD), lambda b, kv, i: (0, 0))] * 3
                 + [cspec, cspec, pl.BlockSpec(memory_space=pltpu.SMEM)],
        out_specs=pl.BlockSpec((SW_BLOCK, qw), lambda b, kv, i: (b * nb + i, kv)),
        scratch_shapes=[pltpu.VMEM((n + 2 * SW_BLOCK, HEAD), BF16), pltpu.VMEM((n + 2 * SW_BLOCK, HEAD), BF16),
                        pltpu.VMEM((n_ctx, HEAD), BF16), pltpu.VMEM((n_ctx, HEAD), BF16)],
        compiler_params=_cparams("parallel", "parallel", "arbitrary"),
        name="banded_sink_attention",
    )(p, p, p, *rope_tabs, *rope_tabs, cache_k, cache_v, sink)


def _rope_tables(n, half):
    h = half // 2
    pos = jnp.arange(n)
    inv = ROPE_BASE ** (-jnp.arange(h, dtype=F32) / h)
    zero = jnp.zeros((n, h), F32)
    c, sa, sb = [], [], []
    for axis_pos in (pos // GRID_W, pos % GRID_W):
        ang = axis_pos.astype(F32)[:, None] * inv[None, :]
        cos, sin = jnp.cos(ang), jnp.sin(ang)
        c += [cos, cos]
        sa += [-sin, zero]
        sb += [zero, sin]
    reps = HEAD // (2 * half)
    cat = lambda parts: jnp.tile(jnp.concatenate(parts, axis=1), (1, reps))
    return cat(c), cat(sa), cat(sb)


def _geometry(n_seq, n, d_model):
    da_heads = hg_heads = d_model // 512
    sw_heads = d_model // 256
    sw_kv = sw_heads // SW_GROUP
    cq_off = (3 * da_heads + 5 * hg_heads) * HEAD
    return dict(n_seq=n_seq, n=n, t=n_seq * n, da_heads=da_heads, hg_heads=hg_heads, sw_kv=sw_kv,
                cq_off=cq_off, ck_off=cq_off + sw_heads * HEAD)


def _token_tiles(n):
    return _pick_tile(n, TM_PROJ), _pick_tile(n, TM_OUT), _pick_tile(n, TM_FFN)


def _layer(x, geom, layer, mod, mod_row, wts, params, lb_params, scan_consts, tabs, caches, alpha):
    w_in, w_out, w_gu, w_down = wts
    tm_proj, tm_out, tm_ffn = _token_tiles(geom["t"] if caches is None else geom["n"])
    p = _in_proj(x, mod, mod_row, w_in, layer, tm_proj)
    if caches is None:
        oa = _diff_attn(p, geom, layer, params["diff_lambda"], params["diff_norm_g"], None, None, None)
        ob, st = _hgrn(p, geom, layer, lb_params, params["hgrn_norm_g"], scan_consts, None)
        oc = _swa_ctx(p, geom, layer, params["swa_sink"])
    else:
        ck_d, cv_d, ck_s, cv_s, state = caches
        oa = _diff_attn(p, geom, layer, params["diff_lambda"], params["diff_norm_g"], tabs[0], ck_d, cv_d)
        ob, st = _hgrn(p, geom, layer, lb_params, params["hgrn_norm_g"], scan_consts, state)
        oc = _swa_lat(p, geom, layer, params["swa_sink"], tabs[1], ck_s, cv_s)
    x1 = _out_proj(oa, ob, oc, w_out, x, mod, mod_row, params["ln1_g"], params["ln1_b"], layer, tm_out, alpha)
    y = _ffn(x1, w_gu, w_down, mod, mod_row, params["ln2_g"], params["ln2_b"], layer, tm_ffn, alpha)
    return y, p, st


def kernel(x_prompt, x_sample, cache_diff_k, cache_diff_v, cache_swa_k, cache_swa_v, state_hgrn, c, c_ctx, w_mod,
           b_mod, w_in, w_out, diff_lambda, diff_norm_g, hgrn_lb_logits, hgrn_norm_g, swa_sink, ln1_g, ln1_b, ln2_g,
           ln2_b, w_gate_up, w_down):
    batch, seq, d = x_prompt.shape
    dec_batch, dec_seq, _ = x_sample.shape
    depth = w_mod.shape[0]
    alpha = (2 * depth) ** 0.25
    geom_c = _geometry(batch, seq, d)
    geom_l = _geometry(dec_batch, dec_seq, d)
    assert 1 + dec_batch <= MOD_ROWS

    cond = jnp.zeros((MOD_ROWS, d), F32).at[0].set(c_ctx).at[1:1 + dec_batch].set(c)
    mod = _modulation(cond, w_mod, b_mod).reshape(depth * MOD_ROWS * 6, 1, d)
    lb_params = _lb_params(hgrn_lb_logits)
    params = dict(diff_lambda=diff_lambda, diff_norm_g=diff_norm_g, hgrn_norm_g=hgrn_norm_g, swa_sink=swa_sink,
                  ln1_g=ln1_g, ln1_b=ln1_b, ln2_g=ln2_g, ln2_b=ln2_b)
    wts = tuple(w.astype(BF16) for w in (w_in, w_out, w_gate_up, w_down))
    tabs = (_rope_tables(dec_seq, DA_QK // 2), _rope_tables(dec_seq, HEAD // 2))
    consts_c = _scan_constants(min(HG_CHUNK, seq))
    consts_l = _scan_constants(min(HG_CHUNK, dec_seq))

    y_p = x_prompt.reshape(batch * seq, d)
    y_s = x_sample.reshape(dec_batch * dec_seq, d)
    nk_d, nv_d, nk_s, nv_s, n_st = [], [], [], [], []
    heads, kvh = geom_c["da_heads"], geom_c["sw_kv"]
    for l in range(depth):
        row_c = lambda m, tm, j, l=l: (l * MOD_ROWS) * 6 + j
        row_l = lambda m, tm, j, l=l: (l * MOD_ROWS + 1 + (m * tm) // dec_seq) * 6 + j
        y_p, p_c, st = _layer(y_p, geom_c, l, mod, row_c, wts, params, lb_params, consts_c, None, None, alpha)
        caches = (cache_diff_k, cache_diff_v, cache_swa_k, cache_swa_v, state_hgrn)
        y_s, _, _ = _layer(y_s, geom_l, l, mod, row_l, wts, params, lb_params, consts_l, tabs, caches, alpha)

        def heads_first(off, nh):
            a = p_c[:, off:off + nh * HEAD].reshape(batch, seq, nh, HEAD)
            return a.transpose(0, 2, 1, 3)

        nk_d.append(heads_first(heads * HEAD, heads))
        nv_d.append(heads_first(2 * heads * HEAD, heads))
        nk_s.append(heads_first(geom_c["ck_off"], kvh))
        nv_s.append(heads_first(geom_c["ck_off"] + kvh * HEAD, kvh))
        n_st.append(st)
    return (y_p.reshape(batch, seq, d), y_s.reshape(dec_batch, dec_seq, d),
            jnp.stack(nk_d, axis=1), jnp.stack(nv_d, axis=1), jnp.stack(nk_s, axis=1), jnp.stack(nv_s, axis=1),
            jnp.stack(n_st, axis=1))
```

```python
import functools
import math

import numpy as np
import jax
import jax.numpy as jnp
from jax import lax
from jax.experimental import pallas as pl
from jax.experimental.pallas import tpu as pltpu

F32 = jnp.float32
BF16 = jnp.bfloat16

GRID_W = 64
ROPE_BASE = 10000.0
LN_EPS = 1e-5
RMS_EPS = 1e-6
NEG_INF = -1e30
LB_FLOOR = 1e-30
HEAD = 128
DA_QK = 64
SW_GROUP = 4
SW_BLOCK = 128
SW_WINDOW = 128
HG_CHUNK = 64
HG_UNROLL = 4
TM_PROJ = 256
TM_OUT = 512
TM_FFN = 512
MOD_ROWS = 8
VMEM_LIMIT = 56 * 1024 * 1024


def _cparams(*sem):
    return pltpu.CompilerParams(dimension_semantics=sem, vmem_limit_bytes=VMEM_LIMIT)


def _dot(a, b):
    return jnp.dot(a, b, preferred_element_type=F32)


def _dot_t(a, b):
    return lax.dot_general(a, b, (((1,), (1,)), ((), ())), preferred_element_type=F32)


def _silu(x):
    return x / (1.0 + jnp.exp(-x))


def _layer_norm(y, g, b):
    mu = jnp.mean(y, axis=-1, keepdims=True)
    d = y - mu
    var = jnp.mean(d * d, axis=-1, keepdims=True)
    return d * lax.rsqrt(var + LN_EPS) * g + b


def _rms_norm(o, g):
    ms = jnp.mean(o * o, axis=-1, keepdims=True)
    return o * lax.rsqrt(ms + RMS_EPS) * g


def _rope(x, c, sa, sb, w):
    return x * c + pltpu.roll(x, HEAD - w, 1) * sa + pltpu.roll(x, w, 1) * sb


def _pick_tile(n, target):
    t = min(n, target)
    while n % t or t % 128:
        t -= 128
    return t


def _lb_kernel(logit_ref, loglb_ref, log1m_ref, onem_ref):
    depth = logit_ref.shape[0]
    x = [logit_ref[l] for l in range(depth)]
    m = functools.reduce(jnp.maximum, x)
    e = [jnp.exp(xi - m) for xi in x]
    tot = functools.reduce(lambda a, b: a + b, e)
    w = [ei / tot for ei in e]
    acc = jnp.zeros_like(w[0])
    for l in range(depth):
        acc = acc + w[l]
        lb = acc - w[0]
        loglb_ref[l] = jnp.log(jnp.maximum(lb, LB_FLOOR))
        log1m_ref[l] = jnp.log1p(-lb)
        onem_ref[l] = 1.0 - lb


def _lb_params(logits):
    shp = jax.ShapeDtypeStruct(logits.shape, F32)
    return pl.pallas_call(_lb_kernel, out_shape=(shp, shp, shp), name="hgrn_lb_params")(logits)


def _mod_kernel(c_ref, w_ref, b_ref, o_ref):
    s = _silu(c_ref[...])
    s_hi = s.astype(BF16)
    s_lo = (s - s_hi.astype(F32)).astype(BF16)
    w = w_ref[...]
    w_hi = w.astype(BF16)
    w_lo = (w - w_hi.astype(F32)).astype(BF16)
    o_ref[...] = _dot(s_hi, w_hi) + _dot(s_lo, w_hi) + _dot(s_hi, w_lo) + b_ref[...]


def _modulation(cond, w_mod, b_mod):
    depth, d, d6 = w_mod.shape
    tn = _pick_tile(d6, 1024)
    return pl.pallas_call(
        _mod_kernel,
        out_shape=jax.ShapeDtypeStruct((depth, MOD_ROWS, d6), F32),
        grid=(depth, d6 // tn),
        in_specs=[pl.BlockSpec((MOD_ROWS, d), lambda l, n: (0, 0)),
                  pl.BlockSpec((None, d, tn), lambda l, n: (l, 0, n)),
                  pl.BlockSpec((None, 1, tn), lambda l, n: (l, 0, n))],
        out_specs=pl.BlockSpec((None, MOD_ROWS, tn), lambda l, n: (l, 0, n)),
        compiler_params=_cparams("parallel", "parallel"),
        name="adaln_modulation",
    )(cond, w_mod, b_mod.reshape(depth, 1, d6))


def _proj_kernel(x_ref, sh_ref, sc_ref, w_ref, o_ref):
    h = (x_ref[...] * (1.0 + sc_ref[0]) + sh_ref[0]).astype(BF16)
    o_ref[...] = _dot(h, w_ref[...])


def _in_proj(x, mod, mod_row, w_in, layer, tm):
    t, d = x.shape
    d_in = w_in.shape[2]
    return pl.pallas_call(
        _proj_kernel,
        out_shape=jax.ShapeDtypeStruct((t, d_in), F32),
        grid=(t // tm,),
        in_specs=[pl.BlockSpec((tm, d), lambda m: (m, 0)),
                  pl.BlockSpec((1, 1, d), lambda m: (mod_row(m, tm, 0), 0, 0)),
                  pl.BlockSpec((1, 1, d), lambda m: (mod_row(m, tm, 1), 0, 0)),
                  pl.BlockSpec((None, d, d_in), lambda m: (layer, 0, 0), pipeline_mode=pl.Buffered(1))],
        out_specs=pl.BlockSpec((tm, d_in), lambda m: (m, 0)),
        compiler_params=_cparams("parallel"),
        name="in_proj",
    )(x, mod, mod, w_in)


def _out_proj_kernel(oa_ref, ob_ref, oc_ref, wa_ref, wb_ref, wc_ref, x_ref, g1_ref, lng_ref, lnb_ref, x1_ref, *,
                     alpha):
    mix = _dot(oa_ref[...], wa_ref[...]) + _dot(ob_ref[...], wb_ref[...]) + _dot(oc_ref[...], wc_ref[...])
    x1_ref[...] = _layer_norm(alpha * x_ref[...] + g1_ref[0] * mix, lng_ref[...], lnb_ref[...])


def _out_proj(oa, ob, oc, w_out, x, mod, mod_row, ln_g, ln_b, layer, tm, alpha):
    t, d = x.shape
    wa, wc = oa.shape[1], oc.shape[1]
    depth = ln_g.shape[0]
    vec = pl.BlockSpec((None, 1, d), lambda m: (layer, 0, 0))
    return pl.pallas_call(
        functools.partial(_out_proj_kernel, alpha=alpha),
        out_shape=jax.ShapeDtypeStruct((t, d), F32),
        grid=(t // tm,),
        in_specs=[pl.BlockSpec((tm, wa), lambda m: (m, 0)),
                  pl.BlockSpec((tm, wa), lambda m: (m, 0)),
                  pl.BlockSpec((tm, wc), lambda m: (m, 0)),
                  pl.BlockSpec((None, wa, d), lambda m: (layer, 0, 0)),
                  pl.BlockSpec((None, wa, d), lambda m: (layer, 1, 0)),
                  pl.BlockSpec((None, wc, d), lambda m: (layer, 1, 0)),
                  pl.BlockSpec((tm, d), lambda m: (m, 0)),
                  pl.BlockSpec((1, 1, d), lambda m: (mod_row(m, tm, 2), 0, 0)),
                  vec, vec],
        out_specs=pl.BlockSpec((tm, d), lambda m: (m, 0)),
        compiler_params=_cparams("parallel"),
        name="out_proj_ln",
    )(oa, ob, oc, w_out, w_out, w_out, x, mod, ln_g.reshape(depth, 1, d), ln_b.reshape(depth, 1, d))


def _ffn_kernel(x_ref, sh2_ref, sc2_ref, wg_ref, wu_ref, wd_ref, g2_ref, lng_ref, lnb_ref, o_ref, h_ref, *, alpha):
    f = pl.program_id(1)

    @pl.when(f == 0)
    def _():
        h_ref[...] = (x_ref[...] * (1.0 + sc2_ref[0]) + sh2_ref[0]).astype(BF16)
        o_ref[...] = jnp.zeros_like(o_ref)

    h = h_ref[...]
    a = _dot(h, wg_ref[...])
    u = _dot(h, wu_ref[...])
    o_ref[...] += _dot((_silu(a) * u).astype(BF16), wd_ref[...])

    @pl.when(f == pl.num_programs(1) - 1)
    def _():
        y = alpha * x_ref[...] + g2_ref[0] * o_ref[...]
        o_ref[...] = _layer_norm(y, lng_ref[...], lnb_ref[...])


def _ffn(x1, w_gu, w_down, mod, mod_row, ln_g, ln_b, layer, tm, alpha):
    t, d = x1.shape
    d_ff = w_down.shape[1]
    depth = ln_g.shape[0]
    tf = _pick_tile(d_ff, 512)
    nf = d_ff // tf
    vec = pl.BlockSpec((None, 1, d), lambda m, f: (layer, 0, 0))
    row = lambda j: pl.BlockSpec((1, 1, d), lambda m, f: (mod_row(m, tm, j), 0, 0))
    return pl.pallas_call(
        functools.partial(_ffn_kernel, alpha=alpha),
        out_shape=jax.ShapeDtypeStruct((t, d), F32),
        grid=(t // tm, nf),
        in_specs=[pl.BlockSpec((tm, d), lambda m, f: (m, 0)),
                  row(3), row(4),
                  pl.BlockSpec((None, d, tf), lambda m, f: (layer, 0, f)),
                  pl.BlockSpec((None, d, tf), lambda m, f: (layer, 0, nf + f)),
                  pl.BlockSpec((None, tf, d), lambda m, f: (layer, f, 0)),
                  row(5), vec, vec],
        out_specs=pl.BlockSpec((tm, d), lambda m, f: (m, 0)),
        scratch_shapes=[pltpu.VMEM((tm, d), BF16)],
        compiler_params=_cparams("parallel", "arbitrary"),
        name="ffn_ln",
    )(x1, mod, mod, w_gu, w_gu, w_down, mod, ln_g.reshape(depth, 1, d), ln_b.reshape(depth, 1, d))


def _diff_attn_kernel(*refs, rope, cached, n_self, kb, lam_init):
    it = iter(refs)
    q_ref, k_ref, v_ref = next(it), next(it), next(it)
    if rope:
        qc_ref, qsa_ref, qsb_ref, kc_ref, ksa_ref, ksb_ref = (next(it) for _ in range(6))
    if cached:
        ck_ref, cv_ref = next(it), next(it)
    lam_ref, g_ref, o_ref, kr_ref, vt_ref, s_ref, acc_ref = (next(it) for _ in range(7))
    nblk = kr_ref.shape[0]
    nself = n_self // kb
    tq = q_ref.shape[0]

    @pl.when(pl.program_id(2) == 0)
    def _():
        for j in range(nself):
            rows = slice(j * kb, (j + 1) * kb)
            k = k_ref[rows, :]
            if rope:
                k = _rope(k, kc_ref[rows, :], ksa_ref[rows, :], ksb_ref[rows, :], DA_QK // 4)
            kr_ref[j] = k.astype(BF16)
            vt_ref[j] = v_ref[rows, :].T.astype(BF16)
        if cached:
            for j in range(nblk - nself):
                rows = slice(j * kb, (j + 1) * kb)
                kr_ref[nself + j] = ck_ref[rows, :].astype(BF16)
                vt_ref[nself + j] = cv_ref[rows, :].T.astype(BF16)

    q = q_ref[...]
    if rope:
        q = _rope(q, qc_ref[...], qsa_ref[...], qsb_ref[...], DA_QK // 4)
    q = q * (DA_QK ** -0.5)
    lane = lax.broadcasted_iota(jnp.int32, q.shape, 1)
    qz = (jnp.where(lane < DA_QK, q, 0.0).astype(BF16), jnp.where(lane >= DA_QK, q, 0.0).astype(BF16))

    def fold(x, op):
        return op(x.reshape(kb // 8, 8, tq), axis=0)

    def scores(j, ms):
        kblk = kr_ref[j]
        out = []
        for mp in range(2):
            s = _dot_t(kblk, qz[mp])
            s_ref[mp, j] = s
            out.append(jnp.maximum(ms[mp], fold(s, jnp.max)))
        return tuple(out)

    neg = jnp.full((8, tq), -jnp.inf, F32)
    ms = lax.fori_loop(0, nblk, scores, (neg, neg), unroll=True)
    ms = tuple(jnp.max(m, axis=0, keepdims=True) for m in ms)

    acc_ref[...] = jnp.zeros_like(acc_ref)

    def values(j, ls):
        out = []
        for mp in range(2):
            p = jnp.exp(s_ref[mp, j] - ms[mp])
            acc_ref[mp] += _dot(vt_ref[j], p.astype(BF16))
            out.append(ls[mp] + fold(p, jnp.sum))
        return tuple(out)

    zero = jnp.zeros((8, tq), F32)
    ls = lax.fori_loop(0, nblk, values, (zero, zero), unroll=True)
    l1, l2 = (jnp.sum(l, axis=0, keepdims=True) for l in ls)
    lp = lam_ref[...]
    lam = (jnp.exp(jnp.sum(lp[0:1] * lp[1:2], axis=-1, keepdims=True))
           - jnp.exp(jnp.sum(lp[2:3] * lp[3:4], axis=-1, keepdims=True)) + lam_init)
    ot = acc_ref[0] * (1.0 / l1) - acc_ref[1] * (lam / l2)
    ms_o = jnp.mean(ot * ot, axis=0, keepdims=True)
    ot = ot * lax.rsqrt(ms_o + RMS_EPS) * g_ref[...] * (1.0 - lam_init)
    o_ref[...] = ot.T.astype(BF16)


def _diff_attn(p, geom, layer, diff_lambda, diff_g, rope_tabs, cache_k, cache_v):
    n_seq, n, t = geom["n_seq"], geom["n"], geom["t"]
    heads = geom["da_heads"]
    rope = rope_tabs is not None
    cached = cache_k is not None
    tq = min(n, 256)
    kb = min(n, 256)
    nq = n // tq
    n_ctx = cache_k.shape[3] if cached else 0
    nblk = (n + n_ctx) // kb
    assert n % kb == 0 and n_ctx % kb == 0
    k_off, v_off = heads, 2 * heads
    depth = diff_g.shape[0]
    lam_init = 0.8 - 0.6 * math.exp(-0.3 * layer)

    in_specs = [pl.BlockSpec((tq, HEAD), lambda b, h, i: (b * nq + i, h)),
                pl.BlockSpec((n, HEAD), lambda b, h, i: (b, k_off + h)),
                pl.BlockSpec((n, HEAD), lambda b, h, i: (b, v_off + h))]
    args = [p, p, p]
    if rope:
        in_specs += [pl.BlockSpec((tq, HEAD), lambda b, h, i: (i, 0))] * 3
        in_specs += [pl.BlockSpec((n, HEAD), lambda b, h, i: (0, 0))] * 3
        args += list(rope_tabs) * 2
    if cached:
        spec = pl.BlockSpec((None, None, None, n_ctx, HEAD), lambda b, h, i: (b, layer, h, 0, 0))
        in_specs += [spec, spec]
        args += [cache_k, cache_v]
    in_specs += [pl.BlockSpec((None, 4, DA_QK), lambda b, h, i: (layer, 0, 0)),
                 pl.BlockSpec((None, HEAD, 1), lambda b, h, i: (layer, 0, 0))]
    args += [diff_lambda, diff_g.reshape(depth, HEAD, 1)]
    return pl.pallas_call(
        functools.partial(_diff_attn_kernel, rope=rope, cached=cached, n_self=n, kb=kb, lam_init=lam_init),
        out_shape=jax.ShapeDtypeStruct((t, heads * HEAD), BF16),
        grid=(n_seq, heads, nq),
        in_specs=in_specs,
        out_specs=pl.BlockSpec((tq, HEAD), lambda b, h, i: (b * nq + i, h)),
        scratch_shapes=[pltpu.VMEM((nblk, kb, HEAD), BF16), pltpu.VMEM((nblk, HEAD, kb), BF16),
                        pltpu.VMEM((2, nblk, kb, tq), F32), pltpu.VMEM((2, HEAD, tq), F32)],
        compiler_params=_cparams("parallel", "parallel", "arbitrary"),
        name="diff_attention",
    )(*args)


def _scan_constants(c):
    levels = int(math.log2(c))
    t = np.arange(c)[:, None]
    s = np.arange(c)[None, :]
    mall = [(s <= t)]
    masks = []
    for j in range(levels):
        m = c >> (j + 1)
        base = (t // (2 * m)) * (2 * m)
        second = (t - base) >= m
        pref = (s >= base + m) & (s <= t)
        suff = (s > t) & (s < base + m)
        mall.append(np.where(second, pref, suff))
        sbase = (s // (2 * m)) * (2 * m)
        masks.append((sbase == base) & second & ((s - sbase) < m))
    masks.append(s == t)
    mall = np.concatenate(mall, axis=0).astype(np.float32)
    masks = np.stack(masks).astype(np.float32)
    flip = lambda a: a.reshape(-1, c, c)[:, ::-1, ::-1].reshape(a.shape)
    return (jnp.asarray(mall, BF16), jnp.asarray(flip(mall), BF16),
            jnp.asarray(masks, F32), jnp.asarray(flip(masks), F32))


def _hgrn_kernel(*refs, n, chunk, unroll, has_state, emit_state):
    it = iter(refs)
    q_ref, zf_ref, zb_ref, v_ref, hg_ref = (next(it) for _ in range(5))
    loglb_ref, log1m_ref, onem_ref, g_ref = (next(it) for _ in range(4))
    mall_refs = (next(it), next(it))
    mask_refs = (next(it), next(it))
    s0_ref = next(it) if has_state else None
    o_ref = next(it)
    st_ref = next(it) if emit_state else None
    obuf_refs = (next(it), next(it))
    s_ref = next(it)

    levels = int(math.log2(chunk))
    nchunks = n // chunk
    z_refs = (zf_ref, zb_ref)

    for d in range(2):
        s_ref[d] = s0_ref[d].T if has_state else jnp.zeros((HEAD, HEAD), F32)

    def gates(d, start):
        rows = pl.ds(start, chunk)
        z = z_refs[d][rows, :]
        q = _silu(q_ref[rows, :])
        v = v_ref[rows, :].astype(BF16)
        e = jnp.exp(-jnp.abs(z))
        log_sig = jnp.minimum(z, 0.0) - jnp.log1p(e)
        sig_neg = jnp.where(z >= 0.0, e, 1.0) / (1.0 + e)
        a = loglb_ref[d:d + 1, :]
        b = log1m_ref[d:d + 1, :] + log_sig
        g = jnp.maximum(a, b) + jnp.log1p(jnp.exp(-jnp.abs(a - b)))
        k = onem_ref[d:d + 1, :] * sig_neg
        g_hi = g.astype(BF16)
        g_lo = (g - g_hi.astype(F32)).astype(BF16)
        sums = _dot(mall_refs[d][...], jnp.concatenate([g_hi, g_lo], axis=1))
        return q, k, v, sums[:, :HEAD] + sums[:, HEAD:]

    def body(i, carry):
        chains = []
        for u in range(unroll):
            c = i * unroll + u
            chains.append((0, pl.multiple_of(c * chunk, chunk)))
            chains.append((1, pl.multiple_of((nchunks - 1 - c) * chunk, chunk)))
        work = [gates(d, start) for d, start in chains]
        atts = [mask_refs[d][levels] * _dot_t(q.astype(BF16), k.astype(BF16))
                for (d, _), (q, k, _, _) in zip(chains, work)]
        for j in range(levels):
            for ci, ((d, _), (q, k, _, sums)) in enumerate(zip(chains, work)):
                fac = jnp.exp(sums[(j + 1) * chunk:(j + 2) * chunk])
                atts[ci] = atts[ci] + mask_refs[d][j] * _dot_t((q * fac).astype(BF16), (k * fac).astype(BF16))
        intra, delta, q_in, decay = [], [], [], []
        for (d, _), (q, k, v, sums), att in zip(chains, work, atts):
            cum = sums[0:chunk]
            last = cum[chunk - 1:chunk] if d == 0 else cum[0:1]
            kt = (k * jnp.exp(last - cum)).astype(BF16)
            intra.append(_dot(att.astype(BF16), v))
            delta.append(lax.dot_general(v, kt, (((0,), (0,)), ((), ())), preferred_element_type=F32))
            q_in.append((q * jnp.exp(cum)).astype(BF16))
            decay.append(jnp.exp(last))
        st = [s_ref[0], s_ref[1]]
        for ci, (d, start) in enumerate(chains):
            obuf_refs[d][pl.ds(start, chunk), :] = intra[ci] + _dot_t(q_in[ci], st[d].astype(BF16))
            st[d] = decay[ci] * st[d] + delta[ci]
        s_ref[0] = st[0]
        s_ref[1] = st[1]
        return carry

    lax.fori_loop(0, nchunks // unroll, body, 0)

    o = obuf_refs[0][...] + obuf_refs[1][...]
    o_ref[...] = (_rms_norm(o, g_ref[...]) * _silu(hg_ref[...])).astype(BF16)
    if emit_state:
        for d in range(2):
            st_ref[d] = s_ref[d].T


def _hgrn(p, geom, layer, lb_params, hgrn_g, consts, state):
    n_seq, n, t = geom["n_seq"], geom["n"], geom["t"]
    heads = geom["hg_heads"]
    base = 3 * geom["da_heads"]
    has_state = state is not None
    emit_state = not has_state
    depth = hgrn_g.shape[0]
    chunk = min(HG_CHUNK, n)
    col = lambda j: pl.BlockSpec((n, HEAD), lambda b, h: (b, base + j * heads + h))
    lbspec = pl.BlockSpec((None, 2, HEAD), lambda b, h: (layer, 0, h))
    const_specs = [pl.BlockSpec(c.shape, lambda b, h, nd=c.ndim: (0,) * nd) for c in consts]
    in_specs = [col(0), col(1), col(2), col(3), col(4), lbspec, lbspec, lbspec,
                pl.BlockSpec((None, 1, HEAD), lambda b, h: (layer, 0, 0))] + const_specs
    args = [p] * 5 + list(lb_params) + [hgrn_g.reshape(depth, 1, HEAD)] + list(consts)
    if has_state:
        in_specs.append(pl.BlockSpec((None, None, 2, None, HEAD, HEAD), lambda b, h: (b, layer, 0, h, 0, 0)))
        args.append(state)
    out_shape = [jax.ShapeDtypeStruct((t, heads * HEAD), BF16)]
    out_specs = [pl.BlockSpec((n, HEAD), lambda b, h: (b, h))]
    if emit_state:
        out_shape.append(jax.ShapeDtypeStruct((n_seq, 2, heads, HEAD, HEAD), F32))
        out_specs.append(pl.BlockSpec((None, 2, None, HEAD, HEAD), lambda b, h: (b, 0, h, 0, 0)))
    res = pl.pallas_call(
        functools.partial(_hgrn_kernel, n=n, chunk=chunk, unroll=HG_UNROLL if (n // chunk) % HG_UNROLL == 0 else 1,
                          has_state=has_state, emit_state=emit_state),
        out_shape=tuple(out_shape),
        grid=(n_seq, heads),
        in_specs=in_specs,
        out_specs=tuple(out_specs),
        scratch_shapes=[pltpu.VMEM((n, HEAD), F32), pltpu.VMEM((n, HEAD), F32),
                        pltpu.VMEM((2, HEAD, HEAD), F32)],
        compiler_params=_cparams("parallel", "parallel"),
        name="hgrn2_scan",
    )(*args)
    return res if emit_state else (res[0], None)


def _swa_ctx_kernel(q_ref, k_ref, v_ref, sink_ref, o_ref, *, layer):
    kv = pl.program_id(1)
    k = k_ref[...].astype(BF16)
    v = v_ref[...].astype(BF16)
    for g in range(SW_GROUP):
        q = (q_ref[:, g * HEAD:(g + 1) * HEAD] * (HEAD ** -0.5)).astype(BF16)
        sink = sink_ref[layer, kv * SW_GROUP + g]
        s = _dot_t(q, k)
        m = jnp.maximum(jnp.max(s, axis=-1, keepdims=True), sink)
        p = jnp.exp(s - m)
        l = jnp.sum(p, axis=-1, keepdims=True) + jnp.exp(sink - m)
        o_ref[:, g * HEAD:(g + 1) * HEAD] = _dot((p * (1.0 / l)).astype(BF16), v).astype(BF16)


def _swa_ctx(p, geom, layer, sink):
    n_seq, n, t = geom["n_seq"], geom["n"], geom["t"]
    kvh = geom["sw_kv"]
    qw = SW_GROUP * HEAD
    q_blk = geom["cq_off"] // qw
    k_blk = geom["ck_off"] // HEAD
    v_blk = k_blk + kvh
    return pl.pallas_call(
        functools.partial(_swa_ctx_kernel, layer=layer),
        out_shape=jax.ShapeDtypeStruct((t, kvh * qw), BF16),
        grid=(n_seq, kvh),
        in_specs=[pl.BlockSpec((n, qw), lambda b, kv: (b, q_blk + kv)),
                  pl.BlockSpec((n, HEAD), lambda b, kv: (b, k_blk + kv)),
                  pl.BlockSpec((n, HEAD), lambda b, kv: (b, v_blk + kv)),
                  pl.BlockSpec(memory_space=pltpu.SMEM)],
        out_specs=pl.BlockSpec((n, qw), lambda b, kv: (b, kv)),
        compiler_params=_cparams("parallel", "parallel"),
        name="sink_attention",
    )(p, p, p, sink)


def _band_masks(n):
    r = np.arange(3 * SW_BLOCK)[:, None]
    i = (np.arange(SW_GROUP * SW_BLOCK) % SW_BLOCK)[None, :]
    window = np.abs(SW_BLOCK + i - r) <= SW_WINDOW
    not_before = r >= SW_BLOCK
    not_after = r < 2 * SW_BLOCK
    variants = [window, window & not_before, window & not_after, window & not_before & not_after]
    return jnp.asarray(np.stack(variants).astype(np.float32))


def _swa_lat_kernel(q_ref, k_ref, v_ref, qc_ref, qsa_ref, qsb_ref, kc_ref, ksa_ref, ksb_ref, ck_ref, cv_ref,
                    mask_ref, sink_ref, o_ref, kr_ref, vt_ref, kctx_ref, vctxt_ref, *, layer, n):
    kv = pl.program_id(1)
    qb = pl.program_id(2)
    nb = n // SW_BLOCK
    w = HEAD // 4
    band = 3 * SW_BLOCK

    @pl.when(qb == 0)
    def _():
        zeros = jnp.zeros((SW_BLOCK, HEAD), BF16)
        for j in (0, nb + 1):
            kr_ref[j] = zeros
            vt_ref[j] = zeros

        def fill(j, carry):
            rows = pl.ds(pl.multiple_of(j * SW_BLOCK, SW_BLOCK), SW_BLOCK)
            k = _rope(k_ref[rows, :], kc_ref[rows, :], ksa_ref[rows, :], ksb_ref[rows, :], w)
            kr_ref[j + 1] = k.astype(BF16)
            vt_ref[j + 1] = v_ref[rows, :].T.astype(BF16)
            return carry

        lax.fori_loop(0, nb, fill, 0)
        kctx_ref[...] = ck_ref[...].astype(BF16)
        vctxt_ref[...] = cv_ref[...].T.astype(BF16)

    qc, qsa, qsb = qc_ref[...], qsa_ref[...], qsb_ref[...]
    q_all = jnp.concatenate(
        [(_rope(q_ref[:, g * HEAD:(g + 1) * HEAD], qc, qsa, qsb, w) * (HEAD ** -0.5)).astype(BF16)
         for g in range(SW_GROUP)], axis=0)
    k_all = jnp.concatenate([kr_ref[qb], kr_ref[qb + 1], kr_ref[qb + 2], kctx_ref[...]], axis=0)
    s = _dot_t(k_all, q_all)
    variant = jnp.where(qb == 0, 1, 0) + jnp.where(qb == nb - 1, 2, 0)
    s_band = jnp.where(mask_ref[variant] > 0.5, s[:band], NEG_INF)
    s_ctx = s[band:]
    lane = lax.broadcasted_iota(jnp.int32, (1, SW_GROUP * SW_BLOCK), 1)
    sink = jnp.zeros((1, SW_GROUP * SW_BLOCK), F32)
    for g in range(SW_GROUP):
        sink = jnp.where(lane // SW_BLOCK == g, sink_ref[layer, kv * SW_GROUP + g], sink)

    def fold(x, op):
        return op(op(x.reshape(x.shape[0] // 8, 8, x.shape[1]), axis=0), axis=0, keepdims=True)

    m = jnp.maximum(jnp.maximum(fold(s_band, jnp.max), fold(s_ctx, jnp.max)), sink)
    p_band = jnp.exp(s_band - m)
    p_ctx = jnp.exp(s_ctx - m)
    inv = 1.0 / (fold(p_band, jnp.sum) + fold(p_ctx, jnp.sum) + jnp.exp(sink - m))
    p_all = jnp.concatenate([p_band * inv, p_ctx * inv], axis=0).astype(BF16)
    vt_all = jnp.concatenate([vt_ref[qb], vt_ref[qb + 1], vt_ref[qb + 2], vctxt_ref[...]], axis=1)
    ot = _dot(vt_all, p_all)
    for g in range(SW_GROUP):
        o_ref[:, g * HEAD:(g + 1) * HEAD] = ot[:, g * SW_BLOCK:(g + 1) * SW_BLOCK].T.astype(BF16)


def _swa_lat(p, geom, layer, sink, rope_tabs, cache_k, cache_v):
    n_seq, n, t = geom["n_seq"], geom["n"], geom["t"]
    kvh = geom["sw_kv"]
    qw = SW_GROUP * HEAD
    q_blk = geom["cq_off"] // qw
    k_blk = geom["ck_off"] // HEAD
    v_blk = k_blk + kvh
    nb = n // SW_BLOCK
    n_ctx = cache_k.shape[3]
    masks = _band_masks(n)
    cspec = pl.BlockSpec((None, None, None, n_ctx, HEAD), lambda b, kv, i: (b, layer, kv, 0, 0))
    return pl.pallas_call(
        functools.partial(_swa_lat_kernel, layer=layer, n=n),
        out_shape=jax.ShapeDtypeStruct((t, kvh * qw), BF16),
        grid=(n_seq, kvh, nb),
        in_specs=[pl.BlockSpec((SW_BLOCK, qw), lambda b, kv, i: (b * nb + i, q_blk + kv)),
                  pl.BlockSpec((n, HEAD), lambda b, kv, i: (b, k_blk + kv)),
                  pl.BlockSpec((n, HEAD), lambda b, kv, i: (b, v_blk + kv))]
                 + [pl.BlockSpec((SW_BLOCK, HEAD), lambda b, kv, i: (i, 0))] * 3
                 + [pl.BlockSpec((n, HEAD), lambda b, kv, i: (0, 0))] * 3
                 + [cspec, cspec, pl.BlockSpec(masks.shape, lambda b, kv, i: (0, 0, 0)),
                    pl.BlockSpec(memory_space=pltpu.SMEM)],
        out_specs=pl.BlockSpec((SW_BLOCK, qw), lambda b, kv, i: (b * nb + i, kv)),
        scratch_shapes=[pltpu.VMEM((nb + 2, SW_BLOCK, HEAD), BF16), pltpu.VMEM((nb + 2, HEAD, SW_BLOCK), BF16),
                        pltpu.VMEM((n_ctx, HEAD), BF16), pltpu.VMEM((HEAD, n_ctx), BF16)],
        compiler_params=_cparams("parallel", "parallel", "arbitrary"),
        name="banded_sink_attention",
    )(p, p, p, *rope_tabs, *rope_tabs, cache_k, cache_v, masks, sink)


def _rope_tables(n, half):
    h = half // 2
    pos = jnp.arange(n)
    inv = ROPE_BASE ** (-jnp.arange(h, dtype=F32) / h)
    zero = jnp.zeros((n, h), F32)
    c, sa, sb = [], [], []
    for axis_pos in (pos // GRID_W, pos % GRID_W):
        ang = axis_pos.astype(F32)[:, None] * inv[None, :]
        cos, sin = jnp.cos(ang), jnp.sin(ang)
        c += [cos, cos]
        sa += [-sin, zero]
        sb += [zero, sin]
    reps = HEAD // (2 * half)
    cat = lambda parts: jnp.tile(jnp.concatenate(parts, axis=1), (1, reps))
    return cat(c), cat(sa), cat(sb)


def _geometry(n_seq, n, d_model):
    da_heads = hg_heads = d_model // 512
    sw_heads = d_model // 256
    sw_kv = sw_heads // SW_GROUP
    cq_off = (3 * da_heads + 5 * hg_heads) * HEAD
    return dict(n_seq=n_seq, n=n, t=n_seq * n, da_heads=da_heads, hg_heads=hg_heads, sw_kv=sw_kv,
                cq_off=cq_off, ck_off=cq_off + sw_heads * HEAD)


def _token_tiles(n):
    return _pick_tile(n, TM_PROJ), _pick_tile(n, TM_OUT), _pick_tile(n, TM_FFN)


def _layer(x, geom, layer, mod, mod_row, wts, params, lb_params, scan_consts, tabs, caches, alpha):
    w_in, w_out, w_gu, w_down = wts
    tm_proj, tm_out, tm_ffn = _token_tiles(geom["t"] if caches is None else geom["n"])
    p = _in_proj(x, mod, mod_row, w_in, layer, tm_proj)
    if caches is None:
        oa = _diff_attn(p, geom, layer, params["diff_lambda"], params["diff_norm_g"], None, None, None)
        ob, st = _hgrn(p, geom, layer, lb_params, params["hgrn_norm_g"], scan_consts, None)
        oc = _swa_ctx(p, geom, layer, params["swa_sink"])
    else:
        ck_d, cv_d, ck_s, cv_s, state = caches
        oa = _diff_attn(p, geom, layer, params["diff_lambda"], params["diff_norm_g"], tabs[0], ck_d, cv_d)
        ob, st = _hgrn(p, geom, layer, lb_params, params["hgrn_norm_g"], scan_consts, state)
        oc = _swa_lat(p, geom, layer, params["swa_sink"], tabs[1], ck_s, cv_s)
    x1 = _out_proj(oa, ob, oc, w_out, x, mod, mod_row, params["ln1_g"], params["ln1_b"], layer, tm_out, alpha)
    y = _ffn(x1, w_gu, w_down, mod, mod_row, params["ln2_g"], params["ln2_b"], layer, tm_ffn, alpha)
    return y, p, st


def kernel(x_prompt, x_sample, cache_diff_k, cache_diff_v, cache_swa_k, cache_swa_v, state_hgrn, c, c_ctx, w_mod,
           b_mod, w_in, w_out, diff_lambda, diff_norm_g, hgrn_lb_logits, hgrn_norm_g, swa_sink, ln1_g, ln1_b, ln2_g,
           ln2_b, w_gate_up, w_down):
    batch, seq, d = x_prompt.shape
    dec_batch, dec_seq, _ = x_sample.shape
    depth = w_mod.shape[0]
    alpha = (2 * depth) ** 0.25
    geom_c = _geometry(batch, seq, d)
    geom_l = _geometry(dec_batch, dec_seq, d)
    assert 1 + dec_batch <= MOD_ROWS

    cond = jnp.zeros((MOD_ROWS, d), F32).at[0].set(c_ctx).at[1:1 + dec_batch].set(c)
    mod = _modulation(cond, w_mod, b_mod).reshape(depth * MOD_ROWS * 6, 1, d)
    lb_params = _lb_params(hgrn_lb_logits)
    params = dict(diff_lambda=diff_lambda, diff_norm_g=diff_norm_g, hgrn_norm_g=hgrn_norm_g, swa_sink=swa_sink,
                  ln1_g=ln1_g, ln1_b=ln1_b, ln2_g=ln2_g, ln2_b=ln2_b)
    wts = tuple(w.astype(BF16) for w in (w_in, w_out, w_gate_up, w_down))
    tabs = (_rope_tables(dec_seq, DA_QK // 2), _rope_tables(dec_seq, HEAD // 2))
    consts_c = _scan_constants(min(HG_CHUNK, seq))
    consts_l = _scan_constants(min(HG_CHUNK, dec_seq))

    y_p = x_prompt.reshape(batch * seq, d)
    y_s = x_sample.reshape(dec_batch * dec_seq, d)
    nk_d, nv_d, nk_s, nv_s, n_st = [], [], [], [], []
    heads, kvh = geom_c["da_heads"], geom_c["sw_kv"]
    for l in range(depth):
        row_c = lambda m, tm, j, l=l: (l * MOD_ROWS) * 6 + j
        row_l = lambda m, tm, j, l=l: (l * MOD_ROWS + 1 + (m * tm) // dec_seq) * 6 + j
        y_p, p_c, st = _layer(y_p, geom_c, l, mod, row_c, wts, params, lb_params, consts_c, None, None, alpha)
        caches = (cache_diff_k, cache_diff_v, cache_swa_k, cache_swa_v, state_hgrn)
        y_s, _, _ = _layer(y_s, geom_l, l, mod, row_l, wts, params, lb_params, consts_l, tabs, caches, alpha)

        def heads_first(off, nh):
            a = p_c[:, off:off + nh * HEAD].reshape(batch, seq, nh, HEAD)
            return a.transpose(0, 2, 1, 3)

        nk_d.append(heads_first(heads * HEAD, heads))
        nv_d.append(heads_first(2 * heads * HEAD, heads))
        nk_s.append(heads_first(geom_c["ck_off"], kvh))
        nv_s.append(heads_first(geom_c["ck_off"] + kvh * HEAD, kvh))
        n_st.append(st)
    return (y_p.reshape(batch, seq, d), y_s.reshape(dec_batch, dec_seq, d),
            jnp.stack(nk_d, axis=1), jnp.stack(nv_d, axis=1), jnp.stack(nk_s, axis=1), jnp.stack(nv_s, axis=1),
            jnp.stack(n_st, axis=1))
```

```python
import functools
import math

import numpy as np
import jax
import jax.numpy as jnp
from jax import lax
from jax.experimental import pallas as pl
from jax.experimental.pallas import tpu as pltpu

F32 = jnp.float32
BF16 = jnp.bfloat16

GRID_W = 64
ROPE_BASE = 10000.0
LN_EPS = 1e-5
RMS_EPS = 1e-6
NEG_INF = -1e30
LB_FLOOR = 1e-30
LOG2_E = math.log2(math.e)
HEAD = 128
DA_QK = 64
SW_GROUP = 4
SW_BLOCK = 128
SW_WINDOW = 128
HG_CHUNK = 64
HG_UNROLL = 4
TM_PROJ = 256
TM_OUT = 512
TM_FFN = 512
MOD_ROWS = 8
VMEM_LIMIT = 56 * 1024 * 1024


def _cparams(*sem):
    return pltpu.CompilerParams(dimension_semantics=sem, vmem_limit_bytes=VMEM_LIMIT)


def _dot(a, b):
    return jnp.dot(a, b, preferred_element_type=F32)


def _dot_t(a, b):
    return lax.dot_general(a, b, (((1,), (1,)), ((), ())), preferred_element_type=F32)


def _silu(x):
    return x / (1.0 + jnp.exp(-x))


def _layer_norm(y, g, b):
    mu = jnp.mean(y, axis=-1, keepdims=True)
    d = y - mu
    var = jnp.mean(d * d, axis=-1, keepdims=True)
    return d * lax.rsqrt(var + LN_EPS) * g + b


def _rms_norm(o, g):
    ms = jnp.mean(o * o, axis=-1, keepdims=True)
    return o * lax.rsqrt(ms + RMS_EPS) * g


def _rope(x, c, sa, sb, w):
    return x * c + pltpu.roll(x, HEAD - w, 1) * sa + pltpu.roll(x, w, 1) * sb


def _pick_tile(n, target):
    t = min(n, target)
    while n % t or t % 128:
        t -= 128
    return t


def _lb_kernel(logit_ref, loglb_ref, log1m_ref, onem_ref):
    depth = logit_ref.shape[0]
    x = [logit_ref[l] for l in range(depth)]
    m = functools.reduce(jnp.maximum, x)
    e = [jnp.exp(xi - m) for xi in x]
    tot = functools.reduce(lambda a, b: a + b, e)
    w = [ei / tot for ei in e]
    acc = jnp.zeros_like(w[0])
    for l in range(depth):
        acc = acc + w[l]
        lb = acc - w[0]
        loglb_ref[l] = jnp.log(jnp.maximum(lb, LB_FLOOR))
        log1m_ref[l] = jnp.log1p(-lb)
        onem_ref[l] = 1.0 - lb


def _lb_params(logits):
    shp = jax.ShapeDtypeStruct(logits.shape, F32)
    return pl.pallas_call(_lb_kernel, out_shape=(shp, shp, shp), name="hgrn_lb_params")(logits)


def _mod_kernel(c_ref, w_ref, b_ref, o_ref):
    s = _silu(c_ref[...])
    s_hi = s.astype(BF16)
    s_lo = (s - s_hi.astype(F32)).astype(BF16)
    w = w_ref[...]
    w_hi = w.astype(BF16)
    w_lo = (w - w_hi.astype(F32)).astype(BF16)
    o_ref[...] = _dot(s_hi, w_hi) + _dot(s_lo, w_hi) + _dot(s_hi, w_lo) + b_ref[...]


def _modulation(cond, w_mod, b_mod):
    depth, d, d6 = w_mod.shape
    tn = _pick_tile(d6, 1024)
    return pl.pallas_call(
        _mod_kernel,
        out_shape=jax.ShapeDtypeStruct((depth, MOD_ROWS, d6), F32),
        grid=(depth, d6 // tn),
        in_specs=[pl.BlockSpec((MOD_ROWS, d), lambda l, n: (0, 0)),
                  pl.BlockSpec((None, d, tn), lambda l, n: (l, 0, n)),
                  pl.BlockSpec((None, 1, tn), lambda l, n: (l, 0, n))],
        out_specs=pl.BlockSpec((None, MOD_ROWS, tn), lambda l, n: (l, 0, n)),
        compiler_params=_cparams("parallel", "parallel"),
        name="adaln_modulation",
    )(cond, w_mod, b_mod.reshape(depth, 1, d6))


def _proj_kernel(x_ref, sh_ref, sc_ref, w_ref, o_ref):
    h = (x_ref[...] * (1.0 + sc_ref[0]) + sh_ref[0]).astype(BF16)
    o_ref[...] = _dot(h, w_ref[...])


def _in_proj(x, mod, mod_row, w_in, layer, tm):
    t, d = x.shape
    d_in = w_in.shape[2]
    return pl.pallas_call(
        _proj_kernel,
        out_shape=jax.ShapeDtypeStruct((t, d_in), F32),
        grid=(t // tm,),
        in_specs=[pl.BlockSpec((tm, d), lambda m: (m, 0)),
                  pl.BlockSpec((1, 1, d), lambda m: (mod_row(m, tm, 0), 0, 0)),
                  pl.BlockSpec((1, 1, d), lambda m: (mod_row(m, tm, 1), 0, 0)),
                  pl.BlockSpec((None, d, d_in), lambda m: (layer, 0, 0), pipeline_mode=pl.Buffered(1))],
        out_specs=pl.BlockSpec((tm, d_in), lambda m: (m, 0)),
        compiler_params=_cparams("parallel"),
        name="in_proj",
    )(x, mod, mod, w_in)


def _out_proj_kernel(oa_ref, ob_ref, oc_ref, wa_ref, wb_ref, wc_ref, x_ref, g1_ref, lng_ref, lnb_ref, x1_ref, *,
                     alpha):
    mix = _dot(oa_ref[...], wa_ref[...]) + _dot(ob_ref[...], wb_ref[...]) + _dot(oc_ref[...], wc_ref[...])
    x1_ref[...] = _layer_norm(alpha * x_ref[...] + g1_ref[0] * mix, lng_ref[...], lnb_ref[...])


def _out_proj(oa, ob, oc, w_out, x, mod, mod_row, ln_g, ln_b, layer, tm, alpha):
    t, d = x.shape
    wa, wc = oa.shape[1], oc.shape[1]
    depth = ln_g.shape[0]
    vec = pl.BlockSpec((None, 1, d), lambda m: (layer, 0, 0))
    return pl.pallas_call(
        functools.partial(_out_proj_kernel, alpha=alpha),
        out_shape=jax.ShapeDtypeStruct((t, d), F32),
        grid=(t // tm,),
        in_specs=[pl.BlockSpec((tm, wa), lambda m: (m, 0)),
                  pl.BlockSpec((tm, wa), lambda m: (m, 0)),
                  pl.BlockSpec((tm, wc), lambda m: (m, 0)),
                  pl.BlockSpec((None, wa, d), lambda m: (layer, 0, 0)),
                  pl.BlockSpec((None, wa, d), lambda m: (layer, 1, 0)),
                  pl.BlockSpec((None, wc, d), lambda m: (layer, 1, 0)),
                  pl.BlockSpec((tm, d), lambda m: (m, 0)),
                  pl.BlockSpec((1, 1, d), lambda m: (mod_row(m, tm, 2), 0, 0)),
                  vec, vec],
        out_specs=pl.BlockSpec((tm, d), lambda m: (m, 0)),
        compiler_params=_cparams("parallel"),
        name="out_proj_ln",
    )(oa, ob, oc, w_out, w_out, w_out, x, mod, ln_g.reshape(depth, 1, d), ln_b.reshape(depth, 1, d))


def _ffn_kernel(x_ref, sh2_ref, sc2_ref, wg_ref, wu_ref, wd_ref, g2_ref, lng_ref, lnb_ref, o_ref, h_ref, *, alpha):
    f = pl.program_id(1)

    @pl.when(f == 0)
    def _():
        h_ref[...] = (x_ref[...] * (1.0 + sc2_ref[0]) + sh2_ref[0]).astype(BF16)
        o_ref[...] = jnp.zeros_like(o_ref)

    h = h_ref[...]
    a = _dot(h, wg_ref[...])
    u = _dot(h, wu_ref[...])
    o_ref[...] += _dot((_silu(a) * u).astype(BF16), wd_ref[...])

    @pl.when(f == pl.num_programs(1) - 1)
    def _():
        y = alpha * x_ref[...] + g2_ref[0] * o_ref[...]
        o_ref[...] = _layer_norm(y, lng_ref[...], lnb_ref[...])


def _ffn(x1, w_gu, w_down, mod, mod_row, ln_g, ln_b, layer, tm, alpha):
    t, d = x1.shape
    d_ff = w_down.shape[1]
    depth = ln_g.shape[0]
    tf = _pick_tile(d_ff, 512)
    nf = d_ff // tf
    vec = pl.BlockSpec((None, 1, d), lambda m, f: (layer, 0, 0))
    row = lambda j: pl.BlockSpec((1, 1, d), lambda m, f: (mod_row(m, tm, j), 0, 0))
    return pl.pallas_call(
        functools.partial(_ffn_kernel, alpha=alpha),
        out_shape=jax.ShapeDtypeStruct((t, d), F32),
        grid=(t // tm, nf),
        in_specs=[pl.BlockSpec((tm, d), lambda m, f: (m, 0)),
                  row(3), row(4),
                  pl.BlockSpec((None, d, tf), lambda m, f: (layer, 0, f)),
                  pl.BlockSpec((None, d, tf), lambda m, f: (layer, 0, nf + f)),
                  pl.BlockSpec((None, tf, d), lambda m, f: (layer, f, 0)),
                  row(5), vec, vec],
        out_specs=pl.BlockSpec((tm, d), lambda m, f: (m, 0)),
        scratch_shapes=[pltpu.VMEM((tm, d), BF16)],
        compiler_params=_cparams("parallel", "arbitrary"),
        name="ffn_ln",
    )(x1, mod, mod, w_gu, w_gu, w_down, mod, ln_g.reshape(depth, 1, d), ln_b.reshape(depth, 1, d))


def _diff_attn_kernel(*refs, rope, cached, n_self, kb, lam_init, pipelined, n_tiles):
    it = iter(refs)
    q_ref, k_ref, v_ref = next(it), next(it), next(it)
    if rope:
        qc_ref, qsa_ref, qsb_ref, kc_ref, ksa_ref, ksb_ref = (next(it) for _ in range(6))
    if cached:
        ck_ref, cv_ref = next(it), next(it)
    lam_ref, g_ref, o_ref, kr_ref, vt_ref, m_ref, acc_ref = (next(it) for _ in range(7))
    s_refs = tuple(it)
    nblk = kr_ref.shape[0]
    nself = n_self // kb
    tq = q_ref.shape[0]
    step = pl.program_id(2)
    last = pl.num_programs(2) - 1

    def prepare_keys():
        for j in range(nself):
            rows = slice(j * kb, (j + 1) * kb)
            k = k_ref[rows, :]
            if rope:
                k = _rope(k, kc_ref[rows, :], ksa_ref[rows, :], ksb_ref[rows, :], DA_QK // 4)
            kr_ref[j] = k.astype(BF16)
            vt_ref[j] = v_ref[rows, :].T.astype(BF16)
        if cached:
            for j in range(nblk - nself):
                rows = slice(j * kb, (j + 1) * kb)
                kr_ref[nself + j] = ck_ref[rows, :].astype(BF16)
                vt_ref[nself + j] = cv_ref[rows, :].T.astype(BF16)

    def queries():
        q = q_ref[...]
        if rope:
            q = _rope(q, qc_ref[...], qsa_ref[...], qsb_ref[...], DA_QK // 4)
        q = q * (DA_QK ** -0.5 * LOG2_E)
        lane = lax.broadcasted_iota(jnp.int32, q.shape, 1)
        return (jnp.where(lane < DA_QK, q, 0.0).astype(BF16), jnp.where(lane >= DA_QK, q, 0.0).astype(BF16))

    def fold(x, op):
        return op(x.reshape(kb // 8, 8, tq), axis=0)

    def scores_block(j, qz, buf, ms):
        kblk = kr_ref[j]
        out = []
        for mp in range(2):
            s = _dot_t(kblk, qz[mp])
            buf[mp, j] = s
            out.append(jnp.maximum(ms[mp], fold(s, jnp.max)))
        return tuple(out)

    def values_block(j, buf, ms, ls):
        out = []
        for mp in range(2):
            p = jnp.exp2(buf[mp, j] - ms[mp])
            acc_ref[mp] += _dot(vt_ref[j], p.astype(BF16))
            out.append(ls[mp] + fold(p, jnp.sum))
        return tuple(out)

    neg = jnp.full((8, tq), -jnp.inf, F32)
    zero = jnp.zeros((8, tq), F32)

    def save_max(ms):
        for mp in range(2):
            m_ref[mp] = jnp.max(ms[mp], axis=0, keepdims=True)

    def finish(ls):
        l1, l2 = (jnp.sum(l, axis=0, keepdims=True) for l in ls)
        lp = lam_ref[...]
        lam = (jnp.exp(jnp.sum(lp[0:1] * lp[1:2], axis=-1, keepdims=True))
               - jnp.exp(jnp.sum(lp[2:3] * lp[3:4], axis=-1, keepdims=True)) + lam_init)
        ot = acc_ref[0] * (1.0 / l1) - acc_ref[1] * (lam / l2)
        ms_o = jnp.mean(ot * ot, axis=0, keepdims=True)
        ot = ot * lax.rsqrt(ms_o + RMS_EPS) * g_ref[...] * (1.0 - lam_init)
        o_ref[...] = ot.T.astype(BF16)

    if not pipelined:
        pl.when(step == 0)(prepare_keys)
        qz = queries()
        ms = (neg, neg)
        for j in range(nblk):
            ms = scores_block(j, qz, s_refs[0], ms)
        ms = tuple(jnp.max(m, axis=0, keepdims=True) for m in ms)
        acc_ref[...] = jnp.zeros_like(acc_ref)
        ls = (zero, zero)
        for j in range(nblk):
            ls = values_block(j, s_refs[0], ms, ls)
        finish(ls)
        return

    @pl.when(step == 0)
    def _():
        prepare_keys()
        qz = queries()
        ms = (neg, neg)
        for j in range(nblk):
            ms = scores_block(j, qz, s_refs[0], ms)
        save_max(ms)

    def interior(parity):
        prev = (m_ref[0], m_ref[1])
        qz = queries()
        acc_ref[...] = jnp.zeros_like(acc_ref)
        ms, ls = (neg, neg), (zero, zero)
        for j in range(nblk):
            ms = scores_block(j, qz, s_refs[parity], ms)
            ls = values_block(j, s_refs[1 - parity], prev, ls)
        save_max(ms)
        finish(ls)

    inside = (step > 0) & (step < last)
    pl.when(inside & (step % 2 == 1))(functools.partial(interior, 1))
    pl.when(inside & (step % 2 == 0))(functools.partial(interior, 0))

    @pl.when(step == last)
    def _():
        prev = (m_ref[0], m_ref[1])
        acc_ref[...] = jnp.zeros_like(acc_ref)
        ls = (zero, zero)
        for j in range(nblk):
            ls = values_block(j, s_refs[(n_tiles - 1) % 2], prev, ls)
        finish(ls)


def _diff_attn(p, geom, layer, diff_lambda, diff_g, rope_tabs, cache_k, cache_v):
    n_seq, n, t = geom["n_seq"], geom["n"], geom["t"]
    heads = geom["da_heads"]
    rope = rope_tabs is not None
    cached = cache_k is not None
    tq = min(n, 256)
    kb = min(n, 256)
    nq = n // tq
    n_ctx = cache_k.shape[3] if cached else 0
    nblk = (n + n_ctx) // kb
    assert n % kb == 0 and n_ctx % kb == 0
    k_off, v_off = heads, 2 * heads
    depth = diff_g.shape[0]
    lam_init = 0.8 - 0.6 * math.exp(-0.3 * layer)

    pipelined = nq > 1
    q_tile = (lambda i: jnp.minimum(i, nq - 1)) if pipelined else (lambda i: i)
    o_tile = (lambda i: jnp.maximum(i - 1, 0)) if pipelined else (lambda i: i)
    in_specs = [pl.BlockSpec((tq, HEAD), lambda b, h, i: (b * nq + q_tile(i), h)),
                pl.BlockSpec((n, HEAD), lambda b, h, i: (b, k_off + h)),
                pl.BlockSpec((n, HEAD), lambda b, h, i: (b, v_off + h))]
    args = [p, p, p]
    if rope:
        in_specs += [pl.BlockSpec((tq, HEAD), lambda b, h, i: (q_tile(i), 0))] * 3
        in_specs += [pl.BlockSpec((n, HEAD), lambda b, h, i: (0, 0))] * 3
        args += list(rope_tabs) * 2
    if cached:
        spec = pl.BlockSpec((None, None, None, n_ctx, HEAD), lambda b, h, i: (b, layer, h, 0, 0))
        in_specs += [spec, spec]
        args += [cache_k, cache_v]
    in_specs += [pl.BlockSpec((None, 4, DA_QK), lambda b, h, i: (layer, 0, 0)),
                 pl.BlockSpec((None, HEAD, 1), lambda b, h, i: (layer, 0, 0))]
    args += [diff_lambda, diff_g.reshape(depth, HEAD, 1)]
    return pl.pallas_call(
        functools.partial(_diff_attn_kernel, rope=rope, cached=cached, n_self=n, kb=kb, lam_init=lam_init,
                          pipelined=pipelined, n_tiles=nq),
        out_shape=jax.ShapeDtypeStruct((t, heads * HEAD), BF16),
        grid=(n_seq, heads, nq + 1 if pipelined else nq),
        in_specs=in_specs,
        out_specs=pl.BlockSpec((tq, HEAD), lambda b, h, i: (b * nq + o_tile(i), h)),
        scratch_shapes=[pltpu.VMEM((nblk, kb, HEAD), BF16), pltpu.VMEM((nblk, HEAD, kb), BF16),
                        pltpu.VMEM((2, 1, tq), F32), pltpu.VMEM((2, HEAD, tq), F32)]
                       + [pltpu.VMEM((2, nblk, kb, tq), F32)] * (2 if pipelined else 1),
        compiler_params=_cparams("parallel", "parallel", "arbitrary"),
        name="diff_attention",
    )(*args)


def _scan_constants(c):
    levels = int(math.log2(c))
    t = np.arange(c)[:, None]
    s = np.arange(c)[None, :]
    mall = [(s <= t)]
    masks = []
    for j in range(levels):
        m = c >> (j + 1)
        base = (t // (2 * m)) * (2 * m)
        second = (t - base) >= m
        pref = (s >= base + m) & (s <= t)
        suff = (s > t) & (s < base + m)
        mall.append(np.where(second, pref, suff))
        sbase = (s // (2 * m)) * (2 * m)
        masks.append((sbase == base) & second & ((s - sbase) < m))
    masks.append(s == t)
    mall = np.concatenate(mall, axis=0).astype(np.float32)
    masks = np.stack(masks).astype(np.float32)
    flip = lambda a: a.reshape(-1, c, c)[:, ::-1, ::-1].reshape(a.shape)
    return (jnp.asarray(mall, BF16), jnp.asarray(flip(mall), BF16),
            jnp.asarray(masks, F32), jnp.asarray(flip(masks), F32))


def _hgrn_kernel(*refs, n, chunk, unroll, has_state, emit_state):
    it = iter(refs)
    q_ref, zf_ref, zb_ref, v_ref, hg_ref = (next(it) for _ in range(5))
    loglb_ref, log1m_ref, onem_ref, g_ref = (next(it) for _ in range(4))
    mall_refs = (next(it), next(it))
    mask_refs = (next(it), next(it))
    s0_ref = next(it) if has_state else None
    o_ref = next(it)
    st_ref = next(it) if emit_state else None
    obuf_refs = (next(it), next(it))
    s_ref = next(it)

    levels = int(math.log2(chunk))
    nchunks = n // chunk
    z_refs = (zf_ref, zb_ref)

    for d in range(2):
        s_ref[d] = s0_ref[d].T if has_state else jnp.zeros((HEAD, HEAD), F32)

    def gates(d, start):
        rows = pl.ds(start, chunk)
        z = z_refs[d][rows, :]
        q = _silu(q_ref[rows, :])
        v = v_ref[rows, :].astype(BF16)
        e = jnp.exp(-jnp.abs(z))
        log_sig = jnp.minimum(z, 0.0) - jnp.log1p(e)
        sig_neg = jnp.where(z >= 0.0, e, 1.0) / (1.0 + e)
        a = loglb_ref[d:d + 1, :]
        b = log1m_ref[d:d + 1, :] + log_sig
        g = jnp.maximum(a, b) + jnp.log1p(jnp.exp(-jnp.abs(a - b)))
        k = onem_ref[d:d + 1, :] * sig_neg
        g_hi = g.astype(BF16)
        g_lo = (g - g_hi.astype(F32)).astype(BF16)
        sums = _dot(mall_refs[d][...], jnp.concatenate([g_hi, g_lo], axis=1))
        return q, k, v, sums[:, :HEAD] + sums[:, HEAD:]

    def body(i, carry):
        chains = []
        for u in range(unroll):
            c = i * unroll + u
            chains.append((0, pl.multiple_of(c * chunk, chunk)))
            chains.append((1, pl.multiple_of((nchunks - 1 - c) * chunk, chunk)))
        work = [gates(d, start) for d, start in chains]
        atts = [mask_refs[d][levels] * _dot_t(q.astype(BF16), k.astype(BF16))
                for (d, _), (q, k, _, _) in zip(chains, work)]
        for j in range(levels):
            for ci, ((d, _), (q, k, _, sums)) in enumerate(zip(chains, work)):
                fac = jnp.exp(sums[(j + 1) * chunk:(j + 2) * chunk])
                atts[ci] = atts[ci] + mask_refs[d][j] * _dot_t((q * fac).astype(BF16), (k * fac).astype(BF16))
        intra, delta, q_in, decay = [], [], [], []
        for (d, _), (q, k, v, sums), att in zip(chains, work, atts):
            cum = sums[0:chunk]
            last = cum[chunk - 1:chunk] if d == 0 else cum[0:1]
            kt = (k * jnp.exp(last - cum)).astype(BF16)
            intra.append(_dot(att.astype(BF16), v))
            delta.append(lax.dot_general(v, kt, (((0,), (0,)), ((), ())), preferred_element_type=F32))
            q_in.append((q * jnp.exp(cum)).astype(BF16))
            decay.append(jnp.exp(last))
        st = [s_ref[0], s_ref[1]]
        for ci, (d, start) in enumerate(chains):
            obuf_refs[d][pl.ds(start, chunk), :] = intra[ci] + _dot_t(q_in[ci], st[d].astype(BF16))
            st[d] = decay[ci] * st[d] + delta[ci]
        s_ref[0] = st[0]
        s_ref[1] = st[1]
        return carry

    lax.fori_loop(0, nchunks // unroll, body, 0)

    o = obuf_refs[0][...] + obuf_refs[1][...]
    o_ref[...] = (_rms_norm(o, g_ref[...]) * _silu(hg_ref[...])).astype(BF16)
    if emit_state:
        for d in range(2):
            st_ref[d] = s_ref[d].T


def _hgrn(p, geom, layer, lb_params, hgrn_g, consts, state):
    n_seq, n, t = geom["n_seq"], geom["n"], geom["t"]
    heads = geom["hg_heads"]
    base = 3 * geom["da_heads"]
    has_state = state is not None
    emit_state = not has_state
    depth = hgrn_g.shape[0]
    chunk = min(HG_CHUNK, n)
    col = lambda j: pl.BlockSpec((n, HEAD), lambda b, h: (b, base + j * heads + h))
    lbspec = pl.BlockSpec((None, 2, HEAD), lambda b, h: (layer, 0, h))
    const_specs = [pl.BlockSpec(c.shape, lambda b, h, nd=c.ndim: (0,) * nd) for c in consts]
    in_specs = [col(0), col(1), col(2), col(3), col(4), lbspec, lbspec, lbspec,
                pl.BlockSpec((None, 1, HEAD), lambda b, h: (layer, 0, 0))] + const_specs
    args = [p] * 5 + list(lb_params) + [hgrn_g.reshape(depth, 1, HEAD)] + list(consts)
    if has_state:
        in_specs.append(pl.BlockSpec((None, None, 2, None, HEAD, HEAD), lambda b, h: (b, layer, 0, h, 0, 0)))
        args.append(state)
    out_shape = [jax.ShapeDtypeStruct((t, heads * HEAD), BF16)]
    out_specs = [pl.BlockSpec((n, HEAD), lambda b, h: (b, h))]
    if emit_state:
        out_shape.append(jax.ShapeDtypeStruct((n_seq, 2, heads, HEAD, HEAD), F32))
        out_specs.append(pl.BlockSpec((None, 2, None, HEAD, HEAD), lambda b, h: (b, 0, h, 0, 0)))
    res = pl.pallas_call(
        functools.partial(_hgrn_kernel, n=n, chunk=chunk, unroll=HG_UNROLL if (n // chunk) % HG_UNROLL == 0 else 1,
                          has_state=has_state, emit_state=emit_state),
        out_shape=tuple(out_shape),
        grid=(n_seq, heads),
        in_specs=in_specs,
        out_specs=tuple(out_specs),
        scratch_shapes=[pltpu.VMEM((n, HEAD), F32), pltpu.VMEM((n, HEAD), F32),
                        pltpu.VMEM((2, HEAD, HEAD), F32)],
        compiler_params=_cparams("parallel", "parallel"),
        name="hgrn2_scan",
    )(*args)
    return res if emit_state else (res[0], None)


def _swa_ctx_kernel(q_ref, k_ref, v_ref, sink_ref, o_ref, *, layer):
    kv = pl.program_id(1)
    k = k_ref[...].astype(BF16)
    v = v_ref[...].astype(BF16)
    for g in range(SW_GROUP):
        q = (q_ref[:, g * HEAD:(g + 1) * HEAD] * (HEAD ** -0.5)).astype(BF16)
        sink = sink_ref[layer, kv * SW_GROUP + g]
        s = _dot_t(q, k)
        m = jnp.maximum(jnp.max(s, axis=-1, keepdims=True), sink)
        p = jnp.exp(s - m)
        l = jnp.sum(p, axis=-1, keepdims=True) + jnp.exp(sink - m)
        o_ref[:, g * HEAD:(g + 1) * HEAD] = _dot((p * (1.0 / l)).astype(BF16), v).astype(BF16)


def _swa_ctx(p, geom, layer, sink):
    n_seq, n, t = geom["n_seq"], geom["n"], geom["t"]
    kvh = geom["sw_kv"]
    qw = SW_GROUP * HEAD
    q_blk = geom["cq_off"] // qw
    k_blk = geom["ck_off"] // HEAD
    v_blk = k_blk + kvh
    return pl.pallas_call(
        functools.partial(_swa_ctx_kernel, layer=layer),
        out_shape=jax.ShapeDtypeStruct((t, kvh * qw), BF16),
        grid=(n_seq, kvh),
        in_specs=[pl.BlockSpec((n, qw), lambda b, kv: (b, q_blk + kv)),
                  pl.BlockSpec((n, HEAD), lambda b, kv: (b, k_blk + kv)),
                  pl.BlockSpec((n, HEAD), lambda b, kv: (b, v_blk + kv)),
                  pl.BlockSpec(memory_space=pltpu.SMEM)],
        out_specs=pl.BlockSpec((n, qw), lambda b, kv: (b, kv)),
        compiler_params=_cparams("parallel", "parallel"),
        name="sink_attention",
    )(p, p, p, sink)


def _band_masks(n):
    r = np.arange(3 * SW_BLOCK)[:, None]
    i = (np.arange(SW_GROUP * SW_BLOCK) % SW_BLOCK)[None, :]
    window = np.abs(SW_BLOCK + i - r) <= SW_WINDOW
    not_before = r >= SW_BLOCK
    not_after = r < 2 * SW_BLOCK
    variants = [window, window & not_before, window & not_after, window & not_before & not_after]
    return jnp.asarray(np.stack(variants).astype(np.float32))


def _swa_lat_kernel(q_ref, k_ref, v_ref, qc_ref, qsa_ref, qsb_ref, kc_ref, ksa_ref, ksb_ref, ck_ref, cv_ref,
                    mask_ref, sink_ref, o_ref, kr_ref, vt_ref, kctx_ref, vctxt_ref, *, layer, n):
    kv = pl.program_id(1)
    qb = pl.program_id(2)
    nb = n // SW_BLOCK
    w = HEAD // 4
    band = 3 * SW_BLOCK

    @pl.when(qb == 0)
    def _():
        zeros = jnp.zeros((SW_BLOCK, HEAD), BF16)
        for j in (0, nb + 1):
            kr_ref[j] = zeros
            vt_ref[j] = zeros

        def fill(j, carry):
            rows = pl.ds(pl.multiple_of(j * SW_BLOCK, SW_BLOCK), SW_BLOCK)
            k = _rope(k_ref[rows, :], kc_ref[rows, :], ksa_ref[rows, :], ksb_ref[rows, :], w)
            kr_ref[j + 1] = k.astype(BF16)
            vt_ref[j + 1] = v_ref[rows, :].T.astype(BF16)
            return carry

        lax.fori_loop(0, nb, fill, 0)
        kctx_ref[...] = ck_ref[...].astype(BF16)
        vctxt_ref[...] = cv_ref[...].T.astype(BF16)

    qc, qsa, qsb = qc_ref[...], qsa_ref[...], qsb_ref[...]
    q_all = jnp.concatenate(
        [(_rope(q_ref[:, g * HEAD:(g + 1) * HEAD], qc, qsa, qsb, w) * (HEAD ** -0.5)).astype(BF16)
         for g in range(SW_GROUP)], axis=0)
    k_all = jnp.concatenate([kr_ref[qb], kr_ref[qb + 1], kr_ref[qb + 2], kctx_ref[...]], axis=0)
    s = _dot_t(k_all, q_all)
    variant = jnp.where(qb == 0, 1, 0) + jnp.where(qb == nb - 1, 2, 0)
    s_band = jnp.where(mask_ref[variant] > 0.5, s[:band], NEG_INF)
    s_ctx = s[band:]
    lane = lax.broadcasted_iota(jnp.int32, (1, SW_GROUP * SW_BLOCK), 1)
    sink = jnp.zeros((1, SW_GROUP * SW_BLOCK), F32)
    for g in range(SW_GROUP):
        sink = jnp.where(lane // SW_BLOCK == g, sink_ref[layer, kv * SW_GROUP + g], sink)

    def fold(x, op):
        return op(op(x.reshape(x.shape[0] // 8, 8, x.shape[1]), axis=0), axis=0, keepdims=True)

    m = jnp.maximum(jnp.maximum(fold(s_band, jnp.max), fold(s_ctx, jnp.max)), sink)
    p_band = jnp.exp(s_band - m)
    p_ctx = jnp.exp(s_ctx - m)
    inv = 1.0 / (fold(p_band, jnp.sum) + fold(p_ctx, jnp.sum) + jnp.exp(sink - m))
    p_all = jnp.concatenate([p_band * inv, p_ctx * inv], axis=0).astype(BF16)
    vt_all = jnp.concatenate([vt_ref[qb], vt_ref[qb + 1], vt_ref[qb + 2], vctxt_ref[...]], axis=1)
    ot = _dot(vt_all, p_all)
    for g in range(SW_GROUP):
        o_ref[:, g * HEAD:(g + 1) * HEAD] = ot[:, g * SW_BLOCK:(g + 1) * SW_BLOCK].T.astype(BF16)


def _swa_lat(p, geom, layer, sink, rope_tabs, cache_k, cache_v):
    n_seq, n, t = geom["n_seq"], geom["n"], geom["t"]
    kvh = geom["sw_kv"]
    qw = SW_GROUP * HEAD
    q_blk = geom["cq_off"] // qw
    k_blk = geom["ck_off"] // HEAD
    v_blk = k_blk + kvh
    nb = n // SW_BLOCK
    n_ctx = cache_k.shape[3]
    masks = _band_masks(n)
    cspec = pl.BlockSpec((None, None, None, n_ctx, HEAD), lambda b, kv, i: (b, layer, kv, 0, 0))
    return pl.pallas_call(
        functools.partial(_swa_lat_kernel, layer=layer, n=n),
        out_shape=jax.ShapeDtypeStruct((t, kvh * qw), BF16),
        grid=(n_seq, kvh, nb),
        in_specs=[pl.BlockSpec((SW_BLOCK, qw), lambda b, kv, i: (b * nb + i, q_blk + kv)),
                  pl.BlockSpec((n, HEAD), lambda b, kv, i: (b, k_blk + kv)),
                  pl.BlockSpec((n, HEAD), lambda b, kv, i: (b, v_blk + kv))]
                 + [pl.BlockSpec((SW_BLOCK, HEAD), lambda b, kv, i: (i, 0))] * 3
                 + [pl.BlockSpec((n, HEAD), lambda b, kv, i: (0, 0))] * 3
                 + [cspec, cspec, pl.BlockSpec(masks.shape, lambda b, kv, i: (0, 0, 0)),
                    pl.BlockSpec(memory_space=pltpu.SMEM)],
        out_specs=pl.BlockSpec((SW_BLOCK, qw), lambda b, kv, i: (b * nb + i, kv)),
        scratch_shapes=[pltpu.VMEM((nb + 2, SW_BLOCK, HEAD), BF16), pltpu.VMEM((nb + 2, HEAD, SW_BLOCK), BF16),
                        pltpu.VMEM((n_ctx, HEAD), BF16), pltpu.VMEM((HEAD, n_ctx), BF16)],
        compiler_params=_cparams("parallel", "parallel", "arbitrary"),
        name="banded_sink_attention",
    )(p, p, p, *rope_tabs, *rope_tabs, cache_k, cache_v, masks, sink)


def _rope_tables(n, half):
    h = half // 2
    pos = jnp.arange(n)
    inv = ROPE_BASE ** (-jnp.arange(h, dtype=F32) / h)
    zero = jnp.zeros((n, h), F32)
    c, sa, sb = [], [], []
    for axis_pos in (pos // GRID_W, pos % GRID_W):
        ang = axis_pos.astype(F32)[:, None] * inv[None, :]
        cos, sin = jnp.cos(ang), jnp.sin(ang)
        c += [cos, cos]
        sa += [-sin, zero]
        sb += [zero, sin]
    reps = HEAD // (2 * half)
    cat = lambda parts: jnp.tile(jnp.concatenate(parts, axis=1), (1, reps))
    return cat(c), cat(sa), cat(sb)


def _geometry(n_seq, n, d_model):
    da_heads = hg_heads = d_model // 512
    sw_heads = d_model // 256
    sw_kv = sw_heads // SW_GROUP
    cq_off = (3 * da_heads + 5 * hg_heads) * HEAD
    return dict(n_seq=n_seq, n=n, t=n_seq * n, da_heads=da_heads, hg_heads=hg_heads, sw_kv=sw_kv,
                cq_off=cq_off, ck_off=cq_off + sw_heads * HEAD)


def _token_tiles(n):
    return _pick_tile(n, TM_PROJ), _pick_tile(n, TM_OUT), _pick_tile(n, TM_FFN)


def _layer(x, geom, layer, mod, mod_row, wts, params, lb_params, scan_consts, tabs, caches, alpha):
    w_in, w_out, w_gu, w_down = wts
    tm_proj, tm_out, tm_ffn = _token_tiles(geom["t"] if caches is None else geom["n"])
    p = _in_proj(x, mod, mod_row, w_in, layer, tm_proj)
    if caches is None:
        oa = _diff_attn(p, geom, layer, params["diff_lambda"], params["diff_norm_g"], None, None, None)
        ob, st = _hgrn(p, geom, layer, lb_params, params["hgrn_norm_g"], scan_consts, None)
        oc = _swa_ctx(p, geom, layer, params["swa_sink"])
    else:
        ck_d, cv_d, ck_s, cv_s, state = caches
        oa = _diff_attn(p, geom, layer, params["diff_lambda"], params["diff_norm_g"], tabs[0], ck_d, cv_d)
        ob, st = _hgrn(p, geom, layer, lb_params, params["hgrn_norm_g"], scan_consts, state)
        oc = _swa_lat(p, geom, layer, params["swa_sink"], tabs[1], ck_s, cv_s)
    x1 = _out_proj(oa, ob, oc, w_out, x, mod, mod_row, params["ln1_g"], params["ln1_b"], layer, tm_out, alpha)
    y = _ffn(x1, w_gu, w_down, mod, mod_row, params["ln2_g"], params["ln2_b"], layer, tm_ffn, alpha)
    return y, p, st


def kernel(x_prompt, x_sample, cache_diff_k, cache_diff_v, cache_swa_k, cache_swa_v, state_hgrn, c, c_ctx, w_mod,
           b_mod, w_in, w_out, diff_lambda, diff_norm_g, hgrn_lb_logits, hgrn_norm_g, swa_sink, ln1_g, ln1_b, ln2_g,
           ln2_b, w_gate_up, w_down):
    batch, seq, d = x_prompt.shape
    dec_batch, dec_seq, _ = x_sample.shape
    depth = w_mod.shape[0]
    alpha = (2 * depth) ** 0.25
    geom_c = _geometry(batch, seq, d)
    geom_l = _geometry(dec_batch, dec_seq, d)
    assert 1 + dec_batch <= MOD_ROWS

    cond = jnp.zeros((MOD_ROWS, d), F32).at[0].set(c_ctx).at[1:1 + dec_batch].set(c)
    mod = _modulation(cond, w_mod, b_mod).reshape(depth * MOD_ROWS * 6, 1, d)
    lb_params = _lb_params(hgrn_lb_logits)
    params = dict(diff_lambda=diff_lambda, diff_norm_g=diff_norm_g, hgrn_norm_g=hgrn_norm_g, swa_sink=swa_sink,
                  ln1_g=ln1_g, ln1_b=ln1_b, ln2_g=ln2_g, ln2_b=ln2_b)
    wts = tuple(w.astype(BF16) for w in (w_in, w_out, w_gate_up, w_down))
    tabs = (_rope_tables(dec_seq, DA_QK // 2), _rope_tables(dec_seq, HEAD // 2))
    consts_c = _scan_constants(min(HG_CHUNK, seq))
    consts_l = _scan_constants(min(HG_CHUNK, dec_seq))

    y_p = x_prompt.reshape(batch * seq, d)
    y_s = x_sample.reshape(dec_batch * dec_seq, d)
    nk_d, nv_d, nk_s, nv_s, n_st = [], [], [], [], []
    heads, kvh = geom_c["da_heads"], geom_c["sw_kv"]
    for l in range(depth):
        row_c = lambda m, tm, j, l=l: (l * MOD_ROWS) * 6 + j
        row_l = lambda m, tm, j, l=l: (l * MOD_ROWS + 1 + (m * tm) // dec_seq) * 6 + j
        y_p, p_c, st = _layer(y_p, geom_c, l, mod, row_c, wts, params, lb_params, consts_c, None, None, alpha)
        caches = (cache_diff_k, cache_diff_v, cache_swa_k, cache_swa_v, state_hgrn)
        y_s, _, _ = _layer(y_s, geom_l, l, mod, row_l, wts, params, lb_params, consts_l, tabs, caches, alpha)

        def heads_first(off, nh):
            a = p_c[:, off:off + nh * HEAD].reshape(batch, seq, nh, HEAD)
            return a.transpose(0, 2, 1, 3)

        nk_d.append(heads_first(heads * HEAD, heads))
        nv_d.append(heads_first(2 * heads * HEAD, heads))
        nk_s.append(heads_first(geom_c["ck_off"], kvh))
        nv_s.append(heads_first(geom_c["ck_off"] + kvh * HEAD, kvh))
        n_st.append(st)
    return (y_p.reshape(batch, seq, d), y_s.reshape(dec_batch, dec_seq, d),
            jnp.stack(nk_d, axis=1), jnp.stack(nv_d, axis=1), jnp.stack(nk_s, axis=1), jnp.stack(nv_s, axis=1),
            jnp.stack(n_st, axis=1))
```

```python
import functools
import math

import numpy as np
import jax
import jax.numpy as jnp
from jax import lax
from jax.experimental import pallas as pl
from jax.experimental.pallas import tpu as pltpu

F32 = jnp.float32
BF16 = jnp.bfloat16

GRID_W = 64
ROPE_BASE = 10000.0
LN_EPS = 1e-5
RMS_EPS = 1e-6
NEG_INF = -1e30
LB_FLOOR = 1e-30
LOG2_E = math.log2(math.e)
HEAD = 128
DA_QK = 64
SW_GROUP = 4
SW_BLOCK = 128
SW_WINDOW = 128
HG_CHUNK = 64
HG_UNROLL = 4
TM_PROJ = 256
TM_OUT = 512
OUT_SPLIT = 1
SW_PER_STEP = 4
TM_FFN = 512
MOD_ROWS = 8
VMEM_LIMIT = 56 * 1024 * 1024


def _cparams(*sem):
    return pltpu.CompilerParams(dimension_semantics=sem, vmem_limit_bytes=VMEM_LIMIT)


def _dot(a, b):
    return jnp.dot(a, b, preferred_element_type=F32)


def _dot_t(a, b):
    return lax.dot_general(a, b, (((1,), (1,)), ((), ())), preferred_element_type=F32)


def _silu(x):
    return x / (1.0 + jnp.exp(-x))


def _layer_norm(y, g, b):
    mu = jnp.mean(y, axis=-1, keepdims=True)
    d = y - mu
    var = jnp.mean(d * d, axis=-1, keepdims=True)
    return d * lax.rsqrt(var + LN_EPS) * g + b


def _rms_norm(o, g):
    ms = jnp.mean(o * o, axis=-1, keepdims=True)
    return o * lax.rsqrt(ms + RMS_EPS) * g


def _rope(x, c, sa, sb, w):
    return x * c + pltpu.roll(x, HEAD - w, 1) * sa + pltpu.roll(x, w, 1) * sb


def _pick_tile(n, target):
    t = min(n, target)
    while n % t or t % 128:
        t -= 128
    return t


def _lb_kernel(logit_ref, loglb_ref, log1m_ref, onem_ref):
    depth = logit_ref.shape[0]
    x = [logit_ref[l] for l in range(depth)]
    m = functools.reduce(jnp.maximum, x)
    e = [jnp.exp(xi - m) for xi in x]
    tot = functools.reduce(lambda a, b: a + b, e)
    w = [ei / tot for ei in e]
    acc = jnp.zeros_like(w[0])
    for l in range(depth):
        acc = acc + w[l]
        lb = acc - w[0]
        loglb_ref[l] = jnp.log(jnp.maximum(lb, LB_FLOOR))
        log1m_ref[l] = jnp.log1p(-lb)
        onem_ref[l] = 1.0 - lb


def _lb_params(logits):
    shp = jax.ShapeDtypeStruct(logits.shape, F32)
    return pl.pallas_call(_lb_kernel, out_shape=(shp, shp, shp), name="hgrn_lb_params")(logits)


def _mod_kernel(c_ref, w_ref, b_ref, o_ref):
    s = _silu(c_ref[...])
    s_hi = s.astype(BF16)
    s_lo = (s - s_hi.astype(F32)).astype(BF16)
    w = w_ref[...]
    w_hi = w.astype(BF16)
    w_lo = (w - w_hi.astype(F32)).astype(BF16)
    o_ref[...] = _dot(s_hi, w_hi) + _dot(s_lo, w_hi) + _dot(s_hi, w_lo) + b_ref[...]


def _modulation(cond, w_mod, b_mod):
    depth, d, d6 = w_mod.shape
    tn = _pick_tile(d6, 1024)
    return pl.pallas_call(
        _mod_kernel,
        out_shape=jax.ShapeDtypeStruct((depth, MOD_ROWS, d6), F32),
        grid=(depth, d6 // tn),
        in_specs=[pl.BlockSpec((MOD_ROWS, d), lambda l, n: (0, 0)),
                  pl.BlockSpec((None, d, tn), lambda l, n: (l, 0, n)),
                  pl.BlockSpec((None, 1, tn), lambda l, n: (l, 0, n))],
        out_specs=pl.BlockSpec((None, MOD_ROWS, tn), lambda l, n: (l, 0, n)),
        compiler_params=_cparams("parallel", "parallel"),
        name="adaln_modulation",
    )(cond, w_mod, b_mod.reshape(depth, 1, d6))


def _proj_kernel(x_ref, sh_ref, sc_ref, w_ref, *rest, cache_cols):
    o_ref = rest[len(cache_cols)]
    h = (x_ref[...] * (1.0 + sc_ref[0]) + sh_ref[0]).astype(BF16)
    o_ref[...] = _dot(h, w_ref[...])
    for c_ref, (off, nh) in zip(rest[len(cache_cols) + 1:], cache_cols):
        for hd in range(nh):
            c_ref[hd] = o_ref[:, off + hd * HEAD:off + (hd + 1) * HEAD]


def _in_proj(x, mod, mod_row, w_in, layer, tm, caches=None, cache_cols=()):
    t, d = x.shape
    d_in = w_in.shape[2]
    caches = tuple(caches or ())
    out_shape = [jax.ShapeDtypeStruct((t, d_in), F32)] + [jax.ShapeDtypeStruct(c.shape, c.dtype) for c in caches]
    out_specs = [pl.BlockSpec((tm, d_in), lambda m: (m, 0))]
    for c in caches:
        assert c.shape[3] == tm
        out_specs.append(pl.BlockSpec((None, None, c.shape[2], tm, HEAD), lambda m: (m, layer, 0, 0, 0)))
    res = pl.pallas_call(
        functools.partial(_proj_kernel, cache_cols=tuple(cache_cols)),
        out_shape=tuple(out_shape),
        grid=(t // tm,),
        in_specs=[pl.BlockSpec((tm, d), lambda m: (m, 0)),
                  pl.BlockSpec((1, 1, d), lambda m: (mod_row(m, tm, 0), 0, 0)),
                  pl.BlockSpec((1, 1, d), lambda m: (mod_row(m, tm, 1), 0, 0)),
                  pl.BlockSpec((None, d, d_in), lambda m: (layer, 0, 0), pipeline_mode=pl.Buffered(1))]
                 + [pl.BlockSpec(memory_space=pl.ANY)] * len(caches),
        out_specs=tuple(out_specs),
        input_output_aliases={4 + i: 1 + i for i in range(len(caches))},
        compiler_params=_cparams("parallel"),
        name="in_proj",
    )(x, mod, mod, w_in, *caches)
    return res[0], tuple(res[1:])


def _out_proj_kernel(mix_ref, w_ref, x_ref, g1_ref, lng_ref, lnb_ref, x1_ref, *, alpha):
    sub = x_ref.shape[0] // OUT_SPLIT
    for r in range(OUT_SPLIT):
        rows = slice(r * sub, (r + 1) * sub)
        mix = _dot(mix_ref[rows, :], w_ref[...])
        x1_ref[rows, :] = _layer_norm(alpha * x_ref[rows, :] + g1_ref[0] * mix, lng_ref[...], lnb_ref[...])


def _out_proj(mix, w_out, x, mod, mod_row, ln_g, ln_b, layer, tm, alpha):
    t, d = x.shape
    d_mix = mix.shape[1]
    depth = ln_g.shape[0]
    vec = pl.BlockSpec((None, 1, d), lambda m: (layer, 0, 0))
    return pl.pallas_call(
        functools.partial(_out_proj_kernel, alpha=alpha),
        out_shape=jax.ShapeDtypeStruct((t, d), F32),
        grid=(t // tm,),
        in_specs=[pl.BlockSpec((tm, d_mix), lambda m: (m, 0)),
                  pl.BlockSpec((None, d_mix, d), lambda m: (layer, 0, 0), pipeline_mode=pl.Buffered(1)),
                  pl.BlockSpec((tm, d), lambda m: (m, 0)),
                  pl.BlockSpec((1, 1, d), lambda m: (mod_row(m, tm, 2), 0, 0)),
                  vec, vec],
        out_specs=pl.BlockSpec((tm, d), lambda m: (m, 0)),
        compiler_params=_cparams("parallel"),
        name="out_proj_ln",
    )(mix, w_out, x, mod, ln_g.reshape(depth, 1, d), ln_b.reshape(depth, 1, d))


def _ffn_kernel(x_ref, sh2_ref, sc2_ref, wg_ref, wu_ref, wd_ref, g2_ref, lng_ref, lnb_ref, o_ref, h_ref, *, alpha):
    f = pl.program_id(1)

    @pl.when(f == 0)
    def _():
        h_ref[...] = (x_ref[...] * (1.0 + sc2_ref[0]) + sh2_ref[0]).astype(BF16)
        o_ref[...] = jnp.zeros_like(o_ref)

    h = h_ref[...]
    a = _dot(h, wg_ref[...])
    u = _dot(h, wu_ref[...])
    o_ref[...] += _dot((_silu(a) * u).astype(BF16), wd_ref[...])

    @pl.when(f == pl.num_programs(1) - 1)
    def _():
        y = alpha * x_ref[...] + g2_ref[0] * o_ref[...]
        o_ref[...] = _layer_norm(y, lng_ref[...], lnb_ref[...])


def _ffn(x1, w_gu, w_down, mod, mod_row, ln_g, ln_b, layer, tm, alpha):
    t, d = x1.shape
    d_ff = w_down.shape[1]
    depth = ln_g.shape[0]
    tf = _pick_tile(d_ff, 512)
    nf = d_ff // tf
    vec = pl.BlockSpec((None, 1, d), lambda m, f: (layer, 0, 0))
    row = lambda j: pl.BlockSpec((1, 1, d), lambda m, f: (mod_row(m, tm, j), 0, 0))
    return pl.pallas_call(
        functools.partial(_ffn_kernel, alpha=alpha),
        out_shape=jax.ShapeDtypeStruct((t, d), F32),
        grid=(t // tm, nf),
        in_specs=[pl.BlockSpec((tm, d), lambda m, f: (m, 0)),
                  row(3), row(4),
                  pl.BlockSpec((None, d, tf), lambda m, f: (layer, 0, f)),
                  pl.BlockSpec((None, d, tf), lambda m, f: (layer, 0, nf + f)),
                  pl.BlockSpec((None, tf, d), lambda m, f: (layer, f, 0)),
                  row(5), vec, vec],
        out_specs=pl.BlockSpec((tm, d), lambda m, f: (m, 0)),
        scratch_shapes=[pltpu.VMEM((tm, d), BF16)],
        compiler_params=_cparams("parallel", "arbitrary"),
        name="ffn_ln",
    )(x1, mod, mod, w_gu, w_gu, w_down, mod, ln_g.reshape(depth, 1, d), ln_b.reshape(depth, 1, d))


def _diff_attn_kernel(*refs, rope, cached, n_self, kb, lam_init, pipelined, n_tiles):
    it = iter(refs)
    q_ref, k_ref, v_ref = next(it), next(it), next(it)
    if rope:
        qc_ref, qsa_ref, qsb_ref, kc_ref, ksa_ref, ksb_ref = (next(it) for _ in range(6))
    if cached:
        ck_ref, cv_ref = next(it), next(it)
    lam_ref, g_ref, o_ref, kr_ref, vt_ref, m_ref, acc_ref = (next(it) for _ in range(7))
    s_refs = tuple(it)
    nblk = kr_ref.shape[0]
    nself = n_self // kb
    tq = q_ref.shape[0]
    step = pl.program_id(2)
    last = pl.num_programs(2) - 1

    def prepare_keys():
        for j in range(nself):
            rows = slice(j * kb, (j + 1) * kb)
            k = k_ref[rows, :]
            if rope:
                k = _rope(k, kc_ref[rows, :], ksa_ref[rows, :], ksb_ref[rows, :], DA_QK // 4)
            kr_ref[j] = k.astype(BF16)
            vt_ref[j] = v_ref[rows, :].T.astype(BF16)
        if cached:
            for j in range(nblk - nself):
                rows = slice(j * kb, (j + 1) * kb)
                kr_ref[nself + j] = ck_ref[rows, :].astype(BF16)
                vt_ref[nself + j] = cv_ref[rows, :].T.astype(BF16)

    def queries():
        q = q_ref[...]
        if rope:
            q = _rope(q, qc_ref[...], qsa_ref[...], qsb_ref[...], DA_QK // 4)
        q = q * (DA_QK ** -0.5 * LOG2_E)
        lane = lax.broadcasted_iota(jnp.int32, q.shape, 1)
        return (jnp.where(lane < DA_QK, q, 0.0).astype(BF16), jnp.where(lane >= DA_QK, q, 0.0).astype(BF16))

    def fold(x, op):
        return op(x.reshape(kb // 8, 8, tq), axis=0)

    def scores_block(j, qz, buf, ms):
        kblk = kr_ref[j]
        out = []
        for mp in range(2):
            s = _dot_t(kblk, qz[mp])
            buf[mp, j] = s
            out.append(jnp.maximum(ms[mp], fold(s, jnp.max)))
        return tuple(out)

    def values_block(j, buf, ms, ls):
        out = []
        for mp in range(2):
            p = jnp.exp2(buf[mp, j] - ms[mp])
            acc_ref[mp] += _dot(vt_ref[j], p.astype(BF16))
            out.append(ls[mp] + fold(p, jnp.sum))
        return tuple(out)

    neg = jnp.full((8, tq), -jnp.inf, F32)
    zero = jnp.zeros((8, tq), F32)

    def save_max(ms):
        for mp in range(2):
            m_ref[mp] = jnp.max(ms[mp], axis=0, keepdims=True)

    def finish(ls):
        l1, l2 = (jnp.sum(l, axis=0, keepdims=True) for l in ls)
        lp = lam_ref[...]
        lam = (jnp.exp(jnp.sum(lp[0:1] * lp[1:2], axis=-1, keepdims=True))
               - jnp.exp(jnp.sum(lp[2:3] * lp[3:4], axis=-1, keepdims=True)) + lam_init)
        ot = acc_ref[0] * (1.0 / l1) - acc_ref[1] * (lam / l2)
        ms_o = jnp.mean(ot * ot, axis=0, keepdims=True)
        ot = ot * lax.rsqrt(ms_o + RMS_EPS) * g_ref[...] * (1.0 - lam_init)
        o_ref[...] = ot.T.astype(BF16)

    if not pipelined:
        pl.when(step == 0)(prepare_keys)
        qz = queries()
        ms = (neg, neg)
        for j in range(nblk):
            ms = scores_block(j, qz, s_refs[0], ms)
        ms = tuple(jnp.max(m, axis=0, keepdims=True) for m in ms)
        acc_ref[...] = jnp.zeros_like(acc_ref)
        ls = (zero, zero)
        for j in range(nblk):
            ls = values_block(j, s_refs[0], ms, ls)
        finish(ls)
        return

    @pl.when(step == 0)
    def _():
        prepare_keys()
        qz = queries()
        ms = (neg, neg)
        for j in range(nblk):
            ms = scores_block(j, qz, s_refs[0], ms)
        save_max(ms)

    def interior(parity):
        prev = (m_ref[0], m_ref[1])
        qz = queries()
        acc_ref[...] = jnp.zeros_like(acc_ref)
        ms, ls = (neg, neg), (zero, zero)
        for j in range(nblk):
            ms = scores_block(j, qz, s_refs[parity], ms)
            ls = values_block(j, s_refs[1 - parity], prev, ls)
        save_max(ms)
        finish(ls)

    inside = (step > 0) & (step < last)
    pl.when(inside & (step % 2 == 1))(functools.partial(interior, 1))
    pl.when(inside & (step % 2 == 0))(functools.partial(interior, 0))

    @pl.when(step == last)
    def _():
        prev = (m_ref[0], m_ref[1])
        acc_ref[...] = jnp.zeros_like(acc_ref)
        ls = (zero, zero)
        for j in range(nblk):
            ls = values_block(j, s_refs[(n_tiles - 1) % 2], prev, ls)
        finish(ls)


def _diff_attn(p, geom, layer, diff_lambda, diff_g, rope_tabs, cache_k, cache_v):
    n_seq, n, t = geom["n_seq"], geom["n"], geom["t"]
    heads = geom["da_heads"]
    rope = rope_tabs is not None
    cached = cache_k is not None
    tq = min(n, 256)
    kb = min(n, 256)
    nq = n // tq
    n_ctx = cache_k.shape[3] if cached else 0
    nblk = (n + n_ctx) // kb
    assert n % kb == 0 and n_ctx % kb == 0
    k_off, v_off = heads, 2 * heads
    depth = diff_g.shape[0]
    lam_init = 0.8 - 0.6 * math.exp(-0.3 * layer)

    pipelined = nq > 1
    q_tile = (lambda i: jnp.minimum(i, nq - 1)) if pipelined else (lambda i: i)
    o_tile = (lambda i: jnp.maximum(i - 1, 0)) if pipelined else (lambda i: i)
    in_specs = [pl.BlockSpec((tq, HEAD), lambda b, h, i: (b * nq + q_tile(i), h)),
                pl.BlockSpec((n, HEAD), lambda b, h, i: (b, k_off + h)),
                pl.BlockSpec((n, HEAD), lambda b, h, i: (b, v_off + h))]
    args = [p, p, p]
    if rope:
        in_specs += [pl.BlockSpec((tq, HEAD), lambda b, h, i: (q_tile(i), 0))] * 3
        in_specs += [pl.BlockSpec((n, HEAD), lambda b, h, i: (0, 0))] * 3
        args += list(rope_tabs) * 2
    if cached:
        spec = pl.BlockSpec((None, None, None, n_ctx, HEAD), lambda b, h, i: (b, layer, h, 0, 0))
        in_specs += [spec, spec]
        args += [cache_k, cache_v]
    in_specs += [pl.BlockSpec((None, 4, DA_QK), lambda b, h, i: (layer, 0, 0)),
                 pl.BlockSpec((None, HEAD, 1), lambda b, h, i: (layer, 0, 0))]
    args += [diff_lambda, diff_g.reshape(depth, HEAD, 1)]
    return pl.pallas_call(
        functools.partial(_diff_attn_kernel, rope=rope, cached=cached, n_self=n, kb=kb, lam_init=lam_init,
                          pipelined=pipelined, n_tiles=nq),
        out_shape=jax.ShapeDtypeStruct((t, geom["d_mix"]), BF16),
        grid=(n_seq, heads, nq + 1 if pipelined else nq),
        in_specs=in_specs,
        out_specs=pl.BlockSpec((tq, HEAD), lambda b, h, i: (b * nq + o_tile(i), h)),
        scratch_shapes=[pltpu.VMEM((nblk, kb, HEAD), BF16), pltpu.VMEM((nblk, HEAD, kb), BF16),
                        pltpu.VMEM((2, 1, tq), F32), pltpu.VMEM((2, HEAD, tq), F32)]
                       + [pltpu.VMEM((2, nblk, kb, tq), F32)] * (2 if pipelined else 1),
        compiler_params=_cparams("parallel", "parallel", "arbitrary"),
        name="diff_attention",
    )(*args)


def _scan_constants(c):
    levels = int(math.log2(c))
    t = np.arange(c)[:, None]
    s = np.arange(c)[None, :]
    mall = [(s <= t)]
    masks = []
    for j in range(levels):
        m = c >> (j + 1)
        base = (t // (2 * m)) * (2 * m)
        second = (t - base) >= m
        pref = (s >= base + m) & (s <= t)
        suff = (s > t) & (s < base + m)
        mall.append(np.where(second, pref, suff))
        sbase = (s // (2 * m)) * (2 * m)
        masks.append((sbase == base) & second & ((s - sbase) < m))
    masks.append(s == t)
    mall = np.concatenate(mall, axis=0).astype(np.float32)
    masks = np.stack(masks).astype(np.float32)
    flip = lambda a: a.reshape(-1, c, c)[:, ::-1, ::-1].reshape(a.shape)
    return (jnp.asarray(mall, BF16), jnp.asarray(flip(mall), BF16),
            jnp.asarray(masks, F32), jnp.asarray(flip(masks), F32))


def _hgrn_kernel(*refs, n, chunk, unroll, has_state, emit_state):
    it = iter(refs)
    q_ref, zf_ref, zb_ref, v_ref, hg_ref = (next(it) for _ in range(5))
    loglb_ref, log1m_ref, onem_ref, g_ref = (next(it) for _ in range(4))
    mall_refs = (next(it), next(it))
    mask_refs = (next(it), next(it))
    s0_ref = next(it) if has_state else None
    next(it)
    if emit_state:
        next(it)
    o_ref = next(it)
    st_ref = next(it) if emit_state else None
    obuf_refs = (next(it), next(it))
    s_ref = next(it)

    levels = int(math.log2(chunk))
    nchunks = n // chunk
    z_refs = (zf_ref, zb_ref)

    for d in range(2):
        s_ref[d] = s0_ref[d].T if has_state else jnp.zeros((HEAD, HEAD), F32)

    def gates(d, start):
        rows = pl.ds(start, chunk)
        z = z_refs[d][rows, :]
        q = _silu(q_ref[rows, :])
        v = v_ref[rows, :].astype(BF16)
        e = jnp.exp(-jnp.abs(z))
        log_sig = jnp.minimum(z, 0.0) - jnp.log1p(e)
        sig_neg = jnp.where(z >= 0.0, e, 1.0) / (1.0 + e)
        a = loglb_ref[d:d + 1, :]
        b = log1m_ref[d:d + 1, :] + log_sig
        g = jnp.maximum(a, b) + jnp.log1p(jnp.exp(-jnp.abs(a - b)))
        k = onem_ref[d:d + 1, :] * sig_neg
        g_hi = g.astype(BF16)
        g_lo = (g - g_hi.astype(F32)).astype(BF16)
        sums = _dot(mall_refs[d][...], jnp.concatenate([g_hi, g_lo], axis=1))
        return q, k, v, sums[:, :HEAD] + sums[:, HEAD:]

    def body(i, carry):
        chains = []
        for u in range(unroll):
            c = i * unroll + u
            chains.append((0, pl.multiple_of(c * chunk, chunk)))
            chains.append((1, pl.multiple_of((nchunks - 1 - c) * chunk, chunk)))
        work = [gates(d, start) for d, start in chains]
        atts = [mask_refs[d][levels] * _dot_t(q.astype(BF16), k.astype(BF16))
                for (d, _), (q, k, _, _) in zip(chains, work)]
        for j in range(levels):
            for ci, ((d, _), (q, k, _, sums)) in enumerate(zip(chains, work)):
                fac = jnp.exp(sums[(j + 1) * chunk:(j + 2) * chunk])
                atts[ci] = atts[ci] + mask_refs[d][j] * _dot_t((q * fac).astype(BF16), (k * fac).astype(BF16))
        intra, delta, q_in, decay = [], [], [], []
        for (d, _), (q, k, v, sums), att in zip(chains, work, atts):
            cum = sums[0:chunk]
            last = cum[chunk - 1:chunk] if d == 0 else cum[0:1]
            kt = (k * jnp.exp(last - cum)).astype(BF16)
            intra.append(_dot(att.astype(BF16), v))
            delta.append(lax.dot_general(v, kt, (((0,), (0,)), ((), ())), preferred_element_type=F32))
            q_in.append((q * jnp.exp(cum)).astype(BF16))
            decay.append(jnp.exp(last))
        st = [s_ref[0], s_ref[1]]
        for ci, (d, start) in enumerate(chains):
            obuf_refs[d][pl.ds(start, chunk), :] = intra[ci] + _dot_t(q_in[ci], st[d].astype(BF16))
            st[d] = decay[ci] * st[d] + delta[ci]
        s_ref[0] = st[0]
        s_ref[1] = st[1]
        return carry

    lax.fori_loop(0, nchunks // unroll, body, 0)

    o = obuf_refs[0][...] + obuf_refs[1][...]
    o_ref[...] = (_rms_norm(o, g_ref[...]) * _silu(hg_ref[...])).astype(BF16)
    if emit_state:
        for d in range(2):
            st_ref[d] = s_ref[d].T


def _hgrn(p, mix, geom, layer, lb_params, hgrn_g, consts, state, new_state=None):
    n_seq, n, t = geom["n_seq"], geom["n"], geom["t"]
    heads = geom["hg_heads"]
    mix_col = geom["da_heads"]
    base = 3 * geom["da_heads"]
    has_state = state is not None
    emit_state = not has_state
    depth = hgrn_g.shape[0]
    chunk = min(HG_CHUNK, n)
    col = lambda j: pl.BlockSpec((n, HEAD), lambda b, h: (b, base + j * heads + h))
    lbspec = pl.BlockSpec((None, 2, HEAD), lambda b, h: (layer, 0, h))
    const_specs = [pl.BlockSpec(c.shape, lambda b, h, nd=c.ndim: (0,) * nd) for c in consts]
    in_specs = [col(0), col(1), col(2), col(3), col(4), lbspec, lbspec, lbspec,
                pl.BlockSpec((None, 1, HEAD), lambda b, h: (layer, 0, 0))] + const_specs
    args = [p] * 5 + list(lb_params) + [hgrn_g.reshape(depth, 1, HEAD)] + list(consts)
    if has_state:
        in_specs.append(pl.BlockSpec((None, None, 2, None, HEAD, HEAD), lambda b, h: (b, layer, 0, h, 0, 0)))
        args.append(state)
    in_specs.append(pl.BlockSpec(memory_space=pl.ANY))
    args.append(mix)
    aliases = {len(args) - 1: 0}
    out_shape = [jax.ShapeDtypeStruct(mix.shape, mix.dtype)]
    out_specs = [pl.BlockSpec((n, HEAD), lambda b, h: (b, mix_col + h))]
    if emit_state:
        in_specs.append(pl.BlockSpec(memory_space=pl.ANY))
        args.append(new_state)
        aliases[len(args) - 1] = 1
        out_shape.append(jax.ShapeDtypeStruct(new_state.shape, new_state.dtype))
        out_specs.append(pl.BlockSpec((None, None, 2, None, HEAD, HEAD), lambda b, h: (b, layer, 0, h, 0, 0)))
    res = pl.pallas_call(
        functools.partial(_hgrn_kernel, n=n, chunk=chunk, unroll=HG_UNROLL if (n // chunk) % HG_UNROLL == 0 else 1,
                          has_state=has_state, emit_state=emit_state),
        out_shape=tuple(out_shape),
        grid=(n_seq, heads),
        in_specs=in_specs,
        out_specs=tuple(out_specs),
        scratch_shapes=[pltpu.VMEM((n, HEAD), F32), pltpu.VMEM((n, HEAD), F32),
                        pltpu.VMEM((2, HEAD, HEAD), F32)],
        input_output_aliases=aliases,
        compiler_params=_cparams("parallel", "parallel"),
        name="hgrn2_scan",
    )(*args)
    return res if emit_state else (res[0], None)


def _swa_ctx_kernel(q_ref, k_ref, v_ref, sink_ref, mix_ref, o_ref, *, layer):
    del mix_ref
    kv = pl.program_id(1)
    k = k_ref[...].astype(BF16)
    v = v_ref[...].astype(BF16)
    for g in range(SW_GROUP):
        q = (q_ref[:, g * HEAD:(g + 1) * HEAD] * (HEAD ** -0.5)).astype(BF16)
        sink = sink_ref[layer, kv * SW_GROUP + g]
        s = _dot_t(q, k)
        m = jnp.maximum(jnp.max(s, axis=-1, keepdims=True), sink)
        p = jnp.exp(s - m)
        l = jnp.sum(p, axis=-1, keepdims=True) + jnp.exp(sink - m)
        o_ref[:, g * HEAD:(g + 1) * HEAD] = _dot((p * (1.0 / l)).astype(BF16), v).astype(BF16)


def _swa_ctx(p, mix, geom, layer, sink):
    n_seq, n, t = geom["n_seq"], geom["n"], geom["t"]
    kvh = geom["sw_kv"]
    qw = SW_GROUP * HEAD
    q_blk = geom["cq_off"] // qw
    k_blk = geom["ck_off"] // HEAD
    v_blk = k_blk + kvh
    mix_blk = (geom["da_heads"] + geom["hg_heads"]) * HEAD // qw
    return pl.pallas_call(
        functools.partial(_swa_ctx_kernel, layer=layer),
        out_shape=jax.ShapeDtypeStruct(mix.shape, mix.dtype),
        grid=(n_seq, kvh),
        in_specs=[pl.BlockSpec((n, qw), lambda b, kv: (b, q_blk + kv)),
                  pl.BlockSpec((n, HEAD), lambda b, kv: (b, k_blk + kv)),
                  pl.BlockSpec((n, HEAD), lambda b, kv: (b, v_blk + kv)),
                  pl.BlockSpec(memory_space=pltpu.SMEM),
                  pl.BlockSpec(memory_space=pl.ANY)],
        out_specs=pl.BlockSpec((n, qw), lambda b, kv: (b, mix_blk + kv)),
        input_output_aliases={4: 0},
        compiler_params=_cparams("parallel", "parallel"),
        name="sink_attention",
    )(p, p, p, sink, mix)


def _band_masks(n):
    r = np.arange(3 * SW_BLOCK)[:, None]
    i = (np.arange(SW_GROUP * SW_BLOCK) % SW_BLOCK)[None, :]
    window = np.abs(SW_BLOCK + i - r) <= SW_WINDOW
    not_before = r >= SW_BLOCK
    not_after = r < 2 * SW_BLOCK
    variants = [window, window & not_before, window & not_after, window & not_before & not_after]
    return jnp.asarray(np.stack(variants).astype(np.float32))


def _swa_lat_kernel(q_ref, k_ref, v_ref, qc_ref, qsa_ref, qsb_ref, kc_ref, ksa_ref, ksb_ref, ck_ref, cv_ref,
                    mask_ref, sink_ref, mix_ref, o_ref, kr_ref, vt_ref, kctx_ref, vctxt_ref, *, layer, n):
    kv = pl.program_id(1)
    step = pl.program_id(2)
    nb = n // SW_BLOCK
    w = HEAD // 4
    band = 3 * SW_BLOCK
    per_step = q_ref.shape[0] // SW_BLOCK

    @pl.when(step == 0)
    def _():
        zeros = jnp.zeros((SW_BLOCK, HEAD), BF16)
        for j in (0, nb + 1):
            kr_ref[j] = zeros
            vt_ref[j] = zeros

        def fill(j, carry):
            rows = pl.ds(pl.multiple_of(j * SW_BLOCK, SW_BLOCK), SW_BLOCK)
            k = _rope(k_ref[rows, :], kc_ref[rows, :], ksa_ref[rows, :], ksb_ref[rows, :], w)
            kr_ref[j + 1] = k.astype(BF16)
            vt_ref[j + 1] = v_ref[rows, :].T.astype(BF16)
            return carry

        lax.fori_loop(0, nb, fill, 0)
        kctx_ref[...] = ck_ref[...].astype(BF16)
        vctxt_ref[...] = cv_ref[...].T.astype(BF16)

    lane = lax.broadcasted_iota(jnp.int32, (1, SW_GROUP * SW_BLOCK), 1)
    sink = jnp.zeros((1, SW_GROUP * SW_BLOCK), F32)
    for g in range(SW_GROUP):
        sink = jnp.where(lane // SW_BLOCK == g, sink_ref[layer, kv * SW_GROUP + g], sink)

    def fold(x, op):
        return op(op(x.reshape(x.shape[0] // 8, 8, x.shape[1]), axis=0), axis=0, keepdims=True)

    blocks = [step * per_step + c for c in range(per_step)]
    rows = [slice(c * SW_BLOCK, (c + 1) * SW_BLOCK) for c in range(per_step)]
    scores = []
    for qb, r in zip(blocks, rows):
        qc, qsa, qsb = qc_ref[r, :], qsa_ref[r, :], qsb_ref[r, :]
        q_all = jnp.concatenate(
            [(_rope(q_ref[r, g * HEAD:(g + 1) * HEAD], qc, qsa, qsb, w) * (HEAD ** -0.5)).astype(BF16)
             for g in range(SW_GROUP)], axis=0)
        k_all = jnp.concatenate([kr_ref[qb], kr_ref[qb + 1], kr_ref[qb + 2], kctx_ref[...]], axis=0)
        scores.append(_dot_t(k_all, q_all))
    probs = []
    for qb, s in zip(blocks, scores):
        variant = jnp.where(qb == 0, 1, 0) + jnp.where(qb == nb - 1, 2, 0)
        s_band = jnp.where(mask_ref[variant] > 0.5, s[:band], NEG_INF)
        s_ctx = s[band:]
        m = jnp.maximum(jnp.maximum(fold(s_band, jnp.max), fold(s_ctx, jnp.max)), sink)
        p_band = jnp.exp(s_band - m)
        p_ctx = jnp.exp(s_ctx - m)
        inv = 1.0 / (fold(p_band, jnp.sum) + fold(p_ctx, jnp.sum) + jnp.exp(sink - m))
        probs.append(jnp.concatenate([p_band * inv, p_ctx * inv], axis=0).astype(BF16))
    outs = []
    for qb, p_all in zip(blocks, probs):
        vt_all = jnp.concatenate([vt_ref[qb], vt_ref[qb + 1], vt_ref[qb + 2], vctxt_ref[...]], axis=1)
        outs.append(_dot(vt_all, p_all))
    for r, ot in zip(rows, outs):
        for g in range(SW_GROUP):
            o_ref[r, g * HEAD:(g + 1) * HEAD] = ot[:, g * SW_BLOCK:(g + 1) * SW_BLOCK].T.astype(BF16)


def _swa_lat(p, mix, geom, layer, sink, rope_tabs, cache_k, cache_v):
    n_seq, n, t = geom["n_seq"], geom["n"], geom["t"]
    kvh = geom["sw_kv"]
    qw = SW_GROUP * HEAD
    q_blk = geom["cq_off"] // qw
    k_blk = geom["ck_off"] // HEAD
    v_blk = k_blk + kvh
    mix_blk = (geom["da_heads"] + geom["hg_heads"]) * HEAD // qw
    nb = n // SW_BLOCK
    per_step = SW_PER_STEP if nb % SW_PER_STEP == 0 else 1
    nsteps = nb // per_step
    tq = per_step * SW_BLOCK
    n_ctx = cache_k.shape[3]
    masks = _band_masks(n)
    cspec = pl.BlockSpec((None, None, None, n_ctx, HEAD), lambda b, kv, i: (b, layer, kv, 0, 0))
    return pl.pallas_call(
        functools.partial(_swa_lat_kernel, layer=layer, n=n),
        out_shape=jax.ShapeDtypeStruct(mix.shape, mix.dtype),
        grid=(n_seq, kvh, nsteps),
        in_specs=[pl.BlockSpec((tq, qw), lambda b, kv, i: (b * nsteps + i, q_blk + kv)),
                  pl.BlockSpec((n, HEAD), lambda b, kv, i: (b, k_blk + kv)),
                  pl.BlockSpec((n, HEAD), lambda b, kv, i: (b, v_blk + kv))]
                 + [pl.BlockSpec((tq, HEAD), lambda b, kv, i: (i, 0))] * 3
                 + [pl.BlockSpec((n, HEAD), lambda b, kv, i: (0, 0))] * 3
                 + [cspec, cspec, pl.BlockSpec(masks.shape, lambda b, kv, i: (0, 0, 0)),
                    pl.BlockSpec(memory_space=pltpu.SMEM), pl.BlockSpec(memory_space=pl.ANY)],
        out_specs=pl.BlockSpec((tq, qw), lambda b, kv, i: (b * nsteps + i, mix_blk + kv)),
        scratch_shapes=[pltpu.VMEM((nb + 2, SW_BLOCK, HEAD), BF16), pltpu.VMEM((nb + 2, HEAD, SW_BLOCK), BF16),
                        pltpu.VMEM((n_ctx, HEAD), BF16), pltpu.VMEM((HEAD, n_ctx), BF16)],
        input_output_aliases={13: 0},
        compiler_params=_cparams("parallel", "parallel", "arbitrary"),
        name="banded_sink_attention",
    )(p, p, p, *rope_tabs, *rope_tabs, cache_k, cache_v, masks, sink, mix)


def _rope_tables(n, half):
    h = half // 2
    pos = jnp.arange(n)
    inv = ROPE_BASE ** (-jnp.arange(h, dtype=F32) / h)
    zero = jnp.zeros((n, h), F32)
    c, sa, sb = [], [], []
    for axis_pos in (pos // GRID_W, pos % GRID_W):
        ang = axis_pos.astype(F32)[:, None] * inv[None, :]
        cos, sin = jnp.cos(ang), jnp.sin(ang)
        c += [cos, cos]
        sa += [-sin, zero]
        sb += [zero, sin]
    reps = HEAD // (2 * half)
    cat = lambda parts: jnp.tile(jnp.concatenate(parts, axis=1), (1, reps))
    return cat(c), cat(sa), cat(sb)


def _geometry(n_seq, n, d_model):
    da_heads = hg_heads = d_model // 512
    sw_heads = d_model // 256
    sw_kv = sw_heads // SW_GROUP
    cq_off = (3 * da_heads + 5 * hg_heads) * HEAD
    return dict(n_seq=n_seq, n=n, t=n_seq * n, da_heads=da_heads, hg_heads=hg_heads, sw_kv=sw_kv,
                cq_off=cq_off, ck_off=cq_off + sw_heads * HEAD, d_mix=(da_heads + hg_heads + sw_heads) * HEAD)


def _token_tiles(n):
    return _pick_tile(n, TM_PROJ), _pick_tile(n, TM_OUT), _pick_tile(n, TM_FFN)


def _layer(x, geom, layer, mod, mod_row, wts, params, lb_params, scan_consts, tabs, caches, new_caches, alpha):
    w_in, w_out, w_gu, w_down = wts
    tm_proj, tm_out, tm_ffn = _token_tiles(geom["t"] if caches is None else geom["n"])
    if caches is None:
        heads, kvh = geom["da_heads"], geom["sw_kv"]
        cols = ((heads * HEAD, heads), (2 * heads * HEAD, heads), (geom["ck_off"], kvh),
                (geom["ck_off"] + kvh * HEAD, kvh))
        p, kv_caches = _in_proj(x, mod, mod_row, w_in, layer, geom["n"], new_caches[:4], cols)
        mix = _diff_attn(p, geom, layer, params["diff_lambda"], params["diff_norm_g"], None, None, None)
        mix, st = _hgrn(p, mix, geom, layer, lb_params, params["hgrn_norm_g"], scan_consts, None, new_caches[4])
        mix = _swa_ctx(p, mix, geom, layer, params["swa_sink"])
        new_caches = kv_caches + (st,)
    else:
        ck_d, cv_d, ck_s, cv_s, state = caches
        p, _ = _in_proj(x, mod, mod_row, w_in, layer, tm_proj)
        mix = _diff_attn(p, geom, layer, params["diff_lambda"], params["diff_norm_g"], tabs[0], ck_d, cv_d)
        mix, _ = _hgrn(p, mix, geom, layer, lb_params, params["hgrn_norm_g"], scan_consts, state)
        mix = _swa_lat(p, mix, geom, layer, params["swa_sink"], tabs[1], ck_s, cv_s)
    x1 = _out_proj(mix, w_out, x, mod, mod_row, params["ln1_g"], params["ln1_b"], layer, tm_out, alpha)
    y = _ffn(x1, w_gu, w_down, mod, mod_row, params["ln2_g"], params["ln2_b"], layer, tm_ffn, alpha)
    return y, new_caches


def kernel(x_prompt, x_sample, cache_diff_k, cache_diff_v, cache_swa_k, cache_swa_v, state_hgrn, c, c_ctx, w_mod,
           b_mod, w_in, w_out, diff_lambda, diff_norm_g, hgrn_lb_logits, hgrn_norm_g, swa_sink, ln1_g, ln1_b, ln2_g,
           ln2_b, w_gate_up, w_down):
    batch, seq, d = x_prompt.shape
    dec_batch, dec_seq, _ = x_sample.shape
    depth = w_mod.shape[0]
    alpha = (2 * depth) ** 0.25
    geom_c = _geometry(batch, seq, d)
    geom_l = _geometry(dec_batch, dec_seq, d)
    assert 1 + dec_batch <= MOD_ROWS

    cond = jnp.zeros((MOD_ROWS, d), F32).at[0].set(c_ctx).at[1:1 + dec_batch].set(c)
    mod = _modulation(cond, w_mod, b_mod).reshape(depth * MOD_ROWS * 6, 1, d)
    lb_params = _lb_params(hgrn_lb_logits)
    params = dict(diff_lambda=diff_lambda, diff_norm_g=diff_norm_g, hgrn_norm_g=hgrn_norm_g, swa_sink=swa_sink,
                  ln1_g=ln1_g, ln1_b=ln1_b, ln2_g=ln2_g, ln2_b=ln2_b)
    wts = tuple(w.astype(BF16) for w in (w_in, w_out, w_gate_up, w_down))
    tabs = (_rope_tables(dec_seq, DA_QK // 2), _rope_tables(dec_seq, HEAD // 2))
    consts_c = _scan_constants(min(HG_CHUNK, seq))
    consts_l = _scan_constants(min(HG_CHUNK, dec_seq))

    y_p = x_prompt.reshape(batch * seq, d)
    y_s = x_sample.reshape(dec_batch * dec_seq, d)
    heads, kvh = geom_c["da_heads"], geom_c["sw_kv"]
    new_caches = tuple(jnp.zeros((batch, depth, nh, seq, HEAD), F32) for nh in (heads, heads, kvh, kvh))
    new_caches += (jnp.zeros((batch, depth, 2, geom_c["hg_heads"], HEAD, HEAD), F32),)
    caches = (cache_diff_k, cache_diff_v, cache_swa_k, cache_swa_v, state_hgrn)
    for l in range(depth):
        row_c = lambda m, tm, j, l=l: (l * MOD_ROWS) * 6 + j
        row_l = lambda m, tm, j, l=l: (l * MOD_ROWS + 1 + (m * tm) // dec_seq) * 6 + j
        y_p, new_caches = _layer(y_p, geom_c, l, mod, row_c, wts, params, lb_params, consts_c, None, None,
                                 new_caches, alpha)
        y_s, _ = _layer(y_s, geom_l, l, mod, row_l, wts, params, lb_params, consts_l, tabs, caches, None, alpha)
    return (y_p.reshape(batch, seq, d), y_s.reshape(dec_batch, dec_seq, d)) + new_caches
```

```python
import functools
import math

import numpy as np
import jax
import jax.numpy as jnp
from jax import lax
from jax.experimental import pallas as pl
from jax.experimental.pallas import tpu as pltpu

F32 = jnp.float32
BF16 = jnp.bfloat16

GRID_W = 64
ROPE_BASE = 10000.0
LN_EPS = 1e-5
RMS_EPS = 1e-6
NEG_INF = -1e30
LB_FLOOR = 1e-30
LOG2_E = math.log2(math.e)
HEAD = 128
DA_QK = 64
SW_GROUP = 4
SW_BLOCK = 128
SW_WINDOW = 128
HG_CHUNK = 64
HG_UNROLL = 4
TM_PROJ = 256
TM_OUT = 512
OUT_SPLIT = 1
SW_PER_STEP = 4
TM_FFN = 512
MOD_ROWS = 8
VMEM_LIMIT = 56 * 1024 * 1024


def _cparams(*sem):
    return pltpu.CompilerParams(dimension_semantics=sem, vmem_limit_bytes=VMEM_LIMIT)


def _dot(a, b):
    return jnp.dot(a, b, preferred_element_type=F32)


def _dot_t(a, b):
    return lax.dot_general(a, b, (((1,), (1,)), ((), ())), preferred_element_type=F32)


def _silu(x):
    return x / (1.0 + jnp.exp(-x))


def _layer_norm(y, g, b):
    mu = jnp.mean(y, axis=-1, keepdims=True)
    d = y - mu
    var = jnp.mean(d * d, axis=-1, keepdims=True)
    return d * lax.rsqrt(var + LN_EPS) * g + b


def _rms_norm(o, g):
    ms = jnp.mean(o * o, axis=-1, keepdims=True)
    return o * lax.rsqrt(ms + RMS_EPS) * g


def _rope(x, c, sa, sb, w):
    return x * c + pltpu.roll(x, HEAD - w, 1) * sa + pltpu.roll(x, w, 1) * sb


def _pick_tile(n, target):
    t = min(n, target)
    while n % t or t % 128:
        t -= 128
    return t


def _lb_kernel(logit_ref, loglb_ref, log1m_ref, onem_ref):
    depth = logit_ref.shape[0]
    x = [logit_ref[l] for l in range(depth)]
    m = functools.reduce(jnp.maximum, x)
    e = [jnp.exp(xi - m) for xi in x]
    tot = functools.reduce(lambda a, b: a + b, e)
    w = [ei / tot for ei in e]
    acc = jnp.zeros_like(w[0])
    for l in range(depth):
        acc = acc + w[l]
        lb = acc - w[0]
        loglb_ref[l] = jnp.log(jnp.maximum(lb, LB_FLOOR))
        log1m_ref[l] = jnp.log1p(-lb)
        onem_ref[l] = 1.0 - lb


def _lb_params(logits):
    shp = jax.ShapeDtypeStruct(logits.shape, F32)
    return pl.pallas_call(_lb_kernel, out_shape=(shp, shp, shp), name="hgrn_lb_params")(logits)


def _mod_kernel(c_ref, w_ref, b_ref, o_ref):
    s = _silu(c_ref[...])
    s_hi = s.astype(BF16)
    s_lo = (s - s_hi.astype(F32)).astype(BF16)
    w = w_ref[...]
    w_hi = w.astype(BF16)
    w_lo = (w - w_hi.astype(F32)).astype(BF16)
    o_ref[...] = _dot(s_hi, w_hi) + _dot(s_lo, w_hi) + _dot(s_hi, w_lo) + b_ref[...]


def _modulation(cond, w_mod, b_mod):
    depth, d, d6 = w_mod.shape
    tn = _pick_tile(d6, 1024)
    return pl.pallas_call(
        _mod_kernel,
        out_shape=jax.ShapeDtypeStruct((depth, MOD_ROWS, d6), F32),
        grid=(depth, d6 // tn),
        in_specs=[pl.BlockSpec((MOD_ROWS, d), lambda l, n: (0, 0)),
                  pl.BlockSpec((None, d, tn), lambda l, n: (l, 0, n)),
                  pl.BlockSpec((None, 1, tn), lambda l, n: (l, 0, n))],
        out_specs=pl.BlockSpec((None, MOD_ROWS, tn), lambda l, n: (l, 0, n)),
        compiler_params=_cparams("parallel", "parallel"),
        name="adaln_modulation",
    )(cond, w_mod, b_mod.reshape(depth, 1, d6))


def _proj_kernel(x_ref, sh_ref, sc_ref, w_ref, *rest, cache_cols):
    o_ref, mix_ref = rest[len(cache_cols)], rest[len(cache_cols) + 1]
    h = (x_ref[...] * (1.0 + sc_ref[0]) + sh_ref[0]).astype(BF16)
    o_ref[...] = _dot(h, w_ref[...])
    mix_ref[...] = jnp.zeros_like(mix_ref)
    for c_ref, (off, nh) in zip(rest[len(cache_cols) + 2:], cache_cols):
        for hd in range(nh):
            c_ref[hd] = o_ref[:, off + hd * HEAD:off + (hd + 1) * HEAD]


def _in_proj(x, mod, mod_row, w_in, layer, tm, d_mix, caches=None, cache_cols=()):
    t, d = x.shape
    d_in = w_in.shape[2]
    caches = tuple(caches or ())
    out_shape = ([jax.ShapeDtypeStruct((t, d_in), F32), jax.ShapeDtypeStruct((t, d_mix), BF16)]
                 + [jax.ShapeDtypeStruct(c.shape, c.dtype) for c in caches])
    out_specs = [pl.BlockSpec((tm, d_in), lambda m: (m, 0)), pl.BlockSpec((tm, d_mix), lambda m: (m, 0))]
    for c in caches:
        assert c.shape[3] == tm
        out_specs.append(pl.BlockSpec((None, None, c.shape[2], tm, HEAD), lambda m: (m, layer, 0, 0, 0)))
    res = pl.pallas_call(
        functools.partial(_proj_kernel, cache_cols=tuple(cache_cols)),
        out_shape=tuple(out_shape),
        grid=(t // tm,),
        in_specs=[pl.BlockSpec((tm, d), lambda m: (m, 0)),
                  pl.BlockSpec((1, 1, d), lambda m: (mod_row(m, tm, 0), 0, 0)),
                  pl.BlockSpec((1, 1, d), lambda m: (mod_row(m, tm, 1), 0, 0)),
                  pl.BlockSpec((None, d, d_in), lambda m: (layer, 0, 0), pipeline_mode=pl.Buffered(1))]
                 + [pl.BlockSpec(memory_space=pl.ANY)] * len(caches),
        out_specs=tuple(out_specs),
        input_output_aliases={4 + i: 2 + i for i in range(len(caches))},
        compiler_params=_cparams("parallel"),
        name="in_proj",
    )(x, mod, mod, w_in, *caches)
    return res[0], res[1], tuple(res[2:])


def _out_proj_kernel(mix_ref, w_ref, x_ref, g1_ref, lng_ref, lnb_ref, x1_ref, *, alpha):
    sub = x_ref.shape[0] // OUT_SPLIT
    for r in range(OUT_SPLIT):
        rows = slice(r * sub, (r + 1) * sub)
        mix = _dot(mix_ref[rows, :], w_ref[...])
        x1_ref[rows, :] = _layer_norm(alpha * x_ref[rows, :] + g1_ref[0] * mix, lng_ref[...], lnb_ref[...])


def _out_proj(mix, w_out, x, mod, mod_row, ln_g, ln_b, layer, tm, alpha):
    t, d = x.shape
    d_mix = mix.shape[1]
    depth = ln_g.shape[0]
    vec = pl.BlockSpec((None, 1, d), lambda m: (layer, 0, 0))
    return pl.pallas_call(
        functools.partial(_out_proj_kernel, alpha=alpha),
        out_shape=jax.ShapeDtypeStruct((t, d), F32),
        grid=(t // tm,),
        in_specs=[pl.BlockSpec((tm, d_mix), lambda m: (m, 0)),
                  pl.BlockSpec((None, d_mix, d), lambda m: (layer, 0, 0), pipeline_mode=pl.Buffered(1)),
                  pl.BlockSpec((tm, d), lambda m: (m, 0)),
                  pl.BlockSpec((1, 1, d), lambda m: (mod_row(m, tm, 2), 0, 0)),
                  vec, vec],
        out_specs=pl.BlockSpec((tm, d), lambda m: (m, 0)),
        compiler_params=_cparams("parallel"),
        name="out_proj_ln",
    )(mix, w_out, x, mod, ln_g.reshape(depth, 1, d), ln_b.reshape(depth, 1, d))


def _ffn_kernel(x_ref, sh2_ref, sc2_ref, wg_ref, wu_ref, wd_ref, g2_ref, lng_ref, lnb_ref, o_ref, h_ref, *, alpha):
    f = pl.program_id(1)

    @pl.when(f == 0)
    def _():
        h_ref[...] = (x_ref[...] * (1.0 + sc2_ref[0]) + sh2_ref[0]).astype(BF16)
        o_ref[...] = jnp.zeros_like(o_ref)

    h = h_ref[...]
    a = _dot(h, wg_ref[...])
    u = _dot(h, wu_ref[...])
    o_ref[...] += _dot((_silu(a) * u).astype(BF16), wd_ref[...])

    @pl.when(f == pl.num_programs(1) - 1)
    def _():
        y = alpha * x_ref[...] + g2_ref[0] * o_ref[...]
        o_ref[...] = _layer_norm(y, lng_ref[...], lnb_ref[...])


def _ffn(x1, w_gu, w_down, mod, mod_row, ln_g, ln_b, layer, tm, alpha):
    t, d = x1.shape
    d_ff = w_down.shape[1]
    depth = ln_g.shape[0]
    tf = _pick_tile(d_ff, 512)
    nf = d_ff // tf
    vec = pl.BlockSpec((None, 1, d), lambda m, f: (layer, 0, 0))
    row = lambda j: pl.BlockSpec((1, 1, d), lambda m, f: (mod_row(m, tm, j), 0, 0))
    return pl.pallas_call(
        functools.partial(_ffn_kernel, alpha=alpha),
        out_shape=jax.ShapeDtypeStruct((t, d), F32),
        grid=(t // tm, nf),
        in_specs=[pl.BlockSpec((tm, d), lambda m, f: (m, 0)),
                  row(3), row(4),
                  pl.BlockSpec((None, d, tf), lambda m, f: (layer, 0, f)),
                  pl.BlockSpec((None, d, tf), lambda m, f: (layer, 0, nf + f)),
                  pl.BlockSpec((None, tf, d), lambda m, f: (layer, f, 0)),
                  row(5), vec, vec],
        out_specs=pl.BlockSpec((tm, d), lambda m, f: (m, 0)),
        scratch_shapes=[pltpu.VMEM((tm, d), BF16)],
        compiler_params=_cparams("parallel", "arbitrary"),
        name="ffn_ln",
    )(x1, mod, mod, w_gu, w_gu, w_down, mod, ln_g.reshape(depth, 1, d), ln_b.reshape(depth, 1, d))


def _diff_attn_kernel(*refs, rope, cached, n_self, kb, lam_init, pipelined, n_tiles):
    it = iter(refs)
    q_ref, k_ref, v_ref = next(it), next(it), next(it)
    if rope:
        qc_ref, qsa_ref, qsb_ref, kc_ref, ksa_ref, ksb_ref = (next(it) for _ in range(6))
    if cached:
        ck_ref, cv_ref = next(it), next(it)
    lam_ref, g_ref, _, o_ref, kr_ref, vt_ref, m_ref, acc_ref = (next(it) for _ in range(8))
    s_refs = tuple(it)
    nblk = kr_ref.shape[0]
    nself = n_self // kb
    tq = q_ref.shape[0]
    step = pl.program_id(2)
    last = pl.num_programs(2) - 1

    def prepare_keys():
        for j in range(nself):
            rows = slice(j * kb, (j + 1) * kb)
            k = k_ref[rows, :]
            if rope:
                k = _rope(k, kc_ref[rows, :], ksa_ref[rows, :], ksb_ref[rows, :], DA_QK // 4)
            kr_ref[j] = k.astype(BF16)
            vt_ref[j] = v_ref[rows, :].T.astype(BF16)
        if cached:
            for j in range(nblk - nself):
                rows = slice(j * kb, (j + 1) * kb)
                kr_ref[nself + j] = ck_ref[rows, :].astype(BF16)
                vt_ref[nself + j] = cv_ref[rows, :].T.astype(BF16)

    def queries():
        q = q_ref[...]
        if rope:
            q = _rope(q, qc_ref[...], qsa_ref[...], qsb_ref[...], DA_QK // 4)
        q = q * (DA_QK ** -0.5 * LOG2_E)
        lane = lax.broadcasted_iota(jnp.int32, q.shape, 1)
        return (jnp.where(lane < DA_QK, q, 0.0).astype(BF16), jnp.where(lane >= DA_QK, q, 0.0).astype(BF16))

    def fold(x, op):
        return op(x.reshape(kb // 8, 8, tq), axis=0)

    def scores_block(j, qz, buf, ms):
        kblk = kr_ref[j]
        out = []
        for mp in range(2):
            s = _dot_t(kblk, qz[mp])
            buf[mp, j] = s
            out.append(jnp.maximum(ms[mp], fold(s, jnp.max)))
        return tuple(out)

    def values_block(j, buf, ms, ls):
        out = []
        for mp in range(2):
            p = jnp.exp2(buf[mp, j] - ms[mp])
            acc_ref[mp] += _dot(vt_ref[j], p.astype(BF16))
            out.append(ls[mp] + fold(p, jnp.sum))
        return tuple(out)

    neg = jnp.full((8, tq), -jnp.inf, F32)
    zero = jnp.zeros((8, tq), F32)

    def save_max(ms):
        for mp in range(2):
            m_ref[mp] = jnp.max(ms[mp], axis=0, keepdims=True)

    def finish(ls):
        l1, l2 = (jnp.sum(l, axis=0, keepdims=True) for l in ls)
        lp = lam_ref[...]
        lam = (jnp.exp(jnp.sum(lp[0:1] * lp[1:2], axis=-1, keepdims=True))
               - jnp.exp(jnp.sum(lp[2:3] * lp[3:4], axis=-1, keepdims=True)) + lam_init)
        ot = acc_ref[0] * (1.0 / l1) - acc_ref[1] * (lam / l2)
        ms_o = jnp.mean(ot * ot, axis=0, keepdims=True)
        ot = ot * lax.rsqrt(ms_o + RMS_EPS) * g_ref[...] * (1.0 - lam_init)
        o_ref[...] = ot.T.astype(BF16)

    if not pipelined:
        pl.when(step == 0)(prepare_keys)
        qz = queries()
        ms = (neg, neg)
        for j in range(nblk):
            ms = scores_block(j, qz, s_refs[0], ms)
        ms = tuple(jnp.max(m, axis=0, keepdims=True) for m in ms)
        acc_ref[...] = jnp.zeros_like(acc_ref)
        ls = (zero, zero)
        for j in range(nblk):
            ls = values_block(j, s_refs[0], ms, ls)
        finish(ls)
        return

    @pl.when(step == 0)
    def _():
        prepare_keys()
        qz = queries()
        ms = (neg, neg)
        for j in range(nblk):
            ms = scores_block(j, qz, s_refs[0], ms)
        save_max(ms)

    def interior(parity):
        prev = (m_ref[0], m_ref[1])
        qz = queries()
        acc_ref[...] = jnp.zeros_like(acc_ref)
        ms, ls = (neg, neg), (zero, zero)
        for j in range(nblk):
            ms = scores_block(j, qz, s_refs[parity], ms)
            ls = values_block(j, s_refs[1 - parity], prev, ls)
        save_max(ms)
        finish(ls)

    inside = (step > 0) & (step < last)
    pl.when(inside & (step % 2 == 1))(functools.partial(interior, 1))
    pl.when(inside & (step % 2 == 0))(functools.partial(interior, 0))

    @pl.when(step == last)
    def _():
        prev = (m_ref[0], m_ref[1])
        acc_ref[...] = jnp.zeros_like(acc_ref)
        ls = (zero, zero)
        for j in range(nblk):
            ls = values_block(j, s_refs[(n_tiles - 1) % 2], prev, ls)
        finish(ls)


def _diff_attn(p, mix, geom, layer, diff_lambda, diff_g, rope_tabs, cache_k, cache_v):
    n_seq, n, t = geom["n_seq"], geom["n"], geom["t"]
    heads = geom["da_heads"]
    rope = rope_tabs is not None
    cached = cache_k is not None
    tq = min(n, 256)
    kb = min(n, 256)
    nq = n // tq
    n_ctx = cache_k.shape[3] if cached else 0
    nblk = (n + n_ctx) // kb
    assert n % kb == 0 and n_ctx % kb == 0
    k_off, v_off = heads, 2 * heads
    depth = diff_g.shape[0]
    lam_init = 0.8 - 0.6 * math.exp(-0.3 * layer)

    pipelined = nq > 1
    q_tile = (lambda i: jnp.minimum(i, nq - 1)) if pipelined else (lambda i: i)
    o_tile = (lambda i: jnp.maximum(i - 1, 0)) if pipelined else (lambda i: i)
    in_specs = [pl.BlockSpec((tq, HEAD), lambda b, h, i: (b * nq + q_tile(i), h)),
                pl.BlockSpec((n, HEAD), lambda b, h, i: (b, k_off + h)),
                pl.BlockSpec((n, HEAD), lambda b, h, i: (b, v_off + h))]
    args = [p, p, p]
    if rope:
        in_specs += [pl.BlockSpec((tq, HEAD), lambda b, h, i: (q_tile(i), 0))] * 3
        in_specs += [pl.BlockSpec((n, HEAD), lambda b, h, i: (0, 0))] * 3
        args += list(rope_tabs) * 2
    if cached:
        spec = pl.BlockSpec((None, None, None, n_ctx, HEAD), lambda b, h, i: (b, layer, h, 0, 0))
        in_specs += [spec, spec]
        args += [cache_k, cache_v]
    in_specs += [pl.BlockSpec((None, 4, DA_QK), lambda b, h, i: (layer, 0, 0)),
                 pl.BlockSpec((None, HEAD, 1), lambda b, h, i: (layer, 0, 0)),
                 pl.BlockSpec(memory_space=pl.ANY)]
    args += [diff_lambda, diff_g.reshape(depth, HEAD, 1), mix]
    return pl.pallas_call(
        functools.partial(_diff_attn_kernel, rope=rope, cached=cached, n_self=n, kb=kb, lam_init=lam_init,
                          pipelined=pipelined, n_tiles=nq),
        out_shape=jax.ShapeDtypeStruct(mix.shape, mix.dtype),
        grid=(n_seq, heads, nq + 1 if pipelined else nq),
        in_specs=in_specs,
        input_output_aliases={len(args) - 1: 0},
        out_specs=pl.BlockSpec((tq, HEAD), lambda b, h, i: (b * nq + o_tile(i), h)),
        scratch_shapes=[pltpu.VMEM((nblk, kb, HEAD), BF16), pltpu.VMEM((nblk, HEAD, kb), BF16),
                        pltpu.VMEM((2, 1, tq), F32), pltpu.VMEM((2, HEAD, tq), F32)]
                       + [pltpu.VMEM((2, nblk, kb, tq), F32)] * (2 if pipelined else 1),
        compiler_params=_cparams("parallel", "parallel", "arbitrary"),
        name="diff_attention",
    )(*args)


def _scan_constants(c):
    levels = int(math.log2(c))
    t = np.arange(c)[:, None]
    s = np.arange(c)[None, :]
    tri = (s <= t).astype(np.float32)
    masks = []
    for j in range(levels):
        m = c >> (j + 1)
        base = (t // (2 * m)) * (2 * m)
        sbase = (s // (2 * m)) * (2 * m)
        masks.append((sbase == base) & ((t - base) >= m) & ((s - sbase) < m))
    masks.append(s == t)
    masks = np.stack(masks).astype(np.float32)
    flip = lambda a: a.reshape(-1, c, c)[:, ::-1, ::-1].reshape(a.shape)
    return (jnp.asarray(tri, BF16), jnp.asarray(flip(tri), BF16),
            jnp.asarray(masks, F32), jnp.asarray(flip(masks), F32))


def _level_exponents(cum, d, chunk):
    row = lax.broadcasted_iota(jnp.int32, cum.shape, 0)
    sub = lax.broadcasted_iota(jnp.int32, (8, HEAD), 0)
    out = []
    m = chunk // 2
    while m >= 1:
        in_second = (row & m) != 0
        q_side = in_second if d == 0 else jnp.logical_not(in_second)
        if m >= 8:
            pieces = []
            for b0 in range(0, chunk, 2 * m):
                r = b0 + m - 1 + d
                bnd = jnp.broadcast_to(cum[r:r + 1, :], (m, HEAD))
                first, second = cum[b0:b0 + m, :], cum[b0 + m:b0 + 2 * m, :]
                pieces += [bnd - first, second - bnd] if d == 0 else [first - bnd, bnd - second]
            x = jnp.concatenate(pieces, axis=0)
        elif m == 1:
            neighbour = pltpu.roll(cum, 1 if d == 0 else chunk - 1, 0)
            x = jnp.where(q_side, cum - neighbour, 0.0)
        else:
            tiles = []
            for v in range(chunk // 8):
                if m == 4:
                    r = 8 * v + 3 + d
                    tiles.append(jnp.broadcast_to(cum[r:r + 1, :], (8, HEAD)))
                else:
                    lo = jnp.broadcast_to(cum[8 * v + 1 + d:8 * v + 2 + d, :], (8, HEAD))
                    hi = jnp.broadcast_to(cum[8 * v + 5 + d:8 * v + 6 + d, :], (8, HEAD))
                    tiles.append(jnp.where(sub < 4, lo, hi))
            bnd = jnp.concatenate(tiles, axis=0)
            x = jnp.where(q_side, cum - bnd, bnd - cum)
        out.append(x)
        m //= 2
    return out


def _hgrn_kernel(*refs, n, chunk, unroll, has_state, emit_state):
    it = iter(refs)
    q_ref, zf_ref, zb_ref, v_ref, hg_ref = (next(it) for _ in range(5))
    loglb_ref, log1m_ref, onem_ref, g_ref = (next(it) for _ in range(4))
    tri_refs = (next(it), next(it))
    mask_refs = (next(it), next(it))
    s0_ref = next(it) if has_state else None
    next(it)
    if emit_state:
        next(it)
    o_ref = next(it)
    st_ref = next(it) if emit_state else None
    obuf_refs = (next(it), next(it))
    s_ref = next(it)

    levels = int(math.log2(chunk))
    nchunks = n // chunk
    z_refs = (zf_ref, zb_ref)

    for d in range(2):
        s_ref[d] = s0_ref[d].T if has_state else jnp.zeros((HEAD, HEAD), F32)

    def gates(d, start):
        rows = pl.ds(start, chunk)
        z = z_refs[d][rows, :]
        q = _silu(q_ref[rows, :])
        v = v_ref[rows, :].astype(BF16)
        e = jnp.exp(-jnp.abs(z))
        log_sig = jnp.minimum(z, 0.0) - jnp.log1p(e)
        sig_neg = jnp.where(z >= 0.0, e, 1.0) / (1.0 + e)
        a = loglb_ref[d:d + 1, :]
        b = log1m_ref[d:d + 1, :] + log_sig
        g = jnp.maximum(a, b) + jnp.log1p(jnp.exp(-jnp.abs(a - b)))
        k = onem_ref[d:d + 1, :] * sig_neg
        g = g * LOG2_E
        g_hi = g.astype(BF16)
        g_lo = (g - g_hi.astype(F32)).astype(BF16)
        cum = _dot(tri_refs[d][...], jnp.concatenate([g_hi, g_lo], axis=1))
        return q, k, v, cum[:, :HEAD] + cum[:, HEAD:]

    def body(i, carry):
        chains = []
        for u in range(unroll):
            c = i * unroll + u
            chains.append((0, pl.multiple_of(c * chunk, chunk)))
            chains.append((1, pl.multiple_of((nchunks - 1 - c) * chunk, chunk)))
        work = [gates(d, start) for d, start in chains]
        qk16 = [(q.astype(BF16), k.astype(BF16)) for q, k, _, _ in work]
        atts = [mask_refs[d][levels] * _dot_t(qb, kb) for (d, _), (qb, kb) in zip(chains, qk16)]
        expo = [_level_exponents(cum, d, chunk) for (d, _), (_, _, _, cum) in zip(chains, work)]
        for j in range(levels):
            for ci, ((d, _), (qb, kb)) in enumerate(zip(chains, qk16)):
                fac = jnp.exp2(expo[ci][j]).astype(BF16)
                atts[ci] = atts[ci] + mask_refs[d][j] * _dot_t(qb * fac, kb * fac)
        intra, delta, q_in, decay = [], [], [], []
        for (d, _), (q, k, v, cum), att in zip(chains, work, atts):
            last = cum[chunk - 1:chunk] if d == 0 else cum[0:1]
            kt = (k * jnp.exp2(last - cum)).astype(BF16)
            intra.append(_dot(att.astype(BF16), v))
            delta.append(lax.dot_general(v, kt, (((0,), (0,)), ((), ())), preferred_element_type=F32))
            q_in.append((q * jnp.exp2(cum)).astype(BF16))
            decay.append(jnp.exp2(last))
        st = [s_ref[0], s_ref[1]]
        for ci, (d, start) in enumerate(chains):
            obuf_refs[d][pl.ds(start, chunk), :] = intra[ci] + _dot_t(q_in[ci], st[d].astype(BF16))
            st[d] = decay[ci] * st[d] + delta[ci]
        s_ref[0] = st[0]
        s_ref[1] = st[1]
        return carry

    lax.fori_loop(0, nchunks // unroll, body, 0)

    o = obuf_refs[0][...] + obuf_refs[1][...]
    o_ref[...] = (_rms_norm(o, g_ref[...]) * _silu(hg_ref[...])).astype(BF16)
    if emit_state:
        for d in range(2):
            st_ref[d] = s_ref[d].T


def _hgrn(p, mix, geom, layer, lb_params, hgrn_g, consts, state, new_state=None):
    n_seq, n, t = geom["n_seq"], geom["n"], geom["t"]
    heads = geom["hg_heads"]
    mix_col = geom["da_heads"]
    base = 3 * geom["da_heads"]
    has_state = state is not None
    emit_state = not has_state
    depth = hgrn_g.shape[0]
    chunk = min(HG_CHUNK, n)
    col = lambda j: pl.BlockSpec((n, HEAD), lambda b, h: (b, base + j * heads + h))
    lbspec = pl.BlockSpec((None, 2, HEAD), lambda b, h: (layer, 0, h))
    const_specs = [pl.BlockSpec(c.shape, lambda b, h, nd=c.ndim: (0,) * nd) for c in consts]
    in_specs = [col(0), col(1), col(2), col(3), col(4), lbspec, lbspec, lbspec,
                pl.BlockSpec((None, 1, HEAD), lambda b, h: (layer, 0, 0))] + const_specs
    args = [p] * 5 + list(lb_params) + [hgrn_g.reshape(depth, 1, HEAD)] + list(consts)
    if has_state:
        in_specs.append(pl.BlockSpec((None, None, 2, None, HEAD, HEAD), lambda b, h: (b, layer, 0, h, 0, 0)))
        args.append(state)
    in_specs.append(pl.BlockSpec(memory_space=pl.ANY))
    args.append(mix)
    aliases = {len(args) - 1: 0}
    out_shape = [jax.ShapeDtypeStruct(mix.shape, mix.dtype)]
    out_specs = [pl.BlockSpec((n, HEAD), lambda b, h: (b, mix_col + h))]
    if emit_state:
        in_specs.append(pl.BlockSpec(memory_space=pl.ANY))
        args.append(new_state)
        aliases[len(args) - 1] = 1
        out_shape.append(jax.ShapeDtypeStruct(new_state.shape, new_state.dtype))
        out_specs.append(pl.BlockSpec((None, None, 2, None, HEAD, HEAD), lambda b, h: (b, layer, 0, h, 0, 0)))
    res = pl.pallas_call(
        functools.partial(_hgrn_kernel, n=n, chunk=chunk, unroll=HG_UNROLL if (n // chunk) % HG_UNROLL == 0 else 1,
                          has_state=has_state, emit_state=emit_state),
        out_shape=tuple(out_shape),
        grid=(n_seq, heads),
        in_specs=in_specs,
        out_specs=tuple(out_specs),
        scratch_shapes=[pltpu.VMEM((n, HEAD), F32), pltpu.VMEM((n, HEAD), F32),
                        pltpu.VMEM((2, HEAD, HEAD), F32)],
        input_output_aliases=aliases,
        compiler_params=_cparams("parallel", "parallel"),
        name="hgrn2_scan",
    )(*args)
    return res if emit_state else (res[0], None)


def _swa_ctx_kernel(q_ref, k_ref, v_ref, sink_ref, mix_ref, o_ref, *, layer):
    del mix_ref
    kv = pl.program_id(1)
    k = k_ref[...].astype(BF16)
    v = v_ref[...].astype(BF16)
    for g in range(SW_GROUP):
        q = (q_ref[:, g * HEAD:(g + 1) * HEAD] * (HEAD ** -0.5)).astype(BF16)
        sink = sink_ref[layer, kv * SW_GROUP + g]
        s = _dot_t(q, k)
        m = jnp.maximum(jnp.max(s, axis=-1, keepdims=True), sink)
        p = jnp.exp(s - m)
        l = jnp.sum(p, axis=-1, keepdims=True) + jnp.exp(sink - m)
        o_ref[:, g * HEAD:(g + 1) * HEAD] = _dot((p * (1.0 / l)).astype(BF16), v).astype(BF16)


def _swa_ctx(p, mix, geom, layer, sink):
    n_seq, n, t = geom["n_seq"], geom["n"], geom["t"]
    kvh = geom["sw_kv"]
    qw = SW_GROUP * HEAD
    q_blk = geom["cq_off"] // qw
    k_blk = geom["ck_off"] // HEAD
    v_blk = k_blk + kvh
    mix_blk = (geom["da_heads"] + geom["hg_heads"]) * HEAD // qw
    return pl.pallas_call(
        functools.partial(_swa_ctx_kernel, layer=layer),
        out_shape=jax.ShapeDtypeStruct(mix.shape, mix.dtype),
        grid=(n_seq, kvh),
        in_specs=[pl.BlockSpec((n, qw), lambda b, kv: (b, q_blk + kv)),
                  pl.BlockSpec((n, HEAD), lambda b, kv: (b, k_blk + kv)),
                  pl.BlockSpec((n, HEAD), lambda b, kv: (b, v_blk + kv)),
                  pl.BlockSpec(memory_space=pltpu.SMEM),
                  pl.BlockSpec(memory_space=pl.ANY)],
        out_specs=pl.BlockSpec((n, qw), lambda b, kv: (b, mix_blk + kv)),
        input_output_aliases={4: 0},
        compiler_params=_cparams("parallel", "parallel"),
        name="sink_attention",
    )(p, p, p, sink, mix)


def _band_masks(n):
    r = np.arange(3 * SW_BLOCK)[:, None]
    i = (np.arange(SW_GROUP * SW_BLOCK) % SW_BLOCK)[None, :]
    window = np.abs(SW_BLOCK + i - r) <= SW_WINDOW
    not_before = r >= SW_BLOCK
    not_after = r < 2 * SW_BLOCK
    variants = [window, window & not_before, window & not_after, window & not_before & not_after]
    return jnp.asarray(np.stack(variants).astype(np.float32))


def _swa_lat_kernel(q_ref, k_ref, v_ref, qc_ref, qsa_ref, qsb_ref, kc_ref, ksa_ref, ksb_ref, ck_ref, cv_ref,
                    mask_ref, sink_ref, mix_ref, o_ref, kr_ref, vt_ref, kctx_ref, vctxt_ref, *, layer, n):
    kv = pl.program_id(1)
    step = pl.program_id(2)
    nb = n // SW_BLOCK
    w = HEAD // 4
    band = 3 * SW_BLOCK
    per_step = q_ref.shape[0] // SW_BLOCK

    @pl.when(step == 0)
    def _():
        zeros = jnp.zeros((SW_BLOCK, HEAD), BF16)
        for j in (0, nb + 1):
            kr_ref[j] = zeros
            vt_ref[j] = zeros

        def fill(j, carry):
            rows = pl.ds(pl.multiple_of(j * SW_BLOCK, SW_BLOCK), SW_BLOCK)
            k = _rope(k_ref[rows, :], kc_ref[rows, :], ksa_ref[rows, :], ksb_ref[rows, :], w)
            kr_ref[j + 1] = k.astype(BF16)
            vt_ref[j + 1] = v_ref[rows, :].T.astype(BF16)
            return carry

        lax.fori_loop(0, nb, fill, 0)
        kctx_ref[...] = ck_ref[...].astype(BF16)
        vctxt_ref[...] = cv_ref[...].T.astype(BF16)

    lane = lax.broadcasted_iota(jnp.int32, (1, SW_GROUP * SW_BLOCK), 1)
    sink = jnp.zeros((1, SW_GROUP * SW_BLOCK), F32)
    for g in range(SW_GROUP):
        sink = jnp.where(lane // SW_BLOCK == g, sink_ref[layer, kv * SW_GROUP + g], sink)

    def fold(x, op):
        return op(op(x.reshape(x.shape[0] // 8, 8, x.shape[1]), axis=0), axis=0, keepdims=True)

    blocks = [step * per_step + c for c in range(per_step)]
    rows = [slice(c * SW_BLOCK, (c + 1) * SW_BLOCK) for c in range(per_step)]
    scores = []
    for qb, r in zip(blocks, rows):
        qc, qsa, qsb = qc_ref[r, :], qsa_ref[r, :], qsb_ref[r, :]
        q_all = jnp.concatenate(
            [(_rope(q_ref[r, g * HEAD:(g + 1) * HEAD], qc, qsa, qsb, w) * (HEAD ** -0.5)).astype(BF16)
             for g in range(SW_GROUP)], axis=0)
        k_all = jnp.concatenate([kr_ref[qb], kr_ref[qb + 1], kr_ref[qb + 2], kctx_ref[...]], axis=0)
        scores.append(_dot_t(k_all, q_all))
    probs = []
    for qb, s in zip(blocks, scores):
        variant = jnp.where(qb == 0, 1, 0) + jnp.where(qb == nb - 1, 2, 0)
        s_band = jnp.where(mask_ref[variant] > 0.5, s[:band], NEG_INF)
        s_ctx = s[band:]
        m = jnp.maximum(jnp.maximum(fold(s_band, jnp.max), fold(s_ctx, jnp.max)), sink)
        p_band = jnp.exp(s_band - m)
        p_ctx = jnp.exp(s_ctx - m)
        inv = 1.0 / (fold(p_band, jnp.sum) + fold(p_ctx, jnp.sum) + jnp.exp(sink - m))
        probs.append(jnp.concatenate([p_band * inv, p_ctx * inv], axis=0).astype(BF16))
    outs = []
    for qb, p_all in zip(blocks, probs):
        vt_all = jnp.concatenate([vt_ref[qb], vt_ref[qb + 1], vt_ref[qb + 2], vctxt_ref[...]], axis=1)
        outs.append(_dot(vt_all, p_all))
    for r, ot in zip(rows, outs):
        for g in range(SW_GROUP):
            o_ref[r, g * HEAD:(g + 1) * HEAD] = ot[:, g * SW_BLOCK:(g + 1) * SW_BLOCK].T.astype(BF16)


def _swa_lat(p, mix, geom, layer, sink, rope_tabs, cache_k, cache_v):
    n_seq, n, t = geom["n_seq"], geom["n"], geom["t"]
    kvh = geom["sw_kv"]
    qw = SW_GROUP * HEAD
    q_blk = geom["cq_off"] // qw
    k_blk = geom["ck_off"] // HEAD
    v_blk = k_blk + kvh
    mix_blk = (geom["da_heads"] + geom["hg_heads"]) * HEAD // qw
    nb = n // SW_BLOCK
    per_step = SW_PER_STEP if nb % SW_PER_STEP == 0 else 1
    nsteps = nb // per_step
    tq = per_step * SW_BLOCK
    n_ctx = cache_k.shape[3]
    masks = _band_masks(n)
    cspec = pl.BlockSpec((None, None, None, n_ctx, HEAD), lambda b, kv, i: (b, layer, kv, 0, 0))
    return pl.pallas_call(
        functools.partial(_swa_lat_kernel, layer=layer, n=n),
        out_shape=jax.ShapeDtypeStruct(mix.shape, mix.dtype),
        grid=(n_seq, kvh, nsteps),
        in_specs=[pl.BlockSpec((tq, qw), lambda b, kv, i: (b * nsteps + i, q_blk + kv)),
                  pl.BlockSpec((n, HEAD), lambda b, kv, i: (b, k_blk + kv)),
                  pl.BlockSpec((n, HEAD), lambda b, kv, i: (b, v_blk + kv))]
                 + [pl.BlockSpec((tq, HEAD), lambda b, kv, i: (i, 0))] * 3
                 + [pl.BlockSpec((n, HEAD), lambda b, kv, i: (0, 0))] * 3
                 + [cspec, cspec, pl.BlockSpec(masks.shape, lambda b, kv, i: (0, 0, 0)),
                    pl.BlockSpec(memory_space=pltpu.SMEM), pl.BlockSpec(memory_space=pl.ANY)],
        out_specs=pl.BlockSpec((tq, qw), lambda b, kv, i: (b * nsteps + i, mix_blk + kv)),
        scratch_shapes=[pltpu.VMEM((nb + 2, SW_BLOCK, HEAD), BF16), pltpu.VMEM((nb + 2, HEAD, SW_BLOCK), BF16),
                        pltpu.VMEM((n_ctx, HEAD), BF16), pltpu.VMEM((HEAD, n_ctx), BF16)],
        input_output_aliases={13: 0},
        compiler_params=_cparams("parallel", "parallel", "arbitrary"),
        name="banded_sink_attention",
    )(p, p, p, *rope_tabs, *rope_tabs, cache_k, cache_v, masks, sink, mix)


def _rope_tables(n, half):
    h = half // 2
    pos = jnp.arange(n)
    inv = ROPE_BASE ** (-jnp.arange(h, dtype=F32) / h)
    zero = jnp.zeros((n, h), F32)
    c, sa, sb = [], [], []
    for axis_pos in (pos // GRID_W, pos % GRID_W):
        ang = axis_pos.astype(F32)[:, None] * inv[None, :]
        cos, sin = jnp.cos(ang), jnp.sin(ang)
        c += [cos, cos]
        sa += [-sin, zero]
        sb += [zero, sin]
    reps = HEAD // (2 * half)
    cat = lambda parts: jnp.tile(jnp.concatenate(parts, axis=1), (1, reps))
    return cat(c), cat(sa), cat(sb)


def _geometry(n_seq, n, d_model):
    da_heads = hg_heads = d_model // 512
    sw_heads = d_model // 256
    sw_kv = sw_heads // SW_GROUP
    cq_off = (3 * da_heads + 5 * hg_heads) * HEAD
    return dict(n_seq=n_seq, n=n, t=n_seq * n, da_heads=da_heads, hg_heads=hg_heads, sw_kv=sw_kv,
                cq_off=cq_off, ck_off=cq_off + sw_heads * HEAD, d_mix=(da_heads + hg_heads + sw_heads) * HEAD)


def _token_tiles(n):
    return _pick_tile(n, TM_PROJ), _pick_tile(n, TM_OUT), _pick_tile(n, TM_FFN)


def _layer(x, geom, layer, mod, mod_row, wts, params, lb_params, scan_consts, tabs, caches, new_caches, alpha):
    w_in, w_out, w_gu, w_down = wts
    tm_proj, tm_out, tm_ffn = _token_tiles(geom["t"] if caches is None else geom["n"])
    if caches is None:
        heads, kvh = geom["da_heads"], geom["sw_kv"]
        cols = ((heads * HEAD, heads), (2 * heads * HEAD, heads), (geom["ck_off"], kvh),
                (geom["ck_off"] + kvh * HEAD, kvh))
        p, mix, kv_caches = _in_proj(x, mod, mod_row, w_in, layer, geom["n"], geom["d_mix"], new_caches[:4], cols)
        mix = _diff_attn(p, mix, geom, layer, params["diff_lambda"], params["diff_norm_g"], None, None, None)
        mix, st = _hgrn(p, mix, geom, layer, lb_params, params["hgrn_norm_g"], scan_consts, None, new_caches[4])
        mix = _swa_ctx(p, mix, geom, layer, params["swa_sink"])
        new_caches = kv_caches + (st,)
    else:
        ck_d, cv_d, ck_s, cv_s, state = caches
        p, mix, _ = _in_proj(x, mod, mod_row, w_in, layer, tm_proj, geom["d_mix"])
        mix = _diff_attn(p, mix, geom, layer, params["diff_lambda"], params["diff_norm_g"], tabs[0], ck_d, cv_d)
        mix, _ = _hgrn(p, mix, geom, layer, lb_params, params["hgrn_norm_g"], scan_consts, state)
        mix = _swa_lat(p, mix, geom, layer, params["swa_sink"], tabs[1], ck_s, cv_s)
    x1 = _out_proj(mix, w_out, x, mod, mod_row, params["ln1_g"], params["ln1_b"], layer, tm_out, alpha)
    y = _ffn(x1, w_gu, w_down, mod, mod_row, params["ln2_g"], params["ln2_b"], layer, tm_ffn, alpha)
    return y, new_caches


def kernel(x_prompt, x_sample, cache_diff_k, cache_diff_v, cache_swa_k, cache_swa_v, state_hgrn, c, c_ctx, w_mod,
           b_mod, w_in, w_out, diff_lambda, diff_norm_g, hgrn_lb_logits, hgrn_norm_g, swa_sink, ln1_g, ln1_b, ln2_g,
           ln2_b, w_gate_up, w_down):
    batch, seq, d = x_prompt.shape
    dec_batch, dec_seq, _ = x_sample.shape
    depth = w_mod.shape[0]
    alpha = (2 * depth) ** 0.25
    geom_c = _geometry(batch, seq, d)
    geom_l = _geometry(dec_batch, dec_seq, d)
    assert 1 + dec_batch <= MOD_ROWS

    cond = jnp.zeros((MOD_ROWS, d), F32).at[0].set(c_ctx).at[1:1 + dec_batch].set(c)
    mod = _modulation(cond, w_mod, b_mod).reshape(depth * MOD_ROWS * 6, 1, d)
    lb_params = _lb_params(hgrn_lb_logits)
    params = dict(diff_lambda=diff_lambda, diff_norm_g=diff_norm_g, hgrn_norm_g=hgrn_norm_g, swa_sink=swa_sink,
                  ln1_g=ln1_g, ln1_b=ln1_b, ln2_g=ln2_g, ln2_b=ln2_b)
    wts = tuple(w.astype(BF16) for w in (w_in, w_out, w_gate_up, w_down))
    tabs = (_rope_tables(dec_seq, DA_QK // 2), _rope_tables(dec_seq, HEAD // 2))
    consts_c = _scan_constants(min(HG_CHUNK, seq))
    consts_l = _scan_constants(min(HG_CHUNK, dec_seq))

    y_p = x_prompt.reshape(batch * seq, d)
    y_s = x_sample.reshape(dec_batch * dec_seq, d)
    heads, kvh = geom_c["da_heads"], geom_c["sw_kv"]
    new_caches = tuple(jnp.zeros((batch, depth, nh, seq, HEAD), F32) for nh in (heads, heads, kvh, kvh))
    new_caches += (jnp.zeros((batch, depth, 2, geom_c["hg_heads"], HEAD, HEAD), F32),)
    caches = (cache_diff_k, cache_diff_v, cache_swa_k, cache_swa_v, state_hgrn)
    for l in range(depth):
        row_c = lambda m, tm, j, l=l: (l * MOD_ROWS) * 6 + j
        row_l = lambda m, tm, j, l=l: (l * MOD_ROWS + 1 + (m * tm) // dec_seq) * 6 + j
        y_p, new_caches = _layer(y_p, geom_c, l, mod, row_c, wts, params, lb_params, consts_c, None, None,
                                 new_caches, alpha)
        y_s, _ = _layer(y_s, geom_l, l, mod, row_l, wts, params, lb_params, consts_l, tabs, caches, None, alpha)
    return (y_p.reshape(batch, seq, d), y_s.reshape(dec_batch, dec_seq, d)) + new_caches
```

```python
import functools
import math

import numpy as np
import jax
import jax.numpy as jnp
from jax import lax
from jax.experimental import pallas as pl
from jax.experimental.pallas import tpu as pltpu

F32 = jnp.float32
BF16 = jnp.bfloat16

GRID_W = 64
ROPE_BASE = 10000.0
LN_EPS = 1e-5
RMS_EPS = 1e-6
NEG_INF = -1e30
LB_FLOOR = 1e-30
LOG2_E = math.log2(math.e)
HEAD = 128
DA_QK = 64
SW_GROUP = 4
SW_BLOCK = 128
SW_WINDOW = 128
HG_CHUNK = 64
HG_UNROLL = 4
TM_PROJ = 256
TM_OUT = 512
OUT_SPLIT = 1
SW_PER_STEP = 4
TM_FFN = 512
MOD_ROWS = 8
VMEM_LIMIT = 56 * 1024 * 1024


def _cparams(*sem):
    return pltpu.CompilerParams(dimension_semantics=sem, vmem_limit_bytes=VMEM_LIMIT)


def _dot(a, b):
    return jnp.dot(a, b, preferred_element_type=F32)


def _dot_t(a, b):
    return lax.dot_general(a, b, (((1,), (1,)), ((), ())), preferred_element_type=F32)


def _silu(x):
    return x / (1.0 + jnp.exp(-x))


def _layer_norm(y, g, b):
    mu = jnp.mean(y, axis=-1, keepdims=True)
    d = y - mu
    var = jnp.mean(d * d, axis=-1, keepdims=True)
    return d * lax.rsqrt(var + LN_EPS) * g + b


def _rms_norm(o, g):
    ms = jnp.mean(o * o, axis=-1, keepdims=True)
    return o * lax.rsqrt(ms + RMS_EPS) * g


def _rope(x, c, sa, sb, w):
    return x * c + pltpu.roll(x, HEAD - w, 1) * sa + pltpu.roll(x, w, 1) * sb


def _pick_tile(n, target):
    t = min(n, target)
    while n % t or t % 128:
        t -= 128
    return t


def _lb_kernel(logit_ref, loglb_ref, log1m_ref, onem_ref):
    depth = logit_ref.shape[0]
    x = [logit_ref[l] for l in range(depth)]
    m = functools.reduce(jnp.maximum, x)
    e = [jnp.exp(xi - m) for xi in x]
    tot = functools.reduce(lambda a, b: a + b, e)
    w = [ei / tot for ei in e]
    acc = jnp.zeros_like(w[0])
    for l in range(depth):
        acc = acc + w[l]
        lb = acc - w[0]
        loglb_ref[l] = jnp.log(jnp.maximum(lb, LB_FLOOR))
        log1m_ref[l] = jnp.log1p(-lb)
        onem_ref[l] = 1.0 - lb


def _lb_params(logits):
    shp = jax.ShapeDtypeStruct(logits.shape, F32)
    return pl.pallas_call(_lb_kernel, out_shape=(shp, shp, shp), name="hgrn_lb_params")(logits)


def _mod_kernel(c_ref, w_ref, b_ref, o_ref):
    s = _silu(c_ref[...])
    s_hi = s.astype(BF16)
    s_lo = (s - s_hi.astype(F32)).astype(BF16)
    w = w_ref[...]
    w_hi = w.astype(BF16)
    w_lo = (w - w_hi.astype(F32)).astype(BF16)
    o_ref[...] = _dot(s_hi, w_hi) + _dot(s_lo, w_hi) + _dot(s_hi, w_lo) + b_ref[...]


def _modulation(cond, w_mod, b_mod):
    depth, d, d6 = w_mod.shape
    tn = _pick_tile(d6, 1024)
    return pl.pallas_call(
        _mod_kernel,
        out_shape=jax.ShapeDtypeStruct((depth, MOD_ROWS, d6), F32),
        grid=(depth, d6 // tn),
        in_specs=[pl.BlockSpec((MOD_ROWS, d), lambda l, n: (0, 0)),
                  pl.BlockSpec((None, d, tn), lambda l, n: (l, 0, n)),
                  pl.BlockSpec((None, 1, tn), lambda l, n: (l, 0, n))],
        out_specs=pl.BlockSpec((None, MOD_ROWS, tn), lambda l, n: (l, 0, n)),
        compiler_params=_cparams("parallel", "parallel"),
        name="adaln_modulation",
    )(cond, w_mod, b_mod.reshape(depth, 1, d6))


def _proj_kernel(x_ref, sh_ref, sc_ref, w_ref, *rest, cache_cols):
    o_ref, mix_ref = rest[len(cache_cols)], rest[len(cache_cols) + 1]
    h = (x_ref[...] * (1.0 + sc_ref[0]) + sh_ref[0]).astype(BF16)
    o_ref[...] = _dot(h, w_ref[...])
    mix_ref[...] = jnp.zeros_like(mix_ref)
    for c_ref, (off, nh) in zip(rest[len(cache_cols) + 2:], cache_cols):
        for hd in range(nh):
            c_ref[hd] = o_ref[:, off + hd * HEAD:off + (hd + 1) * HEAD]


def _in_proj(x, mod, mod_row, w_in, layer, tm, d_mix, caches=None, cache_cols=()):
    t, d = x.shape
    d_in = w_in.shape[2]
    caches = tuple(caches or ())
    out_shape = ([jax.ShapeDtypeStruct((t, d_in), F32), jax.ShapeDtypeStruct((t, d_mix), BF16)]
                 + [jax.ShapeDtypeStruct(c.shape, c.dtype) for c in caches])
    out_specs = [pl.BlockSpec((tm, d_in), lambda m: (m, 0)), pl.BlockSpec((tm, d_mix), lambda m: (m, 0))]
    for c in caches:
        assert c.shape[3] == tm
        out_specs.append(pl.BlockSpec((None, None, c.shape[2], tm, HEAD), lambda m: (m, layer, 0, 0, 0)))
    res = pl.pallas_call(
        functools.partial(_proj_kernel, cache_cols=tuple(cache_cols)),
        out_shape=tuple(out_shape),
        grid=(t // tm,),
        in_specs=[pl.BlockSpec((tm, d), lambda m: (m, 0)),
                  pl.BlockSpec((1, 1, d), lambda m: (mod_row(m, tm, 0), 0, 0)),
                  pl.BlockSpec((1, 1, d), lambda m: (mod_row(m, tm, 1), 0, 0)),
                  pl.BlockSpec((None, d, d_in), lambda m: (layer, 0, 0), pipeline_mode=pl.Buffered(1))]
                 + [pl.BlockSpec(memory_space=pl.ANY)] * len(caches),
        out_specs=tuple(out_specs),
        input_output_aliases={4 + i: 2 + i for i in range(len(caches))},
        compiler_params=_cparams("parallel"),
        name="in_proj",
    )(x, mod, mod, w_in, *caches)
    return res[0], res[1], tuple(res[2:])


def _out_proj_kernel(mix_ref, w_ref, x_ref, g1_ref, lng_ref, lnb_ref, x1_ref, *, alpha):
    sub = x_ref.shape[0] // OUT_SPLIT
    for r in range(OUT_SPLIT):
        rows = slice(r * sub, (r + 1) * sub)
        mix = _dot(mix_ref[rows, :], w_ref[...])
        x1_ref[rows, :] = _layer_norm(alpha * x_ref[rows, :] + g1_ref[0] * mix, lng_ref[...], lnb_ref[...])


def _out_proj(mix, w_out, x, mod, mod_row, ln_g, ln_b, layer, tm, alpha):
    t, d = x.shape
    d_mix = mix.shape[1]
    depth = ln_g.shape[0]
    vec = pl.BlockSpec((None, 1, d), lambda m: (layer, 0, 0))
    return pl.pallas_call(
        functools.partial(_out_proj_kernel, alpha=alpha),
        out_shape=jax.ShapeDtypeStruct((t, d), F32),
        grid=(t // tm,),
        in_specs=[pl.BlockSpec((tm, d_mix), lambda m: (m, 0)),
                  pl.BlockSpec((None, d_mix, d), lambda m: (layer, 0, 0), pipeline_mode=pl.Buffered(1)),
                  pl.BlockSpec((tm, d), lambda m: (m, 0)),
                  pl.BlockSpec((1, 1, d), lambda m: (mod_row(m, tm, 2), 0, 0)),
                  vec, vec],
        out_specs=pl.BlockSpec((tm, d), lambda m: (m, 0)),
        compiler_params=_cparams("parallel"),
        name="out_proj_ln",
    )(mix, w_out, x, mod, ln_g.reshape(depth, 1, d), ln_b.reshape(depth, 1, d))


def _ffn_kernel(x_ref, sh2_ref, sc2_ref, wg_ref, wu_ref, wd_ref, g2_ref, lng_ref, lnb_ref, o_ref, h_ref, *, alpha):
    f = pl.program_id(1)

    @pl.when(f == 0)
    def _():
        h_ref[...] = (x_ref[...] * (1.0 + sc2_ref[0]) + sh2_ref[0]).astype(BF16)
        o_ref[...] = jnp.zeros_like(o_ref)

    h = h_ref[...]
    a = _dot(h, wg_ref[...])
    u = _dot(h, wu_ref[...])
    o_ref[...] += _dot((_silu(a) * u).astype(BF16), wd_ref[...])

    @pl.when(f == pl.num_programs(1) - 1)
    def _():
        y = alpha * x_ref[...] + g2_ref[0] * o_ref[...]
        o_ref[...] = _layer_norm(y, lng_ref[...], lnb_ref[...])


def _ffn(x1, w_gu, w_down, mod, mod_row, ln_g, ln_b, layer, tm, alpha):
    t, d = x1.shape
    d_ff = w_down.shape[1]
    depth = ln_g.shape[0]
    tf = _pick_tile(d_ff, 512)
    nf = d_ff // tf
    vec = pl.BlockSpec((None, 1, d), lambda m, f: (layer, 0, 0))
    row = lambda j: pl.BlockSpec((1, 1, d), lambda m, f: (mod_row(m, tm, j), 0, 0))
    return pl.pallas_call(
        functools.partial(_ffn_kernel, alpha=alpha),
        out_shape=jax.ShapeDtypeStruct((t, d), F32),
        grid=(t // tm, nf),
        in_specs=[pl.BlockSpec((tm, d), lambda m, f: (m, 0)),
                  row(3), row(4),
                  pl.BlockSpec((None, d, tf), lambda m, f: (layer, 0, f)),
                  pl.BlockSpec((None, d, tf), lambda m, f: (layer, 0, nf + f)),
                  pl.BlockSpec((None, tf, d), lambda m, f: (layer, f, 0)),
                  row(5), vec, vec],
        out_specs=pl.BlockSpec((tm, d), lambda m, f: (m, 0)),
        scratch_shapes=[pltpu.VMEM((tm, d), BF16)],
        compiler_params=_cparams("parallel", "arbitrary"),
        name="ffn_ln",
    )(x1, mod, mod, w_gu, w_gu, w_down, mod, ln_g.reshape(depth, 1, d), ln_b.reshape(depth, 1, d))


def _diff_attn_kernel(*refs, rope, cached, n_self, kb, lam_init, pipelined, n_tiles):
    it = iter(refs)
    q_ref, k_ref, v_ref = next(it), next(it), next(it)
    if rope:
        qc_ref, qsa_ref, qsb_ref, kc_ref, ksa_ref, ksb_ref = (next(it) for _ in range(6))
    if cached:
        ck_ref, cv_ref = next(it), next(it)
    lam_ref, g_ref, _, o_ref, kr_ref, vt_ref, m_ref, acc_ref = (next(it) for _ in range(8))
    s_refs = tuple(it)
    nblk = kr_ref.shape[0]
    nself = n_self // kb
    tq = q_ref.shape[0]
    step = pl.program_id(2)
    last = pl.num_programs(2) - 1

    def prepare_keys():
        for j in range(nself):
            rows = slice(j * kb, (j + 1) * kb)
            k = k_ref[rows, :]
            if rope:
                k = _rope(k, kc_ref[rows, :], ksa_ref[rows, :], ksb_ref[rows, :], DA_QK // 4)
            kr_ref[j] = k.astype(BF16)
            vt_ref[j] = v_ref[rows, :].T.astype(BF16)
        if cached:
            for j in range(nblk - nself):
                rows = slice(j * kb, (j + 1) * kb)
                kr_ref[nself + j] = ck_ref[rows, :].astype(BF16)
                vt_ref[nself + j] = cv_ref[rows, :].T.astype(BF16)

    def queries():
        q = q_ref[...]
        if rope:
            q = _rope(q, qc_ref[...], qsa_ref[...], qsb_ref[...], DA_QK // 4)
        q = q * (DA_QK ** -0.5 * LOG2_E)
        lane = lax.broadcasted_iota(jnp.int32, q.shape, 1)
        return (jnp.where(lane < DA_QK, q, 0.0).astype(BF16), jnp.where(lane >= DA_QK, q, 0.0).astype(BF16))

    def fold(x, op):
        return op(x.reshape(kb // 8, 8, tq), axis=0)

    def scores_block(j, qz, buf, ms):
        kblk = kr_ref[j]
        out = []
        for mp in range(2):
            s = _dot_t(kblk, qz[mp])
            buf[mp, j] = s
            out.append(jnp.maximum(ms[mp], fold(s, jnp.max)))
        return tuple(out)

    def values_block(j, buf, ms, ls):
        out = []
        for mp in range(2):
            p = jnp.exp2(buf[mp, j] - ms[mp])
            acc_ref[mp] += _dot(vt_ref[j], p.astype(BF16))
            out.append(ls[mp] + fold(p, jnp.sum))
        return tuple(out)

    neg = jnp.full((8, tq), -jnp.inf, F32)
    zero = jnp.zeros((8, tq), F32)

    def save_max(ms):
        for mp in range(2):
            m_ref[mp] = jnp.max(ms[mp], axis=0, keepdims=True)

    def finish(ls):
        l1, l2 = (jnp.sum(l, axis=0, keepdims=True) for l in ls)
        lp = lam_ref[...]
        lam = (jnp.exp(jnp.sum(lp[0:1] * lp[1:2], axis=-1, keepdims=True))
               - jnp.exp(jnp.sum(lp[2:3] * lp[3:4], axis=-1, keepdims=True)) + lam_init)
        ot = acc_ref[0] * (1.0 / l1) - acc_ref[1] * (lam / l2)
        ms_o = jnp.mean(ot * ot, axis=0, keepdims=True)
        ot = ot * lax.rsqrt(ms_o + RMS_EPS) * g_ref[...] * (1.0 - lam_init)
        o_ref[...] = ot.T.astype(BF16)

    if not pipelined:
        pl.when(step == 0)(prepare_keys)
        qz = queries()
        ms = (neg, neg)
        for j in range(nblk):
            ms = scores_block(j, qz, s_refs[0], ms)
        ms = tuple(jnp.max(m, axis=0, keepdims=True) for m in ms)
        acc_ref[...] = jnp.zeros_like(acc_ref)
        ls = (zero, zero)
        for j in range(nblk):
            ls = values_block(j, s_refs[0], ms, ls)
        finish(ls)
        return

    @pl.when(step == 0)
    def _():
        prepare_keys()
        qz = queries()
        ms = (neg, neg)
        for j in range(nblk):
            ms = scores_block(j, qz, s_refs[0], ms)
        save_max(ms)

    def interior(parity):
        prev = (m_ref[0], m_ref[1])
        qz = queries()
        acc_ref[...] = jnp.zeros_like(acc_ref)
        ms, ls = (neg, neg), (zero, zero)
        for j in range(nblk):
            ms = scores_block(j, qz, s_refs[parity], ms)
            ls = values_block(j, s_refs[1 - parity], prev, ls)
        save_max(ms)
        finish(ls)

    inside = (step > 0) & (step < last)
    pl.when(inside & (step % 2 == 1))(functools.partial(interior, 1))
    pl.when(inside & (step % 2 == 0))(functools.partial(interior, 0))

    @pl.when(step == last)
    def _():
        prev = (m_ref[0], m_ref[1])
        acc_ref[...] = jnp.zeros_like(acc_ref)
        ls = (zero, zero)
        for j in range(nblk):
            ls = values_block(j, s_refs[(n_tiles - 1) % 2], prev, ls)
        finish(ls)


def _diff_attn_seq_kernel(q_ref, k_ref, v_ref, lam_ref, g_ref, mix_ref, o_ref, *, heads, lam_init):
    del mix_ref
    n = q_ref.shape[0]
    lane = lax.broadcasted_iota(jnp.int32, (n, HEAD), 1)
    scores, vts = [], []
    for hd in range(heads):
        cols = slice(hd * HEAD, (hd + 1) * HEAD)
        q = q_ref[:, cols] * (DA_QK ** -0.5 * LOG2_E)
        k = k_ref[:, cols].astype(BF16)
        scores.append([_dot_t(k, jnp.where(lane < DA_QK, q, 0.0).astype(BF16)),
                       _dot_t(k, jnp.where(lane >= DA_QK, q, 0.0).astype(BF16))])
        vts.append(v_ref[:, cols].T.astype(BF16))
    probs = []
    for hd in range(heads):
        parts = []
        for s in scores[hd]:
            p = jnp.exp2(s - jnp.max(s, axis=0, keepdims=True))
            parts.append((p.astype(BF16), jnp.sum(p, axis=0, keepdims=True)))
        probs.append(parts)
    accs = [[_dot(vts[hd], pb) for pb, _ in probs[hd]] for hd in range(heads)]
    lp = lam_ref[...]
    lam = (jnp.exp(jnp.sum(lp[0:1] * lp[1:2], axis=-1, keepdims=True))
           - jnp.exp(jnp.sum(lp[2:3] * lp[3:4], axis=-1, keepdims=True)) + lam_init)
    for hd in range(heads):
        (_, l1), (_, l2) = probs[hd]
        ot = accs[hd][0] * (1.0 / l1) - accs[hd][1] * (lam / l2)
        ms_o = jnp.mean(ot * ot, axis=0, keepdims=True)
        ot = ot * lax.rsqrt(ms_o + RMS_EPS) * g_ref[...] * (1.0 - lam_init)
        o_ref[:, hd * HEAD:(hd + 1) * HEAD] = ot.T.astype(BF16)


def _diff_attn_seq(p, mix, geom, layer, diff_lambda, diff_g):
    n_seq, n = geom["n_seq"], geom["n"]
    heads = geom["da_heads"]
    depth = diff_g.shape[0]
    lam_init = 0.8 - 0.6 * math.exp(-0.3 * layer)
    blk = lambda j: pl.BlockSpec((n, heads * HEAD), lambda b: (b, j))
    return pl.pallas_call(
        functools.partial(_diff_attn_seq_kernel, heads=heads, lam_init=lam_init),
        out_shape=jax.ShapeDtypeStruct(mix.shape, mix.dtype),
        grid=(n_seq,),
        in_specs=[blk(0), blk(1), blk(2),
                  pl.BlockSpec((None, 4, DA_QK), lambda b: (layer, 0, 0)),
                  pl.BlockSpec((None, HEAD, 1), lambda b: (layer, 0, 0)),
                  pl.BlockSpec(memory_space=pl.ANY)],
        out_specs=blk(0),
        input_output_aliases={5: 0},
        compiler_params=_cparams("parallel"),
        name="diff_attention_seq",
    )(p, p, p, diff_lambda, diff_g.reshape(depth, HEAD, 1), mix)


def _diff_attn(p, mix, geom, layer, diff_lambda, diff_g, rope_tabs, cache_k, cache_v):
    if rope_tabs is None and cache_k is None and geom["n"] <= 256:
        return _diff_attn_seq(p, mix, geom, layer, diff_lambda, diff_g)
    n_seq, n, t = geom["n_seq"], geom["n"], geom["t"]
    heads = geom["da_heads"]
    rope = rope_tabs is not None
    cached = cache_k is not None
    tq = min(n, 256)
    kb = min(n, 256)
    nq = n // tq
    n_ctx = cache_k.shape[3] if cached else 0
    nblk = (n + n_ctx) // kb
    assert n % kb == 0 and n_ctx % kb == 0
    k_off, v_off = heads, 2 * heads
    depth = diff_g.shape[0]
    lam_init = 0.8 - 0.6 * math.exp(-0.3 * layer)

    pipelined = nq > 1
    q_tile = (lambda i: jnp.minimum(i, nq - 1)) if pipelined else (lambda i: i)
    o_tile = (lambda i: jnp.maximum(i - 1, 0)) if pipelined else (lambda i: i)
    in_specs = [pl.BlockSpec((tq, HEAD), lambda b, h, i: (b * nq + q_tile(i), h)),
                pl.BlockSpec((n, HEAD), lambda b, h, i: (b, k_off + h)),
                pl.BlockSpec((n, HEAD), lambda b, h, i: (b, v_off + h))]
    args = [p, p, p]
    if rope:
        in_specs += [pl.BlockSpec((tq, HEAD), lambda b, h, i: (q_tile(i), 0))] * 3
        in_specs += [pl.BlockSpec((n, HEAD), lambda b, h, i: (0, 0))] * 3
        args += list(rope_tabs) * 2
    if cached:
        spec = pl.BlockSpec((None, None, None, n_ctx, HEAD), lambda b, h, i: (b, layer, h, 0, 0))
        in_specs += [spec, spec]
        args += [cache_k, cache_v]
    in_specs += [pl.BlockSpec((None, 4, DA_QK), lambda b, h, i: (layer, 0, 0)),
                 pl.BlockSpec((None, HEAD, 1), lambda b, h, i: (layer, 0, 0)),
                 pl.BlockSpec(memory_space=pl.ANY)]
    args += [diff_lambda, diff_g.reshape(depth, HEAD, 1), mix]
    return pl.pallas_call(
        functools.partial(_diff_attn_kernel, rope=rope, cached=cached, n_self=n, kb=kb, lam_init=lam_init,
                          pipelined=pipelined, n_tiles=nq),
        out_shape=jax.ShapeDtypeStruct(mix.shape, mix.dtype),
        grid=(n_seq, heads, nq + 1 if pipelined else nq),
        in_specs=in_specs,
        input_output_aliases={len(args) - 1: 0},
        out_specs=pl.BlockSpec((tq, HEAD), lambda b, h, i: (b * nq + o_tile(i), h)),
        scratch_shapes=[pltpu.VMEM((nblk, kb, HEAD), BF16), pltpu.VMEM((nblk, HEAD, kb), BF16),
                        pltpu.VMEM((2, 1, tq), F32), pltpu.VMEM((2, HEAD, tq), F32)]
                       + [pltpu.VMEM((2, nblk, kb, tq), F32)] * (2 if pipelined else 1),
        compiler_params=_cparams("parallel", "parallel", "arbitrary"),
        name="diff_attention",
    )(*args)


def _scan_constants(c):
    levels = int(math.log2(c))
    t = np.arange(c)[:, None]
    s = np.arange(c)[None, :]
    tri = (s <= t).astype(np.float32)
    masks = []
    for j in range(levels):
        m = c >> (j + 1)
        base = (t // (2 * m)) * (2 * m)
        sbase = (s // (2 * m)) * (2 * m)
        masks.append((sbase == base) & ((t - base) >= m) & ((s - sbase) < m))
    masks.append(s == t)
    masks = np.stack(masks).astype(np.float32)
    flip = lambda a: a.reshape(-1, c, c)[:, ::-1, ::-1].reshape(a.shape)
    return (jnp.asarray(tri, BF16), jnp.asarray(flip(tri), BF16),
            jnp.asarray(masks, F32), jnp.asarray(flip(masks), F32))


def _level_exponents(cum, d, chunk):
    row = lax.broadcasted_iota(jnp.int32, cum.shape, 0)
    sub = lax.broadcasted_iota(jnp.int32, (8, HEAD), 0)
    out = []
    m = chunk // 2
    while m >= 1:
        in_second = (row & m) != 0
        q_side = in_second if d == 0 else jnp.logical_not(in_second)
        if m >= 8:
            pieces = []
            for b0 in range(0, chunk, 2 * m):
                r = b0 + m - 1 + d
                bnd = jnp.broadcast_to(cum[r:r + 1, :], (m, HEAD))
                first, second = cum[b0:b0 + m, :], cum[b0 + m:b0 + 2 * m, :]
                pieces += [bnd - first, second - bnd] if d == 0 else [first - bnd, bnd - second]
            x = jnp.concatenate(pieces, axis=0)
        elif m == 1:
            neighbour = pltpu.roll(cum, 1 if d == 0 else chunk - 1, 0)
            x = jnp.where(q_side, cum - neighbour, 0.0)
        else:
            tiles = []
            for v in range(chunk // 8):
                if m == 4:
                    r = 8 * v + 3 + d
                    tiles.append(jnp.broadcast_to(cum[r:r + 1, :], (8, HEAD)))
                else:
                    lo = jnp.broadcast_to(cum[8 * v + 1 + d:8 * v + 2 + d, :], (8, HEAD))
                    hi = jnp.broadcast_to(cum[8 * v + 5 + d:8 * v + 6 + d, :], (8, HEAD))
                    tiles.append(jnp.where(sub < 4, lo, hi))
            bnd = jnp.concatenate(tiles, axis=0)
            x = jnp.where(q_side, cum - bnd, bnd - cum)
        out.append(x)
        m //= 2
    return out


def _hgrn_kernel(*refs, n, chunk, unroll, has_state, emit_state):
    it = iter(refs)
    q_ref, zf_ref, zb_ref, v_ref, hg_ref = (next(it) for _ in range(5))
    loglb_ref, log1m_ref, onem_ref, g_ref = (next(it) for _ in range(4))
    tri_refs = (next(it), next(it))
    mask_refs = (next(it), next(it))
    s0_ref = next(it) if has_state else None
    next(it)
    if emit_state:
        next(it)
    o_ref = next(it)
    st_ref = next(it) if emit_state else None
    obuf_refs = (next(it), next(it))
    s_ref = next(it)

    levels = int(math.log2(chunk))
    nchunks = n // chunk
    z_refs = (zf_ref, zb_ref)

    for d in range(2):
        s_ref[d] = s0_ref[d].T if has_state else jnp.zeros((HEAD, HEAD), F32)

    def gates(d, start):
        rows = pl.ds(start, chunk)
        z = z_refs[d][rows, :]
        q = _silu(q_ref[rows, :])
        v = v_ref[rows, :].astype(BF16)
        e = jnp.exp(-jnp.abs(z))
        log_sig = jnp.minimum(z, 0.0) - jnp.log1p(e)
        sig_neg = jnp.where(z >= 0.0, e, 1.0) / (1.0 + e)
        a = loglb_ref[d:d + 1, :]
        b = log1m_ref[d:d + 1, :] + log_sig
        g = jnp.maximum(a, b) + jnp.log1p(jnp.exp(-jnp.abs(a - b)))
        k = onem_ref[d:d + 1, :] * sig_neg
        g = g * LOG2_E
        g_hi = g.astype(BF16)
        g_lo = (g - g_hi.astype(F32)).astype(BF16)
        cum = _dot(tri_refs[d][...], jnp.concatenate([g_hi, g_lo], axis=1))
        return q, k, v, cum[:, :HEAD] + cum[:, HEAD:]

    def body(i, carry):
        chains = []
        for u in range(unroll):
            c = i * unroll + u
            chains.append((0, pl.multiple_of(c * chunk, chunk)))
            chains.append((1, pl.multiple_of((nchunks - 1 - c) * chunk, chunk)))
        work = [gates(d, start) for d, start in chains]
        qk16 = [(q.astype(BF16), k.astype(BF16)) for q, k, _, _ in work]
        atts = [mask_refs[d][levels] * _dot_t(qb, kb) for (d, _), (qb, kb) in zip(chains, qk16)]
        expo = [_level_exponents(cum, d, chunk) for (d, _), (_, _, _, cum) in zip(chains, work)]
        for j in range(levels):
            for ci, ((d, _), (qb, kb)) in enumerate(zip(chains, qk16)):
                fac = jnp.exp2(expo[ci][j]).astype(BF16)
                atts[ci] = atts[ci] + mask_refs[d][j] * _dot_t(qb * fac, kb * fac)
        intra, delta, q_in, decay = [], [], [], []
        for (d, _), (q, k, v, cum), att in zip(chains, work, atts):
            last = cum[chunk - 1:chunk] if d == 0 else cum[0:1]
            kt = (k * jnp.exp2(last - cum)).astype(BF16)
            intra.append(_dot(att.astype(BF16), v))
            delta.append(lax.dot_general(v, kt, (((0,), (0,)), ((), ())), preferred_element_type=F32))
            q_in.append((q * jnp.exp2(cum)).astype(BF16))
            decay.append(jnp.exp2(last))
        st = [s_ref[0], s_ref[1]]
        for ci, (d, start) in enumerate(chains):
            obuf_refs[d][pl.ds(start, chunk), :] = intra[ci] + _dot_t(q_in[ci], st[d].astype(BF16))
            st[d] = decay[ci] * st[d] + delta[ci]
        s_ref[0] = st[0]
        s_ref[1] = st[1]
        return carry

    lax.fori_loop(0, nchunks // unroll, body, 0)

    o = obuf_refs[0][...] + obuf_refs[1][...]
    o_ref[...] = (_rms_norm(o, g_ref[...]) * _silu(hg_ref[...])).astype(BF16)
    if emit_state:
        for d in range(2):
            st_ref[d] = s_ref[d].T


def _hgrn(p, mix, geom, layer, lb_params, hgrn_g, consts, state, new_state=None):
    n_seq, n, t = geom["n_seq"], geom["n"], geom["t"]
    heads = geom["hg_heads"]
    mix_col = geom["da_heads"]
    base = 3 * geom["da_heads"]
    has_state = state is not None
    emit_state = not has_state
    depth = hgrn_g.shape[0]
    chunk = min(HG_CHUNK, n)
    col = lambda j: pl.BlockSpec((n, HEAD), lambda b, h: (b, base + j * heads + h))
    lbspec = pl.BlockSpec((None, 2, HEAD), lambda b, h: (layer, 0, h))
    const_specs = [pl.BlockSpec(c.shape, lambda b, h, nd=c.ndim: (0,) * nd) for c in consts]
    in_specs = [col(0), col(1), col(2), col(3), col(4), lbspec, lbspec, lbspec,
                pl.BlockSpec((None, 1, HEAD), lambda b, h: (layer, 0, 0))] + const_specs
    args = [p] * 5 + list(lb_params) + [hgrn_g.reshape(depth, 1, HEAD)] + list(consts)
    if has_state:
        in_specs.append(pl.BlockSpec((None, None, 2, None, HEAD, HEAD), lambda b, h: (b, layer, 0, h, 0, 0)))
        args.append(state)
    in_specs.append(pl.BlockSpec(memory_space=pl.ANY))
    args.append(mix)
    aliases = {len(args) - 1: 0}
    out_shape = [jax.ShapeDtypeStruct(mix.shape, mix.dtype)]
    out_specs = [pl.BlockSpec((n, HEAD), lambda b, h: (b, mix_col + h))]
    if emit_state:
        in_specs.append(pl.BlockSpec(memory_space=pl.ANY))
        args.append(new_state)
        aliases[len(args) - 1] = 1
        out_shape.append(jax.ShapeDtypeStruct(new_state.shape, new_state.dtype))
        out_specs.append(pl.BlockSpec((None, None, 2, None, HEAD, HEAD), lambda b, h: (b, layer, 0, h, 0, 0)))
    res = pl.pallas_call(
        functools.partial(_hgrn_kernel, n=n, chunk=chunk, unroll=HG_UNROLL if (n // chunk) % HG_UNROLL == 0 else 1,
                          has_state=has_state, emit_state=emit_state),
        out_shape=tuple(out_shape),
        grid=(n_seq, heads),
        in_specs=in_specs,
        out_specs=tuple(out_specs),
        scratch_shapes=[pltpu.VMEM((n, HEAD), F32), pltpu.VMEM((n, HEAD), F32),
                        pltpu.VMEM((2, HEAD, HEAD), F32)],
        input_output_aliases=aliases,
        compiler_params=_cparams("parallel", "parallel"),
        name="hgrn2_scan",
    )(*args)
    return res if emit_state else (res[0], None)


def _swa_ctx_kernel(q_ref, k_ref, v_ref, sink_ref, mix_ref, o_ref, *, layer, kvh):
    del mix_ref
    n = q_ref.shape[0]
    lane = lax.broadcasted_iota(jnp.int32, (1, SW_GROUP * n), 1)

    def fold(x, op):
        return op(op(x.reshape(x.shape[0] // 8, 8, x.shape[1]), axis=0), axis=0, keepdims=True)

    scores, sinks = [], []
    for kv in range(kvh):
        q_all = jnp.concatenate(
            [(q_ref[:, (kv * SW_GROUP + g) * HEAD:(kv * SW_GROUP + g + 1) * HEAD] * (HEAD ** -0.5)).astype(BF16)
             for g in range(SW_GROUP)], axis=0)
        scores.append(_dot_t(k_ref[:, kv * HEAD:(kv + 1) * HEAD].astype(BF16), q_all))
        sink = jnp.zeros((1, SW_GROUP * n), F32)
        for g in range(SW_GROUP):
            sink = jnp.where(lane // n == g, sink_ref[layer, kv * SW_GROUP + g], sink)
        sinks.append(sink)
    probs = []
    for s, sink in zip(scores, sinks):
        m = jnp.maximum(fold(s, jnp.max), sink)
        p = jnp.exp(s - m)
        inv = 1.0 / (fold(p, jnp.sum) + jnp.exp(sink - m))
        probs.append((p * inv).astype(BF16))
    outs = [_dot(v_ref[:, kv * HEAD:(kv + 1) * HEAD].T.astype(BF16), probs[kv]) for kv in range(kvh)]
    for kv in range(kvh):
        for g in range(SW_GROUP):
            hd = kv * SW_GROUP + g
            o_ref[:, hd * HEAD:(hd + 1) * HEAD] = outs[kv][:, g * n:(g + 1) * n].T.astype(BF16)


def _swa_ctx(p, mix, geom, layer, sink):
    n_seq, n, t = geom["n_seq"], geom["n"], geom["t"]
    kvh = geom["sw_kv"]
    qw = kvh * SW_GROUP * HEAD
    kw = kvh * HEAD
    q_blk = geom["cq_off"] // qw
    k_blk = geom["ck_off"] // kw
    mix_blk = (geom["da_heads"] + geom["hg_heads"]) * HEAD // qw
    assert geom["cq_off"] % qw == 0 and geom["ck_off"] % kw == 0
    return pl.pallas_call(
        functools.partial(_swa_ctx_kernel, layer=layer, kvh=kvh),
        out_shape=jax.ShapeDtypeStruct(mix.shape, mix.dtype),
        grid=(n_seq,),
        in_specs=[pl.BlockSpec((n, qw), lambda b: (b, q_blk)),
                  pl.BlockSpec((n, kw), lambda b: (b, k_blk)),
                  pl.BlockSpec((n, kw), lambda b: (b, k_blk + 1)),
                  pl.BlockSpec(memory_space=pltpu.SMEM),
                  pl.BlockSpec(memory_space=pl.ANY)],
        out_specs=pl.BlockSpec((n, qw), lambda b: (b, mix_blk)),
        input_output_aliases={4: 0},
        compiler_params=_cparams("parallel"),
        name="sink_attention",
    )(p, p, p, sink, mix)


def _band_masks(n):
    r = np.arange(3 * SW_BLOCK)[:, None]
    i = (np.arange(SW_GROUP * SW_BLOCK) % SW_BLOCK)[None, :]
    window = np.abs(SW_BLOCK + i - r) <= SW_WINDOW
    not_before = r >= SW_BLOCK
    not_after = r < 2 * SW_BLOCK
    variants = [window, window & not_before, window & not_after, window & not_before & not_after]
    return jnp.asarray(np.stack(variants).astype(np.float32))


def _swa_lat_kernel(q_ref, k_ref, v_ref, qc_ref, qsa_ref, qsb_ref, kc_ref, ksa_ref, ksb_ref, ck_ref, cv_ref,
                    mask_ref, sink_ref, mix_ref, o_ref, kr_ref, vt_ref, kctx_ref, vctxt_ref, *, layer, n):
    kv = pl.program_id(1)
    step = pl.program_id(2)
    nb = n // SW_BLOCK
    w = HEAD // 4
    band = 3 * SW_BLOCK
    per_step = q_ref.shape[0] // SW_BLOCK

    @pl.when(step == 0)
    def _():
        zeros = jnp.zeros((SW_BLOCK, HEAD), BF16)
        for j in (0, nb + 1):
            kr_ref[j] = zeros
            vt_ref[j] = zeros

        def fill(j, carry):
            rows = pl.ds(pl.multiple_of(j * SW_BLOCK, SW_BLOCK), SW_BLOCK)
            k = _rope(k_ref[rows, :], kc_ref[rows, :], ksa_ref[rows, :], ksb_ref[rows, :], w)
            kr_ref[j + 1] = k.astype(BF16)
            vt_ref[j + 1] = v_ref[rows, :].T.astype(BF16)
            return carry

        lax.fori_loop(0, nb, fill, 0)
        kctx_ref[...] = ck_ref[...].astype(BF16)
        vctxt_ref[...] = cv_ref[...].T.astype(BF16)

    lane = lax.broadcasted_iota(jnp.int32, (1, SW_GROUP * SW_BLOCK), 1)
    sink = jnp.zeros((1, SW_GROUP * SW_BLOCK), F32)
    for g in range(SW_GROUP):
        sink = jnp.where(lane // SW_BLOCK == g, sink_ref[layer, kv * SW_GROUP + g], sink)

    def fold(x, op):
        return op(op(x.reshape(x.shape[0] // 8, 8, x.shape[1]), axis=0), axis=0, keepdims=True)

    blocks = [step * per_step + c for c in range(per_step)]
    rows = [slice(c * SW_BLOCK, (c + 1) * SW_BLOCK) for c in range(per_step)]
    scores = []
    for qb, r in zip(blocks, rows):
        qc, qsa, qsb = qc_ref[r, :], qsa_ref[r, :], qsb_ref[r, :]
        q_all = jnp.concatenate(
            [(_rope(q_ref[r, g * HEAD:(g + 1) * HEAD], qc, qsa, qsb, w) * (HEAD ** -0.5)).astype(BF16)
             for g in range(SW_GROUP)], axis=0)
        k_all = jnp.concatenate([kr_ref[qb], kr_ref[qb + 1], kr_ref[qb + 2], kctx_ref[...]], axis=0)
        scores.append(_dot_t(k_all, q_all))
    probs = []
    for qb, s in zip(blocks, scores):
        variant = jnp.where(qb == 0, 1, 0) + jnp.where(qb == nb - 1, 2, 0)
        s_band = jnp.where(mask_ref[variant] > 0.5, s[:band], NEG_INF)
        s_ctx = s[band:]
        m = jnp.maximum(jnp.maximum(fold(s_band, jnp.max), fold(s_ctx, jnp.max)), sink)
        p_band = jnp.exp(s_band - m)
        p_ctx = jnp.exp(s_ctx - m)
        inv = 1.0 / (fold(p_band, jnp.sum) + fold(p_ctx, jnp.sum) + jnp.exp(sink - m))
        probs.append(jnp.concatenate([p_band * inv, p_ctx * inv], axis=0).astype(BF16))
    outs = []
    for qb, p_all in zip(blocks, probs):
        vt_all = jnp.concatenate([vt_ref[qb], vt_ref[qb + 1], vt_ref[qb + 2], vctxt_ref[...]], axis=1)
        outs.append(_dot(vt_all, p_all))
    for r, ot in zip(rows, outs):
        for g in range(SW_GROUP):
            o_ref[r, g * HEAD:(g + 1) * HEAD] = ot[:, g * SW_BLOCK:(g + 1) * SW_BLOCK].T.astype(BF16)


def _swa_lat(p, mix, geom, layer, sink, rope_tabs, cache_k, cache_v):
    n_seq, n, t = geom["n_seq"], geom["n"], geom["t"]
    kvh = geom["sw_kv"]
    qw = SW_GROUP * HEAD
    q_blk = geom["cq_off"] // qw
    k_blk = geom["ck_off"] // HEAD
    v_blk = k_blk + kvh
    mix_blk = (geom["da_heads"] + geom["hg_heads"]) * HEAD // qw
    nb = n // SW_BLOCK
    per_step = SW_PER_STEP if nb % SW_PER_STEP == 0 else 1
    nsteps = nb // per_step
    tq = per_step * SW_BLOCK
    n_ctx = cache_k.shape[3]
    masks = _band_masks(n)
    cspec = pl.BlockSpec((None, None, None, n_ctx, HEAD), lambda b, kv, i: (b, layer, kv, 0, 0))
    return pl.pallas_call(
        functools.partial(_swa_lat_kernel, layer=layer, n=n),
        out_shape=jax.ShapeDtypeStruct(mix.shape, mix.dtype),
        grid=(n_seq, kvh, nsteps),
        in_specs=[pl.BlockSpec((tq, qw), lambda b, kv, i: (b * nsteps + i, q_blk + kv)),
                  pl.BlockSpec((n, HEAD), lambda b, kv, i: (b, k_blk + kv)),
                  pl.BlockSpec((n, HEAD), lambda b, kv, i: (b, v_blk + kv))]
                 + [pl.BlockSpec((tq, HEAD), lambda b, kv, i: (i, 0))] * 3
                 + [pl.BlockSpec((n, HEAD), lambda b, kv, i: (0, 0))] * 3
                 + [cspec, cspec, pl.BlockSpec(masks.shape, lambda b, kv, i: (0, 0, 0)),
                    pl.BlockSpec(memory_space=pltpu.SMEM), pl.BlockSpec(memory_space=pl.ANY)],
        out_specs=pl.BlockSpec((tq, qw), lambda b, kv, i: (b * nsteps + i, mix_blk + kv)),
        scratch_shapes=[pltpu.VMEM((nb + 2, SW_BLOCK, HEAD), BF16), pltpu.VMEM((nb + 2, HEAD, SW_BLOCK), BF16),
                        pltpu.VMEM((n_ctx, HEAD), BF16), pltpu.VMEM((HEAD, n_ctx), BF16)],
        input_output_aliases={13: 0},
        compiler_params=_cparams("parallel", "parallel", "arbitrary"),
        name="banded_sink_attention",
    )(p, p, p, *rope_tabs, *rope_tabs, cache_k, cache_v, masks, sink, mix)


def _rope_tables(n, half):
    h = half // 2
    pos = jnp.arange(n)
    inv = ROPE_BASE ** (-jnp.arange(h, dtype=F32) / h)
    zero = jnp.zeros((n, h), F32)
    c, sa, sb = [], [], []
    for axis_pos in (pos // GRID_W, pos % GRID_W):
        ang = axis_pos.astype(F32)[:, None] * inv[None, :]
        cos, sin = jnp.cos(ang), jnp.sin(ang)
        c += [cos, cos]
        sa += [-sin, zero]
        sb += [zero, sin]
    reps = HEAD // (2 * half)
    cat = lambda parts: jnp.tile(jnp.concatenate(parts, axis=1), (1, reps))
    return cat(c), cat(sa), cat(sb)


def _geometry(n_seq, n, d_model):
    da_heads = hg_heads = d_model // 512
    sw_heads = d_model // 256
    sw_kv = sw_heads // SW_GROUP
    cq_off = (3 * da_heads + 5 * hg_heads) * HEAD
    return dict(n_seq=n_seq, n=n, t=n_seq * n, da_heads=da_heads, hg_heads=hg_heads, sw_kv=sw_kv,
                cq_off=cq_off, ck_off=cq_off + sw_heads * HEAD, d_mix=(da_heads + hg_heads + sw_heads) * HEAD)


def _token_tiles(n):
    return _pick_tile(n, TM_PROJ), _pick_tile(n, TM_OUT), _pick_tile(n, TM_FFN)


def _layer(x, geom, layer, mod, mod_row, wts, params, lb_params, scan_consts, tabs, caches, new_caches, alpha):
    w_in, w_out, w_gu, w_down = wts
    tm_proj, tm_out, tm_ffn = _token_tiles(geom["t"] if caches is None else geom["n"])
    if caches is None:
        heads, kvh = geom["da_heads"], geom["sw_kv"]
        cols = ((heads * HEAD, heads), (2 * heads * HEAD, heads), (geom["ck_off"], kvh),
                (geom["ck_off"] + kvh * HEAD, kvh))
        p, mix, kv_caches = _in_proj(x, mod, mod_row, w_in, layer, geom["n"], geom["d_mix"], new_caches[:4], cols)
        mix = _diff_attn(p, mix, geom, layer, params["diff_lambda"], params["diff_norm_g"], None, None, None)
        mix, st = _hgrn(p, mix, geom, layer, lb_params, params["hgrn_norm_g"], scan_consts, None, new_caches[4])
        mix = _swa_ctx(p, mix, geom, layer, params["swa_sink"])
        new_caches = kv_caches + (st,)
    else:
        ck_d, cv_d, ck_s, cv_s, state = caches
        p, mix, _ = _in_proj(x, mod, mod_row, w_in, layer, tm_proj, geom["d_mix"])
        mix = _diff_attn(p, mix, geom, layer, params["diff_lambda"], params["diff_norm_g"], tabs[0], ck_d, cv_d)
        mix, _ = _hgrn(p, mix, geom, layer, lb_params, params["hgrn_norm_g"], scan_consts, state)
        mix = _swa_lat(p, mix, geom, layer, params["swa_sink"], tabs[1], ck_s, cv_s)
    x1 = _out_proj(mix, w_out, x, mod, mod_row, params["ln1_g"], params["ln1_b"], layer, tm_out, alpha)
    y = _ffn(x1, w_gu, w_down, mod, mod_row, params["ln2_g"], params["ln2_b"], layer, tm_ffn, alpha)
    return y, new_caches


def kernel(x_prompt, x_sample, cache_diff_k, cache_diff_v, cache_swa_k, cache_swa_v, state_hgrn, c, c_ctx, w_mod,
           b_mod, w_in, w_out, diff_lambda, diff_norm_g, hgrn_lb_logits, hgrn_norm_g, swa_sink, ln1_g, ln1_b, ln2_g,
           ln2_b, w_gate_up, w_down):
    batch, seq, d = x_prompt.shape
    dec_batch, dec_seq, _ = x_sample.shape
    depth = w_mod.shape[0]
    alpha = (2 * depth) ** 0.25
    geom_c = _geometry(batch, seq, d)
    geom_l = _geometry(dec_batch, dec_seq, d)
    assert 1 + dec_batch <= MOD_ROWS

    cond = jnp.zeros((MOD_ROWS, d), F32).at[0].set(c_ctx).at[1:1 + dec_batch].set(c)
    mod = _modulation(cond, w_mod, b_mod).reshape(depth * MOD_ROWS * 6, 1, d)
    lb_params = _lb_params(hgrn_lb_logits)
    params = dict(diff_lambda=diff_lambda, diff_norm_g=diff_norm_g, hgrn_norm_g=hgrn_norm_g, swa_sink=swa_sink,
                  ln1_g=ln1_g, ln1_b=ln1_b, ln2_g=ln2_g, ln2_b=ln2_b)
    wts = tuple(w.astype(BF16) for w in (w_in, w_out, w_gate_up, w_down))
    tabs = (_rope_tables(dec_seq, DA_QK // 2), _rope_tables(dec_seq, HEAD // 2))
    consts_c = _scan_constants(min(HG_CHUNK, seq))
    consts_l = _scan_constants(min(HG_CHUNK, dec_seq))

    y_p = x_prompt.reshape(batch * seq, d)
    y_s = x_sample.reshape(dec_batch * dec_seq, d)
    heads, kvh = geom_c["da_heads"], geom_c["sw_kv"]
    new_caches = tuple(jnp.zeros((batch, depth, nh, seq, HEAD), F32) for nh in (heads, heads, kvh, kvh))
    new_caches += (jnp.zeros((batch, depth, 2, geom_c["hg_heads"], HEAD, HEAD), F32),)
    caches = (cache_diff_k, cache_diff_v, cache_swa_k, cache_swa_v, state_hgrn)
    for l in range(depth):
        row_c = lambda m, tm, j, l=l: (l * MOD_ROWS) * 6 + j
        row_l = lambda m, tm, j, l=l: (l * MOD_ROWS + 1 + (m * tm) // dec_seq) * 6 + j
        y_p, new_caches = _layer(y_p, geom_c, l, mod, row_c, wts, params, lb_params, consts_c, None, None,
                                 new_caches, alpha)
        y_s, _ = _layer(y_s, geom_l, l, mod, row_l, wts, params, lb_params, consts_l, tabs, caches, None, alpha)
    return (y_p.reshape(batch, seq, d), y_s.reshape(dec_batch, dec_seq, d)) + new_caches
```

```python
import functools
import math

import numpy as np
import jax
import jax.numpy as jnp
from jax import lax
from jax.experimental import pallas as pl
from jax.experimental.pallas import tpu as pltpu

F32 = jnp.float32
BF16 = jnp.bfloat16

GRID_W = 64
ROPE_BASE = 10000.0
LN_EPS = 1e-5
RMS_EPS = 1e-6
NEG_INF = -1e30
LB_FLOOR = 1e-30
LOG2_E = math.log2(math.e)
HEAD = 128
DA_QK = 64
SW_GROUP = 4
SW_BLOCK = 128
SW_WINDOW = 128
HG_CHUNK = 64
HG_UNROLL = 4
TM_PROJ = 256
TM_OUT = 512
SW_PER_STEP = 4
TM_FFN = 512
MOD_ROWS = 8
VMEM_LIMIT = 56 * 1024 * 1024


def _cparams(*sem):
    return pltpu.CompilerParams(dimension_semantics=sem, vmem_limit_bytes=VMEM_LIMIT)


def _dot(a, b):
    return jnp.dot(a, b, preferred_element_type=F32)


def _dot_t(a, b):
    return lax.dot_general(a, b, (((1,), (1,)), ((), ())), preferred_element_type=F32)


def _silu(x):
    return x / (1.0 + jnp.exp2(x * (-LOG2_E)))


def _layer_norm(y, g, b):
    mu = jnp.mean(y, axis=-1, keepdims=True)
    d = y - mu
    var = jnp.mean(d * d, axis=-1, keepdims=True)
    return d * lax.rsqrt(var + LN_EPS) * g + b


def _rms_norm(o, g):
    ms = jnp.mean(o * o, axis=-1, keepdims=True)
    return o * lax.rsqrt(ms + RMS_EPS) * g


def _rope(x, c, sa, sb, w):
    return x * c + pltpu.roll(x, HEAD - w, 1) * sa + pltpu.roll(x, w, 1) * sb


def _pick_tile(n, target):
    t = min(n, target)
    while n % t or t % 128:
        t -= 128
    return t


def _lb_kernel(logit_ref, loglb_ref, log1m_ref, onem_ref):
    depth = logit_ref.shape[0]
    x = [logit_ref[l] for l in range(depth)]
    m = functools.reduce(jnp.maximum, x)
    e = [jnp.exp(xi - m) for xi in x]
    tot = functools.reduce(lambda a, b: a + b, e)
    w = [ei / tot for ei in e]
    acc = jnp.zeros_like(w[0])
    for l in range(depth):
        acc = acc + w[l]
        lb = acc - w[0]
        loglb_ref[l] = jnp.log(jnp.maximum(lb, LB_FLOOR))
        log1m_ref[l] = jnp.log1p(-lb)
        onem_ref[l] = 1.0 - lb


def _lb_params(logits):
    shp = jax.ShapeDtypeStruct(logits.shape, F32)
    return pl.pallas_call(_lb_kernel, out_shape=(shp, shp, shp), name="hgrn_lb_params")(logits)


def _mod_kernel(c_ref, w_ref, b_ref, o_ref):
    s = _silu(c_ref[...])
    s_hi = s.astype(BF16)
    s_lo = (s - s_hi.astype(F32)).astype(BF16)
    w = w_ref[...]
    w_hi = w.astype(BF16)
    w_lo = (w - w_hi.astype(F32)).astype(BF16)
    o_ref[...] = _dot(s_hi, w_hi) + _dot(s_lo, w_hi) + _dot(s_hi, w_lo) + b_ref[...]


def _modulation(cond, w_mod, b_mod):
    depth, d, d6 = w_mod.shape
    tn = _pick_tile(d6, 1024)
    return pl.pallas_call(
        _mod_kernel,
        out_shape=jax.ShapeDtypeStruct((depth, MOD_ROWS, d6), F32),
        grid=(depth, d6 // tn),
        in_specs=[pl.BlockSpec((MOD_ROWS, d), lambda l, n: (0, 0)),
                  pl.BlockSpec((None, d, tn), lambda l, n: (l, 0, n)),
                  pl.BlockSpec((None, 1, tn), lambda l, n: (l, 0, n))],
        out_specs=pl.BlockSpec((None, MOD_ROWS, tn), lambda l, n: (l, 0, n)),
        compiler_params=_cparams("parallel", "parallel"),
        name="adaln_modulation",
    )(cond, w_mod, b_mod.reshape(depth, 1, d6))


def _proj_kernel(x_ref, sh_ref, sc_ref, w_ref, *rest, cache_cols):
    o_ref, mix_ref = rest[len(cache_cols)], rest[len(cache_cols) + 1]
    h = (x_ref[...] * (1.0 + sc_ref[0]) + sh_ref[0]).astype(BF16)
    o_ref[...] = _dot(h, w_ref[...])
    mix_ref[...] = jnp.zeros_like(mix_ref)
    for c_ref, (off, nh) in zip(rest[len(cache_cols) + 2:], cache_cols):
        for hd in range(nh):
            c_ref[hd] = o_ref[:, off + hd * HEAD:off + (hd + 1) * HEAD]


def _in_proj(x, mod, mod_row, w_in, layer, tm, d_mix, caches=None, cache_cols=()):
    t, d = x.shape
    d_in = w_in.shape[2]
    caches = tuple(caches or ())
    out_shape = ([jax.ShapeDtypeStruct((t, d_in), F32), jax.ShapeDtypeStruct((t, d_mix), BF16)]
                 + [jax.ShapeDtypeStruct(c.shape, c.dtype) for c in caches])
    out_specs = [pl.BlockSpec((tm, d_in), lambda m: (m, 0)), pl.BlockSpec((tm, d_mix), lambda m: (m, 0))]
    for c in caches:
        assert c.shape[3] == tm
        out_specs.append(pl.BlockSpec((None, None, c.shape[2], tm, HEAD), lambda m: (m, layer, 0, 0, 0)))
    res = pl.pallas_call(
        functools.partial(_proj_kernel, cache_cols=tuple(cache_cols)),
        out_shape=tuple(out_shape),
        grid=(t // tm,),
        in_specs=[pl.BlockSpec((tm, d), lambda m: (m, 0)),
                  pl.BlockSpec((1, 1, d), lambda m: (mod_row(m, tm, 0), 0, 0)),
                  pl.BlockSpec((1, 1, d), lambda m: (mod_row(m, tm, 1), 0, 0)),
                  pl.BlockSpec((None, d, d_in), lambda m: (layer, 0, 0), pipeline_mode=pl.Buffered(1))]
                 + [pl.BlockSpec(memory_space=pl.ANY)] * len(caches),
        out_specs=tuple(out_specs),
        input_output_aliases={4 + i: 2 + i for i in range(len(caches))},
        compiler_params=_cparams("parallel"),
        name="in_proj",
    )(x, mod, mod, w_in, *caches)
    return res[0], res[1], tuple(res[2:])


def _out_proj_kernel(mix_ref, w_ref, x_ref, g1_ref, lng_ref, lnb_ref, x1_ref, *, alpha):
    mix = _dot(mix_ref[...], w_ref[...])
    x1_ref[...] = _layer_norm(alpha * x_ref[...] + g1_ref[0] * mix, lng_ref[...], lnb_ref[...])


def _out_proj(mix, w_out, x, mod, mod_row, ln_g, ln_b, layer, tm, alpha):
    t, d = x.shape
    d_mix = mix.shape[1]
    depth = ln_g.shape[0]
    vec = pl.BlockSpec((None, 1, d), lambda m: (layer, 0, 0))
    return pl.pallas_call(
        functools.partial(_out_proj_kernel, alpha=alpha),
        out_shape=jax.ShapeDtypeStruct((t, d), F32),
        grid=(t // tm,),
        in_specs=[pl.BlockSpec((tm, d_mix), lambda m: (m, 0)),
                  pl.BlockSpec((None, d_mix, d), lambda m: (layer, 0, 0), pipeline_mode=pl.Buffered(1)),
                  pl.BlockSpec((tm, d), lambda m: (m, 0)),
                  pl.BlockSpec((1, 1, d), lambda m: (mod_row(m, tm, 2), 0, 0)),
                  vec, vec],
        out_specs=pl.BlockSpec((tm, d), lambda m: (m, 0)),
        compiler_params=_cparams("parallel"),
        name="out_proj_ln",
    )(mix, w_out, x, mod, ln_g.reshape(depth, 1, d), ln_b.reshape(depth, 1, d))


def _ffn_kernel(x_ref, sh2_ref, sc2_ref, wg_ref, wu_ref, wd_ref, g2_ref, lng_ref, lnb_ref, o_ref, h_ref, *, alpha):
    f = pl.program_id(1)

    @pl.when(f == 0)
    def _():
        h_ref[...] = (x_ref[...] * (1.0 + sc2_ref[0]) + sh2_ref[0]).astype(BF16)
        o_ref[...] = jnp.zeros_like(o_ref)

    h = h_ref[...]
    a = _dot(h, wg_ref[...])
    u = _dot(h, wu_ref[...])
    o_ref[...] += _dot((_silu(a) * u).astype(BF16), wd_ref[...])

    @pl.when(f == pl.num_programs(1) - 1)
    def _():
        y = alpha * x_ref[...] + g2_ref[0] * o_ref[...]
        o_ref[...] = _layer_norm(y, lng_ref[...], lnb_ref[...])


def _ffn(x1, w_gu, w_down, mod, mod_row, ln_g, ln_b, layer, tm, alpha):
    t, d = x1.shape
    d_ff = w_down.shape[1]
    depth = ln_g.shape[0]
    tf = _pick_tile(d_ff, 512)
    nf = d_ff // tf
    vec = pl.BlockSpec((None, 1, d), lambda m, f: (layer, 0, 0))
    row = lambda j: pl.BlockSpec((1, 1, d), lambda m, f: (mod_row(m, tm, j), 0, 0))
    return pl.pallas_call(
        functools.partial(_ffn_kernel, alpha=alpha),
        out_shape=jax.ShapeDtypeStruct((t, d), F32),
        grid=(t // tm, nf),
        in_specs=[pl.BlockSpec((tm, d), lambda m, f: (m, 0)),
                  row(3), row(4),
                  pl.BlockSpec((None, d, tf), lambda m, f: (layer, 0, f)),
                  pl.BlockSpec((None, d, tf), lambda m, f: (layer, 0, nf + f)),
                  pl.BlockSpec((None, tf, d), lambda m, f: (layer, f, 0)),
                  row(5), vec, vec],
        out_specs=pl.BlockSpec((tm, d), lambda m, f: (m, 0)),
        scratch_shapes=[pltpu.VMEM((tm, d), BF16)],
        compiler_params=_cparams("parallel", "arbitrary"),
        name="ffn_ln",
    )(x1, mod, mod, w_gu, w_gu, w_down, mod, ln_g.reshape(depth, 1, d), ln_b.reshape(depth, 1, d))


def _diff_attn_kernel(*refs, rope, cached, n_self, kb, lam_init, pipelined, n_tiles):
    it = iter(refs)
    q_ref, k_ref, v_ref = next(it), next(it), next(it)
    if rope:
        qc_ref, qsa_ref, qsb_ref, kc_ref, ksa_ref, ksb_ref = (next(it) for _ in range(6))
    if cached:
        ck_ref, cv_ref = next(it), next(it)
    lam_ref, g_ref, _, o_ref, kr_ref, vt_ref, m_ref, acc_ref = (next(it) for _ in range(8))
    s_refs = tuple(it)
    nblk = kr_ref.shape[0]
    nself = n_self // kb
    tq = q_ref.shape[0]
    step = pl.program_id(2)
    last = pl.num_programs(2) - 1

    def prepare_keys():
        for j in range(nself):
            rows = slice(j * kb, (j + 1) * kb)
            k = k_ref[rows, :]
            if rope:
                k = _rope(k, kc_ref[rows, :], ksa_ref[rows, :], ksb_ref[rows, :], DA_QK // 4)
            kr_ref[j] = k.astype(BF16)
            vt_ref[j] = v_ref[rows, :].T.astype(BF16)
        if cached:
            for j in range(nblk - nself):
                rows = slice(j * kb, (j + 1) * kb)
                kr_ref[nself + j] = ck_ref[rows, :].astype(BF16)
                vt_ref[nself + j] = cv_ref[rows, :].T.astype(BF16)

    def queries():
        q = q_ref[...]
        if rope:
            q = _rope(q, qc_ref[...], qsa_ref[...], qsb_ref[...], DA_QK // 4)
        q = q * (DA_QK ** -0.5 * LOG2_E)
        lane = lax.broadcasted_iota(jnp.int32, q.shape, 1)
        return (jnp.where(lane < DA_QK, q, 0.0).astype(BF16), jnp.where(lane >= DA_QK, q, 0.0).astype(BF16))

    def fold(x, op):
        return op(x.reshape(kb // 8, 8, tq), axis=0)

    def scores_block(j, qz, buf, ms):
        kblk = kr_ref[j]
        out = []
        for mp in range(2):
            s = _dot_t(kblk, qz[mp])
            buf[mp, j] = s
            out.append(jnp.maximum(ms[mp], fold(s, jnp.max)))
        return tuple(out)

    def values_block(j, buf, ms, ls):
        out = []
        for mp in range(2):
            p = jnp.exp2(buf[mp, j] - ms[mp])
            acc_ref[mp] += _dot(vt_ref[j], p.astype(BF16))
            out.append(ls[mp] + fold(p, jnp.sum))
        return tuple(out)

    neg = jnp.full((8, tq), -jnp.inf, F32)
    zero = jnp.zeros((8, tq), F32)

    def save_max(ms):
        for mp in range(2):
            m_ref[mp] = jnp.max(ms[mp], axis=0, keepdims=True)

    def finish(ls):
        l1, l2 = (jnp.sum(l, axis=0, keepdims=True) for l in ls)
        lp = lam_ref[...]
        lam = (jnp.exp(jnp.sum(lp[0:1] * lp[1:2], axis=-1, keepdims=True))
               - jnp.exp(jnp.sum(lp[2:3] * lp[3:4], axis=-1, keepdims=True)) + lam_init)
        ot = acc_ref[0] * (1.0 / l1) - acc_ref[1] * (lam / l2)
        ms_o = jnp.mean(ot * ot, axis=0, keepdims=True)
        ot = ot * lax.rsqrt(ms_o + RMS_EPS) * g_ref[...] * (1.0 - lam_init)
        o_ref[...] = ot.T.astype(BF16)

    if not pipelined:
        pl.when(step == 0)(prepare_keys)
        qz = queries()
        ms = (neg, neg)
        for j in range(nblk):
            ms = scores_block(j, qz, s_refs[0], ms)
        ms = tuple(jnp.max(m, axis=0, keepdims=True) for m in ms)
        acc_ref[...] = jnp.zeros_like(acc_ref)
        ls = (zero, zero)
        for j in range(nblk):
            ls = values_block(j, s_refs[0], ms, ls)
        finish(ls)
        return

    @pl.when(step == 0)
    def _():
        prepare_keys()
        qz = queries()
        ms = (neg, neg)
        for j in range(nblk):
            ms = scores_block(j, qz, s_refs[0], ms)
        save_max(ms)

    def interior(parity):
        prev = (m_ref[0], m_ref[1])
        qz = queries()
        acc_ref[...] = jnp.zeros_like(acc_ref)
        ms, ls = (neg, neg), (zero, zero)
        for j in range(nblk):
            ms = scores_block(j, qz, s_refs[parity], ms)
            ls = values_block(j, s_refs[1 - parity], prev, ls)
        save_max(ms)
        finish(ls)

    inside = (step > 0) & (step < last)
    pl.when(inside & (step % 2 == 1))(functools.partial(interior, 1))
    pl.when(inside & (step % 2 == 0))(functools.partial(interior, 0))

    @pl.when(step == last)
    def _():
        prev = (m_ref[0], m_ref[1])
        acc_ref[...] = jnp.zeros_like(acc_ref)
        ls = (zero, zero)
        for j in range(nblk):
            ls = values_block(j, s_refs[(n_tiles - 1) % 2], prev, ls)
        finish(ls)


def _diff_attn_seq_kernel(q_ref, k_ref, v_ref, lam_ref, g_ref, mix_ref, o_ref, *, heads, lam_init):
    del mix_ref
    n = q_ref.shape[0]
    lane = lax.broadcasted_iota(jnp.int32, (n, HEAD), 1)
    scores, vts = [], []
    for hd in range(heads):
        cols = slice(hd * HEAD, (hd + 1) * HEAD)
        q = q_ref[:, cols] * (DA_QK ** -0.5 * LOG2_E)
        k = k_ref[:, cols].astype(BF16)
        scores.append([_dot_t(k, jnp.where(lane < DA_QK, q, 0.0).astype(BF16)),
                       _dot_t(k, jnp.where(lane >= DA_QK, q, 0.0).astype(BF16))])
        vts.append(v_ref[:, cols].T.astype(BF16))
    probs = []
    for hd in range(heads):
        parts = []
        for s in scores[hd]:
            p = jnp.exp2(s - jnp.max(s, axis=0, keepdims=True))
            parts.append((p.astype(BF16), jnp.sum(p, axis=0, keepdims=True)))
        probs.append(parts)
    accs = [[_dot(vts[hd], pb) for pb, _ in probs[hd]] for hd in range(heads)]
    lp = lam_ref[...]
    lam = (jnp.exp(jnp.sum(lp[0:1] * lp[1:2], axis=-1, keepdims=True))
           - jnp.exp(jnp.sum(lp[2:3] * lp[3:4], axis=-1, keepdims=True)) + lam_init)
    for hd in range(heads):
        (_, l1), (_, l2) = probs[hd]
        ot = accs[hd][0] * (1.0 / l1) - accs[hd][1] * (lam / l2)
        ms_o = jnp.mean(ot * ot, axis=0, keepdims=True)
        ot = ot * lax.rsqrt(ms_o + RMS_EPS) * g_ref[...] * (1.0 - lam_init)
        o_ref[:, hd * HEAD:(hd + 1) * HEAD] = ot.T.astype(BF16)


def _diff_attn_seq(p, mix, geom, layer, diff_lambda, diff_g):
    n_seq, n = geom["n_seq"], geom["n"]
    heads = geom["da_heads"]
    depth = diff_g.shape[0]
    lam_init = 0.8 - 0.6 * math.exp(-0.3 * layer)
    blk = lambda j: pl.BlockSpec((n, heads * HEAD), lambda b: (b, j))
    return pl.pallas_call(
        functools.partial(_diff_attn_seq_kernel, heads=heads, lam_init=lam_init),
        out_shape=jax.ShapeDtypeStruct(mix.shape, mix.dtype),
        grid=(n_seq,),
        in_specs=[blk(0), blk(1), blk(2),
                  pl.BlockSpec((None, 4, DA_QK), lambda b: (layer, 0, 0)),
                  pl.BlockSpec((None, HEAD, 1), lambda b: (layer, 0, 0)),
                  pl.BlockSpec(memory_space=pl.ANY)],
        out_specs=blk(0),
        input_output_aliases={5: 0},
        compiler_params=_cparams("parallel"),
        name="diff_attention_seq",
    )(p, p, p, diff_lambda, diff_g.reshape(depth, HEAD, 1), mix)


def _diff_attn(p, mix, geom, layer, diff_lambda, diff_g, rope_tabs, cache_k, cache_v):
    if rope_tabs is None and cache_k is None and geom["n"] <= 256:
        return _diff_attn_seq(p, mix, geom, layer, diff_lambda, diff_g)
    n_seq, n, t = geom["n_seq"], geom["n"], geom["t"]
    heads = geom["da_heads"]
    rope = rope_tabs is not None
    cached = cache_k is not None
    tq = min(n, 256)
    kb = min(n, 256)
    nq = n // tq
    n_ctx = cache_k.shape[3] if cached else 0
    nblk = (n + n_ctx) // kb
    assert n % kb == 0 and n_ctx % kb == 0
    k_off, v_off = heads, 2 * heads
    depth = diff_g.shape[0]
    lam_init = 0.8 - 0.6 * math.exp(-0.3 * layer)

    pipelined = nq > 1
    q_tile = (lambda i: jnp.minimum(i, nq - 1)) if pipelined else (lambda i: i)
    o_tile = (lambda i: jnp.maximum(i - 1, 0)) if pipelined else (lambda i: i)
    in_specs = [pl.BlockSpec((tq, HEAD), lambda b, h, i: (b * nq + q_tile(i), h)),
                pl.BlockSpec((n, HEAD), lambda b, h, i: (b, k_off + h)),
                pl.BlockSpec((n, HEAD), lambda b, h, i: (b, v_off + h))]
    args = [p, p, p]
    if rope:
        in_specs += [pl.BlockSpec((tq, HEAD), lambda b, h, i: (q_tile(i), 0))] * 3
        in_specs += [pl.BlockSpec((n, HEAD), lambda b, h, i: (0, 0))] * 3
        args += list(rope_tabs) * 2
    if cached:
        spec = pl.BlockSpec((None, None, None, n_ctx, HEAD), lambda b, h, i: (b, layer, h, 0, 0))
        in_specs += [spec, spec]
        args += [cache_k, cache_v]
    in_specs += [pl.BlockSpec((None, 4, DA_QK), lambda b, h, i: (layer, 0, 0)),
                 pl.BlockSpec((None, HEAD, 1), lambda b, h, i: (layer, 0, 0)),
                 pl.BlockSpec(memory_space=pl.ANY)]
    args += [diff_lambda, diff_g.reshape(depth, HEAD, 1), mix]
    return pl.pallas_call(
        functools.partial(_diff_attn_kernel, rope=rope, cached=cached, n_self=n, kb=kb, lam_init=lam_init,
                          pipelined=pipelined, n_tiles=nq),
        out_shape=jax.ShapeDtypeStruct(mix.shape, mix.dtype),
        grid=(n_seq, heads, nq + 1 if pipelined else nq),
        in_specs=in_specs,
        input_output_aliases={len(args) - 1: 0},
        out_specs=pl.BlockSpec((tq, HEAD), lambda b, h, i: (b * nq + o_tile(i), h)),
        scratch_shapes=[pltpu.VMEM((nblk, kb, HEAD), BF16), pltpu.VMEM((nblk, HEAD, kb), BF16),
                        pltpu.VMEM((2, 1, tq), F32), pltpu.VMEM((2, HEAD, tq), F32)]
                       + [pltpu.VMEM((2, nblk, kb, tq), F32)] * (2 if pipelined else 1),
        compiler_params=_cparams("parallel", "parallel", "arbitrary"),
        name="diff_attention",
    )(*args)


def _scan_constants(c):
    levels = int(math.log2(c))
    t = np.arange(c)[:, None]
    s = np.arange(c)[None, :]
    tri = (s <= t).astype(np.float32)
    masks = []
    for j in range(levels):
        m = c >> (j + 1)
        base = (t // (2 * m)) * (2 * m)
        sbase = (s // (2 * m)) * (2 * m)
        masks.append((sbase == base) & ((t - base) >= m) & ((s - sbase) < m))
    masks.append(s == t)
    masks = np.stack(masks).astype(np.float32)
    flip = lambda a: a.reshape(-1, c, c)[:, ::-1, ::-1].reshape(a.shape)
    return (jnp.asarray(tri, BF16), jnp.asarray(flip(tri), BF16),
            jnp.asarray(masks, F32), jnp.asarray(flip(masks), F32))


def _level_exponents(cum, d, chunk):
    row = lax.broadcasted_iota(jnp.int32, cum.shape, 0)
    sub = lax.broadcasted_iota(jnp.int32, (8, HEAD), 0)
    out = []
    m = chunk // 2
    while m >= 1:
        in_second = (row & m) != 0
        q_side = in_second if d == 0 else jnp.logical_not(in_second)
        if m >= 8:
            pieces = []
            for b0 in range(0, chunk, 2 * m):
                r = b0 + m - 1 + d
                bnd = jnp.broadcast_to(cum[r:r + 1, :], (m, HEAD))
                first, second = cum[b0:b0 + m, :], cum[b0 + m:b0 + 2 * m, :]
                pieces += [bnd - first, second - bnd] if d == 0 else [first - bnd, bnd - second]
            x = jnp.concatenate(pieces, axis=0)
        elif m == 1:
            neighbour = pltpu.roll(cum, 1 if d == 0 else chunk - 1, 0)
            x = jnp.where(q_side, cum - neighbour, 0.0)
        else:
            tiles = []
            for v in range(chunk // 8):
                if m == 4:
                    r = 8 * v + 3 + d
                    tiles.append(jnp.broadcast_to(cum[r:r + 1, :], (8, HEAD)))
                else:
                    lo = jnp.broadcast_to(cum[8 * v + 1 + d:8 * v + 2 + d, :], (8, HEAD))
                    hi = jnp.broadcast_to(cum[8 * v + 5 + d:8 * v + 6 + d, :], (8, HEAD))
                    tiles.append(jnp.where(sub < 4, lo, hi))
            bnd = jnp.concatenate(tiles, axis=0)
            x = jnp.where(q_side, cum - bnd, bnd - cum)
        out.append(x)
        m //= 2
    return out


def _hgrn_kernel(*refs, n, chunk, unroll, has_state, emit_state):
    it = iter(refs)
    q_ref, zf_ref, zb_ref, v_ref, hg_ref = (next(it) for _ in range(5))
    loglb_ref, log1m_ref, onem_ref, g_ref = (next(it) for _ in range(4))
    tri_refs = (next(it), next(it))
    mask_refs = (next(it), next(it))
    s0_ref = next(it) if has_state else None
    next(it)
    if emit_state:
        next(it)
    o_ref = next(it)
    st_ref = next(it) if emit_state else None
    obuf_refs = (next(it), next(it))
    s_ref = next(it)

    levels = int(math.log2(chunk))
    nchunks = n // chunk
    z_refs = (zf_ref, zb_ref)

    for d in range(2):
        s_ref[d] = s0_ref[d].T if has_state else jnp.zeros((HEAD, HEAD), F32)

    def gates(d, start):
        rows = pl.ds(start, chunk)
        z = z_refs[d][rows, :]
        q = _silu(q_ref[rows, :])
        v = v_ref[rows, :].astype(BF16)
        z2 = z * LOG2_E
        e = jnp.exp2(-jnp.abs(z2))
        one_e = 1.0 + e
        log_sig = jnp.minimum(z2, 0.0) - jnp.log2(one_e)
        sig_neg = jnp.where(z >= 0.0, e, 1.0) / one_e
        a = loglb_ref[d:d + 1, :] * LOG2_E
        b = log1m_ref[d:d + 1, :] * LOG2_E + log_sig
        g = jnp.maximum(a, b) + jnp.log2(1.0 + jnp.exp2(-jnp.abs(a - b)))
        k = onem_ref[d:d + 1, :] * sig_neg
        g_hi = g.astype(BF16)
        g_lo = (g - g_hi.astype(F32)).astype(BF16)
        cum = _dot(tri_refs[d][...], jnp.concatenate([g_hi, g_lo], axis=1))
        return q, k, v, cum[:, :HEAD] + cum[:, HEAD:]

    def body(i, carry):
        chains = []
        for u in range(unroll):
            c = i * unroll + u
            chains.append((0, pl.multiple_of(c * chunk, chunk)))
            chains.append((1, pl.multiple_of((nchunks - 1 - c) * chunk, chunk)))
        work = [gates(d, start) for d, start in chains]
        qk16 = [(q.astype(BF16), k.astype(BF16)) for q, k, _, _ in work]
        atts = [mask_refs[d][levels] * _dot_t(qb, kb) for (d, _), (qb, kb) in zip(chains, qk16)]
        expo = [_level_exponents(cum, d, chunk) for (d, _), (_, _, _, cum) in zip(chains, work)]
        for j in range(levels):
            for ci, ((d, _), (qb, kb)) in enumerate(zip(chains, qk16)):
                fac = jnp.exp2(expo[ci][j]).astype(BF16)
                atts[ci] = atts[ci] + mask_refs[d][j] * _dot_t(qb * fac, kb * fac)
        intra, delta, q_in, decay = [], [], [], []
        for (d, _), (q, k, v, cum), att in zip(chains, work, atts):
            last = cum[chunk - 1:chunk] if d == 0 else cum[0:1]
            kt = (k * jnp.exp2(last - cum)).astype(BF16)
            intra.append(_dot(att.astype(BF16), v))
            delta.append(lax.dot_general(v, kt, (((0,), (0,)), ((), ())), preferred_element_type=F32))
            q_in.append((q * jnp.exp2(cum)).astype(BF16))
            decay.append(jnp.exp2(last))
        st = [s_ref[0], s_ref[1]]
        for ci, (d, start) in enumerate(chains):
            obuf_refs[d][pl.ds(start, chunk), :] = intra[ci] + _dot_t(q_in[ci], st[d].astype(BF16))
            st[d] = decay[ci] * st[d] + delta[ci]
        s_ref[0] = st[0]
        s_ref[1] = st[1]
        return carry

    lax.fori_loop(0, nchunks // unroll, body, 0)

    o = obuf_refs[0][...] + obuf_refs[1][...]
    o_ref[...] = (_rms_norm(o, g_ref[...]) * _silu(hg_ref[...])).astype(BF16)
    if emit_state:
        for d in range(2):
            st_ref[d] = s_ref[d].T


def _hgrn(p, mix, geom, layer, lb_params, hgrn_g, consts, state, new_state=None):
    n_seq, n, t = geom["n_seq"], geom["n"], geom["t"]
    heads = geom["hg_heads"]
    mix_col = geom["da_heads"]
    base = 3 * geom["da_heads"]
    has_state = state is not None
    emit_state = not has_state
    depth = hgrn_g.shape[0]
    chunk = min(HG_CHUNK, n)
    col = lambda j: pl.BlockSpec((n, HEAD), lambda b, h: (b, base + j * heads + h))
    lbspec = pl.BlockSpec((None, 2, HEAD), lambda b, h: (layer, 0, h))
    const_specs = [pl.BlockSpec(c.shape, lambda b, h, nd=c.ndim: (0,) * nd) for c in consts]
    in_specs = [col(0), col(1), col(2), col(3), col(4), lbspec, lbspec, lbspec,
                pl.BlockSpec((None, 1, HEAD), lambda b, h: (layer, 0, 0))] + const_specs
    args = [p] * 5 + list(lb_params) + [hgrn_g.reshape(depth, 1, HEAD)] + list(consts)
    if has_state:
        in_specs.append(pl.BlockSpec((None, None, 2, None, HEAD, HEAD), lambda b, h: (b, layer, 0, h, 0, 0)))
        args.append(state)
    in_specs.append(pl.BlockSpec(memory_space=pl.ANY))
    args.append(mix)
    aliases = {len(args) - 1: 0}
    out_shape = [jax.ShapeDtypeStruct(mix.shape, mix.dtype)]
    out_specs = [pl.BlockSpec((n, HEAD), lambda b, h: (b, mix_col + h))]
    if emit_state:
        in_specs.append(pl.BlockSpec(memory_space=pl.ANY))
        args.append(new_state)
        aliases[len(args) - 1] = 1
        out_shape.append(jax.ShapeDtypeStruct(new_state.shape, new_state.dtype))
        out_specs.append(pl.BlockSpec((None, None, 2, None, HEAD, HEAD), lambda b, h: (b, layer, 0, h, 0, 0)))
    res = pl.pallas_call(
        functools.partial(_hgrn_kernel, n=n, chunk=chunk, unroll=HG_UNROLL if (n // chunk) % HG_UNROLL == 0 else 1,
                          has_state=has_state, emit_state=emit_state),
        out_shape=tuple(out_shape),
        grid=(n_seq, heads),
        in_specs=in_specs,
        out_specs=tuple(out_specs),
        scratch_shapes=[pltpu.VMEM((n, HEAD), F32), pltpu.VMEM((n, HEAD), F32),
                        pltpu.VMEM((2, HEAD, HEAD), F32)],
        input_output_aliases=aliases,
        compiler_params=_cparams("parallel", "parallel"),
        name="hgrn2_scan",
    )(*args)
    return res if emit_state else (res[0], None)


def _swa_ctx_kernel(q_ref, k_ref, v_ref, sink_ref, mix_ref, o_ref, *, layer, kvh):
    del mix_ref
    n = q_ref.shape[0]
    lane = lax.broadcasted_iota(jnp.int32, (1, SW_GROUP * n), 1)

    def fold(x, op):
        return op(op(x.reshape(x.shape[0] // 8, 8, x.shape[1]), axis=0), axis=0, keepdims=True)

    scores, sinks = [], []
    for kv in range(kvh):
        q_all = jnp.concatenate(
            [(q_ref[:, (kv * SW_GROUP + g) * HEAD:(kv * SW_GROUP + g + 1) * HEAD] * (HEAD ** -0.5)).astype(BF16)
             for g in range(SW_GROUP)], axis=0)
        scores.append(_dot_t(k_ref[:, kv * HEAD:(kv + 1) * HEAD].astype(BF16), q_all))
        sink = jnp.zeros((1, SW_GROUP * n), F32)
        for g in range(SW_GROUP):
            sink = jnp.where(lane // n == g, sink_ref[layer, kv * SW_GROUP + g], sink)
        sinks.append(sink)
    probs = []
    for s, sink in zip(scores, sinks):
        m = jnp.maximum(fold(s, jnp.max), sink)
        p = jnp.exp(s - m)
        inv = 1.0 / (fold(p, jnp.sum) + jnp.exp(sink - m))
        probs.append((p * inv).astype(BF16))
    outs = [_dot(v_ref[:, kv * HEAD:(kv + 1) * HEAD].T.astype(BF16), probs[kv]) for kv in range(kvh)]
    for kv in range(kvh):
        for g in range(SW_GROUP):
            hd = kv * SW_GROUP + g
            o_ref[:, hd * HEAD:(hd + 1) * HEAD] = outs[kv][:, g * n:(g + 1) * n].T.astype(BF16)


def _swa_ctx(p, mix, geom, layer, sink):
    n_seq, n, t = geom["n_seq"], geom["n"], geom["t"]
    kvh = geom["sw_kv"]
    qw = kvh * SW_GROUP * HEAD
    kw = kvh * HEAD
    q_blk = geom["cq_off"] // qw
    k_blk = geom["ck_off"] // kw
    mix_blk = (geom["da_heads"] + geom["hg_heads"]) * HEAD // qw
    assert geom["cq_off"] % qw == 0 and geom["ck_off"] % kw == 0
    return pl.pallas_call(
        functools.partial(_swa_ctx_kernel, layer=layer, kvh=kvh),
        out_shape=jax.ShapeDtypeStruct(mix.shape, mix.dtype),
        grid=(n_seq,),
        in_specs=[pl.BlockSpec((n, qw), lambda b: (b, q_blk)),
                  pl.BlockSpec((n, kw), lambda b: (b, k_blk)),
                  pl.BlockSpec((n, kw), lambda b: (b, k_blk + 1)),
                  pl.BlockSpec(memory_space=pltpu.SMEM),
                  pl.BlockSpec(memory_space=pl.ANY)],
        out_specs=pl.BlockSpec((n, qw), lambda b: (b, mix_blk)),
        input_output_aliases={4: 0},
        compiler_params=_cparams("parallel"),
        name="sink_attention",
    )(p, p, p, sink, mix)


def _band_masks(n):
    r = np.arange(3 * SW_BLOCK)[:, None]
    i = (np.arange(SW_GROUP * SW_BLOCK) % SW_BLOCK)[None, :]
    window = np.abs(SW_BLOCK + i - r) <= SW_WINDOW
    not_before = r >= SW_BLOCK
    not_after = r < 2 * SW_BLOCK
    variants = [window, window & not_before, window & not_after, window & not_before & not_after]
    return jnp.asarray(np.stack(variants).astype(np.float32))


def _swa_lat_kernel(q_ref, k_ref, v_ref, qc_ref, qsa_ref, qsb_ref, kc_ref, ksa_ref, ksb_ref, ck_ref, cv_ref,
                    mask_ref, sink_ref, mix_ref, o_ref, kr_ref, vt_ref, kctx_ref, vctxt_ref, *, layer, n):
    kv = pl.program_id(1)
    step = pl.program_id(2)
    nb = n // SW_BLOCK
    w = HEAD // 4
    band = 3 * SW_BLOCK
    per_step = q_ref.shape[0] // SW_BLOCK

    @pl.when(step == 0)
    def _():
        zeros = jnp.zeros((SW_BLOCK, HEAD), BF16)
        for j in (0, nb + 1):
            kr_ref[j] = zeros
            vt_ref[j] = zeros

        def fill(j, carry):
            rows = pl.ds(pl.multiple_of(j * SW_BLOCK, SW_BLOCK), SW_BLOCK)
            k = _rope(k_ref[rows, :], kc_ref[rows, :], ksa_ref[rows, :], ksb_ref[rows, :], w)
            kr_ref[j + 1] = k.astype(BF16)
            vt_ref[j + 1] = v_ref[rows, :].T.astype(BF16)
            return carry

        lax.fori_loop(0, nb, fill, 0)
        kctx_ref[...] = ck_ref[...].astype(BF16)
        vctxt_ref[...] = cv_ref[...].T.astype(BF16)

    lane = lax.broadcasted_iota(jnp.int32, (1, SW_GROUP * SW_BLOCK), 1)
    sink = jnp.zeros((1, SW_GROUP * SW_BLOCK), F32)
    for g in range(SW_GROUP):
        sink = jnp.where(lane // SW_BLOCK == g, sink_ref[layer, kv * SW_GROUP + g], sink)

    def fold(x, op):
        return op(op(x.reshape(x.shape[0] // 8, 8, x.shape[1]), axis=0), axis=0, keepdims=True)

    blocks = [step * per_step + c for c in range(per_step)]
    rows = [slice(c * SW_BLOCK, (c + 1) * SW_BLOCK) for c in range(per_step)]
    scores = []
    for qb, r in zip(blocks, rows):
        qc, qsa, qsb = qc_ref[r, :], qsa_ref[r, :], qsb_ref[r, :]
        q_all = jnp.concatenate(
            [(_rope(q_ref[r, g * HEAD:(g + 1) * HEAD], qc, qsa, qsb, w) * (HEAD ** -0.5)).astype(BF16)
             for g in range(SW_GROUP)], axis=0)
        k_all = jnp.concatenate([kr_ref[qb], kr_ref[qb + 1], kr_ref[qb + 2], kctx_ref[...]], axis=0)
        scores.append(_dot_t(k_all, q_all))
    probs = []
    for qb, s in zip(blocks, scores):
        variant = jnp.where(qb == 0, 1, 0) + jnp.where(qb == nb - 1, 2, 0)
        s_band = jnp.where(mask_ref[variant] > 0.5, s[:band], NEG_INF)
        s_ctx = s[band:]
        m = jnp.maximum(jnp.maximum(fold(s_band, jnp.max), fold(s_ctx, jnp.max)), sink)
        p_band = jnp.exp(s_band - m)
        p_ctx = jnp.exp(s_ctx - m)
        inv = 1.0 / (fold(p_band, jnp.sum) + fold(p_ctx, jnp.sum) + jnp.exp(sink - m))
        probs.append(jnp.concatenate([p_band * inv, p_ctx * inv], axis=0).astype(BF16))
    outs = []
    for qb, p_all in zip(blocks, probs):
        vt_all = jnp.concatenate([vt_ref[qb], vt_ref[qb + 1], vt_ref[qb + 2], vctxt_ref[...]], axis=1)
        outs.append(_dot(vt_all, p_all))
    for r, ot in zip(rows, outs):
        for g in range(SW_GROUP):
            o_ref[r, g * HEAD:(g + 1) * HEAD] = ot[:, g * SW_BLOCK:(g + 1) * SW_BLOCK].T.astype(BF16)


def _swa_lat(p, mix, geom, layer, sink, rope_tabs, cache_k, cache_v):
    n_seq, n, t = geom["n_seq"], geom["n"], geom["t"]
    kvh = geom["sw_kv"]
    qw = SW_GROUP * HEAD
    q_blk = geom["cq_off"] // qw
    k_blk = geom["ck_off"] // HEAD
    v_blk = k_blk + kvh
    mix_blk = (geom["da_heads"] + geom["hg_heads"]) * HEAD // qw
    nb = n // SW_BLOCK
    per_step = SW_PER_STEP if nb % SW_PER_STEP == 0 else 1
    nsteps = nb // per_step
    tq = per_step * SW_BLOCK
    n_ctx = cache_k.shape[3]
    masks = _band_masks(n)
    cspec = pl.BlockSpec((None, None, None, n_ctx, HEAD), lambda b, kv, i: (b, layer, kv, 0, 0))
    return pl.pallas_call(
        functools.partial(_swa_lat_kernel, layer=layer, n=n),
        out_shape=jax.ShapeDtypeStruct(mix.shape, mix.dtype),
        grid=(n_seq, kvh, nsteps),
        in_specs=[pl.BlockSpec((tq, qw), lambda b, kv, i: (b * nsteps + i, q_blk + kv)),
                  pl.BlockSpec((n, HEAD), lambda b, kv, i: (b, k_blk + kv)),
                  pl.BlockSpec((n, HEAD), lambda b, kv, i: (b, v_blk + kv))]
                 + [pl.BlockSpec((tq, HEAD), lambda b, kv, i: (i, 0))] * 3
                 + [pl.BlockSpec((n, HEAD), lambda b, kv, i: (0, 0))] * 3
                 + [cspec, cspec, pl.BlockSpec(masks.shape, lambda b, kv, i: (0, 0, 0)),
                    pl.BlockSpec(memory_space=pltpu.SMEM), pl.BlockSpec(memory_space=pl.ANY)],
        out_specs=pl.BlockSpec((tq, qw), lambda b, kv, i: (b * nsteps + i, mix_blk + kv)),
        scratch_shapes=[pltpu.VMEM((nb + 2, SW_BLOCK, HEAD), BF16), pltpu.VMEM((nb + 2, HEAD, SW_BLOCK), BF16),
                        pltpu.VMEM((n_ctx, HEAD), BF16), pltpu.VMEM((HEAD, n_ctx), BF16)],
        input_output_aliases={13: 0},
        compiler_params=_cparams("parallel", "parallel", "arbitrary"),
        name="banded_sink_attention",
    )(p, p, p, *rope_tabs, *rope_tabs, cache_k, cache_v, masks, sink, mix)


def _rope_tables(n, half):
    h = half // 2
    pos = jnp.arange(n)
    inv = ROPE_BASE ** (-jnp.arange(h, dtype=F32) / h)
    zero = jnp.zeros((n, h), F32)
    c, sa, sb = [], [], []
    for axis_pos in (pos // GRID_W, pos % GRID_W):
        ang = axis_pos.astype(F32)[:, None] * inv[None, :]
        cos, sin = jnp.cos(ang), jnp.sin(ang)
        c += [cos, cos]
        sa += [-sin, zero]
        sb += [zero, sin]
    reps = HEAD // (2 * half)
    cat = lambda parts: jnp.tile(jnp.concatenate(parts, axis=1), (1, reps))
    return cat(c), cat(sa), cat(sb)


def _geometry(n_seq, n, d_model):
    da_heads = hg_heads = d_model // 512
    sw_heads = d_model // 256
    sw_kv = sw_heads // SW_GROUP
    cq_off = (3 * da_heads + 5 * hg_heads) * HEAD
    return dict(n_seq=n_seq, n=n, t=n_seq * n, da_heads=da_heads, hg_heads=hg_heads, sw_kv=sw_kv,
                cq_off=cq_off, ck_off=cq_off + sw_heads * HEAD, d_mix=(da_heads + hg_heads + sw_heads) * HEAD)


def _token_tiles(n):
    return _pick_tile(n, TM_PROJ), _pick_tile(n, TM_OUT), _pick_tile(n, TM_FFN)


def _layer(x, geom, layer, mod, mod_row, wts, params, lb_params, scan_consts, tabs, caches, new_caches, alpha):
    w_in, w_out, w_gu, w_down = wts
    tm_proj, tm_out, tm_ffn = _token_tiles(geom["t"] if caches is None else geom["n"])
    if caches is None:
        heads, kvh = geom["da_heads"], geom["sw_kv"]
        cols = ((heads * HEAD, heads), (2 * heads * HEAD, heads), (geom["ck_off"], kvh),
                (geom["ck_off"] + kvh * HEAD, kvh))
        p, mix, kv_caches = _in_proj(x, mod, mod_row, w_in, layer, geom["n"], geom["d_mix"], new_caches[:4], cols)
        mix = _diff_attn(p, mix, geom, layer, params["diff_lambda"], params["diff_norm_g"], None, None, None)
        mix, st = _hgrn(p, mix, geom, layer, lb_params, params["hgrn_norm_g"], scan_consts, None, new_caches[4])
        mix = _swa_ctx(p, mix, geom, layer, params["swa_sink"])
        new_caches = kv_caches + (st,)
    else:
        ck_d, cv_d, ck_s, cv_s, state = caches
        p, mix, _ = _in_proj(x, mod, mod_row, w_in, layer, tm_proj, geom["d_mix"])
        mix = _diff_attn(p, mix, geom, layer, params["diff_lambda"], params["diff_norm_g"], tabs[0], ck_d, cv_d)
        mix, _ = _hgrn(p, mix, geom, layer, lb_params, params["hgrn_norm_g"], scan_consts, state)
        mix = _swa_lat(p, mix, geom, layer, params["swa_sink"], tabs[1], ck_s, cv_s)
    x1 = _out_proj(mix, w_out, x, mod, mod_row, params["ln1_g"], params["ln1_b"], layer, tm_out, alpha)
    y = _ffn(x1, w_gu, w_down, mod, mod_row, params["ln2_g"], params["ln2_b"], layer, tm_ffn, alpha)
    return y, new_caches


def kernel(x_prompt, x_sample, cache_diff_k, cache_diff_v, cache_swa_k, cache_swa_v, state_hgrn, c, c_ctx, w_mod,
           b_mod, w_in, w_out, diff_lambda, diff_norm_g, hgrn_lb_logits, hgrn_norm_g, swa_sink, ln1_g, ln1_b, ln2_g,
           ln2_b, w_gate_up, w_down):
    batch, seq, d = x_prompt.shape
    dec_batch, dec_seq, _ = x_sample.shape
    depth = w_mod.shape[0]
    alpha = (2 * depth) ** 0.25
    geom_c = _geometry(batch, seq, d)
    geom_l = _geometry(dec_batch, dec_seq, d)
    assert 1 + dec_batch <= MOD_ROWS

    cond = jnp.zeros((MOD_ROWS, d), F32).at[0].set(c_ctx).at[1:1 + dec_batch].set(c)
    mod = _modulation(cond, w_mod, b_mod).reshape(depth * MOD_ROWS * 6, 1, d)
    lb_params = _lb_params(hgrn_lb_logits)
    params = dict(diff_lambda=diff_lambda, diff_norm_g=diff_norm_g, hgrn_norm_g=hgrn_norm_g, swa_sink=swa_sink,
                  ln1_g=ln1_g, ln1_b=ln1_b, ln2_g=ln2_g, ln2_b=ln2_b)
    wts = tuple(w.astype(BF16) for w in (w_in, w_out, w_gate_up, w_down))
    tabs = (_rope_tables(dec_seq, DA_QK // 2), _rope_tables(dec_seq, HEAD // 2))
    consts_c = _scan_constants(min(HG_CHUNK, seq))
    consts_l = _scan_constants(min(HG_CHUNK, dec_seq))

    y_p = x_prompt.reshape(batch * seq, d)
    y_s = x_sample.reshape(dec_batch * dec_seq, d)
    heads, kvh = geom_c["da_heads"], geom_c["sw_kv"]
    new_caches = tuple(jnp.zeros((batch, depth, nh, seq, HEAD), F32) for nh in (heads, heads, kvh, kvh))
    new_caches += (jnp.zeros((batch, depth, 2, geom_c["hg_heads"], HEAD, HEAD), F32),)
    caches = (cache_diff_k, cache_diff_v, cache_swa_k, cache_swa_v, state_hgrn)
    for l in range(depth):
        row_c = lambda m, tm, j, l=l: (l * MOD_ROWS) * 6 + j
        row_l = lambda m, tm, j, l=l: (l * MOD_ROWS + 1 + (m * tm) // dec_seq) * 6 + j
        y_p, new_caches = _layer(y_p, geom_c, l, mod, row_c, wts, params, lb_params, consts_c, None, None,
                                 new_caches, alpha)
        y_s, _ = _layer(y_s, geom_l, l, mod, row_l, wts, params, lb_params, consts_l, tabs, caches, None, alpha)
    return (y_p.reshape(batch, seq, d), y_s.reshape(dec_batch, dec_seq, d)) + new_caches
```

```python
import functools
import math

import numpy as np
import jax
import jax.numpy as jnp
from jax import lax
from jax.experimental import pallas as pl
from jax.experimental.pallas import tpu as pltpu

F32 = jnp.float32
BF16 = jnp.bfloat16

GRID_W = 64
ROPE_BASE = 10000.0
LN_EPS = 1e-5
RMS_EPS = 1e-6
NEG_INF = -1e30
LB_FLOOR = 1e-30
LOG2_E = math.log2(math.e)
HEAD = 128
DA_QK = 64
SW_GROUP = 4
SW_BLOCK = 128
SW_WINDOW = 128
HG_CHUNK = 64
HG_UNROLL = 4
TM_PROJ = 256
TM_OUT = 512
SW_PER_STEP = 4
TM_FFN = 512
MOD_ROWS = 8
VMEM_LIMIT = 56 * 1024 * 1024


def _cparams(*sem):
    return pltpu.CompilerParams(dimension_semantics=sem, vmem_limit_bytes=VMEM_LIMIT)


def _dot(a, b):
    return jnp.dot(a, b, preferred_element_type=F32)


def _dot_t(a, b):
    return lax.dot_general(a, b, (((1,), (1,)), ((), ())), preferred_element_type=F32)


def _silu(x):
    return x / (1.0 + jnp.exp2(x * (-LOG2_E)))


def _layer_norm(y, g, b):
    mu = jnp.mean(y, axis=-1, keepdims=True)
    d = y - mu
    var = jnp.mean(d * d, axis=-1, keepdims=True)
    return d * lax.rsqrt(var + LN_EPS) * g + b


def _rms_norm(o, g):
    ms = jnp.mean(o * o, axis=-1, keepdims=True)
    return o * lax.rsqrt(ms + RMS_EPS) * g


def _rope(x, c, sa, sb, w):
    return x * c + pltpu.roll(x, HEAD - w, 1) * sa + pltpu.roll(x, w, 1) * sb


def _pick_tile(n, target):
    t = min(n, target)
    while n % t or t % 128:
        t -= 128
    return t


def _lb_kernel(logit_ref, loglb_ref, log1m_ref, onem_ref):
    depth = logit_ref.shape[0]
    x = [logit_ref[l] for l in range(depth)]
    m = functools.reduce(jnp.maximum, x)
    e = [jnp.exp(xi - m) for xi in x]
    tot = functools.reduce(lambda a, b: a + b, e)
    w = [ei / tot for ei in e]
    acc = jnp.zeros_like(w[0])
    for l in range(depth):
        acc = acc + w[l]
        lb = acc - w[0]
        loglb_ref[l] = jnp.log(jnp.maximum(lb, LB_FLOOR))
        log1m_ref[l] = jnp.log1p(-lb)
        onem_ref[l] = 1.0 - lb


def _lb_params(logits):
    shp = jax.ShapeDtypeStruct(logits.shape, F32)
    return pl.pallas_call(_lb_kernel, out_shape=(shp, shp, shp), name="hgrn_lb_params")(logits)


def _mod_kernel(c_ref, w_ref, b_ref, o_ref):
    s = _silu(c_ref[...])
    s_hi = s.astype(BF16)
    s_lo = (s - s_hi.astype(F32)).astype(BF16)
    w = w_ref[...]
    w_hi = w.astype(BF16)
    w_lo = (w - w_hi.astype(F32)).astype(BF16)
    o_ref[...] = _dot(s_hi, w_hi) + _dot(s_lo, w_hi) + _dot(s_hi, w_lo) + b_ref[...]


def _modulation(cond, w_mod, b_mod):
    depth, d, d6 = w_mod.shape
    tn = _pick_tile(d6, 1024)
    return pl.pallas_call(
        _mod_kernel,
        out_shape=jax.ShapeDtypeStruct((depth, MOD_ROWS, d6), F32),
        grid=(depth, d6 // tn),
        in_specs=[pl.BlockSpec((MOD_ROWS, d), lambda l, n: (0, 0)),
                  pl.BlockSpec((None, d, tn), lambda l, n: (l, 0, n)),
                  pl.BlockSpec((None, 1, tn), lambda l, n: (l, 0, n))],
        out_specs=pl.BlockSpec((None, MOD_ROWS, tn), lambda l, n: (l, 0, n)),
        compiler_params=_cparams("parallel", "parallel"),
        name="adaln_modulation",
    )(cond, w_mod, b_mod.reshape(depth, 1, d6))


def _proj_kernel(x_ref, sh_ref, sc_ref, w_ref, cast_ref, *rest, cache_cols):
    n = len(cache_cols)
    o_ref, mix_ref, cast_out_ref = rest[n:n + 3]
    h = (x_ref[...] * (1.0 + sc_ref[0]) + sh_ref[0]).astype(BF16)
    o_ref[...] = _dot(h, w_ref[...])
    mix_ref[...] = jnp.zeros_like(mix_ref)
    cast_out_ref[...] = cast_ref[...].astype(BF16)
    for c_ref, (off, nh) in zip(rest[n + 3:], cache_cols):
        for hd in range(nh):
            c_ref[hd] = o_ref[:, off + hd * HEAD:off + (hd + 1) * HEAD]


def _in_proj(x, mod, mod_row, w_in, layer, tm, d_mix, cast_src, caches=None, cache_cols=()):
    t, d = x.shape
    d_in = w_in.shape[2]
    steps = t // tm
    _, cast_rows, cast_width = cast_src.shape
    slab = cast_rows // steps
    assert cast_rows % steps == 0 and slab % 16 == 0
    caches = tuple(caches or ())
    out_shape = ([jax.ShapeDtypeStruct((t, d_in), F32), jax.ShapeDtypeStruct((t, d_mix), BF16),
                  jax.ShapeDtypeStruct((cast_rows, cast_width), BF16)]
                 + [jax.ShapeDtypeStruct(c.shape, c.dtype) for c in caches])
    out_specs = [pl.BlockSpec((tm, d_in), lambda m: (m, 0)), pl.BlockSpec((tm, d_mix), lambda m: (m, 0)),
                 pl.BlockSpec((slab, cast_width), lambda m: (m, 0))]
    for c in caches:
        assert c.shape[3] == tm
        out_specs.append(pl.BlockSpec((None, None, c.shape[2], tm, HEAD), lambda m: (m, layer, 0, 0, 0)))
    res = pl.pallas_call(
        functools.partial(_proj_kernel, cache_cols=tuple(cache_cols)),
        out_shape=tuple(out_shape),
        grid=(steps,),
        in_specs=[pl.BlockSpec((tm, d), lambda m: (m, 0)),
                  pl.BlockSpec((1, 1, d), lambda m: (mod_row(m, tm, 0), 0, 0)),
                  pl.BlockSpec((1, 1, d), lambda m: (mod_row(m, tm, 1), 0, 0)),
                  pl.BlockSpec((None, d, d_in), lambda m: (layer, 0, 0), pipeline_mode=pl.Buffered(1)),
                  pl.BlockSpec((None, slab, cast_width), lambda m: (layer, m, 0))]
                 + [pl.BlockSpec(memory_space=pl.ANY)] * len(caches),
        out_specs=tuple(out_specs),
        input_output_aliases={5 + i: 3 + i for i in range(len(caches))},
        compiler_params=_cparams("parallel"),
        name="in_proj",
    )(x, mod, mod, w_in, cast_src, *caches)
    return res[0], res[1], res[2], tuple(res[3:])


def _out_proj_kernel(mix_ref, w_ref, x_ref, g1_ref, lng_ref, lnb_ref, x1_ref, *, alpha):
    mix = _dot(mix_ref[...], w_ref[...])
    x1_ref[...] = _layer_norm(alpha * x_ref[...] + g1_ref[0] * mix, lng_ref[...], lnb_ref[...])


def _out_proj(mix, w_out, x, mod, mod_row, ln_g, ln_b, layer, tm, alpha):
    t, d = x.shape
    d_mix = mix.shape[1]
    depth = ln_g.shape[0]
    vec = pl.BlockSpec((None, 1, d), lambda m: (layer, 0, 0))
    return pl.pallas_call(
        functools.partial(_out_proj_kernel, alpha=alpha),
        out_shape=jax.ShapeDtypeStruct((t, d), F32),
        grid=(t // tm,),
        in_specs=[pl.BlockSpec((tm, d_mix), lambda m: (m, 0)),
                  pl.BlockSpec((None, d_mix, d), lambda m: (layer, 0, 0), pipeline_mode=pl.Buffered(1)),
                  pl.BlockSpec((tm, d), lambda m: (m, 0)),
                  pl.BlockSpec((1, 1, d), lambda m: (mod_row(m, tm, 2), 0, 0)),
                  vec, vec],
        out_specs=pl.BlockSpec((tm, d), lambda m: (m, 0)),
        compiler_params=_cparams("parallel"),
        name="out_proj_ln",
    )(mix, w_out, x, mod, ln_g.reshape(depth, 1, d), ln_b.reshape(depth, 1, d))


def _ffn_kernel(x_ref, sh2_ref, sc2_ref, wg_ref, wu_ref, wd_ref, g2_ref, lng_ref, lnb_ref, o_ref, h_ref, *, alpha):
    f = pl.program_id(1)

    @pl.when(f == 0)
    def _():
        h_ref[...] = (x_ref[...] * (1.0 + sc2_ref[0]) + sh2_ref[0]).astype(BF16)
        o_ref[...] = jnp.zeros_like(o_ref)

    h = h_ref[...]
    a = _dot(h, wg_ref[...])
    u = _dot(h, wu_ref[...])
    o_ref[...] += _dot((_silu(a) * u).astype(BF16), wd_ref[...])

    @pl.when(f == pl.num_programs(1) - 1)
    def _():
        y = alpha * x_ref[...] + g2_ref[0] * o_ref[...]
        o_ref[...] = _layer_norm(y, lng_ref[...], lnb_ref[...])


def _ffn(x1, w_gu, w_down, mod, mod_row, ln_g, ln_b, layer, tm, alpha):
    t, d = x1.shape
    d_ff = w_down.shape[0]
    depth = ln_g.shape[0]
    tf = _pick_tile(d_ff, 512)
    nf = d_ff // tf
    vec = pl.BlockSpec((None, 1, d), lambda m, f: (layer, 0, 0))
    row = lambda j: pl.BlockSpec((1, 1, d), lambda m, f: (mod_row(m, tm, j), 0, 0))
    return pl.pallas_call(
        functools.partial(_ffn_kernel, alpha=alpha),
        out_shape=jax.ShapeDtypeStruct((t, d), F32),
        grid=(t // tm, nf),
        in_specs=[pl.BlockSpec((tm, d), lambda m, f: (m, 0)),
                  row(3), row(4),
                  pl.BlockSpec((d, tf), lambda m, f: (0, f)),
                  pl.BlockSpec((d, tf), lambda m, f: (0, nf + f)),
                  pl.BlockSpec((tf, d), lambda m, f: (f, 0)),
                  row(5), vec, vec],
        out_specs=pl.BlockSpec((tm, d), lambda m, f: (m, 0)),
        scratch_shapes=[pltpu.VMEM((tm, d), BF16)],
        compiler_params=_cparams("parallel", "arbitrary"),
        name="ffn_ln",
    )(x1, mod, mod, w_gu, w_gu, w_down, mod, ln_g.reshape(depth, 1, d), ln_b.reshape(depth, 1, d))


def _diff_attn_kernel(*refs, rope, cached, n_self, kb, lam_init, pipelined, n_tiles):
    it = iter(refs)
    q_ref, k_ref, v_ref = next(it), next(it), next(it)
    if rope:
        qc_ref, qsa_ref, qsb_ref, kc_ref, ksa_ref, ksb_ref = (next(it) for _ in range(6))
    if cached:
        ck_ref, cv_ref = next(it), next(it)
    lam_ref, g_ref, _, o_ref, kr_ref, vt_ref, m_ref, acc_ref = (next(it) for _ in range(8))
    s_refs = tuple(it)
    nblk = kr_ref.shape[0]
    nself = n_self // kb
    tq = q_ref.shape[0]
    step = pl.program_id(2)
    last = pl.num_programs(2) - 1

    def prepare_keys():
        for j in range(nself):
            rows = slice(j * kb, (j + 1) * kb)
            k = k_ref[rows, :]
            if rope:
                k = _rope(k, kc_ref[rows, :], ksa_ref[rows, :], ksb_ref[rows, :], DA_QK // 4)
            kr_ref[j] = k.astype(BF16)
            vt_ref[j] = v_ref[rows, :].T.astype(BF16)
        if cached:
            for j in range(nblk - nself):
                rows = slice(j * kb, (j + 1) * kb)
                kr_ref[nself + j] = ck_ref[rows, :].astype(BF16)
                vt_ref[nself + j] = cv_ref[rows, :].T.astype(BF16)

    def queries():
        q = q_ref[...]
        if rope:
            q = _rope(q, qc_ref[...], qsa_ref[...], qsb_ref[...], DA_QK // 4)
        q = q * (DA_QK ** -0.5 * LOG2_E)
        lane = lax.broadcasted_iota(jnp.int32, q.shape, 1)
        return (jnp.where(lane < DA_QK, q, 0.0).astype(BF16), jnp.where(lane >= DA_QK, q, 0.0).astype(BF16))

    def fold(x, op):
        return op(x.reshape(kb // 8, 8, tq), axis=0)

    def scores_block(j, qz, buf, ms):
        kblk = kr_ref[j]
        out = []
        for mp in range(2):
            s = _dot_t(kblk, qz[mp])
            buf[mp, j] = s
            out.append(jnp.maximum(ms[mp], fold(s, jnp.max)))
        return tuple(out)

    def values_block(j, buf, ms, ls):
        out = []
        for mp in range(2):
            p = jnp.exp2(buf[mp, j] - ms[mp])
            acc_ref[mp] += _dot(vt_ref[j], p.astype(BF16))
            out.append(ls[mp] + fold(p, jnp.sum))
        return tuple(out)

    neg = jnp.full((8, tq), -jnp.inf, F32)
    zero = jnp.zeros((8, tq), F32)

    def save_max(ms):
        for mp in range(2):
            m_ref[mp] = jnp.max(ms[mp], axis=0, keepdims=True)

    def finish(ls):
        l1, l2 = (jnp.sum(l, axis=0, keepdims=True) for l in ls)
        lp = lam_ref[...]
        lam = (jnp.exp(jnp.sum(lp[0:1] * lp[1:2], axis=-1, keepdims=True))
               - jnp.exp(jnp.sum(lp[2:3] * lp[3:4], axis=-1, keepdims=True)) + lam_init)
        ot = acc_ref[0] * (1.0 / l1) - acc_ref[1] * (lam / l2)
        ms_o = jnp.mean(ot * ot, axis=0, keepdims=True)
        ot = ot * lax.rsqrt(ms_o + RMS_EPS) * g_ref[...] * (1.0 - lam_init)
        o_ref[...] = ot.T.astype(BF16)

    if not pipelined:
        pl.when(step == 0)(prepare_keys)
        qz = queries()
        ms = (neg, neg)
        for j in range(nblk):
            ms = scores_block(j, qz, s_refs[0], ms)
        ms = tuple(jnp.max(m, axis=0, keepdims=True) for m in ms)
        acc_ref[...] = jnp.zeros_like(acc_ref)
        ls = (zero, zero)
        for j in range(nblk):
            ls = values_block(j, s_refs[0], ms, ls)
        finish(ls)
        return

    @pl.when(step == 0)
    def _():
        prepare_keys()
        qz = queries()
        ms = (neg, neg)
        for j in range(nblk):
            ms = scores_block(j, qz, s_refs[0], ms)
        save_max(ms)

    def interior(parity):
        prev = (m_ref[0], m_ref[1])
        qz = queries()
        acc_ref[...] = jnp.zeros_like(acc_ref)
        ms, ls = (neg, neg), (zero, zero)
        for j in range(nblk):
            ms = scores_block(j, qz, s_refs[parity], ms)
            ls = values_block(j, s_refs[1 - parity], prev, ls)
        save_max(ms)
        finish(ls)

    inside = (step > 0) & (step < last)
    pl.when(inside & (step % 2 == 1))(functools.partial(interior, 1))
    pl.when(inside & (step % 2 == 0))(functools.partial(interior, 0))

    @pl.when(step == last)
    def _():
        prev = (m_ref[0], m_ref[1])
        acc_ref[...] = jnp.zeros_like(acc_ref)
        ls = (zero, zero)
        for j in range(nblk):
            ls = values_block(j, s_refs[(n_tiles - 1) % 2], prev, ls)
        finish(ls)


def _diff_attn_seq_kernel(q_ref, k_ref, v_ref, lam_ref, g_ref, mix_ref, o_ref, *, heads, lam_init):
    del mix_ref
    n = q_ref.shape[0]
    lane = lax.broadcasted_iota(jnp.int32, (n, HEAD), 1)
    scores, vts = [], []
    for hd in range(heads):
        cols = slice(hd * HEAD, (hd + 1) * HEAD)
        q = q_ref[:, cols] * (DA_QK ** -0.5 * LOG2_E)
        k = k_ref[:, cols].astype(BF16)
        scores.append([_dot_t(k, jnp.where(lane < DA_QK, q, 0.0).astype(BF16)),
                       _dot_t(k, jnp.where(lane >= DA_QK, q, 0.0).astype(BF16))])
        vts.append(v_ref[:, cols].T.astype(BF16))
    probs = []
    for hd in range(heads):
        parts = []
        for s in scores[hd]:
            p = jnp.exp2(s - jnp.max(s, axis=0, keepdims=True))
            parts.append((p.astype(BF16), jnp.sum(p, axis=0, keepdims=True)))
        probs.append(parts)
    accs = [[_dot(vts[hd], pb) for pb, _ in probs[hd]] for hd in range(heads)]
    lp = lam_ref[...]
    lam = (jnp.exp(jnp.sum(lp[0:1] * lp[1:2], axis=-1, keepdims=True))
           - jnp.exp(jnp.sum(lp[2:3] * lp[3:4], axis=-1, keepdims=True)) + lam_init)
    for hd in range(heads):
        (_, l1), (_, l2) = probs[hd]
        ot = accs[hd][0] * (1.0 / l1) - accs[hd][1] * (lam / l2)
        ms_o = jnp.mean(ot * ot, axis=0, keepdims=True)
        ot = ot * lax.rsqrt(ms_o + RMS_EPS) * g_ref[...] * (1.0 - lam_init)
        o_ref[:, hd * HEAD:(hd + 1) * HEAD] = ot.T.astype(BF16)


def _diff_attn_seq(p, mix, geom, layer, diff_lambda, diff_g):
    n_seq, n = geom["n_seq"], geom["n"]
    heads = geom["da_heads"]
    depth = diff_g.shape[0]
    lam_init = 0.8 - 0.6 * math.exp(-0.3 * layer)
    blk = lambda j: pl.BlockSpec((n, heads * HEAD), lambda b: (b, j))
    return pl.pallas_call(
        functools.partial(_diff_attn_seq_kernel, heads=heads, lam_init=lam_init),
        out_shape=jax.ShapeDtypeStruct(mix.shape, mix.dtype),
        grid=(n_seq,),
        in_specs=[blk(0), blk(1), blk(2),
                  pl.BlockSpec((None, 4, DA_QK), lambda b: (layer, 0, 0)),
                  pl.BlockSpec((None, HEAD, 1), lambda b: (layer, 0, 0)),
                  pl.BlockSpec(memory_space=pl.ANY)],
        out_specs=blk(0),
        input_output_aliases={5: 0},
        compiler_params=_cparams("parallel"),
        name="diff_attention_seq",
    )(p, p, p, diff_lambda, diff_g.reshape(depth, HEAD, 1), mix)


def _diff_attn(p, mix, geom, layer, diff_lambda, diff_g, rope_tabs, cache_k, cache_v):
    if rope_tabs is None and cache_k is None and geom["n"] <= 256:
        return _diff_attn_seq(p, mix, geom, layer, diff_lambda, diff_g)
    n_seq, n, t = geom["n_seq"], geom["n"], geom["t"]
    heads = geom["da_heads"]
    rope = rope_tabs is not None
    cached = cache_k is not None
    tq = min(n, 256)
    kb = min(n, 256)
    nq = n // tq
    n_ctx = cache_k.shape[3] if cached else 0
    nblk = (n + n_ctx) // kb
    assert n % kb == 0 and n_ctx % kb == 0
    k_off, v_off = heads, 2 * heads
    depth = diff_g.shape[0]
    lam_init = 0.8 - 0.6 * math.exp(-0.3 * layer)

    pipelined = nq > 1
    q_tile = (lambda i: jnp.minimum(i, nq - 1)) if pipelined else (lambda i: i)
    o_tile = (lambda i: jnp.maximum(i - 1, 0)) if pipelined else (lambda i: i)
    in_specs = [pl.BlockSpec((tq, HEAD), lambda b, h, i: (b * nq + q_tile(i), h)),
                pl.BlockSpec((n, HEAD), lambda b, h, i: (b, k_off + h)),
                pl.BlockSpec((n, HEAD), lambda b, h, i: (b, v_off + h))]
    args = [p, p, p]
    if rope:
        in_specs += [pl.BlockSpec((tq, HEAD), lambda b, h, i: (q_tile(i), 0))] * 3
        in_specs += [pl.BlockSpec((n, HEAD), lambda b, h, i: (0, 0))] * 3
        args += list(rope_tabs) * 2
    if cached:
        spec = pl.BlockSpec((None, None, None, n_ctx, HEAD), lambda b, h, i: (b, layer, h, 0, 0))
        in_specs += [spec, spec]
        args += [cache_k, cache_v]
    in_specs += [pl.BlockSpec((None, 4, DA_QK), lambda b, h, i: (layer, 0, 0)),
                 pl.BlockSpec((None, HEAD, 1), lambda b, h, i: (layer, 0, 0)),
                 pl.BlockSpec(memory_space=pl.ANY)]
    args += [diff_lambda, diff_g.reshape(depth, HEAD, 1), mix]
    return pl.pallas_call(
        functools.partial(_diff_attn_kernel, rope=rope, cached=cached, n_self=n, kb=kb, lam_init=lam_init,
                          pipelined=pipelined, n_tiles=nq),
        out_shape=jax.ShapeDtypeStruct(mix.shape, mix.dtype),
        grid=(n_seq, heads, nq + 1 if pipelined else nq),
        in_specs=in_specs,
        input_output_aliases={len(args) - 1: 0},
        out_specs=pl.BlockSpec((tq, HEAD), lambda b, h, i: (b * nq + o_tile(i), h)),
        scratch_shapes=[pltpu.VMEM((nblk, kb, HEAD), BF16), pltpu.VMEM((nblk, HEAD, kb), BF16),
                        pltpu.VMEM((2, 1, tq), F32), pltpu.VMEM((2, HEAD, tq), F32)]
                       + [pltpu.VMEM((2, nblk, kb, tq), F32)] * (2 if pipelined else 1),
        compiler_params=_cparams("parallel", "parallel", "arbitrary"),
        name="diff_attention",
    )(*args)


def _scan_constants(c):
    levels = int(math.log2(c))
    t = np.arange(c)[:, None]
    s = np.arange(c)[None, :]
    tri = (s <= t).astype(np.float32)
    masks = []
    for j in range(levels):
        m = c >> (j + 1)
        base = (t // (2 * m)) * (2 * m)
        sbase = (s // (2 * m)) * (2 * m)
        masks.append((sbase == base) & ((t - base) >= m) & ((s - sbase) < m))
    masks.append(s == t)
    masks = np.stack(masks).astype(np.float32)
    flip = lambda a: a.reshape(-1, c, c)[:, ::-1, ::-1].reshape(a.shape)
    return (jnp.asarray(tri, BF16), jnp.asarray(flip(tri), BF16),
            jnp.asarray(masks, F32), jnp.asarray(flip(masks), F32))


def _level_exponents(cum, d, chunk):
    row = lax.broadcasted_iota(jnp.int32, cum.shape, 0)
    sub = lax.broadcasted_iota(jnp.int32, (8, HEAD), 0)
    out = []
    m = chunk // 2
    while m >= 1:
        in_second = (row & m) != 0
        q_side = in_second if d == 0 else jnp.logical_not(in_second)
        if m >= 8:
            pieces = []
            for b0 in range(0, chunk, 2 * m):
                r = b0 + m - 1 + d
                bnd = jnp.broadcast_to(cum[r:r + 1, :], (m, HEAD))
                first, second = cum[b0:b0 + m, :], cum[b0 + m:b0 + 2 * m, :]
                pieces += [bnd - first, second - bnd] if d == 0 else [first - bnd, bnd - second]
            x = jnp.concatenate(pieces, axis=0)
        elif m == 1:
            neighbour = pltpu.roll(cum, 1 if d == 0 else chunk - 1, 0)
            x = jnp.where(q_side, cum - neighbour, 0.0)
        else:
            tiles = []
            for v in range(chunk // 8):
                if m == 4:
                    r = 8 * v + 3 + d
                    tiles.append(jnp.broadcast_to(cum[r:r + 1, :], (8, HEAD)))
                else:
                    lo = jnp.broadcast_to(cum[8 * v + 1 + d:8 * v + 2 + d, :], (8, HEAD))
                    hi = jnp.broadcast_to(cum[8 * v + 5 + d:8 * v + 6 + d, :], (8, HEAD))
                    tiles.append(jnp.where(sub < 4, lo, hi))
            bnd = jnp.concatenate(tiles, axis=0)
            x = jnp.where(q_side, cum - bnd, bnd - cum)
        out.append(x)
        m //= 2
    return out


def _hgrn_kernel(*refs, n, chunk, unroll, has_state, emit_state):
    it = iter(refs)
    q_ref, zf_ref, zb_ref, v_ref, hg_ref = (next(it) for _ in range(5))
    loglb_ref, log1m_ref, onem_ref, g_ref = (next(it) for _ in range(4))
    tri_refs = (next(it), next(it))
    mask_refs = (next(it), next(it))
    s0_ref = next(it) if has_state else None
    next(it)
    if emit_state:
        next(it)
    o_ref = next(it)
    st_ref = next(it) if emit_state else None
    obuf_refs = (next(it), next(it))
    s_ref = next(it)

    levels = int(math.log2(chunk))
    nchunks = n // chunk
    z_refs = (zf_ref, zb_ref)

    for d in range(2):
        s_ref[d] = s0_ref[d].T if has_state else jnp.zeros((HEAD, HEAD), F32)

    def gates(d, start):
        rows = pl.ds(start, chunk)
        z = z_refs[d][rows, :]
        q = _silu(q_ref[rows, :])
        v = v_ref[rows, :].astype(BF16)
        z2 = z * LOG2_E
        e = jnp.exp2(-jnp.abs(z2))
        one_e = 1.0 + e
        log_sig = jnp.minimum(z2, 0.0) - jnp.log2(one_e)
        sig_neg = jnp.where(z >= 0.0, e, 1.0) / one_e
        a = loglb_ref[d:d + 1, :] * LOG2_E
        b = log1m_ref[d:d + 1, :] * LOG2_E + log_sig
        g = jnp.maximum(a, b) + jnp.log2(1.0 + jnp.exp2(-jnp.abs(a - b)))
        k = onem_ref[d:d + 1, :] * sig_neg
        g_hi = g.astype(BF16)
        g_lo = (g - g_hi.astype(F32)).astype(BF16)
        cum = _dot(tri_refs[d][...], jnp.concatenate([g_hi, g_lo], axis=1))
        return q, k, v, cum[:, :HEAD] + cum[:, HEAD:]

    def body(i, carry):
        chains = []
        for u in range(unroll):
            c = i * unroll + u
            chains.append((0, pl.multiple_of(c * chunk, chunk)))
            chains.append((1, pl.multiple_of((nchunks - 1 - c) * chunk, chunk)))
        work = [gates(d, start) for d, start in chains]
        qk16 = [(q.astype(BF16), k.astype(BF16)) for q, k, _, _ in work]
        atts = [mask_refs[d][levels] * _dot_t(qb, kb) for (d, _), (qb, kb) in zip(chains, qk16)]
        expo = [_level_exponents(cum, d, chunk) for (d, _), (_, _, _, cum) in zip(chains, work)]
        for j in range(levels):
            for ci, ((d, _), (qb, kb)) in enumerate(zip(chains, qk16)):
                fac = jnp.exp2(expo[ci][j]).astype(BF16)
                atts[ci] = atts[ci] + mask_refs[d][j] * _dot_t(qb * fac, kb * fac)
        intra, delta, q_in, decay = [], [], [], []
        for (d, _), (q, k, v, cum), att in zip(chains, work, atts):
            last = cum[chunk - 1:chunk] if d == 0 else cum[0:1]
            kt = (k * jnp.exp2(last - cum)).astype(BF16)
            intra.append(_dot(att.astype(BF16), v))
            delta.append(lax.dot_general(v, kt, (((0,), (0,)), ((), ())), preferred_element_type=F32))
            q_in.append((q * jnp.exp2(cum)).astype(BF16))
            decay.append(jnp.exp2(last))
        st = [s_ref[0], s_ref[1]]
        for ci, (d, start) in enumerate(chains):
            obuf_refs[d][pl.ds(start, chunk), :] = intra[ci] + _dot_t(q_in[ci], st[d].astype(BF16))
            st[d] = decay[ci] * st[d] + delta[ci]
        s_ref[0] = st[0]
        s_ref[1] = st[1]
        return carry

    lax.fori_loop(0, nchunks // unroll, body, 0)

    o = obuf_refs[0][...] + obuf_refs[1][...]
    o_ref[...] = (_rms_norm(o, g_ref[...]) * _silu(hg_ref[...])).astype(BF16)
    if emit_state:
        for d in range(2):
            st_ref[d] = s_ref[d].T


def _hgrn(p, mix, geom, layer, lb_params, hgrn_g, consts, state, new_state=None):
    n_seq, n, t = geom["n_seq"], geom["n"], geom["t"]
    heads = geom["hg_heads"]
    mix_col = geom["da_heads"]
    base = 3 * geom["da_heads"]
    has_state = state is not None
    emit_state = not has_state
    depth = hgrn_g.shape[0]
    chunk = min(HG_CHUNK, n)
    col = lambda j: pl.BlockSpec((n, HEAD), lambda b, h: (b, base + j * heads + h))
    lbspec = pl.BlockSpec((None, 2, HEAD), lambda b, h: (layer, 0, h))
    const_specs = [pl.BlockSpec(c.shape, lambda b, h, nd=c.ndim: (0,) * nd) for c in consts]
    in_specs = [col(0), col(1), col(2), col(3), col(4), lbspec, lbspec, lbspec,
                pl.BlockSpec((None, 1, HEAD), lambda b, h: (layer, 0, 0))] + const_specs
    args = [p] * 5 + list(lb_params) + [hgrn_g.reshape(depth, 1, HEAD)] + list(consts)
    if has_state:
        in_specs.append(pl.BlockSpec((None, None, 2, None, HEAD, HEAD), lambda b, h: (b, layer, 0, h, 0, 0)))
        args.append(state)
    in_specs.append(pl.BlockSpec(memory_space=pl.ANY))
    args.append(mix)
    aliases = {len(args) - 1: 0}
    out_shape = [jax.ShapeDtypeStruct(mix.shape, mix.dtype)]
    out_specs = [pl.BlockSpec((n, HEAD), lambda b, h: (b, mix_col + h))]
    if emit_state:
        in_specs.append(pl.BlockSpec(memory_space=pl.ANY))
        args.append(new_state)
        aliases[len(args) - 1] = 1
        out_shape.append(jax.ShapeDtypeStruct(new_state.shape, new_state.dtype))
        out_specs.append(pl.BlockSpec((None, None, 2, None, HEAD, HEAD), lambda b, h: (b, layer, 0, h, 0, 0)))
    res = pl.pallas_call(
        functools.partial(_hgrn_kernel, n=n, chunk=chunk, unroll=HG_UNROLL if (n // chunk) % HG_UNROLL == 0 else 1,
                          has_state=has_state, emit_state=emit_state),
        out_shape=tuple(out_shape),
        grid=(n_seq, heads),
        in_specs=in_specs,
        out_specs=tuple(out_specs),
        scratch_shapes=[pltpu.VMEM((n, HEAD), F32), pltpu.VMEM((n, HEAD), F32),
                        pltpu.VMEM((2, HEAD, HEAD), F32)],
        input_output_aliases=aliases,
        compiler_params=_cparams("parallel", "parallel"),
        name="hgrn2_scan",
    )(*args)
    return res if emit_state else (res[0], None)


def _swa_ctx_kernel(q_ref, k_ref, v_ref, sink_ref, mix_ref, o_ref, *, layer, kvh):
    del mix_ref
    n = q_ref.shape[0]
    lane = lax.broadcasted_iota(jnp.int32, (1, SW_GROUP * n), 1)

    def fold(x, op):
        return op(op(x.reshape(x.shape[0] // 8, 8, x.shape[1]), axis=0), axis=0, keepdims=True)

    scores, sinks = [], []
    for kv in range(kvh):
        q_all = jnp.concatenate(
            [(q_ref[:, (kv * SW_GROUP + g) * HEAD:(kv * SW_GROUP + g + 1) * HEAD] * (HEAD ** -0.5)).astype(BF16)
             for g in range(SW_GROUP)], axis=0)
        scores.append(_dot_t(k_ref[:, kv * HEAD:(kv + 1) * HEAD].astype(BF16), q_all))
        sink = jnp.zeros((1, SW_GROUP * n), F32)
        for g in range(SW_GROUP):
            sink = jnp.where(lane // n == g, sink_ref[layer, kv * SW_GROUP + g], sink)
        sinks.append(sink)
    probs = []
    for s, sink in zip(scores, sinks):
        m = jnp.maximum(fold(s, jnp.max), sink)
        p = jnp.exp(s - m)
        inv = 1.0 / (fold(p, jnp.sum) + jnp.exp(sink - m))
        probs.append((p * inv).astype(BF16))
    outs = [_dot(v_ref[:, kv * HEAD:(kv + 1) * HEAD].T.astype(BF16), probs[kv]) for kv in range(kvh)]
    for kv in range(kvh):
        for g in range(SW_GROUP):
            hd = kv * SW_GROUP + g
            o_ref[:, hd * HEAD:(hd + 1) * HEAD] = outs[kv][:, g * n:(g + 1) * n].T.astype(BF16)


def _swa_ctx(p, mix, geom, layer, sink):
    n_seq, n, t = geom["n_seq"], geom["n"], geom["t"]
    kvh = geom["sw_kv"]
    qw = kvh * SW_GROUP * HEAD
    kw = kvh * HEAD
    q_blk = geom["cq_off"] // qw
    k_blk = geom["ck_off"] // kw
    mix_blk = (geom["da_heads"] + geom["hg_heads"]) * HEAD // qw
    assert geom["cq_off"] % qw == 0 and geom["ck_off"] % kw == 0
    return pl.pallas_call(
        functools.partial(_swa_ctx_kernel, layer=layer, kvh=kvh),
        out_shape=jax.ShapeDtypeStruct(mix.shape, mix.dtype),
        grid=(n_seq,),
        in_specs=[pl.BlockSpec((n, qw), lambda b: (b, q_blk)),
                  pl.BlockSpec((n, kw), lambda b: (b, k_blk)),
                  pl.BlockSpec((n, kw), lambda b: (b, k_blk + 1)),
                  pl.BlockSpec(memory_space=pltpu.SMEM),
                  pl.BlockSpec(memory_space=pl.ANY)],
        out_specs=pl.BlockSpec((n, qw), lambda b: (b, mix_blk)),
        input_output_aliases={4: 0},
        compiler_params=_cparams("parallel"),
        name="sink_attention",
    )(p, p, p, sink, mix)


def _band_masks(n):
    r = np.arange(3 * SW_BLOCK)[:, None]
    i = (np.arange(SW_GROUP * SW_BLOCK) % SW_BLOCK)[None, :]
    window = np.abs(SW_BLOCK + i - r) <= SW_WINDOW
    not_before = r >= SW_BLOCK
    not_after = r < 2 * SW_BLOCK
    variants = [window, window & not_before, window & not_after, window & not_before & not_after]
    return jnp.asarray(np.stack(variants).astype(np.float32))


def _swa_lat_kernel(q_ref, k_ref, v_ref, qc_ref, qsa_ref, qsb_ref, kc_ref, ksa_ref, ksb_ref, ck_ref, cv_ref,
                    mask_ref, sink_ref, mix_ref, o_ref, kr_ref, vt_ref, kctx_ref, vctxt_ref, *, layer, n):
    kv = pl.program_id(1)
    step = pl.program_id(2)
    nb = n // SW_BLOCK
    w = HEAD // 4
    band = 3 * SW_BLOCK
    per_step = q_ref.shape[0] // SW_BLOCK

    @pl.when(step == 0)
    def _():
        zeros = jnp.zeros((SW_BLOCK, HEAD), BF16)
        for j in (0, nb + 1):
            kr_ref[j] = zeros
            vt_ref[j] = zeros

        def fill(j, carry):
            rows = pl.ds(pl.multiple_of(j * SW_BLOCK, SW_BLOCK), SW_BLOCK)
            k = _rope(k_ref[rows, :], kc_ref[rows, :], ksa_ref[rows, :], ksb_ref[rows, :], w)
            kr_ref[j + 1] = k.astype(BF16)
            vt_ref[j + 1] = v_ref[rows, :].T.astype(BF16)
            return carry

        lax.fori_loop(0, nb, fill, 0)
        kctx_ref[...] = ck_ref[...].astype(BF16)
        vctxt_ref[...] = cv_ref[...].T.astype(BF16)

    lane = lax.broadcasted_iota(jnp.int32, (1, SW_GROUP * SW_BLOCK), 1)
    sink = jnp.zeros((1, SW_GROUP * SW_BLOCK), F32)
    for g in range(SW_GROUP):
        sink = jnp.where(lane // SW_BLOCK == g, sink_ref[layer, kv * SW_GROUP + g], sink)

    def fold(x, op):
        return op(op(x.reshape(x.shape[0] // 8, 8, x.shape[1]), axis=0), axis=0, keepdims=True)

    blocks = [step * per_step + c for c in range(per_step)]
    rows = [slice(c * SW_BLOCK, (c + 1) * SW_BLOCK) for c in range(per_step)]
    scores = []
    for qb, r in zip(blocks, rows):
        qc, qsa, qsb = qc_ref[r, :], qsa_ref[r, :], qsb_ref[r, :]
        q_all = jnp.concatenate(
            [(_rope(q_ref[r, g * HEAD:(g + 1) * HEAD], qc, qsa, qsb, w) * (HEAD ** -0.5)).astype(BF16)
             for g in range(SW_GROUP)], axis=0)
        k_all = jnp.concatenate([kr_ref[qb], kr_ref[qb + 1], kr_ref[qb + 2], kctx_ref[...]], axis=0)
        scores.append(_dot_t(k_all, q_all))
    probs = []
    for qb, s in zip(blocks, scores):
        variant = jnp.where(qb == 0, 1, 0) + jnp.where(qb == nb - 1, 2, 0)
        s_band = jnp.where(mask_ref[variant] > 0.5, s[:band], NEG_INF)
        s_ctx = s[band:]
        m = jnp.maximum(jnp.maximum(fold(s_band, jnp.max), fold(s_ctx, jnp.max)), sink)
        p_band = jnp.exp(s_band - m)
        p_ctx = jnp.exp(s_ctx - m)
        inv = 1.0 / (fold(p_band, jnp.sum) + fold(p_ctx, jnp.sum) + jnp.exp(sink - m))
        probs.append(jnp.concatenate([p_band * inv, p_ctx * inv], axis=0).astype(BF16))
    outs = []
    for qb, p_all in zip(blocks, probs):
        vt_all = jnp.concatenate([vt_ref[qb], vt_ref[qb + 1], vt_ref[qb + 2], vctxt_ref[...]], axis=1)
        outs.append(_dot(vt_all, p_all))
    for r, ot in zip(rows, outs):
        for g in range(SW_GROUP):
            o_ref[r, g * HEAD:(g + 1) * HEAD] = ot[:, g * SW_BLOCK:(g + 1) * SW_BLOCK].T.astype(BF16)


def _swa_lat(p, mix, geom, layer, sink, rope_tabs, cache_k, cache_v):
    n_seq, n, t = geom["n_seq"], geom["n"], geom["t"]
    kvh = geom["sw_kv"]
    qw = SW_GROUP * HEAD
    q_blk = geom["cq_off"] // qw
    k_blk = geom["ck_off"] // HEAD
    v_blk = k_blk + kvh
    mix_blk = (geom["da_heads"] + geom["hg_heads"]) * HEAD // qw
    nb = n // SW_BLOCK
    per_step = SW_PER_STEP if nb % SW_PER_STEP == 0 else 1
    nsteps = nb // per_step
    tq = per_step * SW_BLOCK
    n_ctx = cache_k.shape[3]
    masks = _band_masks(n)
    cspec = pl.BlockSpec((None, None, None, n_ctx, HEAD), lambda b, kv, i: (b, layer, kv, 0, 0))
    return pl.pallas_call(
        functools.partial(_swa_lat_kernel, layer=layer, n=n),
        out_shape=jax.ShapeDtypeStruct(mix.shape, mix.dtype),
        grid=(n_seq, kvh, nsteps),
        in_specs=[pl.BlockSpec((tq, qw), lambda b, kv, i: (b * nsteps + i, q_blk + kv)),
                  pl.BlockSpec((n, HEAD), lambda b, kv, i: (b, k_blk + kv)),
                  pl.BlockSpec((n, HEAD), lambda b, kv, i: (b, v_blk + kv))]
                 + [pl.BlockSpec((tq, HEAD), lambda b, kv, i: (i, 0))] * 3
                 + [pl.BlockSpec((n, HEAD), lambda b, kv, i: (0, 0))] * 3
                 + [cspec, cspec, pl.BlockSpec(masks.shape, lambda b, kv, i: (0, 0, 0)),
                    pl.BlockSpec(memory_space=pltpu.SMEM), pl.BlockSpec(memory_space=pl.ANY)],
        out_specs=pl.BlockSpec((tq, qw), lambda b, kv, i: (b * nsteps + i, mix_blk + kv)),
        scratch_shapes=[pltpu.VMEM((nb + 2, SW_BLOCK, HEAD), BF16), pltpu.VMEM((nb + 2, HEAD, SW_BLOCK), BF16),
                        pltpu.VMEM((n_ctx, HEAD), BF16), pltpu.VMEM((HEAD, n_ctx), BF16)],
        input_output_aliases={13: 0},
        compiler_params=_cparams("parallel", "parallel", "arbitrary"),
        name="banded_sink_attention",
    )(p, p, p, *rope_tabs, *rope_tabs, cache_k, cache_v, masks, sink, mix)


def _rope_tables(n, half):
    h = half // 2
    pos = jnp.arange(n)
    inv = ROPE_BASE ** (-jnp.arange(h, dtype=F32) / h)
    zero = jnp.zeros((n, h), F32)
    c, sa, sb = [], [], []
    for axis_pos in (pos // GRID_W, pos % GRID_W):
        ang = axis_pos.astype(F32)[:, None] * inv[None, :]
        cos, sin = jnp.cos(ang), jnp.sin(ang)
        c += [cos, cos]
        sa += [-sin, zero]
        sb += [zero, sin]
    reps = HEAD // (2 * half)
    cat = lambda parts: jnp.tile(jnp.concatenate(parts, axis=1), (1, reps))
    return cat(c), cat(sa), cat(sb)


def _geometry(n_seq, n, d_model):
    da_heads = hg_heads = d_model // 512
    sw_heads = d_model // 256
    sw_kv = sw_heads // SW_GROUP
    cq_off = (3 * da_heads + 5 * hg_heads) * HEAD
    return dict(n_seq=n_seq, n=n, t=n_seq * n, da_heads=da_heads, hg_heads=hg_heads, sw_kv=sw_kv,
                cq_off=cq_off, ck_off=cq_off + sw_heads * HEAD, d_mix=(da_heads + hg_heads + sw_heads) * HEAD)


def _token_tiles(n):
    return _pick_tile(n, TM_PROJ), _pick_tile(n, TM_OUT), _pick_tile(n, TM_FFN)


def _project(x, geom, layer, mod, mod_row, w_in, cast_src, new_caches):
    if new_caches is None:
        tm = _token_tiles(geom["n"])[0]
        return _in_proj(x, mod, mod_row, w_in, layer, tm, geom["d_mix"], cast_src)
    heads, kvh = geom["da_heads"], geom["sw_kv"]
    cols = ((heads * HEAD, heads), (2 * heads * HEAD, heads), (geom["ck_off"], kvh),
            (geom["ck_off"] + kvh * HEAD, kvh))
    return _in_proj(x, mod, mod_row, w_in, layer, geom["n"], geom["d_mix"], cast_src, new_caches, cols)


def _mix_and_ffn(x, p, mix, geom, layer, mod, mod_row, wts, params, lb_params, scan_consts, tabs, caches,
                 new_state, alpha):
    w_out, w_gu, w_down = wts
    _, tm_out, tm_ffn = _token_tiles(geom["t"] if caches is None else geom["n"])
    if caches is None:
        mix = _diff_attn(p, mix, geom, layer, params["diff_lambda"], params["diff_norm_g"], None, None, None)
        mix, new_state = _hgrn(p, mix, geom, layer, lb_params, params["hgrn_norm_g"], scan_consts, None, new_state)
        mix = _swa_ctx(p, mix, geom, layer, params["swa_sink"])
    else:
        ck_d, cv_d, ck_s, cv_s, state = caches
        mix = _diff_attn(p, mix, geom, layer, params["diff_lambda"], params["diff_norm_g"], tabs[0], ck_d, cv_d)
        mix, _ = _hgrn(p, mix, geom, layer, lb_params, params["hgrn_norm_g"], scan_consts, state)
        mix = _swa_lat(p, mix, geom, layer, params["swa_sink"], tabs[1], ck_s, cv_s)
    x1 = _out_proj(mix, w_out, x, mod, mod_row, params["ln1_g"], params["ln1_b"], layer, tm_out, alpha)
    y = _ffn(x1, w_gu, w_down, mod, mod_row, params["ln2_g"], params["ln2_b"], layer, tm_ffn, alpha)
    return y, new_state


def kernel(x_prompt, x_sample, cache_diff_k, cache_diff_v, cache_swa_k, cache_swa_v, state_hgrn, c, c_ctx, w_mod,
           b_mod, w_in, w_out, diff_lambda, diff_norm_g, hgrn_lb_logits, hgrn_norm_g, swa_sink, ln1_g, ln1_b, ln2_g,
           ln2_b, w_gate_up, w_down):
    batch, seq, d = x_prompt.shape
    dec_batch, dec_seq, _ = x_sample.shape
    depth = w_mod.shape[0]
    alpha = (2 * depth) ** 0.25
    geom_c = _geometry(batch, seq, d)
    geom_l = _geometry(dec_batch, dec_seq, d)
    assert 1 + dec_batch <= MOD_ROWS

    cond = jnp.zeros((MOD_ROWS, d), F32).at[0].set(c_ctx).at[1:1 + dec_batch].set(c)
    mod = _modulation(cond, w_mod, b_mod).reshape(depth * MOD_ROWS * 6, 1, d)
    lb_params = _lb_params(hgrn_lb_logits)
    params = dict(diff_lambda=diff_lambda, diff_norm_g=diff_norm_g, hgrn_norm_g=hgrn_norm_g, swa_sink=swa_sink,
                  ln1_g=ln1_g, ln1_b=ln1_b, ln2_g=ln2_g, ln2_b=ln2_b)
    w_in16, w_out16 = w_in.astype(BF16), w_out.astype(BF16)
    tabs = (_rope_tables(dec_seq, DA_QK // 2), _rope_tables(dec_seq, HEAD // 2))
    consts_c = _scan_constants(min(HG_CHUNK, seq))
    consts_l = _scan_constants(min(HG_CHUNK, dec_seq))

    y_p = x_prompt.reshape(batch * seq, d)
    y_s = x_sample.reshape(dec_batch * dec_seq, d)
    heads, kvh = geom_c["da_heads"], geom_c["sw_kv"]
    new_kv = tuple(jnp.zeros((batch, depth, nh, seq, HEAD), F32) for nh in (heads, heads, kvh, kvh))
    new_state = jnp.zeros((batch, depth, 2, geom_c["hg_heads"], HEAD, HEAD), F32)
    caches = (cache_diff_k, cache_diff_v, cache_swa_k, cache_swa_v, state_hgrn)
    for l in range(depth):
        row_c = lambda m, tm, j, l=l: (l * MOD_ROWS) * 6 + j
        row_l = lambda m, tm, j, l=l: (l * MOD_ROWS + 1 + (m * tm) // dec_seq) * 6 + j
        p_c, mix_c, w_down16, new_kv = _project(y_p, geom_c, l, mod, row_c, w_in16, w_down, new_kv)
        p_l, mix_l, w_gu16, _ = _project(y_s, geom_l, l, mod, row_l, w_in16, w_gate_up, None)
        wts = (w_out16, w_gu16, w_down16)
        y_p, new_state = _mix_and_ffn(y_p, p_c, mix_c, geom_c, l, mod, row_c, wts, params, lb_params, consts_c, None,
                                      None, new_state, alpha)
        y_s, _ = _mix_and_ffn(y_s, p_l, mix_l, geom_l, l, mod, row_l, wts, params, lb_params, consts_l, tabs,
                              caches, None, alpha)
    return (y_p.reshape(batch, seq, d), y_s.reshape(dec_batch, dec_seq, d)) + new_kv + (new_state,)
```

```python
import functools
import math

import numpy as np
import jax
import jax.numpy as jnp
from jax import lax
from jax.experimental import pallas as pl
from jax.experimental.pallas import tpu as pltpu

F32 = jnp.float32
BF16 = jnp.bfloat16

GRID_W = 64
ROPE_BASE = 10000.0
LN_EPS = 1e-5
RMS_EPS = 1e-6
NEG_INF = -1e30
LB_FLOOR = 1e-30
LOG2_E = math.log2(math.e)
HEAD = 128
DA_QK = 64
SW_GROUP = 4
SW_BLOCK = 128
SW_WINDOW = 128
HG_CHUNK = 64
HG_UNROLL = 4
TM_PROJ = 256
TM_OUT = 512
SW_PER_STEP = 4
TM_FFN = 512
MOD_ROWS = 8
VMEM_LIMIT = 56 * 1024 * 1024


def _cparams(*sem):
    return pltpu.CompilerParams(dimension_semantics=sem, vmem_limit_bytes=VMEM_LIMIT)


def _dot(a, b):
    return jnp.dot(a, b, preferred_element_type=F32)


def _dot_t(a, b):
    return lax.dot_general(a, b, (((1,), (1,)), ((), ())), preferred_element_type=F32)


def _silu(x):
    return x / (1.0 + jnp.exp2(x * (-LOG2_E)))


def _layer_norm(y, g, b):
    mu = jnp.mean(y, axis=-1, keepdims=True)
    d = y - mu
    var = jnp.mean(d * d, axis=-1, keepdims=True)
    return d * lax.rsqrt(var + LN_EPS) * g + b


def _rms_norm(o, g):
    ms = jnp.mean(o * o, axis=-1, keepdims=True)
    return o * lax.rsqrt(ms + RMS_EPS) * g


def _rope(x, c, sa, sb, w):
    return x * c + pltpu.roll(x, HEAD - w, 1) * sa + pltpu.roll(x, w, 1) * sb


def _pick_tile(n, target):
    t = min(n, target)
    while n % t or t % 128:
        t -= 128
    return t


def _lb_kernel(logit_ref, loglb_ref, log1m_ref, onem_ref):
    depth = logit_ref.shape[0]
    x = [logit_ref[l] for l in range(depth)]
    m = functools.reduce(jnp.maximum, x)
    e = [jnp.exp(xi - m) for xi in x]
    tot = functools.reduce(lambda a, b: a + b, e)
    w = [ei / tot for ei in e]
    acc = jnp.zeros_like(w[0])
    for l in range(depth):
        acc = acc + w[l]
        lb = acc - w[0]
        loglb_ref[l] = jnp.log(jnp.maximum(lb, LB_FLOOR))
        log1m_ref[l] = jnp.log1p(-lb)
        onem_ref[l] = 1.0 - lb


def _lb_params(logits):
    shp = jax.ShapeDtypeStruct(logits.shape, F32)
    return pl.pallas_call(_lb_kernel, out_shape=(shp, shp, shp), name="hgrn_lb_params")(logits)


def _mod_kernel(c_ref, w_ref, b_ref, o_ref):
    s = _silu(c_ref[...])
    s_hi = s.astype(BF16)
    s_lo = (s - s_hi.astype(F32)).astype(BF16)
    w = w_ref[...]
    w_hi = w.astype(BF16)
    w_lo = (w - w_hi.astype(F32)).astype(BF16)
    o_ref[...] = _dot(s_hi, w_hi) + _dot(s_lo, w_hi) + _dot(s_hi, w_lo) + b_ref[...]


def _modulation(cond, w_mod, b_mod):
    depth, d, d6 = w_mod.shape
    tn = _pick_tile(d6, 1024)
    return pl.pallas_call(
        _mod_kernel,
        out_shape=jax.ShapeDtypeStruct((depth, MOD_ROWS, d6), F32),
        grid=(depth, d6 // tn),
        in_specs=[pl.BlockSpec((MOD_ROWS, d), lambda l, n: (0, 0)),
                  pl.BlockSpec((None, d, tn), lambda l, n: (l, 0, n)),
                  pl.BlockSpec((None, 1, tn), lambda l, n: (l, 0, n))],
        out_specs=pl.BlockSpec((None, MOD_ROWS, tn), lambda l, n: (l, 0, n)),
        compiler_params=_cparams("parallel", "parallel"),
        name="adaln_modulation",
    )(cond, w_mod, b_mod.reshape(depth, 1, d6))


def _proj_kernel(x_ref, sh_ref, sc_ref, w_ref, cast_ref, *rest, cache_cols):
    n = len(cache_cols)
    o_ref, mix_ref, cast_out_ref = rest[n:n + 3]
    h = (x_ref[...] * (1.0 + sc_ref[0]) + sh_ref[0]).astype(BF16)
    o_ref[...] = _dot(h, w_ref[...])
    mix_ref[...] = jnp.zeros_like(mix_ref)
    cast_out_ref[...] = cast_ref[...].astype(BF16)
    for c_ref, (off, nh) in zip(rest[n + 3:], cache_cols):
        for hd in range(nh):
            c_ref[hd] = o_ref[:, off + hd * HEAD:off + (hd + 1) * HEAD]


def _in_proj(x, mod, mod_row, w_in, layer, tm, d_mix, cast_src, caches=None, cache_cols=()):
    t, d = x.shape
    d_in = w_in.shape[1]
    steps = t // tm
    _, cast_rows, cast_width = cast_src.shape
    slab = cast_rows // steps
    assert cast_rows % steps == 0 and slab % 16 == 0
    caches = tuple(caches or ())
    out_shape = ([jax.ShapeDtypeStruct((t, d_in), F32), jax.ShapeDtypeStruct((t, d_mix), BF16),
                  jax.ShapeDtypeStruct((cast_rows, cast_width), BF16)]
                 + [jax.ShapeDtypeStruct(c.shape, c.dtype) for c in caches])
    out_specs = [pl.BlockSpec((tm, d_in), lambda m: (m, 0)), pl.BlockSpec((tm, d_mix), lambda m: (m, 0)),
                 pl.BlockSpec((slab, cast_width), lambda m: (m, 0))]
    for c in caches:
        assert c.shape[3] == tm
        out_specs.append(pl.BlockSpec((None, None, c.shape[2], tm, HEAD), lambda m: (m, layer, 0, 0, 0)))
    res = pl.pallas_call(
        functools.partial(_proj_kernel, cache_cols=tuple(cache_cols)),
        out_shape=tuple(out_shape),
        grid=(steps,),
        in_specs=[pl.BlockSpec((tm, d), lambda m: (m, 0)),
                  pl.BlockSpec((1, 1, d), lambda m: (mod_row(m, tm, 0), 0, 0)),
                  pl.BlockSpec((1, 1, d), lambda m: (mod_row(m, tm, 1), 0, 0)),
                  pl.BlockSpec((d, d_in), lambda m: (0, 0), pipeline_mode=pl.Buffered(1)),
                  pl.BlockSpec((None, slab, cast_width), lambda m: (layer, m, 0))]
                 + [pl.BlockSpec(memory_space=pl.ANY)] * len(caches),
        out_specs=tuple(out_specs),
        input_output_aliases={5 + i: 3 + i for i in range(len(caches))},
        compiler_params=_cparams("parallel"),
        name="in_proj",
    )(x, mod, mod, w_in, cast_src, *caches)
    return res[0], res[1], res[2], tuple(res[3:])


def _out_proj_kernel(mix_ref, w_ref, x_ref, g1_ref, lng_ref, lnb_ref, x1_ref, *, alpha):
    mix = _dot(mix_ref[...], w_ref[...])
    x1_ref[...] = _layer_norm(alpha * x_ref[...] + g1_ref[0] * mix, lng_ref[...], lnb_ref[...])


def _out_proj(mix, w_out, x, mod, mod_row, ln_g, ln_b, layer, tm, alpha):
    t, d = x.shape
    d_mix = mix.shape[1]
    depth = ln_g.shape[0]
    vec = pl.BlockSpec((None, 1, d), lambda m: (layer, 0, 0))
    return pl.pallas_call(
        functools.partial(_out_proj_kernel, alpha=alpha),
        out_shape=jax.ShapeDtypeStruct((t, d), F32),
        grid=(t // tm,),
        in_specs=[pl.BlockSpec((tm, d_mix), lambda m: (m, 0)),
                  pl.BlockSpec((d_mix, d), lambda m: (0, 0), pipeline_mode=pl.Buffered(1)),
                  pl.BlockSpec((tm, d), lambda m: (m, 0)),
                  pl.BlockSpec((1, 1, d), lambda m: (mod_row(m, tm, 2), 0, 0)),
                  vec, vec],
        out_specs=pl.BlockSpec((tm, d), lambda m: (m, 0)),
        compiler_params=_cparams("parallel"),
        name="out_proj_ln",
    )(mix, w_out, x, mod, ln_g.reshape(depth, 1, d), ln_b.reshape(depth, 1, d))


def _ffn_kernel(x_ref, sh2_ref, sc2_ref, wg_ref, wu_ref, wd_ref, g2_ref, lng_ref, lnb_ref, *rest, alpha, cast):
    if cast:
        cast_ref, o_ref, cast_out_ref, h_ref = rest
    else:
        o_ref, h_ref = rest
    f = pl.program_id(1)

    @pl.when(f == 0)
    def _():
        h_ref[...] = (x_ref[...] * (1.0 + sc2_ref[0]) + sh2_ref[0]).astype(BF16)
        o_ref[...] = jnp.zeros_like(o_ref)
        if cast:
            cast_out_ref[...] = cast_ref[...].astype(BF16)

    h = h_ref[...]
    a = _dot(h, wg_ref[...])
    u = _dot(h, wu_ref[...])
    o_ref[...] += _dot((_silu(a) * u).astype(BF16), wd_ref[...])

    @pl.when(f == pl.num_programs(1) - 1)
    def _():
        y = alpha * x_ref[...] + g2_ref[0] * o_ref[...]
        o_ref[...] = _layer_norm(y, lng_ref[...], lnb_ref[...])


def _ffn(x1, w_gu, w_down, mod, mod_row, ln_g, ln_b, layer, tm, alpha, cast_src=None):
    t, d = x1.shape
    d_ff = w_down.shape[0]
    depth = ln_g.shape[0]
    tf = _pick_tile(d_ff, 512)
    nf = d_ff // tf
    tiles = t // tm
    vec = pl.BlockSpec((None, 1, d), lambda m, f: (layer, 0, 0))
    row = lambda j: pl.BlockSpec((1, 1, d), lambda m, f: (mod_row(m, tm, j), 0, 0))
    in_specs = [pl.BlockSpec((tm, d), lambda m, f: (m, 0)),
                row(3), row(4),
                pl.BlockSpec((d, tf), lambda m, f: (0, f)),
                pl.BlockSpec((d, tf), lambda m, f: (0, nf + f)),
                pl.BlockSpec((tf, d), lambda m, f: (f, 0)),
                row(5), vec, vec]
    args = [x1, mod, mod, w_gu, w_gu, w_down, mod, ln_g.reshape(depth, 1, d), ln_b.reshape(depth, 1, d)]
    out_shape = [jax.ShapeDtypeStruct((t, d), F32)]
    out_specs = [pl.BlockSpec((tm, d), lambda m, f: (m, 0))]
    if cast_src is not None:
        _, rows, width = cast_src.shape
        slab = rows // tiles
        assert rows % tiles == 0 and slab % 16 == 0
        in_specs.append(pl.BlockSpec((None, slab, width), lambda m, f: (layer + 1, m, 0)))
        args.append(cast_src)
        out_shape.append(jax.ShapeDtypeStruct((rows, width), BF16))
        out_specs.append(pl.BlockSpec((slab, width), lambda m, f: (m, 0)))
    res = pl.pallas_call(
        functools.partial(_ffn_kernel, alpha=alpha, cast=cast_src is not None),
        out_shape=tuple(out_shape),
        grid=(tiles, nf),
        in_specs=in_specs,
        out_specs=tuple(out_specs),
        scratch_shapes=[pltpu.VMEM((tm, d), BF16)],
        compiler_params=_cparams("parallel", "arbitrary"),
        name="ffn_ln",
    )(*args)
    return (res[0], res[1]) if cast_src is not None else (res[0], None)


def _diff_attn_kernel(*refs, rope, cached, n_self, kb, lam_init, pipelined, n_tiles):
    it = iter(refs)
    q_ref, k_ref, v_ref = next(it), next(it), next(it)
    if rope:
        qc_ref, qsa_ref, qsb_ref, kc_ref, ksa_ref, ksb_ref = (next(it) for _ in range(6))
    if cached:
        ck_ref, cv_ref = next(it), next(it)
    lam_ref, g_ref, _, o_ref, kr_ref, vt_ref, m_ref, acc_ref = (next(it) for _ in range(8))
    s_refs = tuple(it)
    nblk = kr_ref.shape[0]
    nself = n_self // kb
    tq = q_ref.shape[0]
    step = pl.program_id(2)
    last = pl.num_programs(2) - 1

    def prepare_keys():
        for j in range(nself):
            rows = slice(j * kb, (j + 1) * kb)
            k = k_ref[rows, :]
            if rope:
                k = _rope(k, kc_ref[rows, :], ksa_ref[rows, :], ksb_ref[rows, :], DA_QK // 4)
            kr_ref[j] = k.astype(BF16)
            vt_ref[j] = v_ref[rows, :].T.astype(BF16)
        if cached:
            for j in range(nblk - nself):
                rows = slice(j * kb, (j + 1) * kb)
                kr_ref[nself + j] = ck_ref[rows, :].astype(BF16)
                vt_ref[nself + j] = cv_ref[rows, :].T.astype(BF16)

    def queries():
        q = q_ref[...]
        if rope:
            q = _rope(q, qc_ref[...], qsa_ref[...], qsb_ref[...], DA_QK // 4)
        q = q * (DA_QK ** -0.5 * LOG2_E)
        lane = lax.broadcasted_iota(jnp.int32, q.shape, 1)
        return (jnp.where(lane < DA_QK, q, 0.0).astype(BF16), jnp.where(lane >= DA_QK, q, 0.0).astype(BF16))

    def fold(x, op):
        return op(x.reshape(kb // 8, 8, tq), axis=0)

    def scores_block(j, qz, buf, ms):
        kblk = kr_ref[j]
        out = []
        for mp in range(2):
            s = _dot_t(kblk, qz[mp])
            buf[mp, j] = s
            out.append(jnp.maximum(ms[mp], fold(s, jnp.max)))
        return tuple(out)

    def values_block(j, buf, ms, ls):
        out = []
        for mp in range(2):
            p = jnp.exp2(buf[mp, j] - ms[mp])
            acc_ref[mp] += _dot(vt_ref[j], p.astype(BF16))
            out.append(ls[mp] + fold(p, jnp.sum))
        return tuple(out)

    neg = jnp.full((8, tq), -jnp.inf, F32)
    zero = jnp.zeros((8, tq), F32)

    def save_max(ms):
        for mp in range(2):
            m_ref[mp] = jnp.max(ms[mp], axis=0, keepdims=True)

    def finish(ls):
        l1, l2 = (jnp.sum(l, axis=0, keepdims=True) for l in ls)
        lp = lam_ref[...]
        lam = (jnp.exp(jnp.sum(lp[0:1] * lp[1:2], axis=-1, keepdims=True))
               - jnp.exp(jnp.sum(lp[2:3] * lp[3:4], axis=-1, keepdims=True)) + lam_init)
        ot = acc_ref[0] * (1.0 / l1) - acc_ref[1] * (lam / l2)
        ms_o = jnp.mean(ot * ot, axis=0, keepdims=True)
        ot = ot * lax.rsqrt(ms_o + RMS_EPS) * g_ref[...] * (1.0 - lam_init)
        o_ref[...] = ot.T.astype(BF16)

    if not pipelined:
        pl.when(step == 0)(prepare_keys)
        qz = queries()
        ms = (neg, neg)
        for j in range(nblk):
            ms = scores_block(j, qz, s_refs[0], ms)
        ms = tuple(jnp.max(m, axis=0, keepdims=True) for m in ms)
        acc_ref[...] = jnp.zeros_like(acc_ref)
        ls = (zero, zero)
        for j in range(nblk):
            ls = values_block(j, s_refs[0], ms, ls)
        finish(ls)
        return

    @pl.when(step == 0)
    def _():
        prepare_keys()
        qz = queries()
        ms = (neg, neg)
        for j in range(nblk):
            ms = scores_block(j, qz, s_refs[0], ms)
        save_max(ms)

    def interior(parity):
        prev = (m_ref[0], m_ref[1])
        qz = queries()
        acc_ref[...] = jnp.zeros_like(acc_ref)
        ms, ls = (neg, neg), (zero, zero)
        for j in range(nblk):
            ms = scores_block(j, qz, s_refs[parity], ms)
            ls = values_block(j, s_refs[1 - parity], prev, ls)
        save_max(ms)
        finish(ls)

    inside = (step > 0) & (step < last)
    pl.when(inside & (step % 2 == 1))(functools.partial(interior, 1))
    pl.when(inside & (step % 2 == 0))(functools.partial(interior, 0))

    @pl.when(step == last)
    def _():
        prev = (m_ref[0], m_ref[1])
        acc_ref[...] = jnp.zeros_like(acc_ref)
        ls = (zero, zero)
        for j in range(nblk):
            ls = values_block(j, s_refs[(n_tiles - 1) % 2], prev, ls)
        finish(ls)


def _diff_attn_seq_kernel(q_ref, k_ref, v_ref, lam_ref, g_ref, mix_ref, o_ref, *, heads, lam_init):
    del mix_ref
    n = q_ref.shape[0]
    lane = lax.broadcasted_iota(jnp.int32, (n, HEAD), 1)
    scores, vts = [], []
    for hd in range(heads):
        cols = slice(hd * HEAD, (hd + 1) * HEAD)
        q = q_ref[:, cols] * (DA_QK ** -0.5 * LOG2_E)
        k = k_ref[:, cols].astype(BF16)
        scores.append([_dot_t(k, jnp.where(lane < DA_QK, q, 0.0).astype(BF16)),
                       _dot_t(k, jnp.where(lane >= DA_QK, q, 0.0).astype(BF16))])
        vts.append(v_ref[:, cols].T.astype(BF16))
    probs = []
    for hd in range(heads):
        parts = []
        for s in scores[hd]:
            p = jnp.exp2(s - jnp.max(s, axis=0, keepdims=True))
            parts.append((p.astype(BF16), jnp.sum(p, axis=0, keepdims=True)))
        probs.append(parts)
    accs = [[_dot(vts[hd], pb) for pb, _ in probs[hd]] for hd in range(heads)]
    lp = lam_ref[...]
    lam = (jnp.exp(jnp.sum(lp[0:1] * lp[1:2], axis=-1, keepdims=True))
           - jnp.exp(jnp.sum(lp[2:3] * lp[3:4], axis=-1, keepdims=True)) + lam_init)
    for hd in range(heads):
        (_, l1), (_, l2) = probs[hd]
        ot = accs[hd][0] * (1.0 / l1) - accs[hd][1] * (lam / l2)
        ms_o = jnp.mean(ot * ot, axis=0, keepdims=True)
        ot = ot * lax.rsqrt(ms_o + RMS_EPS) * g_ref[...] * (1.0 - lam_init)
        o_ref[:, hd * HEAD:(hd + 1) * HEAD] = ot.T.astype(BF16)


def _diff_attn_seq(p, mix, geom, layer, diff_lambda, diff_g):
    n_seq, n = geom["n_seq"], geom["n"]
    heads = geom["da_heads"]
    depth = diff_g.shape[0]
    lam_init = 0.8 - 0.6 * math.exp(-0.3 * layer)
    blk = lambda j: pl.BlockSpec((n, heads * HEAD), lambda b: (b, j))
    return pl.pallas_call(
        functools.partial(_diff_attn_seq_kernel, heads=heads, lam_init=lam_init),
        out_shape=jax.ShapeDtypeStruct(mix.shape, mix.dtype),
        grid=(n_seq,),
        in_specs=[blk(0), blk(1), blk(2),
                  pl.BlockSpec((None, 4, DA_QK), lambda b: (layer, 0, 0)),
                  pl.BlockSpec((None, HEAD, 1), lambda b: (layer, 0, 0)),
                  pl.BlockSpec(memory_space=pl.ANY)],
        out_specs=blk(0),
        input_output_aliases={5: 0},
        compiler_params=_cparams("parallel"),
        name="diff_attention_seq",
    )(p, p, p, diff_lambda, diff_g.reshape(depth, HEAD, 1), mix)


def _diff_attn(p, mix, geom, layer, diff_lambda, diff_g, rope_tabs, cache_k, cache_v):
    if rope_tabs is None and cache_k is None and geom["n"] <= 256:
        return _diff_attn_seq(p, mix, geom, layer, diff_lambda, diff_g)
    n_seq, n, t = geom["n_seq"], geom["n"], geom["t"]
    heads = geom["da_heads"]
    rope = rope_tabs is not None
    cached = cache_k is not None
    tq = min(n, 256)
    kb = min(n, 256)
    nq = n // tq
    n_ctx = cache_k.shape[3] if cached else 0
    nblk = (n + n_ctx) // kb
    assert n % kb == 0 and n_ctx % kb == 0
    k_off, v_off = heads, 2 * heads
    depth = diff_g.shape[0]
    lam_init = 0.8 - 0.6 * math.exp(-0.3 * layer)

    pipelined = nq > 1
    q_tile = (lambda i: jnp.minimum(i, nq - 1)) if pipelined else (lambda i: i)
    o_tile = (lambda i: jnp.maximum(i - 1, 0)) if pipelined else (lambda i: i)
    in_specs = [pl.BlockSpec((tq, HEAD), lambda b, h, i: (b * nq + q_tile(i), h)),
                pl.BlockSpec((n, HEAD), lambda b, h, i: (b, k_off + h)),
                pl.BlockSpec((n, HEAD), lambda b, h, i: (b, v_off + h))]
    args = [p, p, p]
    if rope:
        in_specs += [pl.BlockSpec((tq, HEAD), lambda b, h, i: (q_tile(i), 0))] * 3
        in_specs += [pl.BlockSpec((n, HEAD), lambda b, h, i: (0, 0))] * 3
        args += list(rope_tabs) * 2
    if cached:
        spec = pl.BlockSpec((None, None, None, n_ctx, HEAD), lambda b, h, i: (b, layer, h, 0, 0))
        in_specs += [spec, spec]
        args += [cache_k, cache_v]
    in_specs += [pl.BlockSpec((None, 4, DA_QK), lambda b, h, i: (layer, 0, 0)),
                 pl.BlockSpec((None, HEAD, 1), lambda b, h, i: (layer, 0, 0)),
                 pl.BlockSpec(memory_space=pl.ANY)]
    args += [diff_lambda, diff_g.reshape(depth, HEAD, 1), mix]
    return pl.pallas_call(
        functools.partial(_diff_attn_kernel, rope=rope, cached=cached, n_self=n, kb=kb, lam_init=lam_init,
                          pipelined=pipelined, n_tiles=nq),
        out_shape=jax.ShapeDtypeStruct(mix.shape, mix.dtype),
        grid=(n_seq, heads, nq + 1 if pipelined else nq),
        in_specs=in_specs,
        input_output_aliases={len(args) - 1: 0},
        out_specs=pl.BlockSpec((tq, HEAD), lambda b, h, i: (b * nq + o_tile(i), h)),
        scratch_shapes=[pltpu.VMEM((nblk, kb, HEAD), BF16), pltpu.VMEM((nblk, HEAD, kb), BF16),
                        pltpu.VMEM((2, 1, tq), F32), pltpu.VMEM((2, HEAD, tq), F32)]
                       + [pltpu.VMEM((2, nblk, kb, tq), F32)] * (2 if pipelined else 1),
        compiler_params=_cparams("parallel", "parallel", "arbitrary"),
        name="diff_attention",
    )(*args)


def _scan_constants(c):
    levels = int(math.log2(c))
    t = np.arange(c)[:, None]
    s = np.arange(c)[None, :]
    tri = (s <= t).astype(np.float32)
    masks = []
    for j in range(levels):
        m = c >> (j + 1)
        base = (t // (2 * m)) * (2 * m)
        sbase = (s // (2 * m)) * (2 * m)
        masks.append((sbase == base) & ((t - base) >= m) & ((s - sbase) < m))
    masks.append(s == t)
    masks = np.stack(masks).astype(np.float32)
    flip = lambda a: a.reshape(-1, c, c)[:, ::-1, ::-1].reshape(a.shape)
    return (jnp.asarray(tri, BF16), jnp.asarray(flip(tri), BF16),
            jnp.asarray(masks, F32), jnp.asarray(flip(masks), F32))


def _level_exponents(cum, d, chunk):
    row = lax.broadcasted_iota(jnp.int32, cum.shape, 0)
    sub = lax.broadcasted_iota(jnp.int32, (8, HEAD), 0)
    out = []
    m = chunk // 2
    while m >= 1:
        in_second = (row & m) != 0
        q_side = in_second if d == 0 else jnp.logical_not(in_second)
        if m >= 8:
            pieces = []
            for b0 in range(0, chunk, 2 * m):
                r = b0 + m - 1 + d
                bnd = jnp.broadcast_to(cum[r:r + 1, :], (m, HEAD))
                first, second = cum[b0:b0 + m, :], cum[b0 + m:b0 + 2 * m, :]
                pieces += [bnd - first, second - bnd] if d == 0 else [first - bnd, bnd - second]
            x = jnp.concatenate(pieces, axis=0)
        elif m == 1:
            neighbour = pltpu.roll(cum, 1 if d == 0 else chunk - 1, 0)
            x = jnp.where(q_side, cum - neighbour, 0.0)
        else:
            tiles = []
            for v in range(chunk // 8):
                if m == 4:
                    r = 8 * v + 3 + d
                    tiles.append(jnp.broadcast_to(cum[r:r + 1, :], (8, HEAD)))
                else:
                    lo = jnp.broadcast_to(cum[8 * v + 1 + d:8 * v + 2 + d, :], (8, HEAD))
                    hi = jnp.broadcast_to(cum[8 * v + 5 + d:8 * v + 6 + d, :], (8, HEAD))
                    tiles.append(jnp.where(sub < 4, lo, hi))
            bnd = jnp.concatenate(tiles, axis=0)
            x = jnp.where(q_side, cum - bnd, bnd - cum)
        out.append(x)
        m //= 2
    return out


def _hgrn_kernel(*refs, n, chunk, unroll, has_state, emit_state):
    it = iter(refs)
    q_ref, zf_ref, zb_ref, v_ref, hg_ref = (next(it) for _ in range(5))
    loglb_ref, log1m_ref, onem_ref, g_ref = (next(it) for _ in range(4))
    tri_refs = (next(it), next(it))
    mask_refs = (next(it), next(it))
    s0_ref = next(it) if has_state else None
    next(it)
    if emit_state:
        next(it)
    o_ref = next(it)
    st_ref = next(it) if emit_state else None
    obuf_refs = (next(it), next(it))
    s_ref = next(it)

    levels = int(math.log2(chunk))
    nchunks = n // chunk
    z_refs = (zf_ref, zb_ref)

    for d in range(2):
        s_ref[d] = s0_ref[d].T if has_state else jnp.zeros((HEAD, HEAD), F32)

    def gates(d, start):
        rows = pl.ds(start, chunk)
        z = z_refs[d][rows, :]
        q = _silu(q_ref[rows, :])
        v = v_ref[rows, :].astype(BF16)
        z2 = z * LOG2_E
        e = jnp.exp2(-jnp.abs(z2))
        one_e = 1.0 + e
        log_sig = jnp.minimum(z2, 0.0) - jnp.log2(one_e)
        sig_neg = jnp.where(z >= 0.0, e, 1.0) / one_e
        a = loglb_ref[d:d + 1, :] * LOG2_E
        b = log1m_ref[d:d + 1, :] * LOG2_E + log_sig
        g = jnp.maximum(a, b) + jnp.log2(1.0 + jnp.exp2(-jnp.abs(a - b)))
        k = onem_ref[d:d + 1, :] * sig_neg
        g_hi = g.astype(BF16)
        g_lo = (g - g_hi.astype(F32)).astype(BF16)
        cum = _dot(tri_refs[d][...], jnp.concatenate([g_hi, g_lo], axis=1))
        return q, k, v, cum[:, :HEAD] + cum[:, HEAD:]

    def body(i, carry):
        chains = []
        for u in range(unroll):
            c = i * unroll + u
            chains.append((0, pl.multiple_of(c * chunk, chunk)))
            chains.append((1, pl.multiple_of((nchunks - 1 - c) * chunk, chunk)))
        work = [gates(d, start) for d, start in chains]
        qk16 = [(q.astype(BF16), k.astype(BF16)) for q, k, _, _ in work]
        atts = [mask_refs[d][levels] * _dot_t(qb, kb) for (d, _), (qb, kb) in zip(chains, qk16)]
        expo = [_level_exponents(cum, d, chunk) for (d, _), (_, _, _, cum) in zip(chains, work)]
        for j in range(levels):
            for ci, ((d, _), (qb, kb)) in enumerate(zip(chains, qk16)):
                fac = jnp.exp2(expo[ci][j]).astype(BF16)
                atts[ci] = atts[ci] + mask_refs[d][j] * _dot_t(qb * fac, kb * fac)
        intra, delta, q_in, decay = [], [], [], []
        for (d, _), (q, k, v, cum), att in zip(chains, work, atts):
            last = cum[chunk - 1:chunk] if d == 0 else cum[0:1]
            kt = (k * jnp.exp2(last - cum)).astype(BF16)
            intra.append(_dot(att.astype(BF16), v))
            delta.append(lax.dot_general(v, kt, (((0,), (0,)), ((), ())), preferred_element_type=F32))
            q_in.append((q * jnp.exp2(cum)).astype(BF16))
            decay.append(jnp.exp2(last))
        st = [s_ref[0], s_ref[1]]
        for ci, (d, start) in enumerate(chains):
            obuf_refs[d][pl.ds(start, chunk), :] = intra[ci] + _dot_t(q_in[ci], st[d].astype(BF16))
            st[d] = decay[ci] * st[d] + delta[ci]
        s_ref[0] = st[0]
        s_ref[1] = st[1]
        return carry

    lax.fori_loop(0, nchunks // unroll, body, 0)

    o = obuf_refs[0][...] + obuf_refs[1][...]
    o_ref[...] = (_rms_norm(o, g_ref[...]) * _silu(hg_ref[...])).astype(BF16)
    if emit_state:
        for d in range(2):
            st_ref[d] = s_ref[d].T


def _hgrn(p, mix, geom, layer, lb_params, hgrn_g, consts, state, new_state=None):
    n_seq, n, t = geom["n_seq"], geom["n"], geom["t"]
    heads = geom["hg_heads"]
    mix_col = geom["da_heads"]
    base = 3 * geom["da_heads"]
    has_state = state is not None
    emit_state = not has_state
    depth = hgrn_g.shape[0]
    chunk = min(HG_CHUNK, n)
    col = lambda j: pl.BlockSpec((n, HEAD), lambda b, h: (b, base + j * heads + h))
    lbspec = pl.BlockSpec((None, 2, HEAD), lambda b, h: (layer, 0, h))
    const_specs = [pl.BlockSpec(c.shape, lambda b, h, nd=c.ndim: (0,) * nd) for c in consts]
    in_specs = [col(0), col(1), col(2), col(3), col(4), lbspec, lbspec, lbspec,
                pl.BlockSpec((None, 1, HEAD), lambda b, h: (layer, 0, 0))] + const_specs
    args = [p] * 5 + list(lb_params) + [hgrn_g.reshape(depth, 1, HEAD)] + list(consts)
    if has_state:
        in_specs.append(pl.BlockSpec((None, None, 2, None, HEAD, HEAD), lambda b, h: (b, layer, 0, h, 0, 0)))
        args.append(state)
    in_specs.append(pl.BlockSpec(memory_space=pl.ANY))
    args.append(mix)
    aliases = {len(args) - 1: 0}
    out_shape = [jax.ShapeDtypeStruct(mix.shape, mix.dtype)]
    out_specs = [pl.BlockSpec((n, HEAD), lambda b, h: (b, mix_col + h))]
    if emit_state:
        in_specs.append(pl.BlockSpec(memory_space=pl.ANY))
        args.append(new_state)
        aliases[len(args) - 1] = 1
        out_shape.append(jax.ShapeDtypeStruct(new_state.shape, new_state.dtype))
        out_specs.append(pl.BlockSpec((None, None, 2, None, HEAD, HEAD), lambda b, h: (b, layer, 0, h, 0, 0)))
    res = pl.pallas_call(
        functools.partial(_hgrn_kernel, n=n, chunk=chunk, unroll=HG_UNROLL if (n // chunk) % HG_UNROLL == 0 else 1,
                          has_state=has_state, emit_state=emit_state),
        out_shape=tuple(out_shape),
        grid=(n_seq, heads),
        in_specs=in_specs,
        out_specs=tuple(out_specs),
        scratch_shapes=[pltpu.VMEM((n, HEAD), F32), pltpu.VMEM((n, HEAD), F32),
                        pltpu.VMEM((2, HEAD, HEAD), F32)],
        input_output_aliases=aliases,
        compiler_params=_cparams("parallel", "parallel"),
        name="hgrn2_scan",
    )(*args)
    return res if emit_state else (res[0], None)


def _swa_ctx_kernel(q_ref, k_ref, v_ref, sink_ref, mix_ref, o_ref, *, layer, kvh):
    del mix_ref
    n = q_ref.shape[0]
    lane = lax.broadcasted_iota(jnp.int32, (1, SW_GROUP * n), 1)

    def fold(x, op):
        return op(op(x.reshape(x.shape[0] // 8, 8, x.shape[1]), axis=0), axis=0, keepdims=True)

    scores, sinks = [], []
    for kv in range(kvh):
        q_all = jnp.concatenate(
            [(q_ref[:, (kv * SW_GROUP + g) * HEAD:(kv * SW_GROUP + g + 1) * HEAD] * (HEAD ** -0.5)).astype(BF16)
             for g in range(SW_GROUP)], axis=0)
        scores.append(_dot_t(k_ref[:, kv * HEAD:(kv + 1) * HEAD].astype(BF16), q_all))
        sink = jnp.zeros((1, SW_GROUP * n), F32)
        for g in range(SW_GROUP):
            sink = jnp.where(lane // n == g, sink_ref[layer, kv * SW_GROUP + g], sink)
        sinks.append(sink)
    probs = []
    for s, sink in zip(scores, sinks):
        m = jnp.maximum(fold(s, jnp.max), sink)
        p = jnp.exp(s - m)
        inv = 1.0 / (fold(p, jnp.sum) + jnp.exp(sink - m))
        probs.append((p * inv).astype(BF16))
    outs = [_dot(v_ref[:, kv * HEAD:(kv + 1) * HEAD].T.astype(BF16), probs[kv]) for kv in range(kvh)]
    for kv in range(kvh):
        for g in range(SW_GROUP):
            hd = kv * SW_GROUP + g
            o_ref[:, hd * HEAD:(hd + 1) * HEAD] = outs[kv][:, g * n:(g + 1) * n].T.astype(BF16)


def _swa_ctx(p, mix, geom, layer, sink):
    n_seq, n, t = geom["n_seq"], geom["n"], geom["t"]
    kvh = geom["sw_kv"]
    qw = kvh * SW_GROUP * HEAD
    kw = kvh * HEAD
    q_blk = geom["cq_off"] // qw
    k_blk = geom["ck_off"] // kw
    mix_blk = (geom["da_heads"] + geom["hg_heads"]) * HEAD // qw
    assert geom["cq_off"] % qw == 0 and geom["ck_off"] % kw == 0
    return pl.pallas_call(
        functools.partial(_swa_ctx_kernel, layer=layer, kvh=kvh),
        out_shape=jax.ShapeDtypeStruct(mix.shape, mix.dtype),
        grid=(n_seq,),
        in_specs=[pl.BlockSpec((n, qw), lambda b: (b, q_blk)),
                  pl.BlockSpec((n, kw), lambda b: (b, k_blk)),
                  pl.BlockSpec((n, kw), lambda b: (b, k_blk + 1)),
                  pl.BlockSpec(memory_space=pltpu.SMEM),
                  pl.BlockSpec(memory_space=pl.ANY)],
        out_specs=pl.BlockSpec((n, qw), lambda b: (b, mix_blk)),
        input_output_aliases={4: 0},
        compiler_params=_cparams("parallel"),
        name="sink_attention",
    )(p, p, p, sink, mix)


def _band_masks(n):
    r = np.arange(3 * SW_BLOCK)[:, None]
    i = (np.arange(SW_GROUP * SW_BLOCK) % SW_BLOCK)[None, :]
    window = np.abs(SW_BLOCK + i - r) <= SW_WINDOW
    not_before = r >= SW_BLOCK
    not_after = r < 2 * SW_BLOCK
    variants = [window, window & not_before, window & not_after, window & not_before & not_after]
    return jnp.asarray(np.stack(variants).astype(np.float32))


def _swa_lat_kernel(q_ref, k_ref, v_ref, qc_ref, qsa_ref, qsb_ref, kc_ref, ksa_ref, ksb_ref, ck_ref, cv_ref,
                    mask_ref, sink_ref, mix_ref, o_ref, kr_ref, vt_ref, kctx_ref, vctxt_ref, *, layer, n):
    kv = pl.program_id(1)
    step = pl.program_id(2)
    nb = n // SW_BLOCK
    w = HEAD // 4
    band = 3 * SW_BLOCK
    per_step = q_ref.shape[0] // SW_BLOCK

    @pl.when(step == 0)
    def _():
        zeros = jnp.zeros((SW_BLOCK, HEAD), BF16)
        for j in (0, nb + 1):
            kr_ref[j] = zeros
            vt_ref[j] = zeros

        def fill(j, carry):
            rows = pl.ds(pl.multiple_of(j * SW_BLOCK, SW_BLOCK), SW_BLOCK)
            k = _rope(k_ref[rows, :], kc_ref[rows, :], ksa_ref[rows, :], ksb_ref[rows, :], w)
            kr_ref[j + 1] = k.astype(BF16)
            vt_ref[j + 1] = v_ref[rows, :].T.astype(BF16)
            return carry

        lax.fori_loop(0, nb, fill, 0)
        kctx_ref[...] = ck_ref[...].astype(BF16)
        vctxt_ref[...] = cv_ref[...].T.astype(BF16)

    lane = lax.broadcasted_iota(jnp.int32, (1, SW_GROUP * SW_BLOCK), 1)
    sink = jnp.zeros((1, SW_GROUP * SW_BLOCK), F32)
    for g in range(SW_GROUP):
        sink = jnp.where(lane // SW_BLOCK == g, sink_ref[layer, kv * SW_GROUP + g], sink)

    def fold(x, op):
        return op(op(x.reshape(x.shape[0] // 8, 8, x.shape[1]), axis=0), axis=0, keepdims=True)

    blocks = [step * per_step + c for c in range(per_step)]
    rows = [slice(c * SW_BLOCK, (c + 1) * SW_BLOCK) for c in range(per_step)]
    scores = []
    for qb, r in zip(blocks, rows):
        qc, qsa, qsb = qc_ref[r, :], qsa_ref[r, :], qsb_ref[r, :]
        q_all = jnp.concatenate(
            [(_rope(q_ref[r, g * HEAD:(g + 1) * HEAD], qc, qsa, qsb, w) * (HEAD ** -0.5)).astype(BF16)
             for g in range(SW_GROUP)], axis=0)
        k_all = jnp.concatenate([kr_ref[qb], kr_ref[qb + 1], kr_ref[qb + 2], kctx_ref[...]], axis=0)
        scores.append(_dot_t(k_all, q_all))
    probs = []
    for qb, s in zip(blocks, scores):
        variant = jnp.where(qb == 0, 1, 0) + jnp.where(qb == nb - 1, 2, 0)
        s_band = jnp.where(mask_ref[variant] > 0.5, s[:band], NEG_INF)
        s_ctx = s[band:]
        m = jnp.maximum(jnp.maximum(fold(s_band, jnp.max), fold(s_ctx, jnp.max)), sink)
        p_band = jnp.exp(s_band - m)
        p_ctx = jnp.exp(s_ctx - m)
        inv = 1.0 / (fold(p_band, jnp.sum) + fold(p_ctx, jnp.sum) + jnp.exp(sink - m))
        probs.append(jnp.concatenate([p_band * inv, p_ctx * inv], axis=0).astype(BF16))
    outs = []
    for qb, p_all in zip(blocks, probs):
        vt_all = jnp.concatenate([vt_ref[qb], vt_ref[qb + 1], vt_ref[qb + 2], vctxt_ref[...]], axis=1)
        outs.append(_dot(vt_all, p_all))
    for r, ot in zip(rows, outs):
        for g in range(SW_GROUP):
            o_ref[r, g * HEAD:(g + 1) * HEAD] = ot[:, g * SW_BLOCK:(g + 1) * SW_BLOCK].T.astype(BF16)


def _swa_lat(p, mix, geom, layer, sink, rope_tabs, cache_k, cache_v):
    n_seq, n, t = geom["n_seq"], geom["n"], geom["t"]
    kvh = geom["sw_kv"]
    qw = SW_GROUP * HEAD
    q_blk = geom["cq_off"] // qw
    k_blk = geom["ck_off"] // HEAD
    v_blk = k_blk + kvh
    mix_blk = (geom["da_heads"] + geom["hg_heads"]) * HEAD // qw
    nb = n // SW_BLOCK
    per_step = SW_PER_STEP if nb % SW_PER_STEP == 0 else 1
    nsteps = nb // per_step
    tq = per_step * SW_BLOCK
    n_ctx = cache_k.shape[3]
    masks = _band_masks(n)
    cspec = pl.BlockSpec((None, None, None, n_ctx, HEAD), lambda b, kv, i: (b, layer, kv, 0, 0))
    return pl.pallas_call(
        functools.partial(_swa_lat_kernel, layer=layer, n=n),
        out_shape=jax.ShapeDtypeStruct(mix.shape, mix.dtype),
        grid=(n_seq, kvh, nsteps),
        in_specs=[pl.BlockSpec((tq, qw), lambda b, kv, i: (b * nsteps + i, q_blk + kv)),
                  pl.BlockSpec((n, HEAD), lambda b, kv, i: (b, k_blk + kv)),
                  pl.BlockSpec((n, HEAD), lambda b, kv, i: (b, v_blk + kv))]
                 + [pl.BlockSpec((tq, HEAD), lambda b, kv, i: (i, 0))] * 3
                 + [pl.BlockSpec((n, HEAD), lambda b, kv, i: (0, 0))] * 3
                 + [cspec, cspec, pl.BlockSpec(masks.shape, lambda b, kv, i: (0, 0, 0)),
                    pl.BlockSpec(memory_space=pltpu.SMEM), pl.BlockSpec(memory_space=pl.ANY)],
        out_specs=pl.BlockSpec((tq, qw), lambda b, kv, i: (b * nsteps + i, mix_blk + kv)),
        scratch_shapes=[pltpu.VMEM((nb + 2, SW_BLOCK, HEAD), BF16), pltpu.VMEM((nb + 2, HEAD, SW_BLOCK), BF16),
                        pltpu.VMEM((n_ctx, HEAD), BF16), pltpu.VMEM((HEAD, n_ctx), BF16)],
        input_output_aliases={13: 0},
        compiler_params=_cparams("parallel", "parallel", "arbitrary"),
        name="banded_sink_attention",
    )(p, p, p, *rope_tabs, *rope_tabs, cache_k, cache_v, masks, sink, mix)


def _rope_tables(n, half):
    h = half // 2
    pos = jnp.arange(n)
    inv = ROPE_BASE ** (-jnp.arange(h, dtype=F32) / h)
    zero = jnp.zeros((n, h), F32)
    c, sa, sb = [], [], []
    for axis_pos in (pos // GRID_W, pos % GRID_W):
        ang = axis_pos.astype(F32)[:, None] * inv[None, :]
        cos, sin = jnp.cos(ang), jnp.sin(ang)
        c += [cos, cos]
        sa += [-sin, zero]
        sb += [zero, sin]
    reps = HEAD // (2 * half)
    cat = lambda parts: jnp.tile(jnp.concatenate(parts, axis=1), (1, reps))
    return cat(c), cat(sa), cat(sb)


def _geometry(n_seq, n, d_model):
    da_heads = hg_heads = d_model // 512
    sw_heads = d_model // 256
    sw_kv = sw_heads // SW_GROUP
    cq_off = (3 * da_heads + 5 * hg_heads) * HEAD
    return dict(n_seq=n_seq, n=n, t=n_seq * n, da_heads=da_heads, hg_heads=hg_heads, sw_kv=sw_kv,
                cq_off=cq_off, ck_off=cq_off + sw_heads * HEAD, d_mix=(da_heads + hg_heads + sw_heads) * HEAD)


def _token_tiles(n):
    return _pick_tile(n, TM_PROJ), _pick_tile(n, TM_OUT), _pick_tile(n, TM_FFN)


def _project(x, geom, layer, mod, mod_row, w_in, cast_src, new_caches):
    if new_caches is None:
        tm = _token_tiles(geom["n"])[0]
        return _in_proj(x, mod, mod_row, w_in, layer, tm, geom["d_mix"], cast_src)
    heads, kvh = geom["da_heads"], geom["sw_kv"]
    cols = ((heads * HEAD, heads), (2 * heads * HEAD, heads), (geom["ck_off"], kvh),
            (geom["ck_off"] + kvh * HEAD, kvh))
    return _in_proj(x, mod, mod_row, w_in, layer, geom["n"], geom["d_mix"], cast_src, new_caches, cols)


def _mix_and_ffn(x, p, mix, geom, layer, mod, mod_row, wts, params, lb_params, scan_consts, tabs, caches,
                 new_state, alpha, cast_src):
    w_out, w_gu, w_down = wts
    _, tm_out, tm_ffn = _token_tiles(geom["t"] if caches is None else geom["n"])
    if caches is None:
        mix = _diff_attn(p, mix, geom, layer, params["diff_lambda"], params["diff_norm_g"], None, None, None)
        mix, new_state = _hgrn(p, mix, geom, layer, lb_params, params["hgrn_norm_g"], scan_consts, None, new_state)
        mix = _swa_ctx(p, mix, geom, layer, params["swa_sink"])
    else:
        ck_d, cv_d, ck_s, cv_s, state = caches
        mix = _diff_attn(p, mix, geom, layer, params["diff_lambda"], params["diff_norm_g"], tabs[0], ck_d, cv_d)
        mix, _ = _hgrn(p, mix, geom, layer, lb_params, params["hgrn_norm_g"], scan_consts, state)
        mix = _swa_lat(p, mix, geom, layer, params["swa_sink"], tabs[1], ck_s, cv_s)
    x1 = _out_proj(mix, w_out, x, mod, mod_row, params["ln1_g"], params["ln1_b"], layer, tm_out, alpha)
    y, next_w = _ffn(x1, w_gu, w_down, mod, mod_row, params["ln2_g"], params["ln2_b"], layer, tm_ffn, alpha, cast_src)
    return y, new_state, next_w


def kernel(x_prompt, x_sample, cache_diff_k, cache_diff_v, cache_swa_k, cache_swa_v, state_hgrn, c, c_ctx, w_mod,
           b_mod, w_in, w_out, diff_lambda, diff_norm_g, hgrn_lb_logits, hgrn_norm_g, swa_sink, ln1_g, ln1_b, ln2_g,
           ln2_b, w_gate_up, w_down):
    batch, seq, d = x_prompt.shape
    dec_batch, dec_seq, _ = x_sample.shape
    depth = w_mod.shape[0]
    alpha = (2 * depth) ** 0.25
    geom_c = _geometry(batch, seq, d)
    geom_l = _geometry(dec_batch, dec_seq, d)
    assert 1 + dec_batch <= MOD_ROWS

    cond = jnp.zeros((MOD_ROWS, d), F32).at[0].set(c_ctx).at[1:1 + dec_batch].set(c)
    mod = _modulation(cond, w_mod, b_mod).reshape(depth * MOD_ROWS * 6, 1, d)
    lb_params = _lb_params(hgrn_lb_logits)
    params = dict(diff_lambda=diff_lambda, diff_norm_g=diff_norm_g, hgrn_norm_g=hgrn_norm_g, swa_sink=swa_sink,
                  ln1_g=ln1_g, ln1_b=ln1_b, ln2_g=ln2_g, ln2_b=ln2_b)
    w_in16, w_out16 = w_in[0].astype(BF16), w_out[0].astype(BF16)
    tabs = (_rope_tables(dec_seq, DA_QK // 2), _rope_tables(dec_seq, HEAD // 2))
    consts_c = _scan_constants(min(HG_CHUNK, seq))
    consts_l = _scan_constants(min(HG_CHUNK, dec_seq))

    y_p = x_prompt.reshape(batch * seq, d)
    y_s = x_sample.reshape(dec_batch * dec_seq, d)
    heads, kvh = geom_c["da_heads"], geom_c["sw_kv"]
    new_kv = tuple(jnp.zeros((batch, depth, nh, seq, HEAD), F32) for nh in (heads, heads, kvh, kvh))
    new_state = jnp.zeros((batch, depth, 2, geom_c["hg_heads"], HEAD, HEAD), F32)
    caches = (cache_diff_k, cache_diff_v, cache_swa_k, cache_swa_v, state_hgrn)
    for l in range(depth):
        row_c = lambda m, tm, j, l=l: (l * MOD_ROWS) * 6 + j
        row_l = lambda m, tm, j, l=l: (l * MOD_ROWS + 1 + (m * tm) // dec_seq) * 6 + j
        p_c, mix_c, w_down16, new_kv = _project(y_p, geom_c, l, mod, row_c, w_in16, w_down, new_kv)
        p_l, mix_l, w_gu16, _ = _project(y_s, geom_l, l, mod, row_l, w_in16, w_gate_up, None)
        wts = (w_out16, w_gu16, w_down16)
        more = l + 1 < depth
        y_p, new_state, w_out16 = _mix_and_ffn(y_p, p_c, mix_c, geom_c, l, mod, row_c, wts, params, lb_params,
                                               consts_c, None, None, new_state, alpha, w_out if more else None)
        y_s, _, w_in16 = _mix_and_ffn(y_s, p_l, mix_l, geom_l, l, mod, row_l, wts, params, lb_params, consts_l,
                                      tabs, caches, None, alpha, w_in if more else None)
    return (y_p.reshape(batch, seq, d), y_s.reshape(dec_batch, dec_seq, d)) + new_kv + (new_state,)
```

```python
import functools
import math

import numpy as np
import jax
import jax.numpy as jnp
from jax import lax
from jax.experimental import pallas as pl
from jax.experimental.pallas import tpu as pltpu

F32 = jnp.float32
BF16 = jnp.bfloat16

GRID_W = 64
ROPE_BASE = 10000.0
LN_EPS = 1e-5
RMS_EPS = 1e-6
NEG_INF = -1e30
LB_FLOOR = 1e-30
LOG2_E = math.log2(math.e)
HEAD = 128
DA_QK = 64
SW_GROUP = 4
SW_BLOCK = 128
SW_WINDOW = 128
MOD_ROWS = 8

VMEM_LIMIT = 56 * 1024 * 1024
TM_PROJ = 256
TM_OUT = 512
TM_FFN = 512
TF_FFN = 512
TN_MOD = 1024
DA_TQ = 256
DA_KB = 256
HG_CHUNK = 64
HG_UNROLL = 4
SW_PER_STEP = 4


def _cparams(*sem):
    return pltpu.CompilerParams(dimension_semantics=sem, vmem_limit_bytes=VMEM_LIMIT)


def _dot(a, b):
    return jnp.dot(a, b, preferred_element_type=F32)


def _dot_t(a, b):
    return lax.dot_general(a, b, (((1,), (1,)), ((), ())), preferred_element_type=F32)


def _silu(x):
    return x / (1.0 + jnp.exp2(x * (-LOG2_E)))


def _layer_norm(y, g, b):
    mu = jnp.mean(y, axis=-1, keepdims=True)
    d = y - mu
    var = jnp.mean(d * d, axis=-1, keepdims=True)
    return d * lax.rsqrt(var + LN_EPS) * g + b


def _rms_norm(o, g):
    ms = jnp.mean(o * o, axis=-1, keepdims=True)
    return o * lax.rsqrt(ms + RMS_EPS) * g


def _rope(x, c, sa, sb, w):
    return x * c + pltpu.roll(x, HEAD - w, 1) * sa + pltpu.roll(x, w, 1) * sb


def _pick_tile(n, target):
    t = min(n, target)
    while n % t or t % 128:
        t -= 128
    return t


def _lb_kernel(logit_ref, loglb_ref, log1m_ref, onem_ref):
    depth = logit_ref.shape[0]
    x = [logit_ref[l] for l in range(depth)]
    m = functools.reduce(jnp.maximum, x)
    e = [jnp.exp(xi - m) for xi in x]
    tot = functools.reduce(lambda a, b: a + b, e)
    w = [ei / tot for ei in e]
    acc = jnp.zeros_like(w[0])
    for l in range(depth):
        acc = acc + w[l]
        lb = acc - w[0]
        loglb_ref[l] = jnp.log(jnp.maximum(lb, LB_FLOOR))
        log1m_ref[l] = jnp.log1p(-lb)
        onem_ref[l] = 1.0 - lb


def _lb_params(logits):
    shp = jax.ShapeDtypeStruct(logits.shape, F32)
    return pl.pallas_call(_lb_kernel, out_shape=(shp, shp, shp), name="hgrn_lb_params")(logits)


def _mod_kernel(c_ref, w_ref, b_ref, o_ref):
    s = _silu(c_ref[...])
    s_hi = s.astype(BF16)
    s_lo = (s - s_hi.astype(F32)).astype(BF16)
    w = w_ref[...]
    w_hi = w.astype(BF16)
    w_lo = (w - w_hi.astype(F32)).astype(BF16)
    o_ref[...] = _dot(s_hi, w_hi) + _dot(s_lo, w_hi) + _dot(s_hi, w_lo) + b_ref[...]


def _modulation(cond, w_mod, b_mod):
    depth, d, d6 = w_mod.shape
    tn = _pick_tile(d6, TN_MOD)
    return pl.pallas_call(
        _mod_kernel,
        out_shape=jax.ShapeDtypeStruct((depth, MOD_ROWS, d6), F32),
        grid=(depth, d6 // tn),
        in_specs=[pl.BlockSpec((MOD_ROWS, d), lambda l, n: (0, 0)),
                  pl.BlockSpec((None, d, tn), lambda l, n: (l, 0, n)),
                  pl.BlockSpec((None, 1, tn), lambda l, n: (l, 0, n))],
        out_specs=pl.BlockSpec((None, MOD_ROWS, tn), lambda l, n: (l, 0, n)),
        compiler_params=_cparams("parallel", "parallel"),
        name="adaln_modulation",
    )(cond, w_mod, b_mod.reshape(depth, 1, d6))


def _proj_kernel(x_ref, sh_ref, sc_ref, w_ref, cast_ref, *rest, cache_cols):
    n = len(cache_cols)
    o_ref, mix_ref, cast_out_ref = rest[n:n + 3]
    h = (x_ref[...] * (1.0 + sc_ref[0]) + sh_ref[0]).astype(BF16)
    o_ref[...] = _dot(h, w_ref[...])
    mix_ref[...] = jnp.zeros_like(mix_ref)
    cast_out_ref[...] = cast_ref[...].astype(BF16)
    for c_ref, (off, nh) in zip(rest[n + 3:], cache_cols):
        for hd in range(nh):
            c_ref[hd] = o_ref[:, off + hd * HEAD:off + (hd + 1) * HEAD]


def _in_proj(x, mod, mod_row, w_in, layer, tm, d_mix, cast_src, caches=None, cache_cols=()):
    t, d = x.shape
    d_in = w_in.shape[1]
    steps = t // tm
    _, cast_rows, cast_width = cast_src.shape
    slab = cast_rows // steps
    assert cast_rows % steps == 0 and slab % 16 == 0
    caches = tuple(caches or ())
    out_shape = ([jax.ShapeDtypeStruct((t, d_in), F32), jax.ShapeDtypeStruct((t, d_mix), BF16),
                  jax.ShapeDtypeStruct((cast_rows, cast_width), BF16)]
                 + [jax.ShapeDtypeStruct(c.shape, c.dtype) for c in caches])
    out_specs = [pl.BlockSpec((tm, d_in), lambda m: (m, 0)), pl.BlockSpec((tm, d_mix), lambda m: (m, 0)),
                 pl.BlockSpec((slab, cast_width), lambda m: (m, 0))]
    for c in caches:
        assert c.shape[3] == tm
        out_specs.append(pl.BlockSpec((None, None, c.shape[2], tm, HEAD), lambda m: (m, layer, 0, 0, 0)))
    res = pl.pallas_call(
        functools.partial(_proj_kernel, cache_cols=tuple(cache_cols)),
        out_shape=tuple(out_shape),
        grid=(steps,),
        in_specs=[pl.BlockSpec((tm, d), lambda m: (m, 0)),
                  pl.BlockSpec((1, 1, d), lambda m: (mod_row(m, tm, 0), 0, 0)),
                  pl.BlockSpec((1, 1, d), lambda m: (mod_row(m, tm, 1), 0, 0)),
                  pl.BlockSpec((d, d_in), lambda m: (0, 0), pipeline_mode=pl.Buffered(1)),
                  pl.BlockSpec((None, slab, cast_width), lambda m: (layer, m, 0))]
                 + [pl.BlockSpec(memory_space=pl.ANY)] * len(caches),
        out_specs=tuple(out_specs),
        input_output_aliases={5 + i: 3 + i for i in range(len(caches))},
        compiler_params=_cparams("parallel"),
        name="in_proj",
    )(x, mod, mod, w_in, cast_src, *caches)
    return res[0], res[1], res[2], tuple(res[3:])


def _out_proj_kernel(mix_ref, w_ref, x_ref, g1_ref, lng_ref, lnb_ref, x1_ref, *, alpha):
    mix = _dot(mix_ref[...], w_ref[...])
    x1_ref[...] = _layer_norm(alpha * x_ref[...] + g1_ref[0] * mix, lng_ref[...], lnb_ref[...])


def _out_proj(mix, w_out, x, mod, mod_row, ln_g, ln_b, layer, tm, alpha):
    t, d = x.shape
    d_mix = mix.shape[1]
    depth = ln_g.shape[0]
    vec = pl.BlockSpec((None, 1, d), lambda m: (layer, 0, 0))
    return pl.pallas_call(
        functools.partial(_out_proj_kernel, alpha=alpha),
        out_shape=jax.ShapeDtypeStruct((t, d), F32),
        grid=(t // tm,),
        in_specs=[pl.BlockSpec((tm, d_mix), lambda m: (m, 0)),
                  pl.BlockSpec((d_mix, d), lambda m: (0, 0), pipeline_mode=pl.Buffered(1)),
                  pl.BlockSpec((tm, d), lambda m: (m, 0)),
                  pl.BlockSpec((1, 1, d), lambda m: (mod_row(m, tm, 2), 0, 0)),
                  vec, vec],
        out_specs=pl.BlockSpec((tm, d), lambda m: (m, 0)),
        compiler_params=_cparams("parallel"),
        name="out_proj_ln",
    )(mix, w_out, x, mod, ln_g.reshape(depth, 1, d), ln_b.reshape(depth, 1, d))


def _ffn_kernel(x_ref, sh2_ref, sc2_ref, wg_ref, wu_ref, wd_ref, g2_ref, lng_ref, lnb_ref, *rest, alpha, cast):
    if cast:
        cast_ref, o_ref, cast_out_ref, h_ref = rest
    else:
        o_ref, h_ref = rest
    f = pl.program_id(1)

    @pl.when(f == 0)
    def _():
        h_ref[...] = (x_ref[...] * (1.0 + sc2_ref[0]) + sh2_ref[0]).astype(BF16)
        o_ref[...] = jnp.zeros_like(o_ref)
        if cast:
            cast_out_ref[...] = cast_ref[...].astype(BF16)

    h = h_ref[...]
    a = _dot(h, wg_ref[...])
    u = _dot(h, wu_ref[...])
    o_ref[...] += _dot((_silu(a) * u).astype(BF16), wd_ref[...])

    @pl.when(f == pl.num_programs(1) - 1)
    def _():
        y = alpha * x_ref[...] + g2_ref[0] * o_ref[...]
        o_ref[...] = _layer_norm(y, lng_ref[...], lnb_ref[...])


def _ffn(x1, w_gu, w_down, mod, mod_row, ln_g, ln_b, layer, tm, alpha, cast_src=None):
    t, d = x1.shape
    d_ff = w_down.shape[0]
    depth = ln_g.shape[0]
    tf = _pick_tile(d_ff, TF_FFN)
    nf = d_ff // tf
    tiles = t // tm
    vec = pl.BlockSpec((None, 1, d), lambda m, f: (layer, 0, 0))
    row = lambda j: pl.BlockSpec((1, 1, d), lambda m, f: (mod_row(m, tm, j), 0, 0))
    in_specs = [pl.BlockSpec((tm, d), lambda m, f: (m, 0)),
                row(3), row(4),
                pl.BlockSpec((d, tf), lambda m, f: (0, f)),
                pl.BlockSpec((d, tf), lambda m, f: (0, nf + f)),
                pl.BlockSpec((tf, d), lambda m, f: (f, 0)),
                row(5), vec, vec]
    args = [x1, mod, mod, w_gu, w_gu, w_down, mod, ln_g.reshape(depth, 1, d), ln_b.reshape(depth, 1, d)]
    out_shape = [jax.ShapeDtypeStruct((t, d), F32)]
    out_specs = [pl.BlockSpec((tm, d), lambda m, f: (m, 0))]
    if cast_src is not None:
        _, rows, width = cast_src.shape
        slab = rows // tiles
        assert rows % tiles == 0 and slab % 16 == 0
        in_specs.append(pl.BlockSpec((None, slab, width), lambda m, f: (layer + 1, m, 0)))
        args.append(cast_src)
        out_shape.append(jax.ShapeDtypeStruct((rows, width), BF16))
        out_specs.append(pl.BlockSpec((slab, width), lambda m, f: (m, 0)))
    res = pl.pallas_call(
        functools.partial(_ffn_kernel, alpha=alpha, cast=cast_src is not None),
        out_shape=tuple(out_shape),
        grid=(tiles, nf),
        in_specs=in_specs,
        out_specs=tuple(out_specs),
        scratch_shapes=[pltpu.VMEM((tm, d), BF16)],
        compiler_params=_cparams("parallel", "arbitrary"),
        name="ffn_ln",
    )(*args)
    return (res[0], res[1]) if cast_src is not None else (res[0], None)


def _diff_attn_kernel(*refs, rope, cached, n_self, kb, lam_init, pipelined, n_tiles):
    it = iter(refs)
    q_ref, k_ref, v_ref = next(it), next(it), next(it)
    if rope:
        qc_ref, qsa_ref, qsb_ref, kc_ref, ksa_ref, ksb_ref = (next(it) for _ in range(6))
    if cached:
        ck_ref, cv_ref = next(it), next(it)
    lam_ref, g_ref, _, o_ref, kr_ref, vt_ref, m_ref, acc_ref = (next(it) for _ in range(8))
    s_refs = tuple(it)
    nblk = kr_ref.shape[0]
    nself = n_self // kb
    tq = q_ref.shape[0]
    step = pl.program_id(2)
    last = pl.num_programs(2) - 1

    def prepare_keys():
        for j in range(nself):
            rows = slice(j * kb, (j + 1) * kb)
            k = k_ref[rows, :]
            if rope:
                k = _rope(k, kc_ref[rows, :], ksa_ref[rows, :], ksb_ref[rows, :], DA_QK // 4)
            kr_ref[j] = k.astype(BF16)
            vt_ref[j] = v_ref[rows, :].T.astype(BF16)
        if cached:
            for j in range(nblk - nself):
                rows = slice(j * kb, (j + 1) * kb)
                kr_ref[nself + j] = ck_ref[rows, :].astype(BF16)
                vt_ref[nself + j] = cv_ref[rows, :].T.astype(BF16)

    def queries():
        q = q_ref[...]
        if rope:
            q = _rope(q, qc_ref[...], qsa_ref[...], qsb_ref[...], DA_QK // 4)
        q = q * (DA_QK ** -0.5 * LOG2_E)
        lane = lax.broadcasted_iota(jnp.int32, q.shape, 1)
        return (jnp.where(lane < DA_QK, q, 0.0).astype(BF16), jnp.where(lane >= DA_QK, q, 0.0).astype(BF16))

    def fold(x, op):
        return op(x.reshape(kb // 8, 8, tq), axis=0)

    def scores_block(j, qz, buf, ms):
        kblk = kr_ref[j]
        out = []
        for mp in range(2):
            s = _dot_t(kblk, qz[mp])
            buf[mp, j] = s
            out.append(jnp.maximum(ms[mp], fold(s, jnp.max)))
        return tuple(out)

    def values_block(j, buf, ms, ls):
        out = []
        for mp in range(2):
            p = jnp.exp2(buf[mp, j] - ms[mp])
            acc_ref[mp] += _dot(vt_ref[j], p.astype(BF16))
            out.append(ls[mp] + fold(p, jnp.sum))
        return tuple(out)

    neg = jnp.full((8, tq), -jnp.inf, F32)
    zero = jnp.zeros((8, tq), F32)

    def save_max(ms):
        for mp in range(2):
            m_ref[mp] = jnp.max(ms[mp], axis=0, keepdims=True)

    def finish(ls):
        l1, l2 = (jnp.sum(l, axis=0, keepdims=True) for l in ls)
        lp = lam_ref[...]
        lam = (jnp.exp(jnp.sum(lp[0:1] * lp[1:2], axis=-1, keepdims=True))
               - jnp.exp(jnp.sum(lp[2:3] * lp[3:4], axis=-1, keepdims=True)) + lam_init)
        ot = acc_ref[0] * (1.0 / l1) - acc_ref[1] * (lam / l2)
        ms_o = jnp.mean(ot * ot, axis=0, keepdims=True)
        ot = ot * lax.rsqrt(ms_o + RMS_EPS) * g_ref[...] * (1.0 - lam_init)
        o_ref[...] = ot.T.astype(BF16)

    if not pipelined:
        pl.when(step == 0)(prepare_keys)
        qz = queries()
        ms = (neg, neg)
        for j in range(nblk):
            ms = scores_block(j, qz, s_refs[0], ms)
        ms = tuple(jnp.max(m, axis=0, keepdims=True) for m in ms)
        acc_ref[...] = jnp.zeros_like(acc_ref)
        ls = (zero, zero)
        for j in range(nblk):
            ls = values_block(j, s_refs[0], ms, ls)
        finish(ls)
        return

    @pl.when(step == 0)
    def _():
        prepare_keys()
        qz = queries()
        ms = (neg, neg)
        for j in range(nblk):
            ms = scores_block(j, qz, s_refs[0], ms)
        save_max(ms)

    def interior(parity):
        prev = (m_ref[0], m_ref[1])
        qz = queries()
        acc_ref[...] = jnp.zeros_like(acc_ref)
        ms, ls = (neg, neg), (zero, zero)
        for j in range(nblk):
            ms = scores_block(j, qz, s_refs[parity], ms)
            ls = values_block(j, s_refs[1 - parity], prev, ls)
        save_max(ms)
        finish(ls)

    inside = (step > 0) & (step < last)
    pl.when(inside & (step % 2 == 1))(functools.partial(interior, 1))
    pl.when(inside & (step % 2 == 0))(functools.partial(interior, 0))

    @pl.when(step == last)
    def _():
        prev = (m_ref[0], m_ref[1])
        acc_ref[...] = jnp.zeros_like(acc_ref)
        ls = (zero, zero)
        for j in range(nblk):
            ls = values_block(j, s_refs[(n_tiles - 1) % 2], prev, ls)
        finish(ls)


def _diff_attn_seq_kernel(q_ref, k_ref, v_ref, lam_ref, g_ref, mix_ref, o_ref, *, heads, lam_init):
    del mix_ref
    n = q_ref.shape[0]
    lane = lax.broadcasted_iota(jnp.int32, (n, HEAD), 1)
    scores, vts = [], []
    for hd in range(heads):
        cols = slice(hd * HEAD, (hd + 1) * HEAD)
        q = q_ref[:, cols] * (DA_QK ** -0.5 * LOG2_E)
        k = k_ref[:, cols].astype(BF16)
        scores.append([_dot_t(k, jnp.where(lane < DA_QK, q, 0.0).astype(BF16)),
                       _dot_t(k, jnp.where(lane >= DA_QK, q, 0.0).astype(BF16))])
        vts.append(v_ref[:, cols].T.astype(BF16))
    probs = []
    for hd in range(heads):
        parts = []
        for s in scores[hd]:
            p = jnp.exp2(s - jnp.max(s, axis=0, keepdims=True))
            parts.append((p.astype(BF16), jnp.sum(p, axis=0, keepdims=True)))
        probs.append(parts)
    accs = [[_dot(vts[hd], pb) for pb, _ in probs[hd]] for hd in range(heads)]
    lp = lam_ref[...]
    lam = (jnp.exp(jnp.sum(lp[0:1] * lp[1:2], axis=-1, keepdims=True))
           - jnp.exp(jnp.sum(lp[2:3] * lp[3:4], axis=-1, keepdims=True)) + lam_init)
    for hd in range(heads):
        (_, l1), (_, l2) = probs[hd]
        ot = accs[hd][0] * (1.0 / l1) - accs[hd][1] * (lam / l2)
        ms_o = jnp.mean(ot * ot, axis=0, keepdims=True)
        ot = ot * lax.rsqrt(ms_o + RMS_EPS) * g_ref[...] * (1.0 - lam_init)
        o_ref[:, hd * HEAD:(hd + 1) * HEAD] = ot.T.astype(BF16)


def _diff_attn_seq(p, mix, geom, layer, diff_lambda, diff_g):
    n_seq, n = geom["n_seq"], geom["n"]
    heads = geom["da_heads"]
    depth = diff_g.shape[0]
    lam_init = 0.8 - 0.6 * math.exp(-0.3 * layer)
    blk = lambda j: pl.BlockSpec((n, heads * HEAD), lambda b: (b, j))
    return pl.pallas_call(
        functools.partial(_diff_attn_seq_kernel, heads=heads, lam_init=lam_init),
        out_shape=jax.ShapeDtypeStruct(mix.shape, mix.dtype),
        grid=(n_seq,),
        in_specs=[blk(0), blk(1), blk(2),
                  pl.BlockSpec((None, 4, DA_QK), lambda b: (layer, 0, 0)),
                  pl.BlockSpec((None, HEAD, 1), lambda b: (layer, 0, 0)),
                  pl.BlockSpec(memory_space=pl.ANY)],
        out_specs=blk(0),
        input_output_aliases={5: 0},
        compiler_params=_cparams("parallel"),
        name="diff_attention_seq",
    )(p, p, p, diff_lambda, diff_g.reshape(depth, HEAD, 1), mix)


def _diff_attn(p, mix, geom, layer, diff_lambda, diff_g, rope_tabs, cache_k, cache_v):
    if rope_tabs is None and cache_k is None and geom["n"] <= DA_TQ:
        return _diff_attn_seq(p, mix, geom, layer, diff_lambda, diff_g)
    n_seq, n, t = geom["n_seq"], geom["n"], geom["t"]
    heads = geom["da_heads"]
    rope = rope_tabs is not None
    cached = cache_k is not None
    tq = min(n, DA_TQ)
    kb = min(n, DA_KB)
    nq = n // tq
    n_ctx = cache_k.shape[3] if cached else 0
    nblk = (n + n_ctx) // kb
    assert n % kb == 0 and n_ctx % kb == 0
    k_off, v_off = heads, 2 * heads
    depth = diff_g.shape[0]
    lam_init = 0.8 - 0.6 * math.exp(-0.3 * layer)

    pipelined = nq > 1
    q_tile = (lambda i: jnp.minimum(i, nq - 1)) if pipelined else (lambda i: i)
    o_tile = (lambda i: jnp.maximum(i - 1, 0)) if pipelined else (lambda i: i)
    in_specs = [pl.BlockSpec((tq, HEAD), lambda b, h, i: (b * nq + q_tile(i), h)),
                pl.BlockSpec((n, HEAD), lambda b, h, i: (b, k_off + h)),
                pl.BlockSpec((n, HEAD), lambda b, h, i: (b, v_off + h))]
    args = [p, p, p]
    if rope:
        in_specs += [pl.BlockSpec((tq, HEAD), lambda b, h, i: (q_tile(i), 0))] * 3
        in_specs += [pl.BlockSpec((n, HEAD), lambda b, h, i: (0, 0))] * 3
        args += list(rope_tabs) * 2
    if cached:
        spec = pl.BlockSpec((None, None, None, n_ctx, HEAD), lambda b, h, i: (b, layer, h, 0, 0))
        in_specs += [spec, spec]
        args += [cache_k, cache_v]
    in_specs += [pl.BlockSpec((None, 4, DA_QK), lambda b, h, i: (layer, 0, 0)),
                 pl.BlockSpec((None, HEAD, 1), lambda b, h, i: (layer, 0, 0)),
                 pl.BlockSpec(memory_space=pl.ANY)]
    args += [diff_lambda, diff_g.reshape(depth, HEAD, 1), mix]
    return pl.pallas_call(
        functools.partial(_diff_attn_kernel, rope=rope, cached=cached, n_self=n, kb=kb, lam_init=lam_init,
                          pipelined=pipelined, n_tiles=nq),
        out_shape=jax.ShapeDtypeStruct(mix.shape, mix.dtype),
        grid=(n_seq, heads, nq + 1 if pipelined else nq),
        in_specs=in_specs,
        input_output_aliases={len(args) - 1: 0},
        out_specs=pl.BlockSpec((tq, HEAD), lambda b, h, i: (b * nq + o_tile(i), h)),
        scratch_shapes=[pltpu.VMEM((nblk, kb, HEAD), BF16), pltpu.VMEM((nblk, HEAD, kb), BF16),
                        pltpu.VMEM((2, 1, tq), F32), pltpu.VMEM((2, HEAD, tq), F32)]
                       + [pltpu.VMEM((2, nblk, kb, tq), F32)] * (2 if pipelined else 1),
        compiler_params=_cparams("parallel", "parallel", "arbitrary"),
        name="diff_attention",
    )(*args)


def _scan_constants(c):
    levels = int(math.log2(c))
    t = np.arange(c)[:, None]
    s = np.arange(c)[None, :]
    tri = (s <= t).astype(np.float32)
    masks = []
    for j in range(levels):
        m = c >> (j + 1)
        base = (t // (2 * m)) * (2 * m)
        sbase = (s // (2 * m)) * (2 * m)
        masks.append((sbase == base) & ((t - base) >= m) & ((s - sbase) < m))
    masks.append(s == t)
    masks = np.stack(masks).astype(np.float32)
    flip = lambda a: a.reshape(-1, c, c)[:, ::-1, ::-1].reshape(a.shape)
    return (jnp.asarray(tri, BF16), jnp.asarray(flip(tri), BF16),
            jnp.asarray(masks, F32), jnp.asarray(flip(masks), F32))


def _level_exponents(cum, d, chunk):
    row = lax.broadcasted_iota(jnp.int32, cum.shape, 0)
    sub = lax.broadcasted_iota(jnp.int32, (8, HEAD), 0)
    out = []
    m = chunk // 2
    while m >= 1:
        in_second = (row & m) != 0
        q_side = in_second if d == 0 else jnp.logical_not(in_second)
        if m >= 8:
            pieces = []
            for b0 in range(0, chunk, 2 * m):
                r = b0 + m - 1 + d
                bnd = jnp.broadcast_to(cum[r:r + 1, :], (m, HEAD))
                first, second = cum[b0:b0 + m, :], cum[b0 + m:b0 + 2 * m, :]
                pieces += [bnd - first, second - bnd] if d == 0 else [first - bnd, bnd - second]
            x = jnp.concatenate(pieces, axis=0)
        elif m == 1:
            neighbour = pltpu.roll(cum, 1 if d == 0 else chunk - 1, 0)
            x = jnp.where(q_side, cum - neighbour, 0.0)
        else:
            tiles = []
            for v in range(chunk // 8):
                if m == 4:
                    r = 8 * v + 3 + d
                    tiles.append(jnp.broadcast_to(cum[r:r + 1, :], (8, HEAD)))
                else:
                    lo = jnp.broadcast_to(cum[8 * v + 1 + d:8 * v + 2 + d, :], (8, HEAD))
                    hi = jnp.broadcast_to(cum[8 * v + 5 + d:8 * v + 6 + d, :], (8, HEAD))
                    tiles.append(jnp.where(sub < 4, lo, hi))
            bnd = jnp.concatenate(tiles, axis=0)
            x = jnp.where(q_side, cum - bnd, bnd - cum)
        out.append(x)
        m //= 2
    return out


def _hgrn_kernel(*refs, n, chunk, unroll, has_state, emit_state):
    it = iter(refs)
    q_ref, zf_ref, zb_ref, v_ref, hg_ref = (next(it) for _ in range(5))
    loglb_ref, log1m_ref, onem_ref, g_ref = (next(it) for _ in range(4))
    tri_refs = (next(it), next(it))
    mask_refs = (next(it), next(it))
    s0_ref = next(it) if has_state else None
    next(it)
    if emit_state:
        next(it)
    o_ref = next(it)
    st_ref = next(it) if emit_state else None
    obuf_refs = (next(it), next(it))
    s_ref = next(it)

    levels = int(math.log2(chunk))
    nchunks = n // chunk
    z_refs = (zf_ref, zb_ref)

    for d in range(2):
        s_ref[d] = s0_ref[d].T if has_state else jnp.zeros((HEAD, HEAD), F32)

    def gates(d, start):
        rows = pl.ds(start, chunk)
        z = z_refs[d][rows, :]
        q = _silu(q_ref[rows, :])
        v = v_ref[rows, :].astype(BF16)
        z2 = z * LOG2_E
        e = jnp.exp2(-jnp.abs(z2))
        one_e = 1.0 + e
        log_sig = jnp.minimum(z2, 0.0) - jnp.log2(one_e)
        sig_neg = jnp.where(z >= 0.0, e, 1.0) / one_e
        a = loglb_ref[d:d + 1, :] * LOG2_E
        b = log1m_ref[d:d + 1, :] * LOG2_E + log_sig
        g = jnp.maximum(a, b) + jnp.log2(1.0 + jnp.exp2(-jnp.abs(a - b)))
        k = onem_ref[d:d + 1, :] * sig_neg
        g_hi = g.astype(BF16)
        g_lo = (g - g_hi.astype(F32)).astype(BF16)
        cum = _dot(tri_refs[d][...], jnp.concatenate([g_hi, g_lo], axis=1))
        return q, k, v, cum[:, :HEAD] + cum[:, HEAD:]

    def body(i, carry):
        chains = []
        for u in range(unroll):
            c = i * unroll + u
            chains.append((0, pl.multiple_of(c * chunk, chunk)))
            chains.append((1, pl.multiple_of((nchunks - 1 - c) * chunk, chunk)))
        work = [gates(d, start) for d, start in chains]
        qk16 = [(q.astype(BF16), k.astype(BF16)) for q, k, _, _ in work]
        atts = [mask_refs[d][levels] * _dot_t(qb, kb) for (d, _), (qb, kb) in zip(chains, qk16)]
        expo = [_level_exponents(cum, d, chunk) for (d, _), (_, _, _, cum) in zip(chains, work)]
        for j in range(levels):
            for ci, ((d, _), (qb, kb)) in enumerate(zip(chains, qk16)):
                fac = jnp.exp2(expo[ci][j]).astype(BF16)
                atts[ci] = atts[ci] + mask_refs[d][j] * _dot_t(qb * fac, kb * fac)
        intra, delta, q_in, decay = [], [], [], []
        for (d, _), (q, k, v, cum), att in zip(chains, work, atts):
            last = cum[chunk - 1:chunk] if d == 0 else cum[0:1]
            kt = (k * jnp.exp2(last - cum)).astype(BF16)
            intra.append(_dot(att.astype(BF16), v))
            delta.append(lax.dot_general(v, kt, (((0,), (0,)), ((), ())), preferred_element_type=F32))
            q_in.append((q * jnp.exp2(cum)).astype(BF16))
            decay.append(jnp.exp2(last))
        st = [s_ref[0], s_ref[1]]
        for ci, (d, start) in enumerate(chains):
            obuf_refs[d][pl.ds(start, chunk), :] = intra[ci] + _dot_t(q_in[ci], st[d].astype(BF16))
            st[d] = decay[ci] * st[d] + delta[ci]
        s_ref[0] = st[0]
        s_ref[1] = st[1]
        return carry

    lax.fori_loop(0, nchunks // unroll, body, 0)

    o = obuf_refs[0][...] + obuf_refs[1][...]
    o_ref[...] = (_rms_norm(o, g_ref[...]) * _silu(hg_ref[...])).astype(BF16)
    if emit_state:
        for d in range(2):
            st_ref[d] = s_ref[d].T


def _hgrn(p, mix, geom, layer, lb_params, hgrn_g, consts, state, new_state=None):
    n_seq, n, t = geom["n_seq"], geom["n"], geom["t"]
    heads = geom["hg_heads"]
    mix_col = geom["da_heads"]
    base = 3 * geom["da_heads"]
    has_state = state is not None
    emit_state = not has_state
    depth = hgrn_g.shape[0]
    chunk = min(HG_CHUNK, n)
    col = lambda j: pl.BlockSpec((n, HEAD), lambda b, h: (b, base + j * heads + h))
    lbspec = pl.BlockSpec((None, 2, HEAD), lambda b, h: (layer, 0, h))
    const_specs = [pl.BlockSpec(c.shape, lambda b, h, nd=c.ndim: (0,) * nd) for c in consts]
    in_specs = [col(0), col(1), col(2), col(3), col(4), lbspec, lbspec, lbspec,
                pl.BlockSpec((None, 1, HEAD), lambda b, h: (layer, 0, 0))] + const_specs
    args = [p] * 5 + list(lb_params) + [hgrn_g.reshape(depth, 1, HEAD)] + list(consts)
    if has_state:
        in_specs.append(pl.BlockSpec((None, None, 2, None, HEAD, HEAD), lambda b, h: (b, layer, 0, h, 0, 0)))
        args.append(state)
    in_specs.append(pl.BlockSpec(memory_space=pl.ANY))
    args.append(mix)
    aliases = {len(args) - 1: 0}
    out_shape = [jax.ShapeDtypeStruct(mix.shape, mix.dtype)]
    out_specs = [pl.BlockSpec((n, HEAD), lambda b, h: (b, mix_col + h))]
    if emit_state:
        in_specs.append(pl.BlockSpec(memory_space=pl.ANY))
        args.append(new_state)
        aliases[len(args) - 1] = 1
        out_shape.append(jax.ShapeDtypeStruct(new_state.shape, new_state.dtype))
        out_specs.append(pl.BlockSpec((None, None, 2, None, HEAD, HEAD), lambda b, h: (b, layer, 0, h, 0, 0)))
    res = pl.pallas_call(
        functools.partial(_hgrn_kernel, n=n, chunk=chunk, unroll=HG_UNROLL if (n // chunk) % HG_UNROLL == 0 else 1,
                          has_state=has_state, emit_state=emit_state),
        out_shape=tuple(out_shape),
        grid=(n_seq, heads),
        in_specs=in_specs,
        out_specs=tuple(out_specs),
        scratch_shapes=[pltpu.VMEM((n, HEAD), F32), pltpu.VMEM((n, HEAD), F32),
                        pltpu.VMEM((2, HEAD, HEAD), F32)],
        input_output_aliases=aliases,
        compiler_params=_cparams("parallel", "parallel"),
        name="hgrn2_scan",
    )(*args)
    return res if emit_state else (res[0], None)


def _swa_ctx_kernel(q_ref, k_ref, v_ref, sink_ref, mix_ref, o_ref, *, layer, kvh):
    del mix_ref
    n = q_ref.shape[0]
    lane = lax.broadcasted_iota(jnp.int32, (1, SW_GROUP * n), 1)

    def fold(x, op):
        return op(op(x.reshape(x.shape[0] // 8, 8, x.shape[1]), axis=0), axis=0, keepdims=True)

    scores, sinks = [], []
    for kv in range(kvh):
        q_all = jnp.concatenate(
            [(q_ref[:, (kv * SW_GROUP + g) * HEAD:(kv * SW_GROUP + g + 1) * HEAD] * (HEAD ** -0.5)).astype(BF16)
             for g in range(SW_GROUP)], axis=0)
        scores.append(_dot_t(k_ref[:, kv * HEAD:(kv + 1) * HEAD].astype(BF16), q_all))
        sink = jnp.zeros((1, SW_GROUP * n), F32)
        for g in range(SW_GROUP):
            sink = jnp.where(lane // n == g, sink_ref[layer, kv * SW_GROUP + g], sink)
        sinks.append(sink)
    probs = []
    for s, sink in zip(scores, sinks):
        m = jnp.maximum(fold(s, jnp.max), sink)
        p = jnp.exp(s - m)
        inv = 1.0 / (fold(p, jnp.sum) + jnp.exp(sink - m))
        probs.append((p * inv).astype(BF16))
    outs = [_dot(v_ref[:, kv * HEAD:(kv + 1) * HEAD].T.astype(BF16), probs[kv]) for kv in range(kvh)]
    for kv in range(kvh):
        for g in range(SW_GROUP):
            hd = kv * SW_GROUP + g
            o_ref[:, hd * HEAD:(hd + 1) * HEAD] = outs[kv][:, g * n:(g + 1) * n].T.astype(BF16)


def _swa_ctx(p, mix, geom, layer, sink):
    n_seq, n, t = geom["n_seq"], geom["n"], geom["t"]
    kvh = geom["sw_kv"]
    qw = kvh * SW_GROUP * HEAD
    kw = kvh * HEAD
    q_blk = geom["cq_off"] // qw
    k_blk = geom["ck_off"] // kw
    mix_blk = (geom["da_heads"] + geom["hg_heads"]) * HEAD // qw
    assert geom["cq_off"] % qw == 0 and geom["ck_off"] % kw == 0
    return pl.pallas_call(
        functools.partial(_swa_ctx_kernel, layer=layer, kvh=kvh),
        out_shape=jax.ShapeDtypeStruct(mix.shape, mix.dtype),
        grid=(n_seq,),
        in_specs=[pl.BlockSpec((n, qw), lambda b: (b, q_blk)),
                  pl.BlockSpec((n, kw), lambda b: (b, k_blk)),
                  pl.BlockSpec((n, kw), lambda b: (b, k_blk + 1)),
                  pl.BlockSpec(memory_space=pltpu.SMEM),
                  pl.BlockSpec(memory_space=pl.ANY)],
        out_specs=pl.BlockSpec((n, qw), lambda b: (b, mix_blk)),
        input_output_aliases={4: 0},
        compiler_params=_cparams("parallel"),
        name="sink_attention",
    )(p, p, p, sink, mix)


def _band_masks(n):
    r = np.arange(3 * SW_BLOCK)[:, None]
    i = (np.arange(SW_GROUP * SW_BLOCK) % SW_BLOCK)[None, :]
    window = np.abs(SW_BLOCK + i - r) <= SW_WINDOW
    not_before = r >= SW_BLOCK
    not_after = r < 2 * SW_BLOCK
    variants = [window, window & not_before, window & not_after, window & not_before & not_after]
    return jnp.asarray(np.stack(variants).astype(np.float32))


def _swa_lat_kernel(q_ref, k_ref, v_ref, qc_ref, qsa_ref, qsb_ref, kc_ref, ksa_ref, ksb_ref, ck_ref, cv_ref,
                    mask_ref, sink_ref, mix_ref, o_ref, kr_ref, vt_ref, kctx_ref, vctxt_ref, *, layer, n):
    kv = pl.program_id(1)
    step = pl.program_id(2)
    nb = n // SW_BLOCK
    w = HEAD // 4
    band = 3 * SW_BLOCK
    per_step = q_ref.shape[0] // SW_BLOCK

    @pl.when(step == 0)
    def _():
        zeros = jnp.zeros((SW_BLOCK, HEAD), BF16)
        for j in (0, nb + 1):
            kr_ref[j] = zeros
            vt_ref[j] = zeros

        def fill(j, carry):
            rows = pl.ds(pl.multiple_of(j * SW_BLOCK, SW_BLOCK), SW_BLOCK)
            k = _rope(k_ref[rows, :], kc_ref[rows, :], ksa_ref[rows, :], ksb_ref[rows, :], w)
            kr_ref[j + 1] = k.astype(BF16)
            vt_ref[j + 1] = v_ref[rows, :].T.astype(BF16)
            return carry

        lax.fori_loop(0, nb, fill, 0)
        kctx_ref[...] = ck_ref[...].astype(BF16)
        vctxt_ref[...] = cv_ref[...].T.astype(BF16)

    lane = lax.broadcasted_iota(jnp.int32, (1, SW_GROUP * SW_BLOCK), 1)
    sink = jnp.zeros((1, SW_GROUP * SW_BLOCK), F32)
    for g in range(SW_GROUP):
        sink = jnp.where(lane // SW_BLOCK == g, sink_ref[layer, kv * SW_GROUP + g], sink)

    def fold(x, op):
        return op(op(x.reshape(x.shape[0] // 8, 8, x.shape[1]), axis=0), axis=0, keepdims=True)

    blocks = [step * per_step + c for c in range(per_step)]
    rows = [slice(c * SW_BLOCK, (c + 1) * SW_BLOCK) for c in range(per_step)]
    scores = []
    for qb, r in zip(blocks, rows):
        qc, qsa, qsb = qc_ref[r, :], qsa_ref[r, :], qsb_ref[r, :]
        q_all = jnp.concatenate(
            [(_rope(q_ref[r, g * HEAD:(g + 1) * HEAD], qc, qsa, qsb, w) * (HEAD ** -0.5)).astype(BF16)
             for g in range(SW_GROUP)], axis=0)
        k_all = jnp.concatenate([kr_ref[qb], kr_ref[qb + 1], kr_ref[qb + 2], kctx_ref[...]], axis=0)
        scores.append(_dot_t(k_all, q_all))
    probs = []
    for qb, s in zip(blocks, scores):
        variant = jnp.where(qb == 0, 1, 0) + jnp.where(qb == nb - 1, 2, 0)
        s_band = jnp.where(mask_ref[variant] > 0.5, s[:band], NEG_INF)
        s_ctx = s[band:]
        m = jnp.maximum(jnp.maximum(fold(s_band, jnp.max), fold(s_ctx, jnp.max)), sink)
        p_band = jnp.exp(s_band - m)
        p_ctx = jnp.exp(s_ctx - m)
        inv = 1.0 / (fold(p_band, jnp.sum) + fold(p_ctx, jnp.sum) + jnp.exp(sink - m))
        probs.append(jnp.concatenate([p_band * inv, p_ctx * inv], axis=0).astype(BF16))
    outs = []
    for qb, p_all in zip(blocks, probs):
        vt_all = jnp.concatenate([vt_ref[qb], vt_ref[qb + 1], vt_ref[qb + 2], vctxt_ref[...]], axis=1)
        outs.append(_dot(vt_all, p_all))
    for r, ot in zip(rows, outs):
        for g in range(SW_GROUP):
            o_ref[r, g * HEAD:(g + 1) * HEAD] = ot[:, g * SW_BLOCK:(g + 1) * SW_BLOCK].T.astype(BF16)


def _swa_lat(p, mix, geom, layer, sink, rope_tabs, cache_k, cache_v):
    n_seq, n, t = geom["n_seq"], geom["n"], geom["t"]
    kvh = geom["sw_kv"]
    qw = SW_GROUP * HEAD
    q_blk = geom["cq_off"] // qw
    k_blk = geom["ck_off"] // HEAD
    v_blk = k_blk + kvh
    mix_blk = (geom["da_heads"] + geom["hg_heads"]) * HEAD // qw
    nb = n // SW_BLOCK
    per_step = SW_PER_STEP if nb % SW_PER_STEP == 0 else 1
    nsteps = nb // per_step
    tq = per_step * SW_BLOCK
    n_ctx = cache_k.shape[3]
    masks = _band_masks(n)
    cspec = pl.BlockSpec((None, None, None, n_ctx, HEAD), lambda b, kv, i: (b, layer, kv, 0, 0))
    return pl.pallas_call(
        functools.partial(_swa_lat_kernel, layer=layer, n=n),
        out_shape=jax.ShapeDtypeStruct(mix.shape, mix.dtype),
        grid=(n_seq, kvh, nsteps),
        in_specs=[pl.BlockSpec((tq, qw), lambda b, kv, i: (b * nsteps + i, q_blk + kv)),
                  pl.BlockSpec((n, HEAD), lambda b, kv, i: (b, k_blk + kv)),
                  pl.BlockSpec((n, HEAD), lambda b, kv, i: (b, v_blk + kv))]
                 + [pl.BlockSpec((tq, HEAD), lambda b, kv, i: (i, 0))] * 3
                 + [pl.BlockSpec((n, HEAD), lambda b, kv, i: (0, 0))] * 3
                 + [cspec, cspec, pl.BlockSpec(masks.shape, lambda b, kv, i: (0, 0, 0)),
                    pl.BlockSpec(memory_space=pltpu.SMEM), pl.BlockSpec(memory_space=pl.ANY)],
        out_specs=pl.BlockSpec((tq, qw), lambda b, kv, i: (b * nsteps + i, mix_blk + kv)),
        scratch_shapes=[pltpu.VMEM((nb + 2, SW_BLOCK, HEAD), BF16), pltpu.VMEM((nb + 2, HEAD, SW_BLOCK), BF16),
                        pltpu.VMEM((n_ctx, HEAD), BF16), pltpu.VMEM((HEAD, n_ctx), BF16)],
        input_output_aliases={13: 0},
        compiler_params=_cparams("parallel", "parallel", "arbitrary"),
        name="banded_sink_attention",
    )(p, p, p, *rope_tabs, *rope_tabs, cache_k, cache_v, masks, sink, mix)


def _rope_tables(n, half):
    h = half // 2
    pos = jnp.arange(n)
    inv = ROPE_BASE ** (-jnp.arange(h, dtype=F32) / h)
    zero = jnp.zeros((n, h), F32)
    c, sa, sb = [], [], []
    for axis_pos in (pos // GRID_W, pos % GRID_W):
        ang = axis_pos.astype(F32)[:, None] * inv[None, :]
        cos, sin = jnp.cos(ang), jnp.sin(ang)
        c += [cos, cos]
        sa += [-sin, zero]
        sb += [zero, sin]
    reps = HEAD // (2 * half)
    cat = lambda parts: jnp.tile(jnp.concatenate(parts, axis=1), (1, reps))
    return cat(c), cat(sa), cat(sb)


def _geometry(n_seq, n, d_model):
    da_heads = hg_heads = d_model // 512
    sw_heads = d_model // 256
    sw_kv = sw_heads // SW_GROUP
    cq_off = (3 * da_heads + 5 * hg_heads) * HEAD
    return dict(n_seq=n_seq, n=n, t=n_seq * n, da_heads=da_heads, hg_heads=hg_heads, sw_kv=sw_kv,
                cq_off=cq_off, ck_off=cq_off + sw_heads * HEAD, d_mix=(da_heads + hg_heads + sw_heads) * HEAD)


def _token_tiles(n):
    return _pick_tile(n, TM_PROJ), _pick_tile(n, TM_OUT), _pick_tile(n, TM_FFN)


def _project(x, geom, layer, mod, mod_row, w_in, cast_src, new_caches):
    if new_caches is None:
        tm = _token_tiles(geom["n"])[0]
        return _in_proj(x, mod, mod_row, w_in, layer, tm, geom["d_mix"], cast_src)
    heads, kvh = geom["da_heads"], geom["sw_kv"]
    cols = ((heads * HEAD, heads), (2 * heads * HEAD, heads), (geom["ck_off"], kvh),
            (geom["ck_off"] + kvh * HEAD, kvh))
    return _in_proj(x, mod, mod_row, w_in, layer, geom["n"], geom["d_mix"], cast_src, new_caches, cols)


def _mix_and_ffn(x, p, mix, geom, layer, mod, mod_row, wts, params, lb_params, scan_consts, tabs, caches,
                 new_state, alpha, cast_src):
    w_out, w_gu, w_down = wts
    _, tm_out, tm_ffn = _token_tiles(geom["t"] if caches is None else geom["n"])
    if caches is None:
        mix = _diff_attn(p, mix, geom, layer, params["diff_lambda"], params["diff_norm_g"], None, None, None)
        mix, new_state = _hgrn(p, mix, geom, layer, lb_params, params["hgrn_norm_g"], scan_consts, None, new_state)
        mix = _swa_ctx(p, mix, geom, layer, params["swa_sink"])
    else:
        ck_d, cv_d, ck_s, cv_s, state = caches
        mix = _diff_attn(p, mix, geom, layer, params["diff_lambda"], params["diff_norm_g"], tabs[0], ck_d, cv_d)
        mix, _ = _hgrn(p, mix, geom, layer, lb_params, params["hgrn_norm_g"], scan_consts, state)
        mix = _swa_lat(p, mix, geom, layer, params["swa_sink"], tabs[1], ck_s, cv_s)
    x1 = _out_proj(mix, w_out, x, mod, mod_row, params["ln1_g"], params["ln1_b"], layer, tm_out, alpha)
    y, next_w = _ffn(x1, w_gu, w_down, mod, mod_row, params["ln2_g"], params["ln2_b"], layer, tm_ffn, alpha, cast_src)
    return y, new_state, next_w


def kernel(x_prompt, x_sample, cache_diff_k, cache_diff_v, cache_swa_k, cache_swa_v, state_hgrn, c, c_ctx, w_mod,
           b_mod, w_in, w_out, diff_lambda, diff_norm_g, hgrn_lb_logits, hgrn_norm_g, swa_sink, ln1_g, ln1_b, ln2_g,
           ln2_b, w_gate_up, w_down):
    batch, seq, d = x_prompt.shape
    dec_batch, dec_seq, _ = x_sample.shape
    depth = w_mod.shape[0]
    alpha = (2 * depth) ** 0.25
    geom_c = _geometry(batch, seq, d)
    geom_l = _geometry(dec_batch, dec_seq, d)
    assert 1 + dec_batch <= MOD_ROWS

    cond = jnp.zeros((MOD_ROWS, d), F32).at[0].set(c_ctx).at[1:1 + dec_batch].set(c)
    mod = _modulation(cond, w_mod, b_mod).reshape(depth * MOD_ROWS * 6, 1, d)
    lb_params = _lb_params(hgrn_lb_logits)
    params = dict(diff_lambda=diff_lambda, diff_norm_g=diff_norm_g, hgrn_norm_g=hgrn_norm_g, swa_sink=swa_sink,
                  ln1_g=ln1_g, ln1_b=ln1_b, ln2_g=ln2_g, ln2_b=ln2_b)
    w_in16, w_out16 = w_in[0].astype(BF16), w_out[0].astype(BF16)
    tabs = (_rope_tables(dec_seq, DA_QK // 2), _rope_tables(dec_seq, HEAD // 2))
    consts_c = _scan_constants(min(HG_CHUNK, seq))
    consts_l = _scan_constants(min(HG_CHUNK, dec_seq))

    y_p = x_prompt.reshape(batch * seq, d)
    y_s = x_sample.reshape(dec_batch * dec_seq, d)
    heads, kvh = geom_c["da_heads"], geom_c["sw_kv"]
    new_kv = tuple(jnp.zeros((batch, depth, nh, seq, HEAD), F32) for nh in (heads, heads, kvh, kvh))
    new_state = jnp.zeros((batch, depth, 2, geom_c["hg_heads"], HEAD, HEAD), F32)
    caches = (cache_diff_k, cache_diff_v, cache_swa_k, cache_swa_v, state_hgrn)
    for l in range(depth):
        row_c = lambda m, tm, j, l=l: (l * MOD_ROWS) * 6 + j
        row_l = lambda m, tm, j, l=l: (l * MOD_ROWS + 1 + (m * tm) // dec_seq) * 6 + j
        p_c, mix_c, w_down16, new_kv = _project(y_p, geom_c, l, mod, row_c, w_in16, w_down, new_kv)
        p_l, mix_l, w_gu16, _ = _project(y_s, geom_l, l, mod, row_l, w_in16, w_gate_up, None)
        wts = (w_out16, w_gu16, w_down16)
        more = l + 1 < depth
        y_p, new_state, w_out16 = _mix_and_ffn(y_p, p_c, mix_c, geom_c, l, mod, row_c, wts, params, lb_params,
                                               consts_c, None, None, new_state, alpha, w_out if more else None)
        y_s, _, w_in16 = _mix_and_ffn(y_s, p_l, mix_l, geom_l, l, mod, row_l, wts, params, lb_params, consts_l,
                                      tabs, caches, None, alpha, w_in if more else None)
    return (y_p.reshape(batch, seq, d), y_s.reshape(dec_batch, dec_seq, d)) + new_kv + (new_state,)
```

```python
import functools
import math

import numpy as np
import jax
import jax.numpy as jnp
from jax import lax
from jax.experimental import pallas as pl
from jax.experimental.pallas import tpu as pltpu

F32 = jnp.float32
BF16 = jnp.bfloat16

GRID_W = 64
ROPE_BASE = 10000.0
LN_EPS = 1e-5
RMS_EPS = 1e-6
NEG_INF = -1e30
LB_FLOOR = 1e-30
LOG2_E = math.log2(math.e)
HEAD = 128
DA_QK = 64
SW_GROUP = 4
SW_BLOCK = 128
SW_WINDOW = 128
MOD_ROWS = 8

VMEM_LIMIT = 56 * 1024 * 1024
TM_PROJ = 256
TM_OUT = 512
TM_FFN = 512
TF_FFN = 512
TN_MOD = 1024
DA_TQ = 256
DA_KB = 256
HG_CHUNK = 64
HG_UNROLL = 8
SW_PER_STEP = 4


def _cparams(*sem):
    return pltpu.CompilerParams(dimension_semantics=sem, vmem_limit_bytes=VMEM_LIMIT)


def _dot(a, b):
    return jnp.dot(a, b, preferred_element_type=F32)


def _dot_t(a, b):
    return lax.dot_general(a, b, (((1,), (1,)), ((), ())), preferred_element_type=F32)


def _silu(x):
    return x / (1.0 + jnp.exp2(x * (-LOG2_E)))


def _layer_norm(y, g, b):
    mu = jnp.mean(y, axis=-1, keepdims=True)
    d = y - mu
    var = jnp.mean(d * d, axis=-1, keepdims=True)
    return d * lax.rsqrt(var + LN_EPS) * g + b


def _rms_norm(o, g):
    ms = jnp.mean(o * o, axis=-1, keepdims=True)
    return o * lax.rsqrt(ms + RMS_EPS) * g


def _rope(x, c, sa, sb, w):
    return x * c + pltpu.roll(x, HEAD - w, 1) * sa + pltpu.roll(x, w, 1) * sb


def _pick_tile(n, target):
    t = min(n, target)
    while n % t or t % 128:
        t -= 128
    return t


def _lb_kernel(logit_ref, loglb_ref, log1m_ref, onem_ref):
    depth = logit_ref.shape[0]
    x = [logit_ref[l] for l in range(depth)]
    m = functools.reduce(jnp.maximum, x)
    e = [jnp.exp(xi - m) for xi in x]
    tot = functools.reduce(lambda a, b: a + b, e)
    w = [ei / tot for ei in e]
    acc = jnp.zeros_like(w[0])
    for l in range(depth):
        acc = acc + w[l]
        lb = acc - w[0]
        loglb_ref[l] = jnp.log(jnp.maximum(lb, LB_FLOOR))
        log1m_ref[l] = jnp.log1p(-lb)
        onem_ref[l] = 1.0 - lb


def _lb_params(logits):
    shp = jax.ShapeDtypeStruct(logits.shape, F32)
    return pl.pallas_call(_lb_kernel, out_shape=(shp, shp, shp), name="hgrn_lb_params")(logits)


def _mod_kernel(c_ref, w_ref, b_ref, o_ref):
    s = _silu(c_ref[...])
    s_hi = s.astype(BF16)
    s_lo = (s - s_hi.astype(F32)).astype(BF16)
    w = w_ref[...]
    w_hi = w.astype(BF16)
    w_lo = (w - w_hi.astype(F32)).astype(BF16)
    o_ref[...] = _dot(s_hi, w_hi) + _dot(s_lo, w_hi) + _dot(s_hi, w_lo) + b_ref[...]


def _modulation(cond, w_mod, b_mod):
    depth, d, d6 = w_mod.shape
    tn = _pick_tile(d6, TN_MOD)
    return pl.pallas_call(
        _mod_kernel,
        out_shape=jax.ShapeDtypeStruct((depth, MOD_ROWS, d6), F32),
        grid=(depth, d6 // tn),
        in_specs=[pl.BlockSpec((MOD_ROWS, d), lambda l, n: (0, 0)),
                  pl.BlockSpec((None, d, tn), lambda l, n: (l, 0, n)),
                  pl.BlockSpec((None, 1, tn), lambda l, n: (l, 0, n))],
        out_specs=pl.BlockSpec((None, MOD_ROWS, tn), lambda l, n: (l, 0, n)),
        compiler_params=_cparams("parallel", "parallel"),
        name="adaln_modulation",
    )(cond, w_mod, b_mod.reshape(depth, 1, d6))


def _proj_kernel(x_ref, sh_ref, sc_ref, w_ref, cast_ref, *rest, cache_cols):
    n = len(cache_cols)
    o_ref, mix_ref, cast_out_ref = rest[n:n + 3]
    h = (x_ref[...] * (1.0 + sc_ref[0]) + sh_ref[0]).astype(BF16)
    o_ref[...] = _dot(h, w_ref[...])
    mix_ref[...] = jnp.zeros_like(mix_ref)
    cast_out_ref[...] = cast_ref[...].astype(BF16)
    for c_ref, (off, nh) in zip(rest[n + 3:], cache_cols):
        for hd in range(nh):
            c_ref[hd] = o_ref[:, off + hd * HEAD:off + (hd + 1) * HEAD]


def _in_proj(x, mod, mod_row, w_in, layer, tm, d_mix, cast_src, caches=None, cache_cols=()):
    t, d = x.shape
    d_in = w_in.shape[1]
    steps = t // tm
    _, cast_rows, cast_width = cast_src.shape
    slab = cast_rows // steps
    assert cast_rows % steps == 0 and slab % 16 == 0
    caches = tuple(caches or ())
    out_shape = ([jax.ShapeDtypeStruct((t, d_in), F32), jax.ShapeDtypeStruct((t, d_mix), BF16),
                  jax.ShapeDtypeStruct((cast_rows, cast_width), BF16)]
                 + [jax.ShapeDtypeStruct(c.shape, c.dtype) for c in caches])
    out_specs = [pl.BlockSpec((tm, d_in), lambda m: (m, 0)), pl.BlockSpec((tm, d_mix), lambda m: (m, 0)),
                 pl.BlockSpec((slab, cast_width), lambda m: (m, 0))]
    for c in caches:
        assert c.shape[3] == tm
        out_specs.append(pl.BlockSpec((None, None, c.shape[2], tm, HEAD), lambda m: (m, layer, 0, 0, 0)))
    res = pl.pallas_call(
        functools.partial(_proj_kernel, cache_cols=tuple(cache_cols)),
        out_shape=tuple(out_shape),
        grid=(steps,),
        in_specs=[pl.BlockSpec((tm, d), lambda m: (m, 0)),
                  pl.BlockSpec((1, 1, d), lambda m: (mod_row(m, tm, 0), 0, 0)),
                  pl.BlockSpec((1, 1, d), lambda m: (mod_row(m, tm, 1), 0, 0)),
                  pl.BlockSpec((d, d_in), lambda m: (0, 0), pipeline_mode=pl.Buffered(1)),
                  pl.BlockSpec((None, slab, cast_width), lambda m: (layer, m, 0))]
                 + [pl.BlockSpec(memory_space=pl.ANY)] * len(caches),
        out_specs=tuple(out_specs),
        input_output_aliases={5 + i: 3 + i for i in range(len(caches))},
        compiler_params=_cparams("parallel"),
        name="in_proj",
    )(x, mod, mod, w_in, cast_src, *caches)
    return res[0], res[1], res[2], tuple(res[3:])


def _out_proj_kernel(mix_ref, w_ref, x_ref, g1_ref, lng_ref, lnb_ref, x1_ref, *, alpha):
    mix = _dot(mix_ref[...], w_ref[...])
    x1_ref[...] = _layer_norm(alpha * x_ref[...] + g1_ref[0] * mix, lng_ref[...], lnb_ref[...])


def _out_proj(mix, w_out, x, mod, mod_row, ln_g, ln_b, layer, tm, alpha):
    t, d = x.shape
    d_mix = mix.shape[1]
    depth = ln_g.shape[0]
    vec = pl.BlockSpec((None, 1, d), lambda m: (layer, 0, 0))
    return pl.pallas_call(
        functools.partial(_out_proj_kernel, alpha=alpha),
        out_shape=jax.ShapeDtypeStruct((t, d), F32),
        grid=(t // tm,),
        in_specs=[pl.BlockSpec((tm, d_mix), lambda m: (m, 0)),
                  pl.BlockSpec((d_mix, d), lambda m: (0, 0), pipeline_mode=pl.Buffered(1)),
                  pl.BlockSpec((tm, d), lambda m: (m, 0)),
                  pl.BlockSpec((1, 1, d), lambda m: (mod_row(m, tm, 2), 0, 0)),
                  vec, vec],
        out_specs=pl.BlockSpec((tm, d), lambda m: (m, 0)),
        compiler_params=_cparams("parallel"),
        name="out_proj_ln",
    )(mix, w_out, x, mod, ln_g.reshape(depth, 1, d), ln_b.reshape(depth, 1, d))


def _ffn_kernel(x_ref, sh2_ref, sc2_ref, wg_ref, wu_ref, wd_ref, g2_ref, lng_ref, lnb_ref, *rest, alpha, cast):
    if cast:
        cast_ref, o_ref, cast_out_ref, h_ref = rest
    else:
        o_ref, h_ref = rest
    f = pl.program_id(1)

    @pl.when(f == 0)
    def _():
        h_ref[...] = (x_ref[...] * (1.0 + sc2_ref[0]) + sh2_ref[0]).astype(BF16)
        o_ref[...] = jnp.zeros_like(o_ref)
        if cast:
            cast_out_ref[...] = cast_ref[...].astype(BF16)

    h = h_ref[...]
    a = _dot(h, wg_ref[...])
    u = _dot(h, wu_ref[...])
    o_ref[...] += _dot((_silu(a) * u).astype(BF16), wd_ref[...])

    @pl.when(f == pl.num_programs(1) - 1)
    def _():
        y = alpha * x_ref[...] + g2_ref[0] * o_ref[...]
        o_ref[...] = _layer_norm(y, lng_ref[...], lnb_ref[...])


def _ffn(x1, w_gu, w_down, mod, mod_row, ln_g, ln_b, layer, tm, alpha, cast_src=None):
    t, d = x1.shape
    d_ff = w_down.shape[0]
    depth = ln_g.shape[0]
    tf = _pick_tile(d_ff, TF_FFN)
    nf = d_ff // tf
    tiles = t // tm
    vec = pl.BlockSpec((None, 1, d), lambda m, f: (layer, 0, 0))
    row = lambda j: pl.BlockSpec((1, 1, d), lambda m, f: (mod_row(m, tm, j), 0, 0))
    in_specs = [pl.BlockSpec((tm, d), lambda m, f: (m, 0)),
                row(3), row(4),
                pl.BlockSpec((d, tf), lambda m, f: (0, f)),
                pl.BlockSpec((d, tf), lambda m, f: (0, nf + f)),
                pl.BlockSpec((tf, d), lambda m, f: (f, 0)),
                row(5), vec, vec]
    args = [x1, mod, mod, w_gu, w_gu, w_down, mod, ln_g.reshape(depth, 1, d), ln_b.reshape(depth, 1, d)]
    out_shape = [jax.ShapeDtypeStruct((t, d), F32)]
    out_specs = [pl.BlockSpec((tm, d), lambda m, f: (m, 0))]
    if cast_src is not None:
        _, rows, width = cast_src.shape
        slab = rows // tiles
        assert rows % tiles == 0 and slab % 16 == 0
        in_specs.append(pl.BlockSpec((None, slab, width), lambda m, f: (layer + 1, m, 0)))
        args.append(cast_src)
        out_shape.append(jax.ShapeDtypeStruct((rows, width), BF16))
        out_specs.append(pl.BlockSpec((slab, width), lambda m, f: (m, 0)))
    res = pl.pallas_call(
        functools.partial(_ffn_kernel, alpha=alpha, cast=cast_src is not None),
        out_shape=tuple(out_shape),
        grid=(tiles, nf),
        in_specs=in_specs,
        out_specs=tuple(out_specs),
        scratch_shapes=[pltpu.VMEM((tm, d), BF16)],
        compiler_params=_cparams("parallel", "arbitrary"),
        name="ffn_ln",
    )(*args)
    return (res[0], res[1]) if cast_src is not None else (res[0], None)


def _diff_attn_kernel(*refs, rope, cached, n_self, kb, lam_init, pipelined, n_tiles):
    it = iter(refs)
    q_ref, k_ref, v_ref = next(it), next(it), next(it)
    if rope:
        qc_ref, qsa_ref, qsb_ref, kc_ref, ksa_ref, ksb_ref = (next(it) for _ in range(6))
    if cached:
        ck_ref, cv_ref = next(it), next(it)
    lam_ref, g_ref, _, o_ref, kr_ref, vt_ref, m_ref, acc_ref = (next(it) for _ in range(8))
    s_refs = tuple(it)
    nblk = kr_ref.shape[0]
    nself = n_self // kb
    tq = q_ref.shape[0]
    step = pl.program_id(2)
    last = pl.num_programs(2) - 1

    def prepare_keys():
        for j in range(nself):
            rows = slice(j * kb, (j + 1) * kb)
            k = k_ref[rows, :]
            if rope:
                k = _rope(k, kc_ref[rows, :], ksa_ref[rows, :], ksb_ref[rows, :], DA_QK // 4)
            kr_ref[j] = k.astype(BF16)
            vt_ref[j] = v_ref[rows, :].T.astype(BF16)
        if cached:
            for j in range(nblk - nself):
                rows = slice(j * kb, (j + 1) * kb)
                kr_ref[nself + j] = ck_ref[rows, :].astype(BF16)
                vt_ref[nself + j] = cv_ref[rows, :].T.astype(BF16)

    def queries():
        q = q_ref[...]
        if rope:
            q = _rope(q, qc_ref[...], qsa_ref[...], qsb_ref[...], DA_QK // 4)
        q = q * (DA_QK ** -0.5 * LOG2_E)
        lane = lax.broadcasted_iota(jnp.int32, q.shape, 1)
        return (jnp.where(lane < DA_QK, q, 0.0).astype(BF16), jnp.where(lane >= DA_QK, q, 0.0).astype(BF16))

    def fold(x, op):
        return op(x.reshape(kb // 8, 8, tq), axis=0)

    def scores_block(j, qz, buf, ms):
        kblk = kr_ref[j]
        out = []
        for mp in range(2):
            s = _dot_t(kblk, qz[mp])
            buf[mp, j] = s
            out.append(jnp.maximum(ms[mp], fold(s, jnp.max)))
        return tuple(out)

    def values_block(j, buf, ms, ls):
        out = []
        for mp in range(2):
            p = jnp.exp2(buf[mp, j] - ms[mp])
            acc_ref[mp] += _dot(vt_ref[j], p.astype(BF16))
            out.append(ls[mp] + fold(p, jnp.sum))
        return tuple(out)

    neg = jnp.full((8, tq), -jnp.inf, F32)
    zero = jnp.zeros((8, tq), F32)

    def save_max(ms):
        for mp in range(2):
            m_ref[mp] = jnp.max(ms[mp], axis=0, keepdims=True)

    def finish(ls):
        l1, l2 = (jnp.sum(l, axis=0, keepdims=True) for l in ls)
        lp = lam_ref[...]
        lam = (jnp.exp(jnp.sum(lp[0:1] * lp[1:2], axis=-1, keepdims=True))
               - jnp.exp(jnp.sum(lp[2:3] * lp[3:4], axis=-1, keepdims=True)) + lam_init)
        ot = acc_ref[0] * (1.0 / l1) - acc_ref[1] * (lam / l2)
        ms_o = jnp.mean(ot * ot, axis=0, keepdims=True)
        ot = ot * lax.rsqrt(ms_o + RMS_EPS) * g_ref[...] * (1.0 - lam_init)
        o_ref[...] = ot.T.astype(BF16)

    if not pipelined:
        pl.when(step == 0)(prepare_keys)
        qz = queries()
        ms = (neg, neg)
        for j in range(nblk):
            ms = scores_block(j, qz, s_refs[0], ms)
        ms = tuple(jnp.max(m, axis=0, keepdims=True) for m in ms)
        acc_ref[...] = jnp.zeros_like(acc_ref)
        ls = (zero, zero)
        for j in range(nblk):
            ls = values_block(j, s_refs[0], ms, ls)
        finish(ls)
        return

    @pl.when(step == 0)
    def _():
        prepare_keys()
        qz = queries()
        ms = (neg, neg)
        for j in range(nblk):
            ms = scores_block(j, qz, s_refs[0], ms)
        save_max(ms)

    def interior(parity):
        prev = (m_ref[0], m_ref[1])
        qz = queries()
        acc_ref[...] = jnp.zeros_like(acc_ref)
        ms, ls = (neg, neg), (zero, zero)
        for j in range(nblk):
            ms = scores_block(j, qz, s_refs[parity], ms)
            ls = values_block(j, s_refs[1 - parity], prev, ls)
        save_max(ms)
        finish(ls)

    inside = (step > 0) & (step < last)
    pl.when(inside & (step % 2 == 1))(functools.partial(interior, 1))
    pl.when(inside & (step % 2 == 0))(functools.partial(interior, 0))

    @pl.when(step == last)
    def _():
        prev = (m_ref[0], m_ref[1])
        acc_ref[...] = jnp.zeros_like(acc_ref)
        ls = (zero, zero)
        for j in range(nblk):
            ls = values_block(j, s_refs[(n_tiles - 1) % 2], prev, ls)
        finish(ls)


def _diff_attn_seq_kernel(q_ref, k_ref, v_ref, lam_ref, g_ref, mix_ref, o_ref, *, heads, lam_init):
    del mix_ref
    n = q_ref.shape[0]
    lane = lax.broadcasted_iota(jnp.int32, (n, HEAD), 1)
    scores, vts = [], []
    for hd in range(heads):
        cols = slice(hd * HEAD, (hd + 1) * HEAD)
        q = q_ref[:, cols] * (DA_QK ** -0.5 * LOG2_E)
        k = k_ref[:, cols].astype(BF16)
        scores.append([_dot_t(k, jnp.where(lane < DA_QK, q, 0.0).astype(BF16)),
                       _dot_t(k, jnp.where(lane >= DA_QK, q, 0.0).astype(BF16))])
        vts.append(v_ref[:, cols].T.astype(BF16))
    probs = []
    for hd in range(heads):
        parts = []
        for s in scores[hd]:
            p = jnp.exp2(s - jnp.max(s, axis=0, keepdims=True))
            parts.append((p.astype(BF16), jnp.sum(p, axis=0, keepdims=True)))
        probs.append(parts)
    accs = [[_dot(vts[hd], pb) for pb, _ in probs[hd]] for hd in range(heads)]
    lp = lam_ref[...]
    lam = (jnp.exp(jnp.sum(lp[0:1] * lp[1:2], axis=-1, keepdims=True))
           - jnp.exp(jnp.sum(lp[2:3] * lp[3:4], axis=-1, keepdims=True)) + lam_init)
    for hd in range(heads):
        (_, l1), (_, l2) = probs[hd]
        ot = accs[hd][0] * (1.0 / l1) - accs[hd][1] * (lam / l2)
        ms_o = jnp.mean(ot * ot, axis=0, keepdims=True)
        ot = ot * lax.rsqrt(ms_o + RMS_EPS) * g_ref[...] * (1.0 - lam_init)
        o_ref[:, hd * HEAD:(hd + 1) * HEAD] = ot.T.astype(BF16)


def _diff_attn_seq(p, mix, geom, layer, diff_lambda, diff_g):
    n_seq, n = geom["n_seq"], geom["n"]
    heads = geom["da_heads"]
    depth = diff_g.shape[0]
    lam_init = 0.8 - 0.6 * math.exp(-0.3 * layer)
    blk = lambda j: pl.BlockSpec((n, heads * HEAD), lambda b: (b, j))
    return pl.pallas_call(
        functools.partial(_diff_attn_seq_kernel, heads=heads, lam_init=lam_init),
        out_shape=jax.ShapeDtypeStruct(mix.shape, mix.dtype),
        grid=(n_seq,),
        in_specs=[blk(0), blk(1), blk(2),
                  pl.BlockSpec((None, 4, DA_QK), lambda b: (layer, 0, 0)),
                  pl.BlockSpec((None, HEAD, 1), lambda b: (layer, 0, 0)),
                  pl.BlockSpec(memory_space=pl.ANY)],
        out_specs=blk(0),
        input_output_aliases={5: 0},
        compiler_params=_cparams("parallel"),
        name="diff_attention_seq",
    )(p, p, p, diff_lambda, diff_g.reshape(depth, HEAD, 1), mix)


def _diff_attn(p, mix, geom, layer, diff_lambda, diff_g, rope_tabs, cache_k, cache_v):
    if rope_tabs is None and cache_k is None and geom["n"] <= DA_TQ:
        return _diff_attn_seq(p, mix, geom, layer, diff_lambda, diff_g)
    n_seq, n, t = geom["n_seq"], geom["n"], geom["t"]
    heads = geom["da_heads"]
    rope = rope_tabs is not None
    cached = cache_k is not None
    tq = min(n, DA_TQ)
    kb = min(n, DA_KB)
    nq = n // tq
    n_ctx = cache_k.shape[3] if cached else 0
    nblk = (n + n_ctx) // kb
    assert n % kb == 0 and n_ctx % kb == 0
    k_off, v_off = heads, 2 * heads
    depth = diff_g.shape[0]
    lam_init = 0.8 - 0.6 * math.exp(-0.3 * layer)

    pipelined = nq > 1
    q_tile = (lambda i: jnp.minimum(i, nq - 1)) if pipelined else (lambda i: i)
    o_tile = (lambda i: jnp.maximum(i - 1, 0)) if pipelined else (lambda i: i)
    in_specs = [pl.BlockSpec((tq, HEAD), lambda b, h, i: (b * nq + q_tile(i), h)),
                pl.BlockSpec((n, HEAD), lambda b, h, i: (b, k_off + h)),
                pl.BlockSpec((n, HEAD), lambda b, h, i: (b, v_off + h))]
    args = [p, p, p]
    if rope:
        in_specs += [pl.BlockSpec((tq, HEAD), lambda b, h, i: (q_tile(i), 0))] * 3
        in_specs += [pl.BlockSpec((n, HEAD), lambda b, h, i: (0, 0))] * 3
        args += list(rope_tabs) * 2
    if cached:
        spec = pl.BlockSpec((None, None, None, n_ctx, HEAD), lambda b, h, i: (b, layer, h, 0, 0))
        in_specs += [spec, spec]
        args += [cache_k, cache_v]
    in_specs += [pl.BlockSpec((None, 4, DA_QK), lambda b, h, i: (layer, 0, 0)),
                 pl.BlockSpec((None, HEAD, 1), lambda b, h, i: (layer, 0, 0)),
                 pl.BlockSpec(memory_space=pl.ANY)]
    args += [diff_lambda, diff_g.reshape(depth, HEAD, 1), mix]
    return pl.pallas_call(
        functools.partial(_diff_attn_kernel, rope=rope, cached=cached, n_self=n, kb=kb, lam_init=lam_init,
                          pipelined=pipelined, n_tiles=nq),
        out_shape=jax.ShapeDtypeStruct(mix.shape, mix.dtype),
        grid=(n_seq, heads, nq + 1 if pipelined else nq),
        in_specs=in_specs,
        input_output_aliases={len(args) - 1: 0},
        out_specs=pl.BlockSpec((tq, HEAD), lambda b, h, i: (b * nq + o_tile(i), h)),
        scratch_shapes=[pltpu.VMEM((nblk, kb, HEAD), BF16), pltpu.VMEM((nblk, HEAD, kb), BF16),
                        pltpu.VMEM((2, 1, tq), F32), pltpu.VMEM((2, HEAD, tq), F32)]
                       + [pltpu.VMEM((2, nblk, kb, tq), F32)] * (2 if pipelined else 1),
        compiler_params=_cparams("parallel", "parallel", "arbitrary"),
        name="diff_attention",
    )(*args)


def _scan_constants(c):
    levels = int(math.log2(c))
    t = np.arange(c)[:, None]
    s = np.arange(c)[None, :]
    tri = (s <= t).astype(np.float32)
    masks = []
    for j in range(levels):
        m = c >> (j + 1)
        base = (t // (2 * m)) * (2 * m)
        sbase = (s // (2 * m)) * (2 * m)
        masks.append((sbase == base) & ((t - base) >= m) & ((s - sbase) < m))
    masks.append(s == t)
    masks = np.stack(masks).astype(np.float32)
    flip = lambda a: a.reshape(-1, c, c)[:, ::-1, ::-1].reshape(a.shape)
    return (jnp.asarray(tri, BF16), jnp.asarray(flip(tri), BF16),
            jnp.asarray(masks, F32), jnp.asarray(flip(masks), F32))


def _level_exponents(cum, d, chunk):
    row = lax.broadcasted_iota(jnp.int32, cum.shape, 0)
    sub = lax.broadcasted_iota(jnp.int32, (8, HEAD), 0)
    out = []
    m = chunk // 2
    while m >= 1:
        in_second = (row & m) != 0
        q_side = in_second if d == 0 else jnp.logical_not(in_second)
        if m >= 8:
            pieces = []
            for b0 in range(0, chunk, 2 * m):
                r = b0 + m - 1 + d
                bnd = jnp.broadcast_to(cum[r:r + 1, :], (m, HEAD))
                first, second = cum[b0:b0 + m, :], cum[b0 + m:b0 + 2 * m, :]
                pieces += [bnd - first, second - bnd] if d == 0 else [first - bnd, bnd - second]
            x = jnp.concatenate(pieces, axis=0)
        elif m == 1:
            neighbour = pltpu.roll(cum, 1 if d == 0 else chunk - 1, 0)
            x = jnp.where(q_side, cum - neighbour, 0.0)
        else:
            tiles = []
            for v in range(chunk // 8):
                if m == 4:
                    r = 8 * v + 3 + d
                    tiles.append(jnp.broadcast_to(cum[r:r + 1, :], (8, HEAD)))
                else:
                    lo = jnp.broadcast_to(cum[8 * v + 1 + d:8 * v + 2 + d, :], (8, HEAD))
                    hi = jnp.broadcast_to(cum[8 * v + 5 + d:8 * v + 6 + d, :], (8, HEAD))
                    tiles.append(jnp.where(sub < 4, lo, hi))
            bnd = jnp.concatenate(tiles, axis=0)
            x = jnp.where(q_side, cum - bnd, bnd - cum)
        out.append(x)
        m //= 2
    return out


def _hgrn_kernel(*refs, n, chunk, unroll, hs, has_state, emit_state):
    it = iter(refs)
    q_ref, zf_ref, zb_ref, v_ref, hg_ref = (next(it) for _ in range(5))
    loglb_ref, log1m_ref, onem_ref, g_ref = (next(it) for _ in range(4))
    tri_refs = (next(it), next(it))
    mask_refs = (next(it), next(it))
    s0_ref = next(it) if has_state else None
    next(it)
    if emit_state:
        next(it)
    o_ref = next(it)
    st_ref = next(it) if emit_state else None
    obuf_refs = (next(it), next(it))
    s_ref = next(it)

    levels = int(math.log2(chunk))
    nchunks = n // chunk
    z_refs = (zf_ref, zb_ref)

    for d in range(2):
        for hd in range(hs):
            s_ref[d * hs + hd] = s0_ref[d, hd].T if has_state else jnp.zeros((HEAD, HEAD), F32)

    def gates(hd, d, start):
        rows = pl.ds(start, chunk)
        cols = slice(hd * HEAD, (hd + 1) * HEAD)
        z = z_refs[d][rows, cols]
        q = _silu(q_ref[rows, cols])
        v = v_ref[rows, cols].astype(BF16)
        z2 = z * LOG2_E
        e = jnp.exp2(-jnp.abs(z2))
        one_e = 1.0 + e
        log_sig = jnp.minimum(z2, 0.0) - jnp.log2(one_e)
        sig_neg = jnp.where(z >= 0.0, e, 1.0) / one_e
        a = loglb_ref[d:d + 1, cols] * LOG2_E
        b = log1m_ref[d:d + 1, cols] * LOG2_E + log_sig
        g = jnp.maximum(a, b) + jnp.log2(1.0 + jnp.exp2(-jnp.abs(a - b)))
        k = onem_ref[d:d + 1, cols] * sig_neg
        g_hi = g.astype(BF16)
        g_lo = (g - g_hi.astype(F32)).astype(BF16)
        cum = _dot(tri_refs[d][...], jnp.concatenate([g_hi, g_lo], axis=1))
        return q, k, v, cum[:, :HEAD] + cum[:, HEAD:]

    def body(i, carry):
        chains = []
        for u in range(unroll):
            c = i * unroll + u
            for hd in range(hs):
                chains.append((hd, 0, pl.multiple_of(c * chunk, chunk)))
                chains.append((hd, 1, pl.multiple_of((nchunks - 1 - c) * chunk, chunk)))
        work = [gates(hd, d, start) for hd, d, start in chains]
        qk16 = [(q.astype(BF16), k.astype(BF16)) for q, k, _, _ in work]
        atts = [mask_refs[d][levels] * _dot_t(qb, kb) for (_, d, _), (qb, kb) in zip(chains, qk16)]
        expo = [_level_exponents(cum, d, chunk) for (_, d, _), (_, _, _, cum) in zip(chains, work)]
        for j in range(levels):
            for ci, ((_, d, _), (qb, kb)) in enumerate(zip(chains, qk16)):
                fac = jnp.exp2(expo[ci][j]).astype(BF16)
                atts[ci] = atts[ci] + mask_refs[d][j] * _dot_t(qb * fac, kb * fac)
        intra, delta, q_in, decay = [], [], [], []
        for (_, d, _), (q, k, v, cum), att in zip(chains, work, atts):
            last = cum[chunk - 1:chunk] if d == 0 else cum[0:1]
            kt = (k * jnp.exp2(last - cum)).astype(BF16)
            intra.append(_dot(att.astype(BF16), v))
            delta.append(lax.dot_general(v, kt, (((0,), (0,)), ((), ())), preferred_element_type=F32))
            q_in.append((q * jnp.exp2(cum)).astype(BF16))
            decay.append(jnp.exp2(last))
        st = [s_ref[j] for j in range(2 * hs)]
        for ci, (hd, d, start) in enumerate(chains):
            j = d * hs + hd
            obuf_refs[d][pl.ds(start, chunk), hd * HEAD:(hd + 1) * HEAD] = (
                intra[ci] + _dot_t(q_in[ci], st[j].astype(BF16)))
            st[j] = decay[ci] * st[j] + delta[ci]
        for j in range(2 * hs):
            s_ref[j] = st[j]
        return carry

    lax.fori_loop(0, nchunks // unroll, body, 0)

    for hd in range(hs):
        cols = slice(hd * HEAD, (hd + 1) * HEAD)
        o = obuf_refs[0][:, cols] + obuf_refs[1][:, cols]
        o_ref[:, cols] = (_rms_norm(o, g_ref[...]) * _silu(hg_ref[:, cols])).astype(BF16)
        if emit_state:
            for d in range(2):
                st_ref[d, hd] = s_ref[d * hs + hd].T


def _hgrn(p, mix, geom, layer, lb_params, hgrn_g, consts, state, new_state=None):
    n_seq, n, t = geom["n_seq"], geom["n"], geom["t"]
    heads = geom["hg_heads"]
    mix_col = geom["da_heads"]
    base = 3 * geom["da_heads"]
    has_state = state is not None
    emit_state = not has_state
    depth = hgrn_g.shape[0]
    chunk = min(HG_CHUNK, n)
    unroll = math.gcd(n // chunk, HG_UNROLL)
    hs = math.gcd(heads, HG_UNROLL // unroll)
    assert base % hs == 0 and mix_col % hs == 0
    hw = hs * HEAD
    col = lambda j: pl.BlockSpec((n, hw), lambda b, h: (b, (base + j * heads) // hs + h))
    lbspec = pl.BlockSpec((None, 2, hw), lambda b, h: (layer, 0, h))
    const_specs = [pl.BlockSpec(c.shape, lambda b, h, nd=c.ndim: (0,) * nd) for c in consts]
    in_specs = [col(0), col(1), col(2), col(3), col(4), lbspec, lbspec, lbspec,
                pl.BlockSpec((None, 1, HEAD), lambda b, h: (layer, 0, 0))] + const_specs
    args = [p] * 5 + list(lb_params) + [hgrn_g.reshape(depth, 1, HEAD)] + list(consts)
    if has_state:
        in_specs.append(pl.BlockSpec((None, None, 2, hs, HEAD, HEAD), lambda b, h: (b, layer, 0, h, 0, 0)))
        args.append(state)
    in_specs.append(pl.BlockSpec(memory_space=pl.ANY))
    args.append(mix)
    aliases = {len(args) - 1: 0}
    out_shape = [jax.ShapeDtypeStruct(mix.shape, mix.dtype)]
    out_specs = [pl.BlockSpec((n, hw), lambda b, h: (b, mix_col // hs + h))]
    if emit_state:
        in_specs.append(pl.BlockSpec(memory_space=pl.ANY))
        args.append(new_state)
        aliases[len(args) - 1] = 1
        out_shape.append(jax.ShapeDtypeStruct(new_state.shape, new_state.dtype))
        out_specs.append(pl.BlockSpec((None, None, 2, hs, HEAD, HEAD), lambda b, h: (b, layer, 0, h, 0, 0)))
    res = pl.pallas_call(
        functools.partial(_hgrn_kernel, n=n, chunk=chunk, unroll=unroll, hs=hs,
                          has_state=has_state, emit_state=emit_state),
        out_shape=tuple(out_shape),
        grid=(n_seq, heads // hs),
        in_specs=in_specs,
        out_specs=tuple(out_specs),
        scratch_shapes=[pltpu.VMEM((n, hw), F32), pltpu.VMEM((n, hw), F32),
                        pltpu.VMEM((2 * hs, HEAD, HEAD), F32)],
        input_output_aliases=aliases,
        compiler_params=_cparams("parallel", "parallel"),
        name="hgrn2_scan",
    )(*args)
    return res if emit_state else (res[0], None)


def _swa_ctx_kernel(q_ref, k_ref, v_ref, sink_ref, mix_ref, o_ref, *, layer, kvh):
    del mix_ref
    n = q_ref.shape[0]
    lane = lax.broadcasted_iota(jnp.int32, (1, SW_GROUP * n), 1)

    def fold(x, op):
        return op(op(x.reshape(x.shape[0] // 8, 8, x.shape[1]), axis=0), axis=0, keepdims=True)

    scores, sinks = [], []
    for kv in range(kvh):
        q_all = jnp.concatenate(
            [(q_ref[:, (kv * SW_GROUP + g) * HEAD:(kv * SW_GROUP + g + 1) * HEAD] * (HEAD ** -0.5)).astype(BF16)
             for g in range(SW_GROUP)], axis=0)
        scores.append(_dot_t(k_ref[:, kv * HEAD:(kv + 1) * HEAD].astype(BF16), q_all))
        sink = jnp.zeros((1, SW_GROUP * n), F32)
        for g in range(SW_GROUP):
            sink = jnp.where(lane // n == g, sink_ref[layer, kv * SW_GROUP + g], sink)
        sinks.append(sink)
    probs = []
    for s, sink in zip(scores, sinks):
        m = jnp.maximum(fold(s, jnp.max), sink)
        p = jnp.exp(s - m)
        inv = 1.0 / (fold(p, jnp.sum) + jnp.exp(sink - m))
        probs.append((p * inv).astype(BF16))
    outs = [_dot(v_ref[:, kv * HEAD:(kv + 1) * HEAD].T.astype(BF16), probs[kv]) for kv in range(kvh)]
    for kv in range(kvh):
        for g in range(SW_GROUP):
            hd = kv * SW_GROUP + g
            o_ref[:, hd * HEAD:(hd + 1) * HEAD] = outs[kv][:, g * n:(g + 1) * n].T.astype(BF16)


def _swa_ctx(p, mix, geom, layer, sink):
    n_seq, n, t = geom["n_seq"], geom["n"], geom["t"]
    kvh = geom["sw_kv"]
    qw = kvh * SW_GROUP * HEAD
    kw = kvh * HEAD
    q_blk = geom["cq_off"] // qw
    k_blk = geom["ck_off"] // kw
    mix_blk = (geom["da_heads"] + geom["hg_heads"]) * HEAD // qw
    assert geom["cq_off"] % qw == 0 and geom["ck_off"] % kw == 0
    return pl.pallas_call(
        functools.partial(_swa_ctx_kernel, layer=layer, kvh=kvh),
        out_shape=jax.ShapeDtypeStruct(mix.shape, mix.dtype),
        grid=(n_seq,),
        in_specs=[pl.BlockSpec((n, qw), lambda b: (b, q_blk)),
                  pl.BlockSpec((n, kw), lambda b: (b, k_blk)),
                  pl.BlockSpec((n, kw), lambda b: (b, k_blk + 1)),
                  pl.BlockSpec(memory_space=pltpu.SMEM),
                  pl.BlockSpec(memory_space=pl.ANY)],
        out_specs=pl.BlockSpec((n, qw), lambda b: (b, mix_blk)),
        input_output_aliases={4: 0},
        compiler_params=_cparams("parallel"),
        name="sink_attention",
    )(p, p, p, sink, mix)


def _band_masks(n):
    r = np.arange(3 * SW_BLOCK)[:, None]
    i = (np.arange(SW_GROUP * SW_BLOCK) % SW_BLOCK)[None, :]
    window = np.abs(SW_BLOCK + i - r) <= SW_WINDOW
    not_before = r >= SW_BLOCK
    not_after = r < 2 * SW_BLOCK
    variants = [window, window & not_before, window & not_after, window & not_before & not_after]
    return jnp.asarray(np.stack(variants).astype(np.float32))


def _swa_lat_kernel(q_ref, k_ref, v_ref, qc_ref, qsa_ref, qsb_ref, kc_ref, ksa_ref, ksb_ref, ck_ref, cv_ref,
                    mask_ref, sink_ref, mix_ref, o_ref, kr_ref, vt_ref, kctx_ref, vctxt_ref, *, layer, n):
    kv = pl.program_id(1)
    step = pl.program_id(2)
    nb = n // SW_BLOCK
    w = HEAD // 4
    band = 3 * SW_BLOCK
    per_step = q_ref.shape[0] // SW_BLOCK

    @pl.when(step == 0)
    def _():
        zeros = jnp.zeros((SW_BLOCK, HEAD), BF16)
        for j in (0, nb + 1):
            kr_ref[j] = zeros
            vt_ref[j] = zeros

        def fill(j, carry):
            rows = pl.ds(pl.multiple_of(j * SW_BLOCK, SW_BLOCK), SW_BLOCK)
            k = _rope(k_ref[rows, :], kc_ref[rows, :], ksa_ref[rows, :], ksb_ref[rows, :], w)
            kr_ref[j + 1] = k.astype(BF16)
            vt_ref[j + 1] = v_ref[rows, :].T.astype(BF16)
            return carry

        lax.fori_loop(0, nb, fill, 0)
        kctx_ref[...] = ck_ref[...].astype(BF16)
        vctxt_ref[...] = cv_ref[...].T.astype(BF16)

    lane = lax.broadcasted_iota(jnp.int32, (1, SW_GROUP * SW_BLOCK), 1)
    sink = jnp.zeros((1, SW_GROUP * SW_BLOCK), F32)
    for g in range(SW_GROUP):
        sink = jnp.where(lane // SW_BLOCK == g, sink_ref[layer, kv * SW_GROUP + g], sink)

    def fold(x, op):
        return op(op(x.reshape(x.shape[0] // 8, 8, x.shape[1]), axis=0), axis=0, keepdims=True)

    blocks = [step * per_step + c for c in range(per_step)]
    rows = [slice(c * SW_BLOCK, (c + 1) * SW_BLOCK) for c in range(per_step)]
    scores = []
    for qb, r in zip(blocks, rows):
        qc, qsa, qsb = qc_ref[r, :], qsa_ref[r, :], qsb_ref[r, :]
        q_all = jnp.concatenate(
            [(_rope(q_ref[r, g * HEAD:(g + 1) * HEAD], qc, qsa, qsb, w) * (HEAD ** -0.5)).astype(BF16)
             for g in range(SW_GROUP)], axis=0)
        k_all = jnp.concatenate([kr_ref[qb], kr_ref[qb + 1], kr_ref[qb + 2], kctx_ref[...]], axis=0)
        scores.append(_dot_t(k_all, q_all))
    probs = []
    for qb, s in zip(blocks, scores):
        variant = jnp.where(qb == 0, 1, 0) + jnp.where(qb == nb - 1, 2, 0)
        s_band = jnp.where(mask_ref[variant] > 0.5, s[:band], NEG_INF)
        s_ctx = s[band:]
        m = jnp.maximum(jnp.maximum(fold(s_band, jnp.max), fold(s_ctx, jnp.max)), sink)
        p_band = jnp.exp(s_band - m)
        p_ctx = jnp.exp(s_ctx - m)
        inv = 1.0 / (fold(p_band, jnp.sum) + fold(p_ctx, jnp.sum) + jnp.exp(sink - m))
        probs.append(jnp.concatenate([p_band * inv, p_ctx * inv], axis=0).astype(BF16))
    outs = []
    for qb, p_all in zip(blocks, probs):
        vt_all = jnp.concatenate([vt_ref[qb], vt_ref[qb + 1], vt_ref[qb + 2], vctxt_ref[...]], axis=1)
        outs.append(_dot(vt_all, p_all))
    for r, ot in zip(rows, outs):
        for g in range(SW_GROUP):
            o_ref[r, g * HEAD:(g + 1) * HEAD] = ot[:, g * SW_BLOCK:(g + 1) * SW_BLOCK].T.astype(BF16)


def _swa_lat(p, mix, geom, layer, sink, rope_tabs, cache_k, cache_v):
    n_seq, n, t = geom["n_seq"], geom["n"], geom["t"]
    kvh = geom["sw_kv"]
    qw = SW_GROUP * HEAD
    q_blk = geom["cq_off"] // qw
    k_blk = geom["ck_off"] // HEAD
    v_blk = k_blk + kvh
    mix_blk = (geom["da_heads"] + geom["hg_heads"]) * HEAD // qw
    nb = n // SW_BLOCK
    per_step = SW_PER_STEP if nb % SW_PER_STEP == 0 else 1
    nsteps = nb // per_step
    tq = per_step * SW_BLOCK
    n_ctx = cache_k.shape[3]
    masks = _band_masks(n)
    cspec = pl.BlockSpec((None, None, None, n_ctx, HEAD), lambda b, kv, i: (b, layer, kv, 0, 0))
    return pl.pallas_call(
        functools.partial(_swa_lat_kernel, layer=layer, n=n),
        out_shape=jax.ShapeDtypeStruct(mix.shape, mix.dtype),
        grid=(n_seq, kvh, nsteps),
        in_specs=[pl.BlockSpec((tq, qw), lambda b, kv, i: (b * nsteps + i, q_blk + kv)),
                  pl.BlockSpec((n, HEAD), lambda b, kv, i: (b, k_blk + kv)),
                  pl.BlockSpec((n, HEAD), lambda b, kv, i: (b, v_blk + kv))]
                 + [pl.BlockSpec((tq, HEAD), lambda b, kv, i: (i, 0))] * 3
                 + [pl.BlockSpec((n, HEAD), lambda b, kv, i: (0, 0))] * 3
                 + [cspec, cspec, pl.BlockSpec(masks.shape, lambda b, kv, i: (0, 0, 0)),
                    pl.BlockSpec(memory_space=pltpu.SMEM), pl.BlockSpec(memory_space=pl.ANY)],
        out_specs=pl.BlockSpec((tq, qw), lambda b, kv, i: (b * nsteps + i, mix_blk + kv)),
        scratch_shapes=[pltpu.VMEM((nb + 2, SW_BLOCK, HEAD), BF16), pltpu.VMEM((nb + 2, HEAD, SW_BLOCK), BF16),
                        pltpu.VMEM((n_ctx, HEAD), BF16), pltpu.VMEM((HEAD, n_ctx), BF16)],
        input_output_aliases={13: 0},
        compiler_params=_cparams("parallel", "parallel", "arbitrary"),
        name="banded_sink_attention",
    )(p, p, p, *rope_tabs, *rope_tabs, cache_k, cache_v, masks, sink, mix)


def _rope_tables(n, half):
    h = half // 2
    pos = jnp.arange(n)
    inv = ROPE_BASE ** (-jnp.arange(h, dtype=F32) / h)
    zero = jnp.zeros((n, h), F32)
    c, sa, sb = [], [], []
    for axis_pos in (pos // GRID_W, pos % GRID_W):
        ang = axis_pos.astype(F32)[:, None] * inv[None, :]
        cos, sin = jnp.cos(ang), jnp.sin(ang)
        c += [cos, cos]
        sa += [-sin, zero]
        sb += [zero, sin]
    reps = HEAD // (2 * half)
    cat = lambda parts: jnp.tile(jnp.concatenate(parts, axis=1), (1, reps))
    return cat(c), cat(sa), cat(sb)


def _geometry(n_seq, n, d_model):
    da_heads = hg_heads = d_model // 512
    sw_heads = d_model // 256
    sw_kv = sw_heads // SW_GROUP
    cq_off = (3 * da_heads + 5 * hg_heads) * HEAD
    return dict(n_seq=n_seq, n=n, t=n_seq * n, da_heads=da_heads, hg_heads=hg_heads, sw_kv=sw_kv,
                cq_off=cq_off, ck_off=cq_off + sw_heads * HEAD, d_mix=(da_heads + hg_heads + sw_heads) * HEAD)


def _token_tiles(n):
    return _pick_tile(n, TM_PROJ), _pick_tile(n, TM_OUT), _pick_tile(n, TM_FFN)


def _project(x, geom, layer, mod, mod_row, w_in, cast_src, new_caches):
    if new_caches is None:
        tm = _token_tiles(geom["n"])[0]
        return _in_proj(x, mod, mod_row, w_in, layer, tm, geom["d_mix"], cast_src)
    heads, kvh = geom["da_heads"], geom["sw_kv"]
    cols = ((heads * HEAD, heads), (2 * heads * HEAD, heads), (geom["ck_off"], kvh),
            (geom["ck_off"] + kvh * HEAD, kvh))
    return _in_proj(x, mod, mod_row, w_in, layer, geom["n"], geom["d_mix"], cast_src, new_caches, cols)


def _mix_and_ffn(x, p, mix, geom, layer, mod, mod_row, wts, params, lb_params, scan_consts, tabs, caches,
                 new_state, alpha, cast_src):
    w_out, w_gu, w_down = wts
    _, tm_out, tm_ffn = _token_tiles(geom["t"] if caches is None else geom["n"])
    if caches is None:
        mix = _diff_attn(p, mix, geom, layer, params["diff_lambda"], params["diff_norm_g"], None, None, None)
        mix, new_state = _hgrn(p, mix, geom, layer, lb_params, params["hgrn_norm_g"], scan_consts, None, new_state)
        mix = _swa_ctx(p, mix, geom, layer, params["swa_sink"])
    else:
        ck_d, cv_d, ck_s, cv_s, state = caches
        mix = _diff_attn(p, mix, geom, layer, params["diff_lambda"], params["diff_norm_g"], tabs[0], ck_d, cv_d)
        mix, _ = _hgrn(p, mix, geom, layer, lb_params, params["hgrn_norm_g"], scan_consts, state)
        mix = _swa_lat(p, mix, geom, layer, params["swa_sink"], tabs[1], ck_s, cv_s)
    x1 = _out_proj(mix, w_out, x, mod, mod_row, params["ln1_g"], params["ln1_b"], layer, tm_out, alpha)
    y, next_w = _ffn(x1, w_gu, w_down, mod, mod_row, params["ln2_g"], params["ln2_b"], layer, tm_ffn, alpha, cast_src)
    return y, new_state, next_w


def kernel(x_prompt, x_sample, cache_diff_k, cache_diff_v, cache_swa_k, cache_swa_v, state_hgrn, c, c_ctx, w_mod,
           b_mod, w_in, w_out, diff_lambda, diff_norm_g, hgrn_lb_logits, hgrn_norm_g, swa_sink, ln1_g, ln1_b, ln2_g,
           ln2_b, w_gate_up, w_down):
    batch, seq, d = x_prompt.shape
    dec_batch, dec_seq, _ = x_sample.shape
    depth = w_mod.shape[0]
    alpha = (2 * depth) ** 0.25
    geom_c = _geometry(batch, seq, d)
    geom_l = _geometry(dec_batch, dec_seq, d)
    assert 1 + dec_batch <= MOD_ROWS

    cond = jnp.zeros((MOD_ROWS, d), F32).at[0].set(c_ctx).at[1:1 + dec_batch].set(c)
    mod = _modulation(cond, w_mod, b_mod).reshape(depth * MOD_ROWS * 6, 1, d)
    lb_params = _lb_params(hgrn_lb_logits)
    params = dict(diff_lambda=diff_lambda, diff_norm_g=diff_norm_g, hgrn_norm_g=hgrn_norm_g, swa_sink=swa_sink,
                  ln1_g=ln1_g, ln1_b=ln1_b, ln2_g=ln2_g, ln2_b=ln2_b)
    w_in16, w_out16 = w_in[0].astype(BF16), w_out[0].astype(BF16)
    tabs = (_rope_tables(dec_seq, DA_QK // 2), _rope_tables(dec_seq, HEAD // 2))
    consts_c = _scan_constants(min(HG_CHUNK, seq))
    consts_l = _scan_constants(min(HG_CHUNK, dec_seq))

    y_p = x_prompt.reshape(batch * seq, d)
    y_s = x_sample.reshape(dec_batch * dec_seq, d)
    heads, kvh = geom_c["da_heads"], geom_c["sw_kv"]
    new_kv = tuple(jnp.zeros((batch, depth, nh, seq, HEAD), F32) for nh in (heads, heads, kvh, kvh))
    new_state = jnp.zeros((batch, depth, 2, geom_c["hg_heads"], HEAD, HEAD), F32)
    caches = (cache_diff_k, cache_diff_v, cache_swa_k, cache_swa_v, state_hgrn)
    for l in range(depth):
        row_c = lambda m, tm, j, l=l: (l * MOD_ROWS) * 6 + j
        row_l = lambda m, tm, j, l=l: (l * MOD_ROWS + 1 + (m * tm) // dec_seq) * 6 + j
        p_c, mix_c, w_down16, new_kv = _project(y_p, geom_c, l, mod, row_c, w_in16, w_down, new_kv)
        p_l, mix_l, w_gu16, _ = _project(y_s, geom_l, l, mod, row_l, w_in16, w_gate_up, None)
        wts = (w_out16, w_gu16, w_down16)
        more = l + 1 < depth
        y_p, new_state, w_out16 = _mix_and_ffn(y_p, p_c, mix_c, geom_c, l, mod, row_c, wts, params, lb_params,
                                               consts_c, None, None, new_state, alpha, w_out if more else None)
        y_s, _, w_in16 = _mix_and_ffn(y_s, p_l, mix_l, geom_l, l, mod, row_l, wts, params, lb_params, consts_l,
                                      tabs, caches, None, alpha, w_in if more else None)
    return (y_p.reshape(batch, seq, d), y_s.reshape(dec_batch, dec_seq, d)) + new_kv + (new_state,)
```

```python
import functools
import math

import numpy as np
import jax
import jax.numpy as jnp
from jax import lax
from jax.experimental import pallas as pl
from jax.experimental.pallas import tpu as pltpu

F32 = jnp.float32
BF16 = jnp.bfloat16

GRID_W = 64
ROPE_BASE = 10000.0
LN_EPS = 1e-5
RMS_EPS = 1e-6
NEG_INF = -1e30
LB_FLOOR = 1e-30
LOG2_E = math.log2(math.e)
HEAD = 128
DA_QK = 64
SW_GROUP = 4
SW_BLOCK = 128
SW_WINDOW = 128
MOD_ROWS = 8

VMEM_LIMIT = 56 * 1024 * 1024
TM_PROJ = 256
TM_OUT = 512
TM_FFN = 512
TF_FFN = 512
TN_MOD = 1024
DA_TQ = 256
DA_KB = 256
HG_CHUNK = 64
HG_UNROLL = 8
SW_PER_STEP = 4
CTX_SEQ_PER_STEP = 2


def _cparams(*sem):
    return pltpu.CompilerParams(dimension_semantics=sem, vmem_limit_bytes=VMEM_LIMIT)


def _dot(a, b):
    return jnp.dot(a, b, preferred_element_type=F32)


def _dot_t(a, b):
    return lax.dot_general(a, b, (((1,), (1,)), ((), ())), preferred_element_type=F32)


def _silu(x):
    return x / (1.0 + jnp.exp2(x * (-LOG2_E)))


def _layer_norm(y, g, b):
    mu = jnp.mean(y, axis=-1, keepdims=True)
    d = y - mu
    var = jnp.mean(d * d, axis=-1, keepdims=True)
    return d * lax.rsqrt(var + LN_EPS) * g + b


def _rms_norm(o, g):
    ms = jnp.mean(o * o, axis=-1, keepdims=True)
    return o * lax.rsqrt(ms + RMS_EPS) * g


def _rope(x, c, sa, sb, w):
    return x * c + pltpu.roll(x, HEAD - w, 1) * sa + pltpu.roll(x, w, 1) * sb


def _pick_tile(n, target):
    t = min(n, target)
    while n % t or t % 128:
        t -= 128
    return t


def _lb_kernel(logit_ref, loglb_ref, log1m_ref, onem_ref):
    depth = logit_ref.shape[0]
    x = [logit_ref[l] for l in range(depth)]
    m = functools.reduce(jnp.maximum, x)
    e = [jnp.exp(xi - m) for xi in x]
    tot = functools.reduce(lambda a, b: a + b, e)
    w = [ei / tot for ei in e]
    acc = jnp.zeros_like(w[0])
    for l in range(depth):
        acc = acc + w[l]
        lb = acc - w[0]
        loglb_ref[l] = jnp.log(jnp.maximum(lb, LB_FLOOR))
        log1m_ref[l] = jnp.log1p(-lb)
        onem_ref[l] = 1.0 - lb


def _lb_params(logits):
    shp = jax.ShapeDtypeStruct(logits.shape, F32)
    return pl.pallas_call(_lb_kernel, out_shape=(shp, shp, shp), name="hgrn_lb_params")(logits)


def _mod_kernel(c_ref, w_ref, b_ref, o_ref):
    s = _silu(c_ref[...])
    s_hi = s.astype(BF16)
    s_lo = (s - s_hi.astype(F32)).astype(BF16)
    w = w_ref[...]
    w_hi = w.astype(BF16)
    w_lo = (w - w_hi.astype(F32)).astype(BF16)
    o_ref[...] = _dot(s_hi, w_hi) + _dot(s_lo, w_hi) + _dot(s_hi, w_lo) + b_ref[...]


def _modulation(cond, w_mod, b_mod):
    depth, d, d6 = w_mod.shape
    tn = _pick_tile(d6, TN_MOD)
    return pl.pallas_call(
        _mod_kernel,
        out_shape=jax.ShapeDtypeStruct((depth, MOD_ROWS, d6), F32),
        grid=(depth, d6 // tn),
        in_specs=[pl.BlockSpec((MOD_ROWS, d), lambda l, n: (0, 0)),
                  pl.BlockSpec((None, d, tn), lambda l, n: (l, 0, n)),
                  pl.BlockSpec((None, 1, tn), lambda l, n: (l, 0, n))],
        out_specs=pl.BlockSpec((None, MOD_ROWS, tn), lambda l, n: (l, 0, n)),
        compiler_params=_cparams("parallel", "parallel"),
        name="adaln_modulation",
    )(cond, w_mod, b_mod.reshape(depth, 1, d6))


def _proj_kernel(x_ref, sh_ref, sc_ref, w_ref, cast_ref, *rest, cache_cols):
    n = len(cache_cols)
    o_ref, mix_ref, cast_out_ref = rest[n:n + 3]
    h = (x_ref[...] * (1.0 + sc_ref[0]) + sh_ref[0]).astype(BF16)
    o_ref[...] = _dot(h, w_ref[...])
    mix_ref[...] = jnp.zeros_like(mix_ref)
    cast_out_ref[...] = cast_ref[...].astype(BF16)
    for c_ref, (off, nh) in zip(rest[n + 3:], cache_cols):
        for hd in range(nh):
            c_ref[hd] = o_ref[:, off + hd * HEAD:off + (hd + 1) * HEAD]


def _in_proj(x, mod, mod_row, w_in, layer, tm, d_mix, cast_src, caches=None, cache_cols=()):
    t, d = x.shape
    d_in = w_in.shape[1]
    steps = t // tm
    _, cast_rows, cast_width = cast_src.shape
    slab = cast_rows // steps
    assert cast_rows % steps == 0 and slab % 16 == 0
    caches = tuple(caches or ())
    out_shape = ([jax.ShapeDtypeStruct((t, d_in), F32), jax.ShapeDtypeStruct((t, d_mix), BF16),
                  jax.ShapeDtypeStruct((cast_rows, cast_width), BF16)]
                 + [jax.ShapeDtypeStruct(c.shape, c.dtype) for c in caches])
    out_specs = [pl.BlockSpec((tm, d_in), lambda m: (m, 0)), pl.BlockSpec((tm, d_mix), lambda m: (m, 0)),
                 pl.BlockSpec((slab, cast_width), lambda m: (m, 0))]
    for c in caches:
        assert c.shape[3] == tm
        out_specs.append(pl.BlockSpec((None, None, c.shape[2], tm, HEAD), lambda m: (m, layer, 0, 0, 0)))
    res = pl.pallas_call(
        functools.partial(_proj_kernel, cache_cols=tuple(cache_cols)),
        out_shape=tuple(out_shape),
        grid=(steps,),
        in_specs=[pl.BlockSpec((tm, d), lambda m: (m, 0)),
                  pl.BlockSpec((1, 1, d), lambda m: (mod_row(m, tm, 0), 0, 0)),
                  pl.BlockSpec((1, 1, d), lambda m: (mod_row(m, tm, 1), 0, 0)),
                  pl.BlockSpec((d, d_in), lambda m: (0, 0), pipeline_mode=pl.Buffered(1)),
                  pl.BlockSpec((None, slab, cast_width), lambda m: (layer, m, 0))]
                 + [pl.BlockSpec(memory_space=pl.ANY)] * len(caches),
        out_specs=tuple(out_specs),
        input_output_aliases={5 + i: 3 + i for i in range(len(caches))},
        compiler_params=_cparams("parallel"),
        name="in_proj",
    )(x, mod, mod, w_in, cast_src, *caches)
    return res[0], res[1], res[2], tuple(res[3:])


def _out_proj_kernel(mix_ref, w_ref, x_ref, g1_ref, lng_ref, lnb_ref, x1_ref, *, alpha):
    mix = _dot(mix_ref[...], w_ref[...])
    x1_ref[...] = _layer_norm(alpha * x_ref[...] + g1_ref[0] * mix, lng_ref[...], lnb_ref[...])


def _out_proj(mix, w_out, x, mod, mod_row, ln_g, ln_b, layer, tm, alpha):
    t, d = x.shape
    d_mix = mix.shape[1]
    depth = ln_g.shape[0]
    vec = pl.BlockSpec((None, 1, d), lambda m: (layer, 0, 0))
    return pl.pallas_call(
        functools.partial(_out_proj_kernel, alpha=alpha),
        out_shape=jax.ShapeDtypeStruct((t, d), F32),
        grid=(t // tm,),
        in_specs=[pl.BlockSpec((tm, d_mix), lambda m: (m, 0)),
                  pl.BlockSpec((d_mix, d), lambda m: (0, 0), pipeline_mode=pl.Buffered(1)),
                  pl.BlockSpec((tm, d), lambda m: (m, 0)),
                  pl.BlockSpec((1, 1, d), lambda m: (mod_row(m, tm, 2), 0, 0)),
                  vec, vec],
        out_specs=pl.BlockSpec((tm, d), lambda m: (m, 0)),
        compiler_params=_cparams("parallel"),
        name="out_proj_ln",
    )(mix, w_out, x, mod, ln_g.reshape(depth, 1, d), ln_b.reshape(depth, 1, d))


def _ffn_kernel(x_ref, sh2_ref, sc2_ref, wg_ref, wu_ref, wd_ref, g2_ref, lng_ref, lnb_ref, *rest, alpha, cast):
    if cast:
        cast_ref, o_ref, cast_out_ref, h_ref = rest
    else:
        o_ref, h_ref = rest
    f = pl.program_id(1)

    @pl.when(f == 0)
    def _():
        h_ref[...] = (x_ref[...] * (1.0 + sc2_ref[0]) + sh2_ref[0]).astype(BF16)
        o_ref[...] = jnp.zeros_like(o_ref)
        if cast:
            cast_out_ref[...] = cast_ref[...].astype(BF16)

    h = h_ref[...]
    a = _dot(h, wg_ref[...])
    u = _dot(h, wu_ref[...])
    o_ref[...] += _dot((_silu(a) * u).astype(BF16), wd_ref[...])

    @pl.when(f == pl.num_programs(1) - 1)
    def _():
        y = alpha * x_ref[...] + g2_ref[0] * o_ref[...]
        o_ref[...] = _layer_norm(y, lng_ref[...], lnb_ref[...])


def _ffn(x1, w_gu, w_down, mod, mod_row, ln_g, ln_b, layer, tm, alpha, cast_src=None):
    t, d = x1.shape
    d_ff = w_down.shape[0]
    depth = ln_g.shape[0]
    tf = _pick_tile(d_ff, TF_FFN)
    nf = d_ff // tf
    tiles = t // tm
    vec = pl.BlockSpec((None, 1, d), lambda m, f: (layer, 0, 0))
    row = lambda j: pl.BlockSpec((1, 1, d), lambda m, f: (mod_row(m, tm, j), 0, 0))
    in_specs = [pl.BlockSpec((tm, d), lambda m, f: (m, 0)),
                row(3), row(4),
                pl.BlockSpec((d, tf), lambda m, f: (0, f)),
                pl.BlockSpec((d, tf), lambda m, f: (0, nf + f)),
                pl.BlockSpec((tf, d), lambda m, f: (f, 0)),
                row(5), vec, vec]
    args = [x1, mod, mod, w_gu, w_gu, w_down, mod, ln_g.reshape(depth, 1, d), ln_b.reshape(depth, 1, d)]
    out_shape = [jax.ShapeDtypeStruct((t, d), F32)]
    out_specs = [pl.BlockSpec((tm, d), lambda m, f: (m, 0))]
    if cast_src is not None:
        _, rows, width = cast_src.shape
        slab = rows // tiles
        assert rows % tiles == 0 and slab % 16 == 0
        in_specs.append(pl.BlockSpec((None, slab, width), lambda m, f: (layer + 1, m, 0)))
        args.append(cast_src)
        out_shape.append(jax.ShapeDtypeStruct((rows, width), BF16))
        out_specs.append(pl.BlockSpec((slab, width), lambda m, f: (m, 0)))
    res = pl.pallas_call(
        functools.partial(_ffn_kernel, alpha=alpha, cast=cast_src is not None),
        out_shape=tuple(out_shape),
        grid=(tiles, nf),
        in_specs=in_specs,
        out_specs=tuple(out_specs),
        scratch_shapes=[pltpu.VMEM((tm, d), BF16)],
        compiler_params=_cparams("parallel", "arbitrary"),
        name="ffn_ln",
    )(*args)
    return (res[0], res[1]) if cast_src is not None else (res[0], None)


def _diff_attn_kernel(*refs, rope, cached, n_self, kb, lam_init, pipelined, n_tiles):
    it = iter(refs)
    q_ref, k_ref, v_ref = next(it), next(it), next(it)
    if rope:
        qc_ref, qsa_ref, qsb_ref, kc_ref, ksa_ref, ksb_ref = (next(it) for _ in range(6))
    if cached:
        ck_ref, cv_ref = next(it), next(it)
    lam_ref, g_ref, _, o_ref, kr_ref, vt_ref, m_ref, acc_ref = (next(it) for _ in range(8))
    s_refs = tuple(it)
    nblk = kr_ref.shape[0]
    nself = n_self // kb
    tq = q_ref.shape[0]
    step = pl.program_id(2)
    last = pl.num_programs(2) - 1

    def prepare_keys():
        for j in range(nself):
            rows = slice(j * kb, (j + 1) * kb)
            k = k_ref[rows, :]
            if rope:
                k = _rope(k, kc_ref[rows, :], ksa_ref[rows, :], ksb_ref[rows, :], DA_QK // 4)
            kr_ref[j] = k.astype(BF16)
            vt_ref[j] = v_ref[rows, :].T.astype(BF16)
        if cached:
            for j in range(nblk - nself):
                rows = slice(j * kb, (j + 1) * kb)
                kr_ref[nself + j] = ck_ref[rows, :].astype(BF16)
                vt_ref[nself + j] = cv_ref[rows, :].T.astype(BF16)

    def queries():
        q = q_ref[...]
        if rope:
            q = _rope(q, qc_ref[...], qsa_ref[...], qsb_ref[...], DA_QK // 4)
        q = q * (DA_QK ** -0.5 * LOG2_E)
        lane = lax.broadcasted_iota(jnp.int32, q.shape, 1)
        return (jnp.where(lane < DA_QK, q, 0.0).astype(BF16), jnp.where(lane >= DA_QK, q, 0.0).astype(BF16))

    def fold(x, op):
        return op(x.reshape(kb // 8, 8, tq), axis=0)

    def scores_block(j, qz, buf, ms):
        kblk = kr_ref[j]
        out = []
        for mp in range(2):
            s = _dot_t(kblk, qz[mp])
            buf[mp, j] = s
            out.append(jnp.maximum(ms[mp], fold(s, jnp.max)))
        return tuple(out)

    def values_block(j, buf, ms, ls):
        out = []
        for mp in range(2):
            p = jnp.exp2(buf[mp, j] - ms[mp])
            acc_ref[mp] += _dot(vt_ref[j], p.astype(BF16))
            out.append(ls[mp] + fold(p, jnp.sum))
        return tuple(out)

    neg = jnp.full((8, tq), -jnp.inf, F32)
    zero = jnp.zeros((8, tq), F32)

    def save_max(ms):
        for mp in range(2):
            m_ref[mp] = jnp.max(ms[mp], axis=0, keepdims=True)

    def finish(ls):
        l1, l2 = (jnp.sum(l, axis=0, keepdims=True) for l in ls)
        lp = lam_ref[...]
        lam = (jnp.exp(jnp.sum(lp[0:1] * lp[1:2], axis=-1, keepdims=True))
               - jnp.exp(jnp.sum(lp[2:3] * lp[3:4], axis=-1, keepdims=True)) + lam_init)
        ot = acc_ref[0] * (1.0 / l1) - acc_ref[1] * (lam / l2)
        ms_o = jnp.mean(ot * ot, axis=0, keepdims=True)
        ot = ot * lax.rsqrt(ms_o + RMS_EPS) * g_ref[...] * (1.0 - lam_init)
        o_ref[...] = ot.T.astype(BF16)

    if not pipelined:
        pl.when(step == 0)(prepare_keys)
        qz = queries()
        ms = (neg, neg)
        for j in range(nblk):
            ms = scores_block(j, qz, s_refs[0], ms)
        ms = tuple(jnp.max(m, axis=0, keepdims=True) for m in ms)
        acc_ref[...] = jnp.zeros_like(acc_ref)
        ls = (zero, zero)
        for j in range(nblk):
            ls = values_block(j, s_refs[0], ms, ls)
        finish(ls)
        return

    @pl.when(step == 0)
    def _():
        prepare_keys()
        qz = queries()
        ms = (neg, neg)
        for j in range(nblk):
            ms = scores_block(j, qz, s_refs[0], ms)
        save_max(ms)

    def interior(parity):
        prev = (m_ref[0], m_ref[1])
        qz = queries()
        acc_ref[...] = jnp.zeros_like(acc_ref)
        ms, ls = (neg, neg), (zero, zero)
        for j in range(nblk):
            ms = scores_block(j, qz, s_refs[parity], ms)
            ls = values_block(j, s_refs[1 - parity], prev, ls)
        save_max(ms)
        finish(ls)

    inside = (step > 0) & (step < last)
    pl.when(inside & (step % 2 == 1))(functools.partial(interior, 1))
    pl.when(inside & (step % 2 == 0))(functools.partial(interior, 0))

    @pl.when(step == last)
    def _():
        prev = (m_ref[0], m_ref[1])
        acc_ref[...] = jnp.zeros_like(acc_ref)
        ls = (zero, zero)
        for j in range(nblk):
            ls = values_block(j, s_refs[(n_tiles - 1) % 2], prev, ls)
        finish(ls)


def _diff_attn_seq_kernel(q_ref, k_ref, v_ref, lam_ref, g_ref, mix_ref, o_ref, *, n, heads, lam_init):
    del mix_ref
    lane = lax.broadcasted_iota(jnp.int32, (n, HEAD), 1)
    units = [(slice(sq * n, (sq + 1) * n), slice(hd * HEAD, (hd + 1) * HEAD))
             for sq in range(q_ref.shape[0] // n) for hd in range(heads)]
    scores, vts = [], []
    for rows, cols in units:
        q = q_ref[rows, cols] * (DA_QK ** -0.5 * LOG2_E)
        k = k_ref[rows, cols].astype(BF16)
        scores.append([_dot_t(k, jnp.where(lane < DA_QK, q, 0.0).astype(BF16)),
                       _dot_t(k, jnp.where(lane >= DA_QK, q, 0.0).astype(BF16))])
        vts.append(v_ref[rows, cols].T.astype(BF16))
    probs = []
    for pair in scores:
        parts = []
        for s in pair:
            p = jnp.exp2(s - jnp.max(s, axis=0, keepdims=True))
            parts.append((p.astype(BF16), jnp.sum(p, axis=0, keepdims=True)))
        probs.append(parts)
    accs = [[_dot(vt, pb) for pb, _ in parts] for vt, parts in zip(vts, probs)]
    lp = lam_ref[...]
    lam = (jnp.exp(jnp.sum(lp[0:1] * lp[1:2], axis=-1, keepdims=True))
           - jnp.exp(jnp.sum(lp[2:3] * lp[3:4], axis=-1, keepdims=True)) + lam_init)
    for (rows, cols), ((_, l1), (_, l2)), acc in zip(units, probs, accs):
        ot = acc[0] * (1.0 / l1) - acc[1] * (lam / l2)
        ms_o = jnp.mean(ot * ot, axis=0, keepdims=True)
        ot = ot * lax.rsqrt(ms_o + RMS_EPS) * g_ref[...] * (1.0 - lam_init)
        o_ref[rows, cols] = ot.T.astype(BF16)


def _diff_attn_seq(p, mix, geom, layer, diff_lambda, diff_g):
    n_seq, n = geom["n_seq"], geom["n"]
    heads = geom["da_heads"]
    depth = diff_g.shape[0]
    lam_init = 0.8 - 0.6 * math.exp(-0.3 * layer)
    per_step = math.gcd(n_seq, CTX_SEQ_PER_STEP)
    blk = lambda j: pl.BlockSpec((per_step * n, heads * HEAD), lambda b: (b, j))
    return pl.pallas_call(
        functools.partial(_diff_attn_seq_kernel, n=n, heads=heads, lam_init=lam_init),
        out_shape=jax.ShapeDtypeStruct(mix.shape, mix.dtype),
        grid=(n_seq // per_step,),
        in_specs=[blk(0), blk(1), blk(2),
                  pl.BlockSpec((None, 4, DA_QK), lambda b: (layer, 0, 0)),
                  pl.BlockSpec((None, HEAD, 1), lambda b: (layer, 0, 0)),
                  pl.BlockSpec(memory_space=pl.ANY)],
        out_specs=blk(0),
        input_output_aliases={5: 0},
        compiler_params=_cparams("parallel"),
        name="diff_attention_seq",
    )(p, p, p, diff_lambda, diff_g.reshape(depth, HEAD, 1), mix)


def _diff_attn(p, mix, geom, layer, diff_lambda, diff_g, rope_tabs, cache_k, cache_v):
    if rope_tabs is None and cache_k is None and geom["n"] <= DA_TQ:
        return _diff_attn_seq(p, mix, geom, layer, diff_lambda, diff_g)
    n_seq, n, t = geom["n_seq"], geom["n"], geom["t"]
    heads = geom["da_heads"]
    rope = rope_tabs is not None
    cached = cache_k is not None
    tq = min(n, DA_TQ)
    kb = min(n, DA_KB)
    nq = n // tq
    n_ctx = cache_k.shape[3] if cached else 0
    nblk = (n + n_ctx) // kb
    assert n % kb == 0 and n_ctx % kb == 0
    k_off, v_off = heads, 2 * heads
    depth = diff_g.shape[0]
    lam_init = 0.8 - 0.6 * math.exp(-0.3 * layer)

    pipelined = nq > 1
    q_tile = (lambda i: jnp.minimum(i, nq - 1)) if pipelined else (lambda i: i)
    o_tile = (lambda i: jnp.maximum(i - 1, 0)) if pipelined else (lambda i: i)
    in_specs = [pl.BlockSpec((tq, HEAD), lambda b, h, i: (b * nq + q_tile(i), h)),
                pl.BlockSpec((n, HEAD), lambda b, h, i: (b, k_off + h)),
                pl.BlockSpec((n, HEAD), lambda b, h, i: (b, v_off + h))]
    args = [p, p, p]
    if rope:
        in_specs += [pl.BlockSpec((tq, HEAD), lambda b, h, i: (q_tile(i), 0))] * 3
        in_specs += [pl.BlockSpec((n, HEAD), lambda b, h, i: (0, 0))] * 3
        args += list(rope_tabs) * 2
    if cached:
        spec = pl.BlockSpec((None, None, None, n_ctx, HEAD), lambda b, h, i: (b, layer, h, 0, 0))
        in_specs += [spec, spec]
        args += [cache_k, cache_v]
    in_specs += [pl.BlockSpec((None, 4, DA_QK), lambda b, h, i: (layer, 0, 0)),
                 pl.BlockSpec((None, HEAD, 1), lambda b, h, i: (layer, 0, 0)),
                 pl.BlockSpec(memory_space=pl.ANY)]
    args += [diff_lambda, diff_g.reshape(depth, HEAD, 1), mix]
    return pl.pallas_call(
        functools.partial(_diff_attn_kernel, rope=rope, cached=cached, n_self=n, kb=kb, lam_init=lam_init,
                          pipelined=pipelined, n_tiles=nq),
        out_shape=jax.ShapeDtypeStruct(mix.shape, mix.dtype),
        grid=(n_seq, heads, nq + 1 if pipelined else nq),
        in_specs=in_specs,
        input_output_aliases={len(args) - 1: 0},
        out_specs=pl.BlockSpec((tq, HEAD), lambda b, h, i: (b * nq + o_tile(i), h)),
        scratch_shapes=[pltpu.VMEM((nblk, kb, HEAD), BF16), pltpu.VMEM((nblk, HEAD, kb), BF16),
                        pltpu.VMEM((2, 1, tq), F32), pltpu.VMEM((2, HEAD, tq), F32)]
                       + [pltpu.VMEM((2, nblk, kb, tq), F32)] * (2 if pipelined else 1),
        compiler_params=_cparams("parallel", "parallel", "arbitrary"),
        name="diff_attention",
    )(*args)


def _scan_constants(c):
    levels = int(math.log2(c))
    t = np.arange(c)[:, None]
    s = np.arange(c)[None, :]
    tri = (s <= t).astype(np.float32)
    masks = []
    for j in range(levels):
        m = c >> (j + 1)
        base = (t // (2 * m)) * (2 * m)
        sbase = (s // (2 * m)) * (2 * m)
        masks.append((sbase == base) & ((t - base) >= m) & ((s - sbase) < m))
    masks.append(s == t)
    masks = np.stack(masks).astype(np.float32)
    flip = lambda a: a.reshape(-1, c, c)[:, ::-1, ::-1].reshape(a.shape)
    return (jnp.asarray(tri, BF16), jnp.asarray(flip(tri), BF16),
            jnp.asarray(masks, F32), jnp.asarray(flip(masks), F32))


def _level_exponents(cum, d, chunk):
    row = lax.broadcasted_iota(jnp.int32, cum.shape, 0)
    sub = lax.broadcasted_iota(jnp.int32, (8, HEAD), 0)
    out = []
    m = chunk // 2
    while m >= 1:
        in_second = (row & m) != 0
        q_side = in_second if d == 0 else jnp.logical_not(in_second)
        if m >= 8:
            pieces = []
            for b0 in range(0, chunk, 2 * m):
                r = b0 + m - 1 + d
                bnd = jnp.broadcast_to(cum[r:r + 1, :], (m, HEAD))
                first, second = cum[b0:b0 + m, :], cum[b0 + m:b0 + 2 * m, :]
                pieces += [bnd - first, second - bnd] if d == 0 else [first - bnd, bnd - second]
            x = jnp.concatenate(pieces, axis=0)
        elif m == 1:
            neighbour = pltpu.roll(cum, 1 if d == 0 else chunk - 1, 0)
            x = jnp.where(q_side, cum - neighbour, 0.0)
        else:
            tiles = []
            for v in range(chunk // 8):
                if m == 4:
                    r = 8 * v + 3 + d
                    tiles.append(jnp.broadcast_to(cum[r:r + 1, :], (8, HEAD)))
                else:
                    lo = jnp.broadcast_to(cum[8 * v + 1 + d:8 * v + 2 + d, :], (8, HEAD))
                    hi = jnp.broadcast_to(cum[8 * v + 5 + d:8 * v + 6 + d, :], (8, HEAD))
                    tiles.append(jnp.where(sub < 4, lo, hi))
            bnd = jnp.concatenate(tiles, axis=0)
            x = jnp.where(q_side, cum - bnd, bnd - cum)
        out.append(x)
        m //= 2
    return out


def _hgrn_kernel(*refs, n, chunk, unroll, hs, has_state, emit_state):
    it = iter(refs)
    q_ref, zf_ref, zb_ref, v_ref, hg_ref = (next(it) for _ in range(5))
    loglb_ref, log1m_ref, onem_ref, g_ref = (next(it) for _ in range(4))
    tri_refs = (next(it), next(it))
    mask_refs = (next(it), next(it))
    s0_ref = next(it) if has_state else None
    next(it)
    if emit_state:
        next(it)
    o_ref = next(it)
    st_ref = next(it) if emit_state else None
    obuf_refs = (next(it), next(it))
    s_ref = next(it)

    levels = int(math.log2(chunk))
    nchunks = n // chunk
    z_refs = (zf_ref, zb_ref)

    for d in range(2):
        for hd in range(hs):
            s_ref[d * hs + hd] = s0_ref[d, hd].T if has_state else jnp.zeros((HEAD, HEAD), F32)

    def gates(hd, d, start):
        rows = pl.ds(start, chunk)
        cols = slice(hd * HEAD, (hd + 1) * HEAD)
        z = z_refs[d][rows, cols]
        q = _silu(q_ref[rows, cols])
        v = v_ref[rows, cols].astype(BF16)
        z2 = z * LOG2_E
        e = jnp.exp2(-jnp.abs(z2))
        one_e = 1.0 + e
        log_sig = jnp.minimum(z2, 0.0) - jnp.log2(one_e)
        sig_neg = jnp.where(z >= 0.0, e, 1.0) / one_e
        a = loglb_ref[d:d + 1, cols] * LOG2_E
        b = log1m_ref[d:d + 1, cols] * LOG2_E + log_sig
        g = jnp.maximum(a, b) + jnp.log2(1.0 + jnp.exp2(-jnp.abs(a - b)))
        k = onem_ref[d:d + 1, cols] * sig_neg
        g_hi = g.astype(BF16)
        g_lo = (g - g_hi.astype(F32)).astype(BF16)
        cum = _dot(tri_refs[d][...], jnp.concatenate([g_hi, g_lo], axis=1))
        return q, k, v, cum[:, :HEAD] + cum[:, HEAD:]

    def body(i, carry):
        chains = []
        for u in range(unroll):
            c = i * unroll + u
            for hd in range(hs):
                chains.append((hd, 0, pl.multiple_of(c * chunk, chunk)))
                chains.append((hd, 1, pl.multiple_of((nchunks - 1 - c) * chunk, chunk)))
        work = [gates(hd, d, start) for hd, d, start in chains]
        qk16 = [(q.astype(BF16), k.astype(BF16)) for q, k, _, _ in work]
        atts = [mask_refs[d][levels] * _dot_t(qb, kb) for (_, d, _), (qb, kb) in zip(chains, qk16)]
        expo = [_level_exponents(cum, d, chunk) for (_, d, _), (_, _, _, cum) in zip(chains, work)]
        for j in range(levels):
            for ci, ((_, d, _), (qb, kb)) in enumerate(zip(chains, qk16)):
                fac = jnp.exp2(expo[ci][j]).astype(BF16)
                atts[ci] = atts[ci] + mask_refs[d][j] * _dot_t(qb * fac, kb * fac)
        intra, delta, q_in, decay = [], [], [], []
        for (_, d, _), (q, k, v, cum), att in zip(chains, work, atts):
            last = cum[chunk - 1:chunk] if d == 0 else cum[0:1]
            kt = (k * jnp.exp2(last - cum)).astype(BF16)
            intra.append(_dot(att.astype(BF16), v))
            delta.append(lax.dot_general(v, kt, (((0,), (0,)), ((), ())), preferred_element_type=F32))
            q_in.append((q * jnp.exp2(cum)).astype(BF16))
            decay.append(jnp.exp2(last))
        st = [s_ref[j] for j in range(2 * hs)]
        for ci, (hd, d, start) in enumerate(chains):
            j = d * hs + hd
            obuf_refs[d][pl.ds(start, chunk), hd * HEAD:(hd + 1) * HEAD] = (
                intra[ci] + _dot_t(q_in[ci], st[j].astype(BF16)))
            st[j] = decay[ci] * st[j] + delta[ci]
        for j in range(2 * hs):
            s_ref[j] = st[j]
        return carry

    lax.fori_loop(0, nchunks // unroll, body, 0)

    for hd in range(hs):
        cols = slice(hd * HEAD, (hd + 1) * HEAD)
        o = obuf_refs[0][:, cols] + obuf_refs[1][:, cols]
        o_ref[:, cols] = (_rms_norm(o, g_ref[...]) * _silu(hg_ref[:, cols])).astype(BF16)
        if emit_state:
            for d in range(2):
                st_ref[d, hd] = s_ref[d * hs + hd].T


def _hgrn(p, mix, geom, layer, lb_params, hgrn_g, consts, state, new_state=None):
    n_seq, n, t = geom["n_seq"], geom["n"], geom["t"]
    heads = geom["hg_heads"]
    mix_col = geom["da_heads"]
    base = 3 * geom["da_heads"]
    has_state = state is not None
    emit_state = not has_state
    depth = hgrn_g.shape[0]
    chunk = min(HG_CHUNK, n)
    unroll = math.gcd(n // chunk, HG_UNROLL)
    hs = math.gcd(heads, HG_UNROLL // unroll)
    assert base % hs == 0 and mix_col % hs == 0
    hw = hs * HEAD
    col = lambda j: pl.BlockSpec((n, hw), lambda b, h: (b, (base + j * heads) // hs + h))
    lbspec = pl.BlockSpec((None, 2, hw), lambda b, h: (layer, 0, h))
    const_specs = [pl.BlockSpec(c.shape, lambda b, h, nd=c.ndim: (0,) * nd) for c in consts]
    in_specs = [col(0), col(1), col(2), col(3), col(4), lbspec, lbspec, lbspec,
                pl.BlockSpec((None, 1, HEAD), lambda b, h: (layer, 0, 0))] + const_specs
    args = [p] * 5 + list(lb_params) + [hgrn_g.reshape(depth, 1, HEAD)] + list(consts)
    if has_state:
        in_specs.append(pl.BlockSpec((None, None, 2, hs, HEAD, HEAD), lambda b, h: (b, layer, 0, h, 0, 0)))
        args.append(state)
    in_specs.append(pl.BlockSpec(memory_space=pl.ANY))
    args.append(mix)
    aliases = {len(args) - 1: 0}
    out_shape = [jax.ShapeDtypeStruct(mix.shape, mix.dtype)]
    out_specs = [pl.BlockSpec((n, hw), lambda b, h: (b, mix_col // hs + h))]
    if emit_state:
        in_specs.append(pl.BlockSpec(memory_space=pl.ANY))
        args.append(new_state)
        aliases[len(args) - 1] = 1
        out_shape.append(jax.ShapeDtypeStruct(new_state.shape, new_state.dtype))
        out_specs.append(pl.BlockSpec((None, None, 2, hs, HEAD, HEAD), lambda b, h: (b, layer, 0, h, 0, 0)))
    res = pl.pallas_call(
        functools.partial(_hgrn_kernel, n=n, chunk=chunk, unroll=unroll, hs=hs,
                          has_state=has_state, emit_state=emit_state),
        out_shape=tuple(out_shape),
        grid=(n_seq, heads // hs),
        in_specs=in_specs,
        out_specs=tuple(out_specs),
        scratch_shapes=[pltpu.VMEM((n, hw), F32), pltpu.VMEM((n, hw), F32),
                        pltpu.VMEM((2 * hs, HEAD, HEAD), F32)],
        input_output_aliases=aliases,
        compiler_params=_cparams("parallel", "parallel"),
        name="hgrn2_scan",
    )(*args)
    return res if emit_state else (res[0], None)


def _swa_ctx_kernel(q_ref, k_ref, v_ref, sink_ref, mix_ref, o_ref, *, layer, n, kvh):
    del mix_ref
    lane = lax.broadcasted_iota(jnp.int32, (1, SW_GROUP * n), 1)
    units = [(slice(sq * n, (sq + 1) * n), kv) for sq in range(q_ref.shape[0] // n) for kv in range(kvh)]

    def fold(x, op):
        return op(op(x.reshape(x.shape[0] // 8, 8, x.shape[1]), axis=0), axis=0, keepdims=True)

    sinks = []
    for kv in range(kvh):
        sink = jnp.zeros((1, SW_GROUP * n), F32)
        for g in range(SW_GROUP):
            sink = jnp.where(lane // n == g, sink_ref[layer, kv * SW_GROUP + g], sink)
        sinks.append(sink)
    scores = []
    for rows, kv in units:
        q_all = jnp.concatenate(
            [(q_ref[rows, (kv * SW_GROUP + g) * HEAD:(kv * SW_GROUP + g + 1) * HEAD] * (HEAD ** -0.5)).astype(BF16)
             for g in range(SW_GROUP)], axis=0)
        scores.append(_dot_t(k_ref[rows, kv * HEAD:(kv + 1) * HEAD].astype(BF16), q_all))
    probs = []
    for s, (_, kv) in zip(scores, units):
        m = jnp.maximum(fold(s, jnp.max), sinks[kv])
        p = jnp.exp(s - m)
        inv = 1.0 / (fold(p, jnp.sum) + jnp.exp(sinks[kv] - m))
        probs.append((p * inv).astype(BF16))
    outs = [_dot(v_ref[rows, kv * HEAD:(kv + 1) * HEAD].T.astype(BF16), pb) for (rows, kv), pb in zip(units, probs)]
    for (rows, kv), ot in zip(units, outs):
        for g in range(SW_GROUP):
            hd = kv * SW_GROUP + g
            o_ref[rows, hd * HEAD:(hd + 1) * HEAD] = ot[:, g * n:(g + 1) * n].T.astype(BF16)


def _swa_ctx(p, mix, geom, layer, sink):
    n_seq, n, t = geom["n_seq"], geom["n"], geom["t"]
    kvh = geom["sw_kv"]
    qw = kvh * SW_GROUP * HEAD
    kw = kvh * HEAD
    q_blk = geom["cq_off"] // qw
    k_blk = geom["ck_off"] // kw
    mix_blk = (geom["da_heads"] + geom["hg_heads"]) * HEAD // qw
    assert geom["cq_off"] % qw == 0 and geom["ck_off"] % kw == 0
    rows = math.gcd(n_seq, CTX_SEQ_PER_STEP) * n
    return pl.pallas_call(
        functools.partial(_swa_ctx_kernel, layer=layer, n=n, kvh=kvh),
        out_shape=jax.ShapeDtypeStruct(mix.shape, mix.dtype),
        grid=(t // rows,),
        in_specs=[pl.BlockSpec((rows, qw), lambda b: (b, q_blk)),
                  pl.BlockSpec((rows, kw), lambda b: (b, k_blk)),
                  pl.BlockSpec((rows, kw), lambda b: (b, k_blk + 1)),
                  pl.BlockSpec(memory_space=pltpu.SMEM),
                  pl.BlockSpec(memory_space=pl.ANY)],
        out_specs=pl.BlockSpec((rows, qw), lambda b: (b, mix_blk)),
        input_output_aliases={4: 0},
        compiler_params=_cparams("parallel"),
        name="sink_attention",
    )(p, p, p, sink, mix)


def _band_masks(n):
    r = np.arange(3 * SW_BLOCK)[:, None]
    i = (np.arange(SW_GROUP * SW_BLOCK) % SW_BLOCK)[None, :]
    window = np.abs(SW_BLOCK + i - r) <= SW_WINDOW
    not_before = r >= SW_BLOCK
    not_after = r < 2 * SW_BLOCK
    variants = [window, window & not_before, window & not_after, window & not_before & not_after]
    return jnp.asarray(np.stack(variants).astype(np.float32))


def _swa_lat_kernel(q_ref, k_ref, v_ref, qc_ref, qsa_ref, qsb_ref, kc_ref, ksa_ref, ksb_ref, ck_ref, cv_ref,
                    mask_ref, sink_ref, mix_ref, o_ref, kr_ref, vt_ref, kctx_ref, vctxt_ref, *, layer, n):
    kv = pl.program_id(1)
    step = pl.program_id(2)
    nb = n // SW_BLOCK
    w = HEAD // 4
    band = 3 * SW_BLOCK
    per_step = q_ref.shape[0] // SW_BLOCK

    @pl.when(step == 0)
    def _():
        zeros = jnp.zeros((SW_BLOCK, HEAD), BF16)
        for j in (0, nb + 1):
            kr_ref[j] = zeros
            vt_ref[j] = zeros

        def fill(j, carry):
            rows = pl.ds(pl.multiple_of(j * SW_BLOCK, SW_BLOCK), SW_BLOCK)
            k = _rope(k_ref[rows, :], kc_ref[rows, :], ksa_ref[rows, :], ksb_ref[rows, :], w)
            kr_ref[j + 1] = k.astype(BF16)
            vt_ref[j + 1] = v_ref[rows, :].T.astype(BF16)
            return carry

        lax.fori_loop(0, nb, fill, 0)
        kctx_ref[...] = ck_ref[...].astype(BF16)
        vctxt_ref[...] = cv_ref[...].T.astype(BF16)

    lane = lax.broadcasted_iota(jnp.int32, (1, SW_GROUP * SW_BLOCK), 1)
    sink = jnp.zeros((1, SW_GROUP * SW_BLOCK), F32)
    for g in range(SW_GROUP):
        sink = jnp.where(lane // SW_BLOCK == g, sink_ref[layer, kv * SW_GROUP + g], sink)

    def fold(x, op):
        return op(op(x.reshape(x.shape[0] // 8, 8, x.shape[1]), axis=0), axis=0, keepdims=True)

    blocks = [step * per_step + c for c in range(per_step)]
    rows = [slice(c * SW_BLOCK, (c + 1) * SW_BLOCK) for c in range(per_step)]
    scores = []
    for qb, r in zip(blocks, rows):
        qc, qsa, qsb = qc_ref[r, :], qsa_ref[r, :], qsb_ref[r, :]
        q_all = jnp.concatenate(
            [(_rope(q_ref[r, g * HEAD:(g + 1) * HEAD], qc, qsa, qsb, w) * (HEAD ** -0.5)).astype(BF16)
             for g in range(SW_GROUP)], axis=0)
        k_all = jnp.concatenate([kr_ref[qb], kr_ref[qb + 1], kr_ref[qb + 2], kctx_ref[...]], axis=0)
        scores.append(_dot_t(k_all, q_all))
    probs = []
    for qb, s in zip(blocks, scores):
        variant = jnp.where(qb == 0, 1, 0) + jnp.where(qb == nb - 1, 2, 0)
        s_band = jnp.where(mask_ref[variant] > 0.5, s[:band], NEG_INF)
        s_ctx = s[band:]
        m = jnp.maximum(jnp.maximum(fold(s_band, jnp.max), fold(s_ctx, jnp.max)), sink)
        p_band = jnp.exp(s_band - m)
        p_ctx = jnp.exp(s_ctx - m)
        inv = 1.0 / (fold(p_band, jnp.sum) + fold(p_ctx, jnp.sum) + jnp.exp(sink - m))
        probs.append(jnp.concatenate([p_band * inv, p_ctx * inv], axis=0).astype(BF16))
    outs = []
    for qb, p_all in zip(blocks, probs):
        vt_all = jnp.concatenate([vt_ref[qb], vt_ref[qb + 1], vt_ref[qb + 2], vctxt_ref[...]], axis=1)
        outs.append(_dot(vt_all, p_all))
    for r, ot in zip(rows, outs):
        for g in range(SW_GROUP):
            o_ref[r, g * HEAD:(g + 1) * HEAD] = ot[:, g * SW_BLOCK:(g + 1) * SW_BLOCK].T.astype(BF16)


def _swa_lat(p, mix, geom, layer, sink, rope_tabs, cache_k, cache_v):
    n_seq, n, t = geom["n_seq"], geom["n"], geom["t"]
    kvh = geom["sw_kv"]
    qw = SW_GROUP * HEAD
    q_blk = geom["cq_off"] // qw
    k_blk = geom["ck_off"] // HEAD
    v_blk = k_blk + kvh
    mix_blk = (geom["da_heads"] + geom["hg_heads"]) * HEAD // qw
    nb = n // SW_BLOCK
    per_step = SW_PER_STEP if nb % SW_PER_STEP == 0 else 1
    nsteps = nb // per_step
    tq = per_step * SW_BLOCK
    n_ctx = cache_k.shape[3]
    masks = _band_masks(n)
    cspec = pl.BlockSpec((None, None, None, n_ctx, HEAD), lambda b, kv, i: (b, layer, kv, 0, 0))
    return pl.pallas_call(
        functools.partial(_swa_lat_kernel, layer=layer, n=n),
        out_shape=jax.ShapeDtypeStruct(mix.shape, mix.dtype),
        grid=(n_seq, kvh, nsteps),
        in_specs=[pl.BlockSpec((tq, qw), lambda b, kv, i: (b * nsteps + i, q_blk + kv)),
                  pl.BlockSpec((n, HEAD), lambda b, kv, i: (b, k_blk + kv)),
                  pl.BlockSpec((n, HEAD), lambda b, kv, i: (b, v_blk + kv))]
                 + [pl.BlockSpec((tq, HEAD), lambda b, kv, i: (i, 0))] * 3
                 + [pl.BlockSpec((n, HEAD), lambda b, kv, i: (0, 0))] * 3
                 + [cspec, cspec, pl.BlockSpec(masks.shape, lambda b, kv, i: (0, 0, 0)),
                    pl.BlockSpec(memory_space=pltpu.SMEM), pl.BlockSpec(memory_space=pl.ANY)],
        out_specs=pl.BlockSpec((tq, qw), lambda b, kv, i: (b * nsteps + i, mix_blk + kv)),
        scratch_shapes=[pltpu.VMEM((nb + 2, SW_BLOCK, HEAD), BF16), pltpu.VMEM((nb + 2, HEAD, SW_BLOCK), BF16),
                        pltpu.VMEM((n_ctx, HEAD), BF16), pltpu.VMEM((HEAD, n_ctx), BF16)],
        input_output_aliases={13: 0},
        compiler_params=_cparams("parallel", "parallel", "arbitrary"),
        name="banded_sink_attention",
    )(p, p, p, *rope_tabs, *rope_tabs, cache_k, cache_v, masks, sink, mix)


def _rope_tables(n, half):
    h = half // 2
    pos = jnp.arange(n)
    inv = ROPE_BASE ** (-jnp.arange(h, dtype=F32) / h)
    zero = jnp.zeros((n, h), F32)
    c, sa, sb = [], [], []
    for axis_pos in (pos // GRID_W, pos % GRID_W):
        ang = axis_pos.astype(F32)[:, None] * inv[None, :]
        cos, sin = jnp.cos(ang), jnp.sin(ang)
        c += [cos, cos]
        sa += [-sin, zero]
        sb += [zero, sin]
    reps = HEAD // (2 * half)
    cat = lambda parts: jnp.tile(jnp.concatenate(parts, axis=1), (1, reps))
    return cat(c), cat(sa), cat(sb)


def _geometry(n_seq, n, d_model):
    da_heads = hg_heads = d_model // 512
    sw_heads = d_model // 256
    sw_kv = sw_heads // SW_GROUP
    cq_off = (3 * da_heads + 5 * hg_heads) * HEAD
    return dict(n_seq=n_seq, n=n, t=n_seq * n, da_heads=da_heads, hg_heads=hg_heads, sw_kv=sw_kv,
                cq_off=cq_off, ck_off=cq_off + sw_heads * HEAD, d_mix=(da_heads + hg_heads + sw_heads) * HEAD)


def _token_tiles(n):
    return _pick_tile(n, TM_PROJ), _pick_tile(n, TM_OUT), _pick_tile(n, TM_FFN)


def _project(x, geom, layer, mod, mod_row, w_in, cast_src, new_caches):
    if new_caches is None:
        tm = _token_tiles(geom["n"])[0]
        return _in_proj(x, mod, mod_row, w_in, layer, tm, geom["d_mix"], cast_src)
    heads, kvh = geom["da_heads"], geom["sw_kv"]
    cols = ((heads * HEAD, heads), (2 * heads * HEAD, heads), (geom["ck_off"], kvh),
            (geom["ck_off"] + kvh * HEAD, kvh))
    return _in_proj(x, mod, mod_row, w_in, layer, geom["n"], geom["d_mix"], cast_src, new_caches, cols)


def _mix_and_ffn(x, p, mix, geom, layer, mod, mod_row, wts, params, lb_params, scan_consts, tabs, caches,
                 new_state, alpha, cast_src):
    w_out, w_gu, w_down = wts
    _, tm_out, tm_ffn = _token_tiles(geom["t"] if caches is None else geom["n"])
    if caches is None:
        mix = _diff_attn(p, mix, geom, layer, params["diff_lambda"], params["diff_norm_g"], None, None, None)
        mix, new_state = _hgrn(p, mix, geom, layer, lb_params, params["hgrn_norm_g"], scan_consts, None, new_state)
        mix = _swa_ctx(p, mix, geom, layer, params["swa_sink"])
    else:
        ck_d, cv_d, ck_s, cv_s, state = caches
        mix = _diff_attn(p, mix, geom, layer, params["diff_lambda"], params["diff_norm_g"], tabs[0], ck_d, cv_d)
        mix, _ = _hgrn(p, mix, geom, layer, lb_params, params["hgrn_norm_g"], scan_consts, state)
        mix = _swa_lat(p, mix, geom, layer, params["swa_sink"], tabs[1], ck_s, cv_s)
    x1 = _out_proj(mix, w_out, x, mod, mod_row, params["ln1_g"], params["ln1_b"], layer, tm_out, alpha)
    y, next_w = _ffn(x1, w_gu, w_down, mod, mod_row, params["ln2_g"], params["ln2_b"], layer, tm_ffn, alpha, cast_src)
    return y, new_state, next_w


def kernel(x_prompt, x_sample, cache_diff_k, cache_diff_v, cache_swa_k, cache_swa_v, state_hgrn, c, c_ctx, w_mod,
           b_mod, w_in, w_out, diff_lambda, diff_norm_g, hgrn_lb_logits, hgrn_norm_g, swa_sink, ln1_g, ln1_b, ln2_g,
           ln2_b, w_gate_up, w_down):
    batch, seq, d = x_prompt.shape
    dec_batch, dec_seq, _ = x_sample.shape
    depth = w_mod.shape[0]
    alpha = (2 * depth) ** 0.25
    geom_c = _geometry(batch, seq, d)
    geom_l = _geometry(dec_batch, dec_seq, d)
    assert 1 + dec_batch <= MOD_ROWS

    cond = jnp.zeros((MOD_ROWS, d), F32).at[0].set(c_ctx).at[1:1 + dec_batch].set(c)
    mod = _modulation(cond, w_mod, b_mod).reshape(depth * MOD_ROWS * 6, 1, d)
    lb_params = _lb_params(hgrn_lb_logits)
    params = dict(diff_lambda=diff_lambda, diff_norm_g=diff_norm_g, hgrn_norm_g=hgrn_norm_g, swa_sink=swa_sink,
                  ln1_g=ln1_g, ln1_b=ln1_b, ln2_g=ln2_g, ln2_b=ln2_b)
    w_in16, w_out16 = w_in[0].astype(BF16), w_out[0].astype(BF16)
    tabs = (_rope_tables(dec_seq, DA_QK // 2), _rope_tables(dec_seq, HEAD // 2))
    consts_c = _scan_constants(min(HG_CHUNK, seq))
    consts_l = _scan_constants(min(HG_CHUNK, dec_seq))

    y_p = x_prompt.reshape(batch * seq, d)
    y_s = x_sample.reshape(dec_batch * dec_seq, d)
    heads, kvh = geom_c["da_heads"], geom_c["sw_kv"]
    new_kv = tuple(jnp.zeros((batch, depth, nh, seq, HEAD), F32) for nh in (heads, heads, kvh, kvh))
    new_state = jnp.zeros((batch, depth, 2, geom_c["hg_heads"], HEAD, HEAD), F32)
    caches = (cache_diff_k, cache_diff_v, cache_swa_k, cache_swa_v, state_hgrn)
    for l in range(depth):
        row_c = lambda m, tm, j, l=l: (l * MOD_ROWS) * 6 + j
        row_l = lambda m, tm, j, l=l: (l * MOD_ROWS + 1 + (m * tm) // dec_seq) * 6 + j
        p_c, mix_c, w_down16, new_kv = _project(y_p, geom_c, l, mod, row_c, w_in16, w_down, new_kv)
        p_l, mix_l, w_gu16, _ = _project(y_s, geom_l, l, mod, row_l, w_in16, w_gate_up, None)
        wts = (w_out16, w_gu16, w_down16)
        more = l + 1 < depth
        y_p, new_state, w_out16 = _mix_and_ffn(y_p, p_c, mix_c, geom_c, l, mod, row_c, wts, params, lb_params,
                                               consts_c, None, None, new_state, alpha, w_out if more else None)
        y_s, _, w_in16 = _mix_and_ffn(y_s, p_l, mix_l, geom_l, l, mod, row_l, wts, params, lb_params, consts_l,
                                      tabs, caches, None, alpha, w_in if more else None)
    return (y_p.reshape(batch, seq, d), y_s.reshape(dec_batch, dec_seq, d)) + new_kv + (new_state,)
```

```python
import functools
import math

import numpy as np
import jax
import jax.numpy as jnp
from jax import lax
from jax.experimental import pallas as pl
from jax.experimental.pallas import tpu as pltpu

F32 = jnp.float32
BF16 = jnp.bfloat16

GRID_W = 64
ROPE_BASE = 10000.0
LN_EPS = 1e-5
RMS_EPS = 1e-6
NEG_INF = -1e30
LB_FLOOR = 1e-30
LOG2_E = math.log2(math.e)
HEAD = 128
DA_QK = 64
SW_GROUP = 4
SW_BLOCK = 128
SW_WINDOW = 128
MOD_ROWS = 8

VMEM_LIMIT = 56 * 1024 * 1024
TM_PROJ = 256
TM_OUT = 512
TM_FFN = 512
TF_FFN = 512
TN_MOD = 1024
DA_TQ = 256
DA_KB = 256
HG_CHUNK = 64
HG_UNROLL = 8
SW_PER_STEP = 8
CTX_SEQ_PER_STEP = 4


def _cparams(*sem):
    return pltpu.CompilerParams(dimension_semantics=sem, vmem_limit_bytes=VMEM_LIMIT)


def _dot(a, b):
    return jnp.dot(a, b, preferred_element_type=F32)


def _dot_t(a, b):
    return lax.dot_general(a, b, (((1,), (1,)), ((), ())), preferred_element_type=F32)


def _silu(x):
    return x / (1.0 + jnp.exp2(x * (-LOG2_E)))


def _layer_norm(y, g, b):
    mu = jnp.mean(y, axis=-1, keepdims=True)
    d = y - mu
    var = jnp.mean(d * d, axis=-1, keepdims=True)
    return d * lax.rsqrt(var + LN_EPS) * g + b


def _rms_norm(o, g):
    ms = jnp.mean(o * o, axis=-1, keepdims=True)
    return o * lax.rsqrt(ms + RMS_EPS) * g


def _rope(x, c, sa, sb, w):
    return x * c + pltpu.roll(x, HEAD - w, 1) * sa + pltpu.roll(x, w, 1) * sb


def _pick_tile(n, target):
    t = min(n, target)
    while n % t or t % 128:
        t -= 128
    return t


def _lb_kernel(logit_ref, loglb_ref, log1m_ref, onem_ref):
    depth = logit_ref.shape[0]
    x = [logit_ref[l] for l in range(depth)]
    m = functools.reduce(jnp.maximum, x)
    e = [jnp.exp(xi - m) for xi in x]
    tot = functools.reduce(lambda a, b: a + b, e)
    w = [ei / tot for ei in e]
    acc = jnp.zeros_like(w[0])
    for l in range(depth):
        acc = acc + w[l]
        lb = acc - w[0]
        loglb_ref[l] = jnp.log(jnp.maximum(lb, LB_FLOOR))
        log1m_ref[l] = jnp.log1p(-lb)
        onem_ref[l] = 1.0 - lb


def _lb_params(logits):
    shp = jax.ShapeDtypeStruct(logits.shape, F32)
    return pl.pallas_call(_lb_kernel, out_shape=(shp, shp, shp), name="hgrn_lb_params")(logits)


def _mod_kernel(c_ref, w_ref, b_ref, o_ref):
    s = _silu(c_ref[...])
    s_hi = s.astype(BF16)
    s_lo = (s - s_hi.astype(F32)).astype(BF16)
    w = w_ref[...]
    w_hi = w.astype(BF16)
    w_lo = (w - w_hi.astype(F32)).astype(BF16)
    o_ref[...] = _dot(s_hi, w_hi) + _dot(s_lo, w_hi) + _dot(s_hi, w_lo) + b_ref[...]


def _modulation(cond, w_mod, b_mod):
    depth, d, d6 = w_mod.shape
    tn = _pick_tile(d6, TN_MOD)
    return pl.pallas_call(
        _mod_kernel,
        out_shape=jax.ShapeDtypeStruct((depth, MOD_ROWS, d6), F32),
        grid=(depth, d6 // tn),
        in_specs=[pl.BlockSpec((MOD_ROWS, d), lambda l, n: (0, 0)),
                  pl.BlockSpec((None, d, tn), lambda l, n: (l, 0, n)),
                  pl.BlockSpec((None, 1, tn), lambda l, n: (l, 0, n))],
        out_specs=pl.BlockSpec((None, MOD_ROWS, tn), lambda l, n: (l, 0, n)),
        compiler_params=_cparams("parallel", "parallel"),
        name="adaln_modulation",
    )(cond, w_mod, b_mod.reshape(depth, 1, d6))


def _proj_kernel(x_ref, sh_ref, sc_ref, w_ref, cast_ref, *rest, cache_cols):
    n = len(cache_cols)
    o_ref, mix_ref, cast_out_ref = rest[n:n + 3]
    h = (x_ref[...] * (1.0 + sc_ref[0]) + sh_ref[0]).astype(BF16)
    o_ref[...] = _dot(h, w_ref[...])
    mix_ref[...] = jnp.zeros_like(mix_ref)
    cast_out_ref[...] = cast_ref[...].astype(BF16)
    for c_ref, (off, nh) in zip(rest[n + 3:], cache_cols):
        for hd in range(nh):
            c_ref[hd] = o_ref[:, off + hd * HEAD:off + (hd + 1) * HEAD]


def _in_proj(x, mod, mod_row, w_in, layer, tm, d_mix, cast_src, caches=None, cache_cols=()):
    t, d = x.shape
    d_in = w_in.shape[1]
    steps = t // tm
    _, cast_rows, cast_width = cast_src.shape
    slab = cast_rows // steps
    assert cast_rows % steps == 0 and slab % 16 == 0
    caches = tuple(caches or ())
    out_shape = ([jax.ShapeDtypeStruct((t, d_in), F32), jax.ShapeDtypeStruct((t, d_mix), BF16),
                  jax.ShapeDtypeStruct((cast_rows, cast_width), BF16)]
                 + [jax.ShapeDtypeStruct(c.shape, c.dtype) for c in caches])
    out_specs = [pl.BlockSpec((tm, d_in), lambda m: (m, 0)), pl.BlockSpec((tm, d_mix), lambda m: (m, 0)),
                 pl.BlockSpec((slab, cast_width), lambda m: (m, 0))]
    for c in caches:
        assert c.shape[3] == tm
        out_specs.append(pl.BlockSpec((None, None, c.shape[2], tm, HEAD), lambda m: (m, layer, 0, 0, 0)))
    res = pl.pallas_call(
        functools.partial(_proj_kernel, cache_cols=tuple(cache_cols)),
        out_shape=tuple(out_shape),
        grid=(steps,),
        in_specs=[pl.BlockSpec((tm, d), lambda m: (m, 0)),
                  pl.BlockSpec((1, 1, d), lambda m: (mod_row(m, tm, 0), 0, 0)),
                  pl.BlockSpec((1, 1, d), lambda m: (mod_row(m, tm, 1), 0, 0)),
                  pl.BlockSpec((d, d_in), lambda m: (0, 0), pipeline_mode=pl.Buffered(1)),
                  pl.BlockSpec((None, slab, cast_width), lambda m: (layer, m, 0))]
                 + [pl.BlockSpec(memory_space=pl.ANY)] * len(caches),
        out_specs=tuple(out_specs),
        input_output_aliases={5 + i: 3 + i for i in range(len(caches))},
        compiler_params=_cparams("parallel"),
        name="in_proj",
    )(x, mod, mod, w_in, cast_src, *caches)
    return res[0], res[1], res[2], tuple(res[3:])


def _out_proj_kernel(mix_ref, w_ref, x_ref, g1_ref, lng_ref, lnb_ref, x1_ref, *, alpha):
    mix = _dot(mix_ref[...], w_ref[...])
    x1_ref[...] = _layer_norm(alpha * x_ref[...] + g1_ref[0] * mix, lng_ref[...], lnb_ref[...])


def _out_proj(mix, w_out, x, mod, mod_row, ln_g, ln_b, layer, tm, alpha):
    t, d = x.shape
    d_mix = mix.shape[1]
    depth = ln_g.shape[0]
    vec = pl.BlockSpec((None, 1, d), lambda m: (layer, 0, 0))
    return pl.pallas_call(
        functools.partial(_out_proj_kernel, alpha=alpha),
        out_shape=jax.ShapeDtypeStruct((t, d), F32),
        grid=(t // tm,),
        in_specs=[pl.BlockSpec((tm, d_mix), lambda m: (m, 0)),
                  pl.BlockSpec((d_mix, d), lambda m: (0, 0), pipeline_mode=pl.Buffered(1)),
                  pl.BlockSpec((tm, d), lambda m: (m, 0)),
                  pl.BlockSpec((1, 1, d), lambda m: (mod_row(m, tm, 2), 0, 0)),
                  vec, vec],
        out_specs=pl.BlockSpec((tm, d), lambda m: (m, 0)),
        compiler_params=_cparams("parallel"),
        name="out_proj_ln",
    )(mix, w_out, x, mod, ln_g.reshape(depth, 1, d), ln_b.reshape(depth, 1, d))


def _ffn_kernel(x_ref, sh2_ref, sc2_ref, wg_ref, wu_ref, wd_ref, g2_ref, lng_ref, lnb_ref, *rest, alpha, cast):
    if cast:
        cast_ref, o_ref, cast_out_ref, h_ref = rest
    else:
        o_ref, h_ref = rest
    f = pl.program_id(1)

    @pl.when(f == 0)
    def _():
        h_ref[...] = (x_ref[...] * (1.0 + sc2_ref[0]) + sh2_ref[0]).astype(BF16)
        o_ref[...] = jnp.zeros_like(o_ref)
        if cast:
            cast_out_ref[...] = cast_ref[...].astype(BF16)

    h = h_ref[...]
    a = _dot(h, wg_ref[...])
    u = _dot(h, wu_ref[...])
    o_ref[...] += _dot((_silu(a) * u).astype(BF16), wd_ref[...])

    @pl.when(f == pl.num_programs(1) - 1)
    def _():
        y = alpha * x_ref[...] + g2_ref[0] * o_ref[...]
        o_ref[...] = _layer_norm(y, lng_ref[...], lnb_ref[...])


def _ffn(x1, w_gu, w_down, mod, mod_row, ln_g, ln_b, layer, tm, alpha, cast_src=None):
    t, d = x1.shape
    d_ff = w_down.shape[0]
    depth = ln_g.shape[0]
    tf = _pick_tile(d_ff, TF_FFN)
    nf = d_ff // tf
    tiles = t // tm
    vec = pl.BlockSpec((None, 1, d), lambda m, f: (layer, 0, 0))
    row = lambda j: pl.BlockSpec((1, 1, d), lambda m, f: (mod_row(m, tm, j), 0, 0))
    in_specs = [pl.BlockSpec((tm, d), lambda m, f: (m, 0)),
                row(3), row(4),
                pl.BlockSpec((d, tf), lambda m, f: (0, f)),
                pl.BlockSpec((d, tf), lambda m, f: (0, nf + f)),
                pl.BlockSpec((tf, d), lambda m, f: (f, 0)),
                row(5), vec, vec]
    args = [x1, mod, mod, w_gu, w_gu, w_down, mod, ln_g.reshape(depth, 1, d), ln_b.reshape(depth, 1, d)]
    out_shape = [jax.ShapeDtypeStruct((t, d), F32)]
    out_specs = [pl.BlockSpec((tm, d), lambda m, f: (m, 0))]
    if cast_src is not None:
        _, rows, width = cast_src.shape
        slab = rows // tiles
        assert rows % tiles == 0 and slab % 16 == 0
        in_specs.append(pl.BlockSpec((None, slab, width), lambda m, f: (layer + 1, m, 0)))
        args.append(cast_src)
        out_shape.append(jax.ShapeDtypeStruct((rows, width), BF16))
        out_specs.append(pl.BlockSpec((slab, width), lambda m, f: (m, 0)))
    res = pl.pallas_call(
        functools.partial(_ffn_kernel, alpha=alpha, cast=cast_src is not None),
        out_shape=tuple(out_shape),
        grid=(tiles, nf),
        in_specs=in_specs,
        out_specs=tuple(out_specs),
        scratch_shapes=[pltpu.VMEM((tm, d), BF16)],
        compiler_params=_cparams("parallel", "arbitrary"),
        name="ffn_ln",
    )(*args)
    return (res[0], res[1]) if cast_src is not None else (res[0], None)


def _diff_attn_kernel(*refs, rope, cached, n_self, kb, lam_init, pipelined, n_tiles):
    it = iter(refs)
    q_ref, k_ref, v_ref = next(it), next(it), next(it)
    if rope:
        qc_ref, qsa_ref, qsb_ref, kc_ref, ksa_ref, ksb_ref = (next(it) for _ in range(6))
    if cached:
        ck_ref, cv_ref = next(it), next(it)
    lam_ref, g_ref, _, o_ref, kr_ref, vt_ref, m_ref, acc_ref = (next(it) for _ in range(8))
    s_refs = tuple(it)
    nblk = kr_ref.shape[0]
    nself = n_self // kb
    tq = q_ref.shape[0]
    step = pl.program_id(2)
    last = pl.num_programs(2) - 1

    def prepare_keys():
        for j in range(nself):
            rows = slice(j * kb, (j + 1) * kb)
            k = k_ref[rows, :]
            if rope:
                k = _rope(k, kc_ref[rows, :], ksa_ref[rows, :], ksb_ref[rows, :], DA_QK // 4)
            kr_ref[j] = k.astype(BF16)
            vt_ref[j] = v_ref[rows, :].T.astype(BF16)
        if cached:
            for j in range(nblk - nself):
                rows = slice(j * kb, (j + 1) * kb)
                kr_ref[nself + j] = ck_ref[rows, :].astype(BF16)
                vt_ref[nself + j] = cv_ref[rows, :].T.astype(BF16)

    def queries():
        q = q_ref[...]
        if rope:
            q = _rope(q, qc_ref[...], qsa_ref[...], qsb_ref[...], DA_QK // 4)
        q = q * (DA_QK ** -0.5 * LOG2_E)
        lane = lax.broadcasted_iota(jnp.int32, q.shape, 1)
        return (jnp.where(lane < DA_QK, q, 0.0).astype(BF16), jnp.where(lane >= DA_QK, q, 0.0).astype(BF16))

    def fold(x, op):
        return op(x.reshape(kb // 8, 8, tq), axis=0)

    def scores_block(j, qz, buf, ms):
        kblk = kr_ref[j]
        out = []
        for mp in range(2):
            s = _dot_t(kblk, qz[mp])
            buf[mp, j] = s
            out.append(jnp.maximum(ms[mp], fold(s, jnp.max)))
        return tuple(out)

    def values_block(j, buf, ms, ls):
        out = []
        for mp in range(2):
            p = jnp.exp2(buf[mp, j] - ms[mp])
            acc_ref[mp] += _dot(vt_ref[j], p.astype(BF16))
            out.append(ls[mp] + fold(p, jnp.sum))
        return tuple(out)

    neg = jnp.full((8, tq), -jnp.inf, F32)
    zero = jnp.zeros((8, tq), F32)

    def save_max(ms):
        for mp in range(2):
            m_ref[mp] = jnp.max(ms[mp], axis=0, keepdims=True)

    def finish(ls):
        l1, l2 = (jnp.sum(l, axis=0, keepdims=True) for l in ls)
        lp = lam_ref[...]
        lam = (jnp.exp(jnp.sum(lp[0:1] * lp[1:2], axis=-1, keepdims=True))
               - jnp.exp(jnp.sum(lp[2:3] * lp[3:4], axis=-1, keepdims=True)) + lam_init)
        ot = acc_ref[0] * (1.0 / l1) - acc_ref[1] * (lam / l2)
        ms_o = jnp.mean(ot * ot, axis=0, keepdims=True)
        ot = ot * lax.rsqrt(ms_o + RMS_EPS) * g_ref[...] * (1.0 - lam_init)
        o_ref[...] = ot.T.astype(BF16)

    if not pipelined:
        pl.when(step == 0)(prepare_keys)
        qz = queries()
        ms = (neg, neg)
        for j in range(nblk):
            ms = scores_block(j, qz, s_refs[0], ms)
        ms = tuple(jnp.max(m, axis=0, keepdims=True) for m in ms)
        acc_ref[...] = jnp.zeros_like(acc_ref)
        ls = (zero, zero)
        for j in range(nblk):
            ls = values_block(j, s_refs[0], ms, ls)
        finish(ls)
        return

    @pl.when(step == 0)
    def _():
        prepare_keys()
        qz = queries()
        ms = (neg, neg)
        for j in range(nblk):
            ms = scores_block(j, qz, s_refs[0], ms)
        save_max(ms)

    def interior(parity):
        prev = (m_ref[0], m_ref[1])
        qz = queries()
        acc_ref[...] = jnp.zeros_like(acc_ref)
        ms, ls = (neg, neg), (zero, zero)
        for j in range(nblk):
            ms = scores_block(j, qz, s_refs[parity], ms)
            ls = values_block(j, s_refs[1 - parity], prev, ls)
        save_max(ms)
        finish(ls)

    inside = (step > 0) & (step < last)
    pl.when(inside & (step % 2 == 1))(functools.partial(interior, 1))
    pl.when(inside & (step % 2 == 0))(functools.partial(interior, 0))

    @pl.when(step == last)
    def _():
        prev = (m_ref[0], m_ref[1])
        acc_ref[...] = jnp.zeros_like(acc_ref)
        ls = (zero, zero)
        for j in range(nblk):
            ls = values_block(j, s_refs[(n_tiles - 1) % 2], prev, ls)
        finish(ls)


def _diff_attn_seq_kernel(q_ref, k_ref, v_ref, lam_ref, g_ref, mix_ref, o_ref, *, n, heads, lam_init):
    del mix_ref
    lane = lax.broadcasted_iota(jnp.int32, (n, HEAD), 1)
    units = [(slice(sq * n, (sq + 1) * n), slice(hd * HEAD, (hd + 1) * HEAD))
             for sq in range(q_ref.shape[0] // n) for hd in range(heads)]
    scores, vts = [], []
    for rows, cols in units:
        q = q_ref[rows, cols] * (DA_QK ** -0.5 * LOG2_E)
        k = k_ref[rows, cols].astype(BF16)
        scores.append([_dot_t(k, jnp.where(lane < DA_QK, q, 0.0).astype(BF16)),
                       _dot_t(k, jnp.where(lane >= DA_QK, q, 0.0).astype(BF16))])
        vts.append(v_ref[rows, cols].T.astype(BF16))
    probs = []
    for pair in scores:
        parts = []
        for s in pair:
            p = jnp.exp2(s - jnp.max(s, axis=0, keepdims=True))
            parts.append((p.astype(BF16), jnp.sum(p, axis=0, keepdims=True)))
        probs.append(parts)
    accs = [[_dot(vt, pb) for pb, _ in parts] for vt, parts in zip(vts, probs)]
    lp = lam_ref[...]
    lam = (jnp.exp(jnp.sum(lp[0:1] * lp[1:2], axis=-1, keepdims=True))
           - jnp.exp(jnp.sum(lp[2:3] * lp[3:4], axis=-1, keepdims=True)) + lam_init)
    for (rows, cols), ((_, l1), (_, l2)), acc in zip(units, probs, accs):
        ot = acc[0] * (1.0 / l1) - acc[1] * (lam / l2)
        ms_o = jnp.mean(ot * ot, axis=0, keepdims=True)
        ot = ot * lax.rsqrt(ms_o + RMS_EPS) * g_ref[...] * (1.0 - lam_init)
        o_ref[rows, cols] = ot.T.astype(BF16)


def _diff_attn_seq(p, mix, geom, layer, diff_lambda, diff_g):
    n_seq, n = geom["n_seq"], geom["n"]
    heads = geom["da_heads"]
    depth = diff_g.shape[0]
    lam_init = 0.8 - 0.6 * math.exp(-0.3 * layer)
    per_step = math.gcd(n_seq, CTX_SEQ_PER_STEP)
    blk = lambda j: pl.BlockSpec((per_step * n, heads * HEAD), lambda b: (b, j))
    return pl.pallas_call(
        functools.partial(_diff_attn_seq_kernel, n=n, heads=heads, lam_init=lam_init),
        out_shape=jax.ShapeDtypeStruct(mix.shape, mix.dtype),
        grid=(n_seq // per_step,),
        in_specs=[blk(0), blk(1), blk(2),
                  pl.BlockSpec((None, 4, DA_QK), lambda b: (layer, 0, 0)),
                  pl.BlockSpec((None, HEAD, 1), lambda b: (layer, 0, 0)),
                  pl.BlockSpec(memory_space=pl.ANY)],
        out_specs=blk(0),
        input_output_aliases={5: 0},
        compiler_params=_cparams("parallel"),
        name="diff_attention_seq",
    )(p, p, p, diff_lambda, diff_g.reshape(depth, HEAD, 1), mix)


def _diff_attn(p, mix, geom, layer, diff_lambda, diff_g, rope_tabs, cache_k, cache_v):
    if rope_tabs is None and cache_k is None and geom["n"] <= DA_TQ:
        return _diff_attn_seq(p, mix, geom, layer, diff_lambda, diff_g)
    n_seq, n, t = geom["n_seq"], geom["n"], geom["t"]
    heads = geom["da_heads"]
    rope = rope_tabs is not None
    cached = cache_k is not None
    tq = min(n, DA_TQ)
    kb = min(n, DA_KB)
    nq = n // tq
    n_ctx = cache_k.shape[3] if cached else 0
    nblk = (n + n_ctx) // kb
    assert n % kb == 0 and n_ctx % kb == 0
    k_off, v_off = heads, 2 * heads
    depth = diff_g.shape[0]
    lam_init = 0.8 - 0.6 * math.exp(-0.3 * layer)

    pipelined = nq > 1
    q_tile = (lambda i: jnp.minimum(i, nq - 1)) if pipelined else (lambda i: i)
    o_tile = (lambda i: jnp.maximum(i - 1, 0)) if pipelined else (lambda i: i)
    in_specs = [pl.BlockSpec((tq, HEAD), lambda b, h, i: (b * nq + q_tile(i), h)),
                pl.BlockSpec((n, HEAD), lambda b, h, i: (b, k_off + h)),
                pl.BlockSpec((n, HEAD), lambda b, h, i: (b, v_off + h))]
    args = [p, p, p]
    if rope:
        in_specs += [pl.BlockSpec((tq, HEAD), lambda b, h, i: (q_tile(i), 0))] * 3
        in_specs += [pl.BlockSpec((n, HEAD), lambda b, h, i: (0, 0))] * 3
        args += list(rope_tabs) * 2
    if cached:
        spec = pl.BlockSpec((None, None, None, n_ctx, HEAD), lambda b, h, i: (b, layer, h, 0, 0))
        in_specs += [spec, spec]
        args += [cache_k, cache_v]
    in_specs += [pl.BlockSpec((None, 4, DA_QK), lambda b, h, i: (layer, 0, 0)),
                 pl.BlockSpec((None, HEAD, 1), lambda b, h, i: (layer, 0, 0)),
                 pl.BlockSpec(memory_space=pl.ANY)]
    args += [diff_lambda, diff_g.reshape(depth, HEAD, 1), mix]
    return pl.pallas_call(
        functools.partial(_diff_attn_kernel, rope=rope, cached=cached, n_self=n, kb=kb, lam_init=lam_init,
                          pipelined=pipelined, n_tiles=nq),
        out_shape=jax.ShapeDtypeStruct(mix.shape, mix.dtype),
        grid=(n_seq, heads, nq + 1 if pipelined else nq),
        in_specs=in_specs,
        input_output_aliases={len(args) - 1: 0},
        out_specs=pl.BlockSpec((tq, HEAD), lambda b, h, i: (b * nq + o_tile(i), h)),
        scratch_shapes=[pltpu.VMEM((nblk, kb, HEAD), BF16), pltpu.VMEM((nblk, HEAD, kb), BF16),
                        pltpu.VMEM((2, 1, tq), F32), pltpu.VMEM((2, HEAD, tq), F32)]
                       + [pltpu.VMEM((2, nblk, kb, tq), F32)] * (2 if pipelined else 1),
        compiler_params=_cparams("parallel", "parallel", "arbitrary"),
        name="diff_attention",
    )(*args)


def _scan_constants(c):
    levels = int(math.log2(c))
    t = np.arange(c)[:, None]
    s = np.arange(c)[None, :]
    tri = (s <= t).astype(np.float32)
    masks = []
    for j in range(levels):
        m = c >> (j + 1)
        base = (t // (2 * m)) * (2 * m)
        sbase = (s // (2 * m)) * (2 * m)
        masks.append((sbase == base) & ((t - base) >= m) & ((s - sbase) < m))
    masks.append(s == t)
    masks = np.stack(masks).astype(np.float32)
    flip = lambda a: a.reshape(-1, c, c)[:, ::-1, ::-1].reshape(a.shape)
    return (jnp.asarray(tri, BF16), jnp.asarray(flip(tri), BF16),
            jnp.asarray(masks, F32), jnp.asarray(flip(masks), F32))


def _level_exponents(cum, d, chunk):
    row = lax.broadcasted_iota(jnp.int32, cum.shape, 0)
    sub = lax.broadcasted_iota(jnp.int32, (8, HEAD), 0)
    out = []
    m = chunk // 2
    while m >= 1:
        in_second = (row & m) != 0
        q_side = in_second if d == 0 else jnp.logical_not(in_second)
        if m >= 8:
            pieces = []
            for b0 in range(0, chunk, 2 * m):
                r = b0 + m - 1 + d
                bnd = jnp.broadcast_to(cum[r:r + 1, :], (m, HEAD))
                first, second = cum[b0:b0 + m, :], cum[b0 + m:b0 + 2 * m, :]
                pieces += [bnd - first, second - bnd] if d == 0 else [first - bnd, bnd - second]
            x = jnp.concatenate(pieces, axis=0)
        elif m == 1:
            neighbour = pltpu.roll(cum, 1 if d == 0 else chunk - 1, 0)
            x = jnp.where(q_side, cum - neighbour, 0.0)
        else:
            tiles = []
            for v in range(chunk // 8):
                if m == 4:
                    r = 8 * v + 3 + d
                    tiles.append(jnp.broadcast_to(cum[r:r + 1, :], (8, HEAD)))
                else:
                    lo = jnp.broadcast_to(cum[8 * v + 1 + d:8 * v + 2 + d, :], (8, HEAD))
                    hi = jnp.broadcast_to(cum[8 * v + 5 + d:8 * v + 6 + d, :], (8, HEAD))
                    tiles.append(jnp.where(sub < 4, lo, hi))
            bnd = jnp.concatenate(tiles, axis=0)
            x = jnp.where(q_side, cum - bnd, bnd - cum)
        out.append(x)
        m //= 2
    return out


def _hgrn_kernel(*refs, n, chunk, unroll, hs, has_state, emit_state):
    it = iter(refs)
    q_ref, zf_ref, zb_ref, v_ref, hg_ref = (next(it) for _ in range(5))
    loglb_ref, log1m_ref, onem_ref, g_ref = (next(it) for _ in range(4))
    tri_refs = (next(it), next(it))
    mask_refs = (next(it), next(it))
    s0_ref = next(it) if has_state else None
    next(it)
    if emit_state:
        next(it)
    o_ref = next(it)
    st_ref = next(it) if emit_state else None
    obuf_refs = (next(it), next(it))
    s_ref = next(it)

    levels = int(math.log2(chunk))
    nchunks = n // chunk
    z_refs = (zf_ref, zb_ref)

    for d in range(2):
        for hd in range(hs):
            s_ref[d * hs + hd] = s0_ref[d, hd].T if has_state else jnp.zeros((HEAD, HEAD), F32)

    def gates(hd, d, start):
        rows = pl.ds(start, chunk)
        cols = slice(hd * HEAD, (hd + 1) * HEAD)
        z = z_refs[d][rows, cols]
        q = _silu(q_ref[rows, cols])
        v = v_ref[rows, cols].astype(BF16)
        z2 = z * LOG2_E
        e = jnp.exp2(-jnp.abs(z2))
        one_e = 1.0 + e
        log_sig = jnp.minimum(z2, 0.0) - jnp.log2(one_e)
        sig_neg = jnp.where(z >= 0.0, e, 1.0) / one_e
        a = loglb_ref[d:d + 1, cols] * LOG2_E
        b = log1m_ref[d:d + 1, cols] * LOG2_E + log_sig
        g = jnp.maximum(a, b) + jnp.log2(1.0 + jnp.exp2(-jnp.abs(a - b)))
        k = onem_ref[d:d + 1, cols] * sig_neg
        g_hi = g.astype(BF16)
        g_lo = (g - g_hi.astype(F32)).astype(BF16)
        cum = _dot(tri_refs[d][...], jnp.concatenate([g_hi, g_lo], axis=1))
        return q, k, v, cum[:, :HEAD] + cum[:, HEAD:]

    def body(i, carry):
        chains = []
        for u in range(unroll):
            c = i * unroll + u
            for hd in range(hs):
                chains.append((hd, 0, pl.multiple_of(c * chunk, chunk)))
                chains.append((hd, 1, pl.multiple_of((nchunks - 1 - c) * chunk, chunk)))
        work = [gates(hd, d, start) for hd, d, start in chains]
        qk16 = [(q.astype(BF16), k.astype(BF16)) for q, k, _, _ in work]
        atts = [mask_refs[d][levels] * _dot_t(qb, kb) for (_, d, _), (qb, kb) in zip(chains, qk16)]
        expo = [_level_exponents(cum, d, chunk) for (_, d, _), (_, _, _, cum) in zip(chains, work)]
        for j in range(levels):
            for ci, ((_, d, _), (qb, kb)) in enumerate(zip(chains, qk16)):
                fac = jnp.exp2(expo[ci][j]).astype(BF16)
                atts[ci] = atts[ci] + mask_refs[d][j] * _dot_t(qb * fac, kb * fac)
        intra, delta, q_in, decay = [], [], [], []
        for (_, d, _), (q, k, v, cum), att in zip(chains, work, atts):
            last = cum[chunk - 1:chunk] if d == 0 else cum[0:1]
            kt = (k * jnp.exp2(last - cum)).astype(BF16)
            intra.append(_dot(att.astype(BF16), v))
            delta.append(lax.dot_general(v, kt, (((0,), (0,)), ((), ())), preferred_element_type=F32))
            q_in.append((q * jnp.exp2(cum)).astype(BF16))
            decay.append(jnp.exp2(last))
        st = [s_ref[j] for j in range(2 * hs)]
        for ci, (hd, d, start) in enumerate(chains):
            j = d * hs + hd
            obuf_refs[d][pl.ds(start, chunk), hd * HEAD:(hd + 1) * HEAD] = (
                intra[ci] + _dot_t(q_in[ci], st[j].astype(BF16)))
            st[j] = decay[ci] * st[j] + delta[ci]
        for j in range(2 * hs):
            s_ref[j] = st[j]
        return carry

    lax.fori_loop(0, nchunks // unroll, body, 0)

    for hd in range(hs):
        cols = slice(hd * HEAD, (hd + 1) * HEAD)
        o = obuf_refs[0][:, cols] + obuf_refs[1][:, cols]
        o_ref[:, cols] = (_rms_norm(o, g_ref[...]) * _silu(hg_ref[:, cols])).astype(BF16)
        if emit_state:
            for d in range(2):
                st_ref[d, hd] = s_ref[d * hs + hd].T


def _hgrn(p, mix, geom, layer, lb_params, hgrn_g, consts, state, new_state=None):
    n_seq, n, t = geom["n_seq"], geom["n"], geom["t"]
    heads = geom["hg_heads"]
    mix_col = geom["da_heads"]
    base = 3 * geom["da_heads"]
    has_state = state is not None
    emit_state = not has_state
    depth = hgrn_g.shape[0]
    chunk = min(HG_CHUNK, n)
    unroll = math.gcd(n // chunk, HG_UNROLL)
    hs = math.gcd(heads, HG_UNROLL // unroll)
    assert base % hs == 0 and mix_col % hs == 0
    hw = hs * HEAD
    col = lambda j: pl.BlockSpec((n, hw), lambda b, h: (b, (base + j * heads) // hs + h))
    lbspec = pl.BlockSpec((None, 2, hw), lambda b, h: (layer, 0, h))
    const_specs = [pl.BlockSpec(c.shape, lambda b, h, nd=c.ndim: (0,) * nd) for c in consts]
    in_specs = [col(0), col(1), col(2), col(3), col(4), lbspec, lbspec, lbspec,
                pl.BlockSpec((None, 1, HEAD), lambda b, h: (layer, 0, 0))] + const_specs
    args = [p] * 5 + list(lb_params) + [hgrn_g.reshape(depth, 1, HEAD)] + list(consts)
    if has_state:
        in_specs.append(pl.BlockSpec((None, None, 2, hs, HEAD, HEAD), lambda b, h: (b, layer, 0, h, 0, 0)))
        args.append(state)
    in_specs.append(pl.BlockSpec(memory_space=pl.ANY))
    args.append(mix)
    aliases = {len(args) - 1: 0}
    out_shape = [jax.ShapeDtypeStruct(mix.shape, mix.dtype)]
    out_specs = [pl.BlockSpec((n, hw), lambda b, h: (b, mix_col // hs + h))]
    if emit_state:
        in_specs.append(pl.BlockSpec(memory_space=pl.ANY))
        args.append(new_state)
        aliases[len(args) - 1] = 1
        out_shape.append(jax.ShapeDtypeStruct(new_state.shape, new_state.dtype))
        out_specs.append(pl.BlockSpec((None, None, 2, hs, HEAD, HEAD), lambda b, h: (b, layer, 0, h, 0, 0)))
    res = pl.pallas_call(
        functools.partial(_hgrn_kernel, n=n, chunk=chunk, unroll=unroll, hs=hs,
                          has_state=has_state, emit_state=emit_state),
        out_shape=tuple(out_shape),
        grid=(n_seq, heads // hs),
        in_specs=in_specs,
        out_specs=tuple(out_specs),
        scratch_shapes=[pltpu.VMEM((n, hw), F32), pltpu.VMEM((n, hw), F32),
                        pltpu.VMEM((2 * hs, HEAD, HEAD), F32)],
        input_output_aliases=aliases,
        compiler_params=_cparams("parallel", "parallel"),
        name="hgrn2_scan",
    )(*args)
    return res if emit_state else (res[0], None)


def _swa_ctx_kernel(q_ref, k_ref, v_ref, sink_ref, mix_ref, o_ref, *, layer, n, kvh):
    del mix_ref
    lane = lax.broadcasted_iota(jnp.int32, (1, SW_GROUP * n), 1)
    units = [(slice(sq * n, (sq + 1) * n), kv) for sq in range(q_ref.shape[0] // n) for kv in range(kvh)]

    def fold(x, op):
        return op(op(x.reshape(x.shape[0] // 8, 8, x.shape[1]), axis=0), axis=0, keepdims=True)

    sinks = []
    for kv in range(kvh):
        sink = jnp.zeros((1, SW_GROUP * n), F32)
        for g in range(SW_GROUP):
            sink = jnp.where(lane // n == g, sink_ref[layer, kv * SW_GROUP + g], sink)
        sinks.append(sink)
    scores = []
    for rows, kv in units:
        q_all = jnp.concatenate(
            [(q_ref[rows, (kv * SW_GROUP + g) * HEAD:(kv * SW_GROUP + g + 1) * HEAD] * (HEAD ** -0.5)).astype(BF16)
             for g in range(SW_GROUP)], axis=0)
        scores.append(_dot_t(k_ref[rows, kv * HEAD:(kv + 1) * HEAD].astype(BF16), q_all))
    probs = []
    for s, (_, kv) in zip(scores, units):
        m = jnp.maximum(fold(s, jnp.max), sinks[kv])
        p = jnp.exp(s - m)
        inv = 1.0 / (fold(p, jnp.sum) + jnp.exp(sinks[kv] - m))
        probs.append((p * inv).astype(BF16))
    outs = [_dot(v_ref[rows, kv * HEAD:(kv + 1) * HEAD].T.astype(BF16), pb) for (rows, kv), pb in zip(units, probs)]
    for (rows, kv), ot in zip(units, outs):
        for g in range(SW_GROUP):
            hd = kv * SW_GROUP + g
            o_ref[rows, hd * HEAD:(hd + 1) * HEAD] = ot[:, g * n:(g + 1) * n].T.astype(BF16)


def _swa_ctx(p, mix, geom, layer, sink):
    n_seq, n, t = geom["n_seq"], geom["n"], geom["t"]
    kvh = geom["sw_kv"]
    qw = kvh * SW_GROUP * HEAD
    kw = kvh * HEAD
    q_blk = geom["cq_off"] // qw
    k_blk = geom["ck_off"] // kw
    mix_blk = (geom["da_heads"] + geom["hg_heads"]) * HEAD // qw
    assert geom["cq_off"] % qw == 0 and geom["ck_off"] % kw == 0
    rows = math.gcd(n_seq, CTX_SEQ_PER_STEP) * n
    return pl.pallas_call(
        functools.partial(_swa_ctx_kernel, layer=layer, n=n, kvh=kvh),
        out_shape=jax.ShapeDtypeStruct(mix.shape, mix.dtype),
        grid=(t // rows,),
        in_specs=[pl.BlockSpec((rows, qw), lambda b: (b, q_blk)),
                  pl.BlockSpec((rows, kw), lambda b: (b, k_blk)),
                  pl.BlockSpec((rows, kw), lambda b: (b, k_blk + 1)),
                  pl.BlockSpec(memory_space=pltpu.SMEM),
                  pl.BlockSpec(memory_space=pl.ANY)],
        out_specs=pl.BlockSpec((rows, qw), lambda b: (b, mix_blk)),
        input_output_aliases={4: 0},
        compiler_params=_cparams("parallel"),
        name="sink_attention",
    )(p, p, p, sink, mix)


def _band_masks(n):
    r = np.arange(3 * SW_BLOCK)[:, None]
    i = (np.arange(SW_GROUP * SW_BLOCK) % SW_BLOCK)[None, :]
    window = np.abs(SW_BLOCK + i - r) <= SW_WINDOW
    not_before = r >= SW_BLOCK
    not_after = r < 2 * SW_BLOCK
    variants = [window, window & not_before, window & not_after, window & not_before & not_after]
    return jnp.asarray(np.stack(variants).astype(np.float32))


def _swa_lat_kernel(q_ref, k_ref, v_ref, qc_ref, qsa_ref, qsb_ref, kc_ref, ksa_ref, ksb_ref, ck_ref, cv_ref,
                    mask_ref, sink_ref, mix_ref, o_ref, kr_ref, vt_ref, kctx_ref, vctxt_ref, *, layer, n):
    kv = pl.program_id(1)
    step = pl.program_id(2)
    nb = n // SW_BLOCK
    w = HEAD // 4
    band = 3 * SW_BLOCK
    per_step = q_ref.shape[0] // SW_BLOCK

    @pl.when(step == 0)
    def _():
        zeros = jnp.zeros((SW_BLOCK, HEAD), BF16)
        for j in (0, nb + 1):
            kr_ref[j] = zeros
            vt_ref[j] = zeros

        def fill(j, carry):
            rows = pl.ds(pl.multiple_of(j * SW_BLOCK, SW_BLOCK), SW_BLOCK)
            k = _rope(k_ref[rows, :], kc_ref[rows, :], ksa_ref[rows, :], ksb_ref[rows, :], w)
            kr_ref[j + 1] = k.astype(BF16)
            vt_ref[j + 1] = v_ref[rows, :].T.astype(BF16)
            return carry

        lax.fori_loop(0, nb, fill, 0)
        kctx_ref[...] = ck_ref[...].astype(BF16)
        vctxt_ref[...] = cv_ref[...].T.astype(BF16)

    lane = lax.broadcasted_iota(jnp.int32, (1, SW_GROUP * SW_BLOCK), 1)
    sink = jnp.zeros((1, SW_GROUP * SW_BLOCK), F32)
    for g in range(SW_GROUP):
        sink = jnp.where(lane // SW_BLOCK == g, sink_ref[layer, kv * SW_GROUP + g], sink)

    def fold(x, op):
        return op(op(x.reshape(x.shape[0] // 8, 8, x.shape[1]), axis=0), axis=0, keepdims=True)

    blocks = [step * per_step + c for c in range(per_step)]
    rows = [slice(c * SW_BLOCK, (c + 1) * SW_BLOCK) for c in range(per_step)]
    scores = []
    for qb, r in zip(blocks, rows):
        qc, qsa, qsb = qc_ref[r, :], qsa_ref[r, :], qsb_ref[r, :]
        q_all = jnp.concatenate(
            [(_rope(q_ref[r, g * HEAD:(g + 1) * HEAD], qc, qsa, qsb, w) * (HEAD ** -0.5)).astype(BF16)
             for g in range(SW_GROUP)], axis=0)
        k_all = jnp.concatenate([kr_ref[qb], kr_ref[qb + 1], kr_ref[qb + 2], kctx_ref[...]], axis=0)
        scores.append(_dot_t(k_all, q_all))
    probs = []
    for qb, s in zip(blocks, scores):
        variant = jnp.where(qb == 0, 1, 0) + jnp.where(qb == nb - 1, 2, 0)
        s_band = jnp.where(mask_ref[variant] > 0.5, s[:band], NEG_INF)
        s_ctx = s[band:]
        m = jnp.maximum(jnp.maximum(fold(s_band, jnp.max), fold(s_ctx, jnp.max)), sink)
        p_band = jnp.exp(s_band - m)
        p_ctx = jnp.exp(s_ctx - m)
        inv = 1.0 / (fold(p_band, jnp.sum) + fold(p_ctx, jnp.sum) + jnp.exp(sink - m))
        probs.append(jnp.concatenate([p_band * inv, p_ctx * inv], axis=0).astype(BF16))
    outs = []
    for qb, p_all in zip(blocks, probs):
        vt_all = jnp.concatenate([vt_ref[qb], vt_ref[qb + 1], vt_ref[qb + 2], vctxt_ref[...]], axis=1)
        outs.append(_dot(vt_all, p_all))
    for r, ot in zip(rows, outs):
        for g in range(SW_GROUP):
            o_ref[r, g * HEAD:(g + 1) * HEAD] = ot[:, g * SW_BLOCK:(g + 1) * SW_BLOCK].T.astype(BF16)


def _swa_lat(p, mix, geom, layer, sink, rope_tabs, cache_k, cache_v):
    n_seq, n, t = geom["n_seq"], geom["n"], geom["t"]
    kvh = geom["sw_kv"]
    qw = SW_GROUP * HEAD
    q_blk = geom["cq_off"] // qw
    k_blk = geom["ck_off"] // HEAD
    v_blk = k_blk + kvh
    mix_blk = (geom["da_heads"] + geom["hg_heads"]) * HEAD // qw
    nb = n // SW_BLOCK
    per_step = SW_PER_STEP if nb % SW_PER_STEP == 0 else 1
    nsteps = nb // per_step
    tq = per_step * SW_BLOCK
    n_ctx = cache_k.shape[3]
    masks = _band_masks(n)
    cspec = pl.BlockSpec((None, None, None, n_ctx, HEAD), lambda b, kv, i: (b, layer, kv, 0, 0))
    return pl.pallas_call(
        functools.partial(_swa_lat_kernel, layer=layer, n=n),
        out_shape=jax.ShapeDtypeStruct(mix.shape, mix.dtype),
        grid=(n_seq, kvh, nsteps),
        in_specs=[pl.BlockSpec((tq, qw), lambda b, kv, i: (b * nsteps + i, q_blk + kv)),
                  pl.BlockSpec((n, HEAD), lambda b, kv, i: (b, k_blk + kv)),
                  pl.BlockSpec((n, HEAD), lambda b, kv, i: (b, v_blk + kv))]
                 + [pl.BlockSpec((tq, HEAD), lambda b, kv, i: (i, 0))] * 3
                 + [pl.BlockSpec((n, HEAD), lambda b, kv, i: (0, 0))] * 3
                 + [cspec, cspec, pl.BlockSpec(masks.shape, lambda b, kv, i: (0, 0, 0)),
                    pl.BlockSpec(memory_space=pltpu.SMEM), pl.BlockSpec(memory_space=pl.ANY)],
        out_specs=pl.BlockSpec((tq, qw), lambda b, kv, i: (b * nsteps + i, mix_blk + kv)),
        scratch_shapes=[pltpu.VMEM((nb + 2, SW_BLOCK, HEAD), BF16), pltpu.VMEM((nb + 2, HEAD, SW_BLOCK), BF16),
                        pltpu.VMEM((n_ctx, HEAD), BF16), pltpu.VMEM((HEAD, n_ctx), BF16)],
        input_output_aliases={13: 0},
        compiler_params=_cparams("parallel", "parallel", "arbitrary"),
        name="banded_sink_attention",
    )(p, p, p, *rope_tabs, *rope_tabs, cache_k, cache_v, masks, sink, mix)


def _rope_tables(n, half):
    h = half // 2
    pos = jnp.arange(n)
    inv = ROPE_BASE ** (-jnp.arange(h, dtype=F32) / h)
    zero = jnp.zeros((n, h), F32)
    c, sa, sb = [], [], []
    for axis_pos in (pos // GRID_W, pos % GRID_W):
        ang = axis_pos.astype(F32)[:, None] * inv[None, :]
        cos, sin = jnp.cos(ang), jnp.sin(ang)
        c += [cos, cos]
        sa += [-sin, zero]
        sb += [zero, sin]
    reps = HEAD // (2 * half)
    cat = lambda parts: jnp.tile(jnp.concatenate(parts, axis=1), (1, reps))
    return cat(c), cat(sa), cat(sb)


def _geometry(n_seq, n, d_model):
    da_heads = hg_heads = d_model // 512
    sw_heads = d_model // 256
    sw_kv = sw_heads // SW_GROUP
    cq_off = (3 * da_heads + 5 * hg_heads) * HEAD
    return dict(n_seq=n_seq, n=n, t=n_seq * n, da_heads=da_heads, hg_heads=hg_heads, sw_kv=sw_kv,
                cq_off=cq_off, ck_off=cq_off + sw_heads * HEAD, d_mix=(da_heads + hg_heads + sw_heads) * HEAD)


def _token_tiles(n):
    return _pick_tile(n, TM_PROJ), _pick_tile(n, TM_OUT), _pick_tile(n, TM_FFN)


def _project(x, geom, layer, mod, mod_row, w_in, cast_src, new_caches):
    if new_caches is None:
        tm = _token_tiles(geom["n"])[0]
        return _in_proj(x, mod, mod_row, w_in, layer, tm, geom["d_mix"], cast_src)
    heads, kvh = geom["da_heads"], geom["sw_kv"]
    cols = ((heads * HEAD, heads), (2 * heads * HEAD, heads), (geom["ck_off"], kvh),
            (geom["ck_off"] + kvh * HEAD, kvh))
    return _in_proj(x, mod, mod_row, w_in, layer, geom["n"], geom["d_mix"], cast_src, new_caches, cols)


def _mix_and_ffn(x, p, mix, geom, layer, mod, mod_row, wts, params, lb_params, scan_consts, tabs, caches,
                 new_state, alpha, cast_src):
    w_out, w_gu, w_down = wts
    _, tm_out, tm_ffn = _token_tiles(geom["t"] if caches is None else geom["n"])
    if caches is None:
        mix = _diff_attn(p, mix, geom, layer, params["diff_lambda"], params["diff_norm_g"], None, None, None)
        mix, new_state = _hgrn(p, mix, geom, layer, lb_params, params["hgrn_norm_g"], scan_consts, None, new_state)
        mix = _swa_ctx(p, mix, geom, layer, params["swa_sink"])
    else:
        ck_d, cv_d, ck_s, cv_s, state = caches
        mix = _diff_attn(p, mix, geom, layer, params["diff_lambda"], params["diff_norm_g"], tabs[0], ck_d, cv_d)
        mix, _ = _hgrn(p, mix, geom, layer, lb_params, params["hgrn_norm_g"], scan_consts, state)
        mix = _swa_lat(p, mix, geom, layer, params["swa_sink"], tabs[1], ck_s, cv_s)
    x1 = _out_proj(mix, w_out, x, mod, mod_row, params["ln1_g"], params["ln1_b"], layer, tm_out, alpha)
    y, next_w = _ffn(x1, w_gu, w_down, mod, mod_row, params["ln2_g"], params["ln2_b"], layer, tm_ffn, alpha, cast_src)
    return y, new_state, next_w


def kernel(x_prompt, x_sample, cache_diff_k, cache_diff_v, cache_swa_k, cache_swa_v, state_hgrn, c, c_ctx, w_mod,
           b_mod, w_in, w_out, diff_lambda, diff_norm_g, hgrn_lb_logits, hgrn_norm_g, swa_sink, ln1_g, ln1_b, ln2_g,
           ln2_b, w_gate_up, w_down):
    batch, seq, d = x_prompt.shape
    dec_batch, dec_seq, _ = x_sample.shape
    depth = w_mod.shape[0]
    alpha = (2 * depth) ** 0.25
    geom_c = _geometry(batch, seq, d)
    geom_l = _geometry(dec_batch, dec_seq, d)
    assert 1 + dec_batch <= MOD_ROWS

    cond = jnp.zeros((MOD_ROWS, d), F32).at[0].set(c_ctx).at[1:1 + dec_batch].set(c)
    mod = _modulation(cond, w_mod, b_mod).reshape(depth * MOD_ROWS * 6, 1, d)
    lb_params = _lb_params(hgrn_lb_logits)
    params = dict(diff_lambda=diff_lambda, diff_norm_g=diff_norm_g, hgrn_norm_g=hgrn_norm_g, swa_sink=swa_sink,
                  ln1_g=ln1_g, ln1_b=ln1_b, ln2_g=ln2_g, ln2_b=ln2_b)
    w_in16, w_out16 = w_in[0].astype(BF16), w_out[0].astype(BF16)
    tabs = (_rope_tables(dec_seq, DA_QK // 2), _rope_tables(dec_seq, HEAD // 2))
    consts_c = _scan_constants(min(HG_CHUNK, seq))
    consts_l = _scan_constants(min(HG_CHUNK, dec_seq))

    y_p = x_prompt.reshape(batch * seq, d)
    y_s = x_sample.reshape(dec_batch * dec_seq, d)
    heads, kvh = geom_c["da_heads"], geom_c["sw_kv"]
    new_kv = tuple(jnp.zeros((batch, depth, nh, seq, HEAD), F32) for nh in (heads, heads, kvh, kvh))
    new_state = jnp.zeros((batch, depth, 2, geom_c["hg_heads"], HEAD, HEAD), F32)
    caches = (cache_diff_k, cache_diff_v, cache_swa_k, cache_swa_v, state_hgrn)
    for l in range(depth):
        row_c = lambda m, tm, j, l=l: (l * MOD_ROWS) * 6 + j
        row_l = lambda m, tm, j, l=l: (l * MOD_ROWS + 1 + (m * tm) // dec_seq) * 6 + j
        p_c, mix_c, w_down16, new_kv = _project(y_p, geom_c, l, mod, row_c, w_in16, w_down, new_kv)
        p_l, mix_l, w_gu16, _ = _project(y_s, geom_l, l, mod, row_l, w_in16, w_gate_up, None)
        wts = (w_out16, w_gu16, w_down16)
        more = l + 1 < depth
        y_p, new_state, w_out16 = _mix_and_ffn(y_p, p_c, mix_c, geom_c, l, mod, row_c, wts, params, lb_params,
                                               consts_c, None, None, new_state, alpha, w_out if more else None)
        y_s, _, w_in16 = _mix_and_ffn(y_s, p_l, mix_l, geom_l, l, mod, row_l, wts, params, lb_params, consts_l,
                                      tabs, caches, None, alpha, w_in if more else None)
    return (y_p.reshape(batch, seq, d), y_s.reshape(dec_batch, dec_seq, d)) + new_kv + (new_state,)
```

```python
import functools
import math

import numpy as np
import jax
import jax.numpy as jnp
from jax import lax
from jax.experimental import pallas as pl
from jax.experimental.pallas import tpu as pltpu

F32 = jnp.float32
BF16 = jnp.bfloat16

GRID_W = 64
ROPE_BASE = 10000.0
LN_EPS = 1e-5
RMS_EPS = 1e-6
NEG_INF = -1e30
LB_FLOOR = 1e-30
LOG2_E = math.log2(math.e)
HEAD = 128
DA_QK = 64
SW_GROUP = 4
SW_BLOCK = 128
SW_WINDOW = 128
MOD_ROWS = 8

VMEM_LIMIT = 56 * 1024 * 1024
TM_PROJ = 256
TM_OUT = 512
TM_FFN = 512
TF_FFN = 512
TN_MOD = 1024
DA_TQ = 256
DA_KB = 256
HG_CHUNK = 64
HG_UNROLL = 8
SW_PER_STEP = 8
CTX_SEQ_PER_STEP = 4


def _cparams(*sem):
    return pltpu.CompilerParams(dimension_semantics=sem, vmem_limit_bytes=VMEM_LIMIT)


def _dot(a, b):
    return jnp.dot(a, b, preferred_element_type=F32)


def _dot_t(a, b):
    return lax.dot_general(a, b, (((1,), (1,)), ((), ())), preferred_element_type=F32)


def _silu(x):
    return x / (1.0 + jnp.exp2(x * (-LOG2_E)))


def _layer_norm(y, g, b):
    mu = jnp.mean(y, axis=-1, keepdims=True)
    d = y - mu
    var = jnp.mean(d * d, axis=-1, keepdims=True)
    return d * lax.rsqrt(var + LN_EPS) * g + b


def _rms_norm(o, g):
    ms = jnp.mean(o * o, axis=-1, keepdims=True)
    return o * lax.rsqrt(ms + RMS_EPS) * g


def _rope(x, c, sa, sb, w):
    return x * c + pltpu.roll(x, HEAD - w, 1) * sa + pltpu.roll(x, w, 1) * sb


def _pick_tile(n, target):
    t = min(n, target)
    while n % t or t % 128:
        t -= 128
    return t


def _lb_kernel(logit_ref, loglb_ref, log1m_ref, onem_ref):
    depth = logit_ref.shape[0]
    x = [logit_ref[l] for l in range(depth)]
    m = functools.reduce(jnp.maximum, x)
    e = [jnp.exp(xi - m) for xi in x]
    tot = functools.reduce(lambda a, b: a + b, e)
    w = [ei / tot for ei in e]
    acc = jnp.zeros_like(w[0])
    for l in range(depth):
        acc = acc + w[l]
        lb = acc - w[0]
        loglb_ref[l] = jnp.log(jnp.maximum(lb, LB_FLOOR))
        log1m_ref[l] = jnp.log1p(-lb)
        onem_ref[l] = 1.0 - lb


def _lb_params(logits):
    shp = jax.ShapeDtypeStruct(logits.shape, F32)
    return pl.pallas_call(_lb_kernel, out_shape=(shp, shp, shp), name="hgrn_lb_params")(logits)


def _mod_kernel(c_ref, w_ref, b_ref, o_ref):
    s = _silu(c_ref[...])
    s_hi = s.astype(BF16)
    s_lo = (s - s_hi.astype(F32)).astype(BF16)
    w = w_ref[...]
    w_hi = w.astype(BF16)
    w_lo = (w - w_hi.astype(F32)).astype(BF16)
    o_ref[...] = _dot(s_hi, w_hi) + _dot(s_lo, w_hi) + _dot(s_hi, w_lo) + b_ref[...]


def _modulation(cond, w_mod, b_mod):
    depth, d, d6 = w_mod.shape
    tn = _pick_tile(d6, TN_MOD)
    return pl.pallas_call(
        _mod_kernel,
        out_shape=jax.ShapeDtypeStruct((depth, MOD_ROWS, d6), F32),
        grid=(depth, d6 // tn),
        in_specs=[pl.BlockSpec((MOD_ROWS, d), lambda l, n: (0, 0)),
                  pl.BlockSpec((None, d, tn), lambda l, n: (l, 0, n)),
                  pl.BlockSpec((None, 1, tn), lambda l, n: (l, 0, n))],
        out_specs=pl.BlockSpec((None, MOD_ROWS, tn), lambda l, n: (l, 0, n)),
        compiler_params=_cparams("parallel", "parallel"),
        name="adaln_modulation",
    )(cond, w_mod, b_mod.reshape(depth, 1, d6))


def _proj_kernel(x_ref, sh_ref, sc_ref, w_ref, *rest, cache_cols, n_cast):
    n = len(cache_cols)
    cast_in, rest = rest[:n_cast], rest[n_cast + n:]
    o_ref, mix_ref = rest[:2]
    cast_out, cache_out = rest[2:2 + n_cast], rest[2 + n_cast:]
    h = (x_ref[...] * (1.0 + sc_ref[0]) + sh_ref[0]).astype(BF16)
    o_ref[...] = _dot(h, w_ref[...])
    mix_ref[...] = jnp.zeros_like(mix_ref)
    for src_ref, dst_ref in zip(cast_in, cast_out):
        dst_ref[...] = src_ref[...].astype(BF16)
    for c_ref, (off, nh) in zip(cache_out, cache_cols):
        for hd in range(nh):
            c_ref[hd] = o_ref[:, off + hd * HEAD:off + (hd + 1) * HEAD]


def _in_proj(x, mod, mod_row, w_in, layer, tm, d_mix, cast_srcs, caches=None, cache_cols=()):
    t, d = x.shape
    d_in = w_in.shape[1]
    steps = t // tm
    caches = tuple(caches or ())
    n_cast = len(cast_srcs)
    out_shape = [jax.ShapeDtypeStruct((t, d_in), F32), jax.ShapeDtypeStruct((t, d_mix), BF16)]
    out_specs = [pl.BlockSpec((tm, d_in), lambda m: (m, 0)), pl.BlockSpec((tm, d_mix), lambda m: (m, 0))]
    cast_specs = []
    for src, src_layer in cast_srcs:
        _, rows, width = src.shape
        slab = rows // steps
        assert rows % steps == 0 and slab % 16 == 0
        cast_specs.append(pl.BlockSpec((None, slab, width), lambda m, src_layer=src_layer: (src_layer, m, 0)))
        out_shape.append(jax.ShapeDtypeStruct((rows, width), BF16))
        out_specs.append(pl.BlockSpec((slab, width), lambda m: (m, 0)))
    for c in caches:
        assert c.shape[3] == tm
        out_shape.append(jax.ShapeDtypeStruct(c.shape, c.dtype))
        out_specs.append(pl.BlockSpec((None, None, c.shape[2], tm, HEAD), lambda m: (m, layer, 0, 0, 0)))
    res = pl.pallas_call(
        functools.partial(_proj_kernel, cache_cols=tuple(cache_cols), n_cast=n_cast),
        out_shape=tuple(out_shape),
        grid=(steps,),
        in_specs=[pl.BlockSpec((tm, d), lambda m: (m, 0)),
                  pl.BlockSpec((1, 1, d), lambda m: (mod_row(m, tm, 0), 0, 0)),
                  pl.BlockSpec((1, 1, d), lambda m: (mod_row(m, tm, 1), 0, 0)),
                  pl.BlockSpec((d, d_in), lambda m: (0, 0), pipeline_mode=pl.Buffered(1))]
                 + cast_specs + [pl.BlockSpec(memory_space=pl.ANY)] * len(caches),
        out_specs=tuple(out_specs),
        input_output_aliases={4 + n_cast + i: 2 + n_cast + i for i in range(len(caches))},
        compiler_params=_cparams("parallel"),
        name="in_proj",
    )(x, mod, mod, w_in, *[src for src, _ in cast_srcs], *caches)
    return res[0], res[1], tuple(res[2:2 + n_cast]), tuple(res[2 + n_cast:])


def _out_proj_kernel(mix_ref, w_ref, x_ref, g1_ref, lng_ref, lnb_ref, x1_ref, *, alpha):
    mix = _dot(mix_ref[...], w_ref[...])
    x1_ref[...] = _layer_norm(alpha * x_ref[...] + g1_ref[0] * mix, lng_ref[...], lnb_ref[...])


def _out_proj(mix, w_out, x, mod, mod_row, ln_g, ln_b, layer, tm, alpha):
    t, d = x.shape
    d_mix = mix.shape[1]
    depth = ln_g.shape[0]
    vec = pl.BlockSpec((None, 1, d), lambda m: (layer, 0, 0))
    return pl.pallas_call(
        functools.partial(_out_proj_kernel, alpha=alpha),
        out_shape=jax.ShapeDtypeStruct((t, d), F32),
        grid=(t // tm,),
        in_specs=[pl.BlockSpec((tm, d_mix), lambda m: (m, 0)),
                  pl.BlockSpec((d_mix, d), lambda m: (0, 0), pipeline_mode=pl.Buffered(1)),
                  pl.BlockSpec((tm, d), lambda m: (m, 0)),
                  pl.BlockSpec((1, 1, d), lambda m: (mod_row(m, tm, 2), 0, 0)),
                  vec, vec],
        out_specs=pl.BlockSpec((tm, d), lambda m: (m, 0)),
        compiler_params=_cparams("parallel"),
        name="out_proj_ln",
    )(mix, w_out, x, mod, ln_g.reshape(depth, 1, d), ln_b.reshape(depth, 1, d))


def _ffn_kernel(x_ref, sh2_ref, sc2_ref, wg_ref, wu_ref, wd_ref, g2_ref, lng_ref, lnb_ref, o_ref, h_ref, *, alpha):
    f = pl.program_id(1)

    @pl.when(f == 0)
    def _():
        h_ref[...] = (x_ref[...] * (1.0 + sc2_ref[0]) + sh2_ref[0]).astype(BF16)
        o_ref[...] = jnp.zeros_like(o_ref)

    h = h_ref[...]
    a = _dot(h, wg_ref[...])
    u = _dot(h, wu_ref[...])
    o_ref[...] += _dot((_silu(a) * u).astype(BF16), wd_ref[...])

    @pl.when(f == pl.num_programs(1) - 1)
    def _():
        y = alpha * x_ref[...] + g2_ref[0] * o_ref[...]
        o_ref[...] = _layer_norm(y, lng_ref[...], lnb_ref[...])


def _ffn(x1, w_gu, w_down, mod, mod_row, ln_g, ln_b, layer, tm, alpha):
    t, d = x1.shape
    d_ff = w_down.shape[0]
    depth = ln_g.shape[0]
    tf = _pick_tile(d_ff, TF_FFN)
    nf = d_ff // tf
    vec = pl.BlockSpec((None, 1, d), lambda m, f: (layer, 0, 0))
    row = lambda j: pl.BlockSpec((1, 1, d), lambda m, f: (mod_row(m, tm, j), 0, 0))
    return pl.pallas_call(
        functools.partial(_ffn_kernel, alpha=alpha),
        out_shape=jax.ShapeDtypeStruct((t, d), F32),
        grid=(t // tm, nf),
        in_specs=[pl.BlockSpec((tm, d), lambda m, f: (m, 0)),
                  row(3), row(4),
                  pl.BlockSpec((d, tf), lambda m, f: (0, f)),
                  pl.BlockSpec((d, tf), lambda m, f: (0, nf + f)),
                  pl.BlockSpec((tf, d), lambda m, f: (f, 0)),
                  row(5), vec, vec],
        out_specs=pl.BlockSpec((tm, d), lambda m, f: (m, 0)),
        scratch_shapes=[pltpu.VMEM((tm, d), BF16)],
        compiler_params=_cparams("parallel", "arbitrary"),
        name="ffn_ln",
    )(x1, mod, mod, w_gu, w_gu, w_down, mod, ln_g.reshape(depth, 1, d), ln_b.reshape(depth, 1, d))


def _diff_attn_kernel(*refs, rope, cached, n_self, kb, lam_init, pipelined, n_tiles):
    it = iter(refs)
    q_ref, k_ref, v_ref = next(it), next(it), next(it)
    if rope:
        qc_ref, qsa_ref, qsb_ref, kc_ref, ksa_ref, ksb_ref = (next(it) for _ in range(6))
    if cached:
        ck_ref, cv_ref = next(it), next(it)
    lam_ref, g_ref, _, o_ref, kr_ref, vt_ref, m_ref, acc_ref = (next(it) for _ in range(8))
    s_refs = tuple(it)
    nblk = kr_ref.shape[0]
    nself = n_self // kb
    tq = q_ref.shape[0]
    step = pl.program_id(2)
    last = pl.num_programs(2) - 1

    def prepare_keys():
        for j in range(nself):
            rows = slice(j * kb, (j + 1) * kb)
            k = k_ref[rows, :]
            if rope:
                k = _rope(k, kc_ref[rows, :], ksa_ref[rows, :], ksb_ref[rows, :], DA_QK // 4)
            kr_ref[j] = k.astype(BF16)
            vt_ref[j] = v_ref[rows, :].T.astype(BF16)
        if cached:
            for j in range(nblk - nself):
                rows = slice(j * kb, (j + 1) * kb)
                kr_ref[nself + j] = ck_ref[rows, :].astype(BF16)
                vt_ref[nself + j] = cv_ref[rows, :].T.astype(BF16)

    def queries():
        q = q_ref[...]
        if rope:
            q = _rope(q, qc_ref[...], qsa_ref[...], qsb_ref[...], DA_QK // 4)
        q = q * (DA_QK ** -0.5 * LOG2_E)
        lane = lax.broadcasted_iota(jnp.int32, q.shape, 1)
        return (jnp.where(lane < DA_QK, q, 0.0).astype(BF16), jnp.where(lane >= DA_QK, q, 0.0).astype(BF16))

    def fold(x, op):
        return op(x.reshape(kb // 8, 8, tq), axis=0)

    def scores_block(j, qz, buf, ms):
        kblk = kr_ref[j]
        out = []
        for mp in range(2):
            s = _dot_t(kblk, qz[mp])
            buf[mp, j] = s
            out.append(jnp.maximum(ms[mp], fold(s, jnp.max)))
        return tuple(out)

    def values_block(j, buf, ms, ls):
        out = []
        for mp in range(2):
            p = jnp.exp2(buf[mp, j] - ms[mp])
            acc_ref[mp] += _dot(vt_ref[j], p.astype(BF16))
            out.append(ls[mp] + fold(p, jnp.sum))
        return tuple(out)

    neg = jnp.full((8, tq), -jnp.inf, F32)
    zero = jnp.zeros((8, tq), F32)

    def save_max(ms):
        for mp in range(2):
            m_ref[mp] = jnp.max(ms[mp], axis=0, keepdims=True)

    def finish(ls):
        l1, l2 = (jnp.sum(l, axis=0, keepdims=True) for l in ls)
        lp = lam_ref[...]
        lam = (jnp.exp(jnp.sum(lp[0:1] * lp[1:2], axis=-1, keepdims=True))
               - jnp.exp(jnp.sum(lp[2:3] * lp[3:4], axis=-1, keepdims=True)) + lam_init)
        ot = acc_ref[0] * (1.0 / l1) - acc_ref[1] * (lam / l2)
        ms_o = jnp.mean(ot * ot, axis=0, keepdims=True)
        ot = ot * lax.rsqrt(ms_o + RMS_EPS) * g_ref[...] * (1.0 - lam_init)
        o_ref[...] = ot.T.astype(BF16)

    if not pipelined:
        pl.when(step == 0)(prepare_keys)
        qz = queries()
        ms = (neg, neg)
        for j in range(nblk):
            ms = scores_block(j, qz, s_refs[0], ms)
        ms = tuple(jnp.max(m, axis=0, keepdims=True) for m in ms)
        acc_ref[...] = jnp.zeros_like(acc_ref)
        ls = (zero, zero)
        for j in range(nblk):
            ls = values_block(j, s_refs[0], ms, ls)
        finish(ls)
        return

    @pl.when(step == 0)
    def _():
        prepare_keys()
        qz = queries()
        ms = (neg, neg)
        for j in range(nblk):
            ms = scores_block(j, qz, s_refs[0], ms)
        save_max(ms)

    def interior(parity):
        prev = (m_ref[0], m_ref[1])
        qz = queries()
        acc_ref[...] = jnp.zeros_like(acc_ref)
        ms, ls = (neg, neg), (zero, zero)
        for j in range(nblk):
            ms = scores_block(j, qz, s_refs[parity], ms)
            ls = values_block(j, s_refs[1 - parity], prev, ls)
        save_max(ms)
        finish(ls)

    inside = (step > 0) & (step < last)
    pl.when(inside & (step % 2 == 1))(functools.partial(interior, 1))
    pl.when(inside & (step % 2 == 0))(functools.partial(interior, 0))

    @pl.when(step == last)
    def _():
        prev = (m_ref[0], m_ref[1])
        acc_ref[...] = jnp.zeros_like(acc_ref)
        ls = (zero, zero)
        for j in range(nblk):
            ls = values_block(j, s_refs[(n_tiles - 1) % 2], prev, ls)
        finish(ls)


def _diff_attn_seq_kernel(q_ref, k_ref, v_ref, lam_ref, g_ref, mix_ref, o_ref, *, n, heads, lam_init):
    del mix_ref
    lane = lax.broadcasted_iota(jnp.int32, (n, HEAD), 1)
    units = [(slice(sq * n, (sq + 1) * n), slice(hd * HEAD, (hd + 1) * HEAD))
             for sq in range(q_ref.shape[0] // n) for hd in range(heads)]
    scores, vts = [], []
    for rows, cols in units:
        q = q_ref[rows, cols] * (DA_QK ** -0.5 * LOG2_E)
        k = k_ref[rows, cols].astype(BF16)
        scores.append([_dot_t(k, jnp.where(lane < DA_QK, q, 0.0).astype(BF16)),
                       _dot_t(k, jnp.where(lane >= DA_QK, q, 0.0).astype(BF16))])
        vts.append(v_ref[rows, cols].T.astype(BF16))
    probs = []
    for pair in scores:
        parts = []
        for s in pair:
            p = jnp.exp2(s - jnp.max(s, axis=0, keepdims=True))
            parts.append((p.astype(BF16), jnp.sum(p, axis=0, keepdims=True)))
        probs.append(parts)
    accs = [[_dot(vt, pb) for pb, _ in parts] for vt, parts in zip(vts, probs)]
    lp = lam_ref[...]
    lam = (jnp.exp(jnp.sum(lp[0:1] * lp[1:2], axis=-1, keepdims=True))
           - jnp.exp(jnp.sum(lp[2:3] * lp[3:4], axis=-1, keepdims=True)) + lam_init)
    for (rows, cols), ((_, l1), (_, l2)), acc in zip(units, probs, accs):
        ot = acc[0] * (1.0 / l1) - acc[1] * (lam / l2)
        ms_o = jnp.mean(ot * ot, axis=0, keepdims=True)
        ot = ot * lax.rsqrt(ms_o + RMS_EPS) * g_ref[...] * (1.0 - lam_init)
        o_ref[rows, cols] = ot.T.astype(BF16)


def _diff_attn_seq(p, mix, geom, layer, diff_lambda, diff_g):
    n_seq, n = geom["n_seq"], geom["n"]
    heads = geom["da_heads"]
    depth = diff_g.shape[0]
    lam_init = 0.8 - 0.6 * math.exp(-0.3 * layer)
    per_step = math.gcd(n_seq, CTX_SEQ_PER_STEP)
    blk = lambda j: pl.BlockSpec((per_step * n, heads * HEAD), lambda b: (b, j))
    return pl.pallas_call(
        functools.partial(_diff_attn_seq_kernel, n=n, heads=heads, lam_init=lam_init),
        out_shape=jax.ShapeDtypeStruct(mix.shape, mix.dtype),
        grid=(n_seq // per_step,),
        in_specs=[blk(0), blk(1), blk(2),
                  pl.BlockSpec((None, 4, DA_QK), lambda b: (layer, 0, 0)),
                  pl.BlockSpec((None, HEAD, 1), lambda b: (layer, 0, 0)),
                  pl.BlockSpec(memory_space=pl.ANY)],
        out_specs=blk(0),
        input_output_aliases={5: 0},
        compiler_params=_cparams("parallel"),
        name="diff_attention_seq",
    )(p, p, p, diff_lambda, diff_g.reshape(depth, HEAD, 1), mix)


def _diff_attn(p, mix, geom, layer, diff_lambda, diff_g, rope_tabs, cache_k, cache_v):
    if rope_tabs is None and cache_k is None and geom["n"] <= DA_TQ:
        return _diff_attn_seq(p, mix, geom, layer, diff_lambda, diff_g)
    n_seq, n, t = geom["n_seq"], geom["n"], geom["t"]
    heads = geom["da_heads"]
    rope = rope_tabs is not None
    cached = cache_k is not None
    tq = min(n, DA_TQ)
    kb = min(n, DA_KB)
    nq = n // tq
    n_ctx = cache_k.shape[3] if cached else 0
    nblk = (n + n_ctx) // kb
    assert n % kb == 0 and n_ctx % kb == 0
    k_off, v_off = heads, 2 * heads
    depth = diff_g.shape[0]
    lam_init = 0.8 - 0.6 * math.exp(-0.3 * layer)

    pipelined = nq > 1
    q_tile = (lambda i: jnp.minimum(i, nq - 1)) if pipelined else (lambda i: i)
    o_tile = (lambda i: jnp.maximum(i - 1, 0)) if pipelined else (lambda i: i)
    in_specs = [pl.BlockSpec((tq, HEAD), lambda b, h, i: (b * nq + q_tile(i), h)),
                pl.BlockSpec((n, HEAD), lambda b, h, i: (b, k_off + h)),
                pl.BlockSpec((n, HEAD), lambda b, h, i: (b, v_off + h))]
    args = [p, p, p]
    if rope:
        in_specs += [pl.BlockSpec((tq, HEAD), lambda b, h, i: (q_tile(i), 0))] * 3
        in_specs += [pl.BlockSpec((n, HEAD), lambda b, h, i: (0, 0))] * 3
        args += list(rope_tabs) * 2
    if cached:
        spec = pl.BlockSpec((None, None, None, n_ctx, HEAD), lambda b, h, i: (b, layer, h, 0, 0))
        in_specs += [spec, spec]
        args += [cache_k, cache_v]
    in_specs += [pl.BlockSpec((None, 4, DA_QK), lambda b, h, i: (layer, 0, 0)),
                 pl.BlockSpec((None, HEAD, 1), lambda b, h, i: (layer, 0, 0)),
                 pl.BlockSpec(memory_space=pl.ANY)]
    args += [diff_lambda, diff_g.reshape(depth, HEAD, 1), mix]
    return pl.pallas_call(
        functools.partial(_diff_attn_kernel, rope=rope, cached=cached, n_self=n, kb=kb, lam_init=lam_init,
                          pipelined=pipelined, n_tiles=nq),
        out_shape=jax.ShapeDtypeStruct(mix.shape, mix.dtype),
        grid=(n_seq, heads, nq + 1 if pipelined else nq),
        in_specs=in_specs,
        input_output_aliases={len(args) - 1: 0},
        out_specs=pl.BlockSpec((tq, HEAD), lambda b, h, i: (b * nq + o_tile(i), h)),
        scratch_shapes=[pltpu.VMEM((nblk, kb, HEAD), BF16), pltpu.VMEM((nblk, HEAD, kb), BF16),
                        pltpu.VMEM((2, 1, tq), F32), pltpu.VMEM((2, HEAD, tq), F32)]
                       + [pltpu.VMEM((2, nblk, kb, tq), F32)] * (2 if pipelined else 1),
        compiler_params=_cparams("parallel", "parallel", "arbitrary"),
        name="diff_attention",
    )(*args)


def _scan_constants(c):
    levels = int(math.log2(c))
    t = np.arange(c)[:, None]
    s = np.arange(c)[None, :]
    tri = (s <= t).astype(np.float32)
    masks = []
    for j in range(levels):
        m = c >> (j + 1)
        base = (t // (2 * m)) * (2 * m)
        sbase = (s // (2 * m)) * (2 * m)
        masks.append((sbase == base) & ((t - base) >= m) & ((s - sbase) < m))
    masks.append(s == t)
    masks = np.stack(masks).astype(np.float32)
    flip = lambda a: a.reshape(-1, c, c)[:, ::-1, ::-1].reshape(a.shape)
    return (jnp.asarray(tri, BF16), jnp.asarray(flip(tri), BF16),
            jnp.asarray(masks, F32), jnp.asarray(flip(masks), F32))


def _level_exponents(cum, d, chunk):
    row = lax.broadcasted_iota(jnp.int32, cum.shape, 0)
    sub = lax.broadcasted_iota(jnp.int32, (8, HEAD), 0)
    out = []
    m = chunk // 2
    while m >= 1:
        in_second = (row & m) != 0
        q_side = in_second if d == 0 else jnp.logical_not(in_second)
        if m >= 8:
            pieces = []
            for b0 in range(0, chunk, 2 * m):
                r = b0 + m - 1 + d
                bnd = jnp.broadcast_to(cum[r:r + 1, :], (m, HEAD))
                first, second = cum[b0:b0 + m, :], cum[b0 + m:b0 + 2 * m, :]
                pieces += [bnd - first, second - bnd] if d == 0 else [first - bnd, bnd - second]
            x = jnp.concatenate(pieces, axis=0)
        elif m == 1:
            neighbour = pltpu.roll(cum, 1 if d == 0 else chunk - 1, 0)
            x = jnp.where(q_side, cum - neighbour, 0.0)
        else:
            tiles = []
            for v in range(chunk // 8):
                if m == 4:
                    r = 8 * v + 3 + d
                    tiles.append(jnp.broadcast_to(cum[r:r + 1, :], (8, HEAD)))
                else:
                    lo = jnp.broadcast_to(cum[8 * v + 1 + d:8 * v + 2 + d, :], (8, HEAD))
                    hi = jnp.broadcast_to(cum[8 * v + 5 + d:8 * v + 6 + d, :], (8, HEAD))
                    tiles.append(jnp.where(sub < 4, lo, hi))
            bnd = jnp.concatenate(tiles, axis=0)
            x = jnp.where(q_side, cum - bnd, bnd - cum)
        out.append(x)
        m //= 2
    return out


def _hgrn_kernel(*refs, n, chunk, unroll, hs, has_state, emit_state):
    it = iter(refs)
    q_ref, zf_ref, zb_ref, v_ref, hg_ref = (next(it) for _ in range(5))
    loglb_ref, log1m_ref, onem_ref, g_ref = (next(it) for _ in range(4))
    tri_refs = (next(it), next(it))
    mask_refs = (next(it), next(it))
    s0_ref = next(it) if has_state else None
    next(it)
    if emit_state:
        next(it)
    o_ref = next(it)
    st_ref = next(it) if emit_state else None
    obuf_refs = (next(it), next(it))
    s_ref = next(it)

    levels = int(math.log2(chunk))
    nchunks = n // chunk
    z_refs = (zf_ref, zb_ref)

    for d in range(2):
        for hd in range(hs):
            s_ref[d * hs + hd] = s0_ref[d, hd].T if has_state else jnp.zeros((HEAD, HEAD), F32)

    def gates(hd, d, start):
        rows = pl.ds(start, chunk)
        cols = slice(hd * HEAD, (hd + 1) * HEAD)
        z = z_refs[d][rows, cols]
        q = _silu(q_ref[rows, cols])
        v = v_ref[rows, cols].astype(BF16)
        z2 = z * LOG2_E
        e = jnp.exp2(-jnp.abs(z2))
        one_e = 1.0 + e
        log_sig = jnp.minimum(z2, 0.0) - jnp.log2(one_e)
        sig_neg = jnp.where(z >= 0.0, e, 1.0) / one_e
        a = loglb_ref[d:d + 1, cols] * LOG2_E
        b = log1m_ref[d:d + 1, cols] * LOG2_E + log_sig
        g = jnp.maximum(a, b) + jnp.log2(1.0 + jnp.exp2(-jnp.abs(a - b)))
        k = onem_ref[d:d + 1, cols] * sig_neg
        g_hi = g.astype(BF16)
        g_lo = (g - g_hi.astype(F32)).astype(BF16)
        cum = _dot(tri_refs[d][...], jnp.concatenate([g_hi, g_lo], axis=1))
        return q, k, v, cum[:, :HEAD] + cum[:, HEAD:]

    def body(i, carry):
        chains = []
        for u in range(unroll):
            c = i * unroll + u
            for hd in range(hs):
                chains.append((hd, 0, pl.multiple_of(c * chunk, chunk)))
                chains.append((hd, 1, pl.multiple_of((nchunks - 1 - c) * chunk, chunk)))
        work = [gates(hd, d, start) for hd, d, start in chains]
        qk16 = [(q.astype(BF16), k.astype(BF16)) for q, k, _, _ in work]
        atts = [mask_refs[d][levels] * _dot_t(qb, kb) for (_, d, _), (qb, kb) in zip(chains, qk16)]
        expo = [_level_exponents(cum, d, chunk) for (_, d, _), (_, _, _, cum) in zip(chains, work)]
        for j in range(levels):
            for ci, ((_, d, _), (qb, kb)) in enumerate(zip(chains, qk16)):
                fac = jnp.exp2(expo[ci][j]).astype(BF16)
                atts[ci] = atts[ci] + mask_refs[d][j] * _dot_t(qb * fac, kb * fac)
        intra, delta, q_in, decay = [], [], [], []
        for (_, d, _), (q, k, v, cum), att in zip(chains, work, atts):
            last = cum[chunk - 1:chunk] if d == 0 else cum[0:1]
            kt = (k * jnp.exp2(last - cum)).astype(BF16)
            intra.append(_dot(att.astype(BF16), v))
            delta.append(lax.dot_general(v, kt, (((0,), (0,)), ((), ())), preferred_element_type=F32))
            q_in.append((q * jnp.exp2(cum)).astype(BF16))
            decay.append(jnp.exp2(last))
        st = [s_ref[j] for j in range(2 * hs)]
        for ci, (hd, d, start) in enumerate(chains):
            j = d * hs + hd
            obuf_refs[d][pl.ds(start, chunk), hd * HEAD:(hd + 1) * HEAD] = (
                intra[ci] + _dot_t(q_in[ci], st[j].astype(BF16)))
            st[j] = decay[ci] * st[j] + delta[ci]
        for j in range(2 * hs):
            s_ref[j] = st[j]
        return carry

    lax.fori_loop(0, nchunks // unroll, body, 0)

    for hd in range(hs):
        cols = slice(hd * HEAD, (hd + 1) * HEAD)
        o = obuf_refs[0][:, cols] + obuf_refs[1][:, cols]
        o_ref[:, cols] = (_rms_norm(o, g_ref[...]) * _silu(hg_ref[:, cols])).astype(BF16)
        if emit_state:
            for d in range(2):
                st_ref[d, hd] = s_ref[d * hs + hd].T


def _hgrn(p, mix, geom, layer, lb_params, hgrn_g, consts, state, new_state=None):
    n_seq, n, t = geom["n_seq"], geom["n"], geom["t"]
    heads = geom["hg_heads"]
    mix_col = geom["da_heads"]
    base = 3 * geom["da_heads"]
    has_state = state is not None
    emit_state = not has_state
    depth = hgrn_g.shape[0]
    chunk = min(HG_CHUNK, n)
    unroll = math.gcd(n // chunk, HG_UNROLL)
    hs = math.gcd(heads, HG_UNROLL // unroll)
    assert base % hs == 0 and mix_col % hs == 0
    hw = hs * HEAD
    col = lambda j: pl.BlockSpec((n, hw), lambda b, h: (b, (base + j * heads) // hs + h))
    lbspec = pl.BlockSpec((None, 2, hw), lambda b, h: (layer, 0, h))
    const_specs = [pl.BlockSpec(c.shape, lambda b, h, nd=c.ndim: (0,) * nd) for c in consts]
    in_specs = [col(0), col(1), col(2), col(3), col(4), lbspec, lbspec, lbspec,
                pl.BlockSpec((None, 1, HEAD), lambda b, h: (layer, 0, 0))] + const_specs
    args = [p] * 5 + list(lb_params) + [hgrn_g.reshape(depth, 1, HEAD)] + list(consts)
    if has_state:
        in_specs.append(pl.BlockSpec((None, None, 2, hs, HEAD, HEAD), lambda b, h: (b, layer, 0, h, 0, 0)))
        args.append(state)
    in_specs.append(pl.BlockSpec(memory_space=pl.ANY))
    args.append(mix)
    aliases = {len(args) - 1: 0}
    out_shape = [jax.ShapeDtypeStruct(mix.shape, mix.dtype)]
    out_specs = [pl.BlockSpec((n, hw), lambda b, h: (b, mix_col // hs + h))]
    if emit_state:
        in_specs.append(pl.BlockSpec(memory_space=pl.ANY))
        args.append(new_state)
        aliases[len(args) - 1] = 1
        out_shape.append(jax.ShapeDtypeStruct(new_state.shape, new_state.dtype))
        out_specs.append(pl.BlockSpec((None, None, 2, hs, HEAD, HEAD), lambda b, h: (b, layer, 0, h, 0, 0)))
    res = pl.pallas_call(
        functools.partial(_hgrn_kernel, n=n, chunk=chunk, unroll=unroll, hs=hs,
                          has_state=has_state, emit_state=emit_state),
        out_shape=tuple(out_shape),
        grid=(n_seq, heads // hs),
        in_specs=in_specs,
        out_specs=tuple(out_specs),
        scratch_shapes=[pltpu.VMEM((n, hw), F32), pltpu.VMEM((n, hw), F32),
                        pltpu.VMEM((2 * hs, HEAD, HEAD), F32)],
        input_output_aliases=aliases,
        compiler_params=_cparams("parallel", "parallel"),
        name="hgrn2_scan",
    )(*args)
    return res if emit_state else (res[0], None)


def _swa_ctx_kernel(q_ref, k_ref, v_ref, sink_ref, mix_ref, o_ref, *, layer, n, kvh):
    del mix_ref
    lane = lax.broadcasted_iota(jnp.int32, (1, SW_GROUP * n), 1)
    units = [(slice(sq * n, (sq + 1) * n), kv) for sq in range(q_ref.shape[0] // n) for kv in range(kvh)]

    def fold(x, op):
        return op(op(x.reshape(x.shape[0] // 8, 8, x.shape[1]), axis=0), axis=0, keepdims=True)

    sinks = []
    for kv in range(kvh):
        sink = jnp.zeros((1, SW_GROUP * n), F32)
        for g in range(SW_GROUP):
            sink = jnp.where(lane // n == g, sink_ref[layer, kv * SW_GROUP + g], sink)
        sinks.append(sink)
    scores = []
    for rows, kv in units:
        q_all = jnp.concatenate(
            [(q_ref[rows, (kv * SW_GROUP + g) * HEAD:(kv * SW_GROUP + g + 1) * HEAD] * (HEAD ** -0.5)).astype(BF16)
             for g in range(SW_GROUP)], axis=0)
        scores.append(_dot_t(k_ref[rows, kv * HEAD:(kv + 1) * HEAD].astype(BF16), q_all))
    probs = []
    for s, (_, kv) in zip(scores, units):
        m = jnp.maximum(fold(s, jnp.max), sinks[kv])
        p = jnp.exp(s - m)
        inv = 1.0 / (fold(p, jnp.sum) + jnp.exp(sinks[kv] - m))
        probs.append((p * inv).astype(BF16))
    outs = [_dot(v_ref[rows, kv * HEAD:(kv + 1) * HEAD].T.astype(BF16), pb) for (rows, kv), pb in zip(units, probs)]
    for (rows, kv), ot in zip(units, outs):
        for g in range(SW_GROUP):
            hd = kv * SW_GROUP + g
            o_ref[rows, hd * HEAD:(hd + 1) * HEAD] = ot[:, g * n:(g + 1) * n].T.astype(BF16)


def _swa_ctx(p, mix, geom, layer, sink):
    n_seq, n, t = geom["n_seq"], geom["n"], geom["t"]
    kvh = geom["sw_kv"]
    qw = kvh * SW_GROUP * HEAD
    kw = kvh * HEAD
    q_blk = geom["cq_off"] // qw
    k_blk = geom["ck_off"] // kw
    mix_blk = (geom["da_heads"] + geom["hg_heads"]) * HEAD // qw
    assert geom["cq_off"] % qw == 0 and geom["ck_off"] % kw == 0
    rows = math.gcd(n_seq, CTX_SEQ_PER_STEP) * n
    return pl.pallas_call(
        functools.partial(_swa_ctx_kernel, layer=layer, n=n, kvh=kvh),
        out_shape=jax.ShapeDtypeStruct(mix.shape, mix.dtype),
        grid=(t // rows,),
        in_specs=[pl.BlockSpec((rows, qw), lambda b: (b, q_blk)),
                  pl.BlockSpec((rows, kw), lambda b: (b, k_blk)),
                  pl.BlockSpec((rows, kw), lambda b: (b, k_blk + 1)),
                  pl.BlockSpec(memory_space=pltpu.SMEM),
                  pl.BlockSpec(memory_space=pl.ANY)],
        out_specs=pl.BlockSpec((rows, qw), lambda b: (b, mix_blk)),
        input_output_aliases={4: 0},
        compiler_params=_cparams("parallel"),
        name="sink_attention",
    )(p, p, p, sink, mix)


def _band_masks(n):
    r = np.arange(3 * SW_BLOCK)[:, None]
    i = (np.arange(SW_GROUP * SW_BLOCK) % SW_BLOCK)[None, :]
    window = np.abs(SW_BLOCK + i - r) <= SW_WINDOW
    not_before = r >= SW_BLOCK
    not_after = r < 2 * SW_BLOCK
    variants = [window, window & not_before, window & not_after, window & not_before & not_after]
    return jnp.asarray(np.stack(variants).astype(np.float32))


def _swa_lat_kernel(q_ref, k_ref, v_ref, qc_ref, qsa_ref, qsb_ref, kc_ref, ksa_ref, ksb_ref, ck_ref, cv_ref,
                    mask_ref, sink_ref, mix_ref, o_ref, kr_ref, vt_ref, kctx_ref, vctxt_ref, *, layer, n):
    kv = pl.program_id(1)
    step = pl.program_id(2)
    nb = n // SW_BLOCK
    w = HEAD // 4
    band = 3 * SW_BLOCK
    per_step = q_ref.shape[0] // SW_BLOCK

    @pl.when(step == 0)
    def _():
        zeros = jnp.zeros((SW_BLOCK, HEAD), BF16)
        for j in (0, nb + 1):
            kr_ref[j] = zeros
            vt_ref[j] = zeros

        def fill(j, carry):
            rows = pl.ds(pl.multiple_of(j * SW_BLOCK, SW_BLOCK), SW_BLOCK)
            k = _rope(k_ref[rows, :], kc_ref[rows, :], ksa_ref[rows, :], ksb_ref[rows, :], w)
            kr_ref[j + 1] = k.astype(BF16)
            vt_ref[j + 1] = v_ref[rows, :].T.astype(BF16)
            return carry

        lax.fori_loop(0, nb, fill, 0)
        kctx_ref[...] = ck_ref[...].astype(BF16)
        vctxt_ref[...] = cv_ref[...].T.astype(BF16)

    lane = lax.broadcasted_iota(jnp.int32, (1, SW_GROUP * SW_BLOCK), 1)
    sink = jnp.zeros((1, SW_GROUP * SW_BLOCK), F32)
    for g in range(SW_GROUP):
        sink = jnp.where(lane // SW_BLOCK == g, sink_ref[layer, kv * SW_GROUP + g], sink)

    def fold(x, op):
        return op(op(x.reshape(x.shape[0] // 8, 8, x.shape[1]), axis=0), axis=0, keepdims=True)

    blocks = [step * per_step + c for c in range(per_step)]
    rows = [slice(c * SW_BLOCK, (c + 1) * SW_BLOCK) for c in range(per_step)]
    scores = []
    for qb, r in zip(blocks, rows):
        qc, qsa, qsb = qc_ref[r, :], qsa_ref[r, :], qsb_ref[r, :]
        q_all = jnp.concatenate(
            [(_rope(q_ref[r, g * HEAD:(g + 1) * HEAD], qc, qsa, qsb, w) * (HEAD ** -0.5)).astype(BF16)
             for g in range(SW_GROUP)], axis=0)
        k_all = jnp.concatenate([kr_ref[qb], kr_ref[qb + 1], kr_ref[qb + 2], kctx_ref[...]], axis=0)
        scores.append(_dot_t(k_all, q_all))
    probs = []
    for qb, s in zip(blocks, scores):
        variant = jnp.where(qb == 0, 1, 0) + jnp.where(qb == nb - 1, 2, 0)
        s_band = jnp.where(mask_ref[variant] > 0.5, s[:band], NEG_INF)
        s_ctx = s[band:]
        m = jnp.maximum(jnp.maximum(fold(s_band, jnp.max), fold(s_ctx, jnp.max)), sink)
        p_band = jnp.exp(s_band - m)
        p_ctx = jnp.exp(s_ctx - m)
        inv = 1.0 / (fold(p_band, jnp.sum) + fold(p_ctx, jnp.sum) + jnp.exp(sink - m))
        probs.append(jnp.concatenate([p_band * inv, p_ctx * inv], axis=0).astype(BF16))
    outs = []
    for qb, p_all in zip(blocks, probs):
        vt_all = jnp.concatenate([vt_ref[qb], vt_ref[qb + 1], vt_ref[qb + 2], vctxt_ref[...]], axis=1)
        outs.append(_dot(vt_all, p_all))
    for r, ot in zip(rows, outs):
        for g in range(SW_GROUP):
            o_ref[r, g * HEAD:(g + 1) * HEAD] = ot[:, g * SW_BLOCK:(g + 1) * SW_BLOCK].T.astype(BF16)


def _swa_lat(p, mix, geom, layer, sink, rope_tabs, cache_k, cache_v):
    n_seq, n, t = geom["n_seq"], geom["n"], geom["t"]
    kvh = geom["sw_kv"]
    qw = SW_GROUP * HEAD
    q_blk = geom["cq_off"] // qw
    k_blk = geom["ck_off"] // HEAD
    v_blk = k_blk + kvh
    mix_blk = (geom["da_heads"] + geom["hg_heads"]) * HEAD // qw
    nb = n // SW_BLOCK
    per_step = SW_PER_STEP if nb % SW_PER_STEP == 0 else 1
    nsteps = nb // per_step
    tq = per_step * SW_BLOCK
    n_ctx = cache_k.shape[3]
    masks = _band_masks(n)
    cspec = pl.BlockSpec((None, None, None, n_ctx, HEAD), lambda b, kv, i: (b, layer, kv, 0, 0))
    return pl.pallas_call(
        functools.partial(_swa_lat_kernel, layer=layer, n=n),
        out_shape=jax.ShapeDtypeStruct(mix.shape, mix.dtype),
        grid=(n_seq, kvh, nsteps),
        in_specs=[pl.BlockSpec((tq, qw), lambda b, kv, i: (b * nsteps + i, q_blk + kv)),
                  pl.BlockSpec((n, HEAD), lambda b, kv, i: (b, k_blk + kv)),
                  pl.BlockSpec((n, HEAD), lambda b, kv, i: (b, v_blk + kv))]
                 + [pl.BlockSpec((tq, HEAD), lambda b, kv, i: (i, 0))] * 3
                 + [pl.BlockSpec((n, HEAD), lambda b, kv, i: (0, 0))] * 3
                 + [cspec, cspec, pl.BlockSpec(masks.shape, lambda b, kv, i: (0, 0, 0)),
                    pl.BlockSpec(memory_space=pltpu.SMEM), pl.BlockSpec(memory_space=pl.ANY)],
        out_specs=pl.BlockSpec((tq, qw), lambda b, kv, i: (b * nsteps + i, mix_blk + kv)),
        scratch_shapes=[pltpu.VMEM((nb + 2, SW_BLOCK, HEAD), BF16), pltpu.VMEM((nb + 2, HEAD, SW_BLOCK), BF16),
                        pltpu.VMEM((n_ctx, HEAD), BF16), pltpu.VMEM((HEAD, n_ctx), BF16)],
        input_output_aliases={13: 0},
        compiler_params=_cparams("parallel", "parallel", "arbitrary"),
        name="banded_sink_attention",
    )(p, p, p, *rope_tabs, *rope_tabs, cache_k, cache_v, masks, sink, mix)


def _rope_tables(n, half):
    h = half // 2
    pos = jnp.arange(n)
    inv = ROPE_BASE ** (-jnp.arange(h, dtype=F32) / h)
    zero = jnp.zeros((n, h), F32)
    c, sa, sb = [], [], []
    for axis_pos in (pos // GRID_W, pos % GRID_W):
        ang = axis_pos.astype(F32)[:, None] * inv[None, :]
        cos, sin = jnp.cos(ang), jnp.sin(ang)
        c += [cos, cos]
        sa += [-sin, zero]
        sb += [zero, sin]
    reps = HEAD // (2 * half)
    cat = lambda parts: jnp.tile(jnp.concatenate(parts, axis=1), (1, reps))
    return cat(c), cat(sa), cat(sb)


def _geometry(n_seq, n, d_model):
    da_heads = hg_heads = d_model // 512
    sw_heads = d_model // 256
    sw_kv = sw_heads // SW_GROUP
    cq_off = (3 * da_heads + 5 * hg_heads) * HEAD
    return dict(n_seq=n_seq, n=n, t=n_seq * n, da_heads=da_heads, hg_heads=hg_heads, sw_kv=sw_kv,
                cq_off=cq_off, ck_off=cq_off + sw_heads * HEAD, d_mix=(da_heads + hg_heads + sw_heads) * HEAD)


def _token_tiles(n):
    return _pick_tile(n, TM_PROJ), _pick_tile(n, TM_OUT), _pick_tile(n, TM_FFN)


def _project(x, geom, layer, mod, mod_row, w_in, cast_srcs, new_caches):
    if new_caches is None:
        tm = _token_tiles(geom["n"])[0]
        return _in_proj(x, mod, mod_row, w_in, layer, tm, geom["d_mix"], cast_srcs)
    heads, kvh = geom["da_heads"], geom["sw_kv"]
    cols = ((heads * HEAD, heads), (2 * heads * HEAD, heads), (geom["ck_off"], kvh),
            (geom["ck_off"] + kvh * HEAD, kvh))
    return _in_proj(x, mod, mod_row, w_in, layer, geom["n"], geom["d_mix"], cast_srcs, new_caches, cols)


def _mix_and_ffn(x, p, mix, geom, layer, mod, mod_row, wts, params, lb_params, scan_consts, tabs, caches,
                 new_state, alpha):
    w_out, w_gu, w_down = wts
    _, tm_out, tm_ffn = _token_tiles(geom["t"] if caches is None else geom["n"])
    if caches is None:
        mix = _diff_attn(p, mix, geom, layer, params["diff_lambda"], params["diff_norm_g"], None, None, None)
        mix, new_state = _hgrn(p, mix, geom, layer, lb_params, params["hgrn_norm_g"], scan_consts, None, new_state)
        mix = _swa_ctx(p, mix, geom, layer, params["swa_sink"])
    else:
        ck_d, cv_d, ck_s, cv_s, state = caches
        mix = _diff_attn(p, mix, geom, layer, params["diff_lambda"], params["diff_norm_g"], tabs[0], ck_d, cv_d)
        mix, _ = _hgrn(p, mix, geom, layer, lb_params, params["hgrn_norm_g"], scan_consts, state)
        mix = _swa_lat(p, mix, geom, layer, params["swa_sink"], tabs[1], ck_s, cv_s)
    x1 = _out_proj(mix, w_out, x, mod, mod_row, params["ln1_g"], params["ln1_b"], layer, tm_out, alpha)
    y = _ffn(x1, w_gu, w_down, mod, mod_row, params["ln2_g"], params["ln2_b"], layer, tm_ffn, alpha)
    return y, new_state


def kernel(x_prompt, x_sample, cache_diff_k, cache_diff_v, cache_swa_k, cache_swa_v, state_hgrn, c, c_ctx, w_mod,
           b_mod, w_in, w_out, diff_lambda, diff_norm_g, hgrn_lb_logits, hgrn_norm_g, swa_sink, ln1_g, ln1_b, ln2_g,
           ln2_b, w_gate_up, w_down):
    batch, seq, d = x_prompt.shape
    dec_batch, dec_seq, _ = x_sample.shape
    depth = w_mod.shape[0]
    alpha = (2 * depth) ** 0.25
    geom_c = _geometry(batch, seq, d)
    geom_l = _geometry(dec_batch, dec_seq, d)
    assert 1 + dec_batch <= MOD_ROWS

    cond = jnp.zeros((MOD_ROWS, d), F32).at[0].set(c_ctx).at[1:1 + dec_batch].set(c)
    mod = _modulation(cond, w_mod, b_mod).reshape(depth * MOD_ROWS * 6, 1, d)
    lb_params = _lb_params(hgrn_lb_logits)
    params = dict(diff_lambda=diff_lambda, diff_norm_g=diff_norm_g, hgrn_norm_g=hgrn_norm_g, swa_sink=swa_sink,
                  ln1_g=ln1_g, ln1_b=ln1_b, ln2_g=ln2_g, ln2_b=ln2_b)
    w_in16, w_out16 = w_in[0].astype(BF16), w_out[0].astype(BF16)
    tabs = (_rope_tables(dec_seq, DA_QK // 2), _rope_tables(dec_seq, HEAD // 2))
    consts_c = _scan_constants(min(HG_CHUNK, seq))
    consts_l = _scan_constants(min(HG_CHUNK, dec_seq))

    y_p = x_prompt.reshape(batch * seq, d)
    y_s = x_sample.reshape(dec_batch * dec_seq, d)
    heads, kvh = geom_c["da_heads"], geom_c["sw_kv"]
    new_kv = tuple(jnp.zeros((batch, depth, nh, seq, HEAD), F32) for nh in (heads, heads, kvh, kvh))
    new_state = jnp.zeros((batch, depth, 2, geom_c["hg_heads"], HEAD, HEAD), F32)
    caches = (cache_diff_k, cache_diff_v, cache_swa_k, cache_swa_v, state_hgrn)
    for l in range(depth):
        row_c = lambda m, tm, j, l=l: (l * MOD_ROWS) * 6 + j
        row_l = lambda m, tm, j, l=l: (l * MOD_ROWS + 1 + (m * tm) // dec_seq) * 6 + j
        more = l + 1 < depth
        casts_c = ((w_down, l),) + (((w_out, l + 1),) if more else ())
        casts_l = ((w_gate_up, l),) + (((w_in, l + 1),) if more else ())
        p_c, mix_c, cast_c, new_kv = _project(y_p, geom_c, l, mod, row_c, w_in16, casts_c, new_kv)
        p_l, mix_l, cast_l, _ = _project(y_s, geom_l, l, mod, row_l, w_in16, casts_l, None)
        wts = (w_out16, cast_l[0], cast_c[0])
        y_p, new_state = _mix_and_ffn(y_p, p_c, mix_c, geom_c, l, mod, row_c, wts, params, lb_params, consts_c,
                                      None, None, new_state, alpha)
        y_s, _ = _mix_and_ffn(y_s, p_l, mix_l, geom_l, l, mod, row_l, wts, params, lb_params, consts_l, tabs,
                              caches, None, alpha)
        if more:
            w_out16, w_in16 = cast_c[1], cast_l[1]
    return (y_p.reshape(batch, seq, d), y_s.reshape(dec_batch, dec_seq, d)) + new_kv + (new_state,)
```

```python
import functools
import math

import numpy as np
import jax
import jax.numpy as jnp
from jax import lax
from jax.experimental import pallas as pl
from jax.experimental.pallas import tpu as pltpu

F32 = jnp.float32
BF16 = jnp.bfloat16

GRID_W = 64
ROPE_BASE = 10000.0
LN_EPS = 1e-5
RMS_EPS = 1e-6
NEG_INF = -1e30
LB_FLOOR = 1e-30
LOG2_E = math.log2(math.e)
HEAD = 128
DA_QK = 64
SW_GROUP = 4
SW_BLOCK = 128
SW_WINDOW = 128
MOD_ROWS = 8

VMEM_LIMIT = 56 * 1024 * 1024
TM_PROJ = 256
TM_OUT = 512
TM_FFN = 512
TF_FFN = 512
TN_MOD = 1024
DA_TQ = 256
DA_KB = 256
HG_CHUNK = 128
HG_UNROLL = 8
SW_PER_STEP = 8
CTX_SEQ_PER_STEP = 4


def _cparams(*sem):
    return pltpu.CompilerParams(dimension_semantics=sem, vmem_limit_bytes=VMEM_LIMIT)


def _dot(a, b):
    return jnp.dot(a, b, preferred_element_type=F32)


def _dot_t(a, b):
    return lax.dot_general(a, b, (((1,), (1,)), ((), ())), preferred_element_type=F32)


def _silu(x):
    return x / (1.0 + jnp.exp2(x * (-LOG2_E)))


def _layer_norm(y, g, b):
    mu = jnp.mean(y, axis=-1, keepdims=True)
    d = y - mu
    var = jnp.mean(d * d, axis=-1, keepdims=True)
    return d * lax.rsqrt(var + LN_EPS) * g + b


def _rms_norm(o, g):
    ms = jnp.mean(o * o, axis=-1, keepdims=True)
    return o * lax.rsqrt(ms + RMS_EPS) * g


def _rope(x, c, sa, sb, w):
    return x * c + pltpu.roll(x, HEAD - w, 1) * sa + pltpu.roll(x, w, 1) * sb


def _pick_tile(n, target):
    t = min(n, target)
    while n % t or t % 128:
        t -= 128
    return t


def _lb_kernel(logit_ref, loglb_ref, log1m_ref, onem_ref):
    depth = logit_ref.shape[0]
    x = [logit_ref[l] for l in range(depth)]
    m = functools.reduce(jnp.maximum, x)
    e = [jnp.exp(xi - m) for xi in x]
    tot = functools.reduce(lambda a, b: a + b, e)
    w = [ei / tot for ei in e]
    acc = jnp.zeros_like(w[0])
    for l in range(depth):
        acc = acc + w[l]
        lb = acc - w[0]
        loglb_ref[l] = jnp.log(jnp.maximum(lb, LB_FLOOR))
        log1m_ref[l] = jnp.log1p(-lb)
        onem_ref[l] = 1.0 - lb


def _lb_params(logits):
    shp = jax.ShapeDtypeStruct(logits.shape, F32)
    return pl.pallas_call(_lb_kernel, out_shape=(shp, shp, shp), name="hgrn_lb_params")(logits)


def _mod_kernel(c_ref, w_ref, b_ref, o_ref):
    s = _silu(c_ref[...])
    s_hi = s.astype(BF16)
    s_lo = (s - s_hi.astype(F32)).astype(BF16)
    w = w_ref[...]
    w_hi = w.astype(BF16)
    w_lo = (w - w_hi.astype(F32)).astype(BF16)
    o_ref[...] = _dot(s_hi, w_hi) + _dot(s_lo, w_hi) + _dot(s_hi, w_lo) + b_ref[...]


def _modulation(cond, w_mod, b_mod):
    depth, d, d6 = w_mod.shape
    tn = _pick_tile(d6, TN_MOD)
    return pl.pallas_call(
        _mod_kernel,
        out_shape=jax.ShapeDtypeStruct((depth, MOD_ROWS, d6), F32),
        grid=(depth, d6 // tn),
        in_specs=[pl.BlockSpec((MOD_ROWS, d), lambda l, n: (0, 0)),
                  pl.BlockSpec((None, d, tn), lambda l, n: (l, 0, n)),
                  pl.BlockSpec((None, 1, tn), lambda l, n: (l, 0, n))],
        out_specs=pl.BlockSpec((None, MOD_ROWS, tn), lambda l, n: (l, 0, n)),
        compiler_params=_cparams("parallel", "parallel"),
        name="adaln_modulation",
    )(cond, w_mod, b_mod.reshape(depth, 1, d6))


def _proj_kernel(x_ref, sh_ref, sc_ref, w_ref, *rest, cache_cols, n_cast):
    n = len(cache_cols)
    cast_in, rest = rest[:n_cast], rest[n_cast + n:]
    o_ref, mix_ref = rest[:2]
    cast_out, cache_out = rest[2:2 + n_cast], rest[2 + n_cast:]
    h = (x_ref[...] * (1.0 + sc_ref[0]) + sh_ref[0]).astype(BF16)
    o_ref[...] = _dot(h, w_ref[...])
    mix_ref[...] = jnp.zeros_like(mix_ref)
    for src_ref, dst_ref in zip(cast_in, cast_out):
        dst_ref[...] = src_ref[...].astype(BF16)
    for c_ref, (off, nh) in zip(cache_out, cache_cols):
        for hd in range(nh):
            c_ref[hd] = o_ref[:, off + hd * HEAD:off + (hd + 1) * HEAD]


def _in_proj(x, mod, mod_row, w_in, layer, tm, d_mix, cast_srcs, caches=None, cache_cols=()):
    t, d = x.shape
    d_in = w_in.shape[1]
    steps = t // tm
    caches = tuple(caches or ())
    n_cast = len(cast_srcs)
    out_shape = [jax.ShapeDtypeStruct((t, d_in), F32), jax.ShapeDtypeStruct((t, d_mix), BF16)]
    out_specs = [pl.BlockSpec((tm, d_in), lambda m: (m, 0)), pl.BlockSpec((tm, d_mix), lambda m: (m, 0))]
    cast_specs = []
    for src, src_layer in cast_srcs:
        _, rows, width = src.shape
        slab = rows // steps
        assert rows % steps == 0 and slab % 16 == 0
        cast_specs.append(pl.BlockSpec((None, slab, width), lambda m, src_layer=src_layer: (src_layer, m, 0)))
        out_shape.append(jax.ShapeDtypeStruct((rows, width), BF16))
        out_specs.append(pl.BlockSpec((slab, width), lambda m: (m, 0)))
    for c in caches:
        assert c.shape[3] == tm
        out_shape.append(jax.ShapeDtypeStruct(c.shape, c.dtype))
        out_specs.append(pl.BlockSpec((None, None, c.shape[2], tm, HEAD), lambda m: (m, layer, 0, 0, 0)))
    res = pl.pallas_call(
        functools.partial(_proj_kernel, cache_cols=tuple(cache_cols), n_cast=n_cast),
        out_shape=tuple(out_shape),
        grid=(steps,),
        in_specs=[pl.BlockSpec((tm, d), lambda m: (m, 0)),
                  pl.BlockSpec((1, 1, d), lambda m: (mod_row(m, tm, 0), 0, 0)),
                  pl.BlockSpec((1, 1, d), lambda m: (mod_row(m, tm, 1), 0, 0)),
                  pl.BlockSpec((d, d_in), lambda m: (0, 0), pipeline_mode=pl.Buffered(1))]
                 + cast_specs + [pl.BlockSpec(memory_space=pl.ANY)] * len(caches),
        out_specs=tuple(out_specs),
        input_output_aliases={4 + n_cast + i: 2 + n_cast + i for i in range(len(caches))},
        compiler_params=_cparams("parallel"),
        name="in_proj",
    )(x, mod, mod, w_in, *[src for src, _ in cast_srcs], *caches)
    return res[0], res[1], tuple(res[2:2 + n_cast]), tuple(res[2 + n_cast:])


def _out_proj_kernel(mix_ref, w_ref, x_ref, g1_ref, lng_ref, lnb_ref, x1_ref, *, alpha):
    mix = _dot(mix_ref[...], w_ref[...])
    x1_ref[...] = _layer_norm(alpha * x_ref[...] + g1_ref[0] * mix, lng_ref[...], lnb_ref[...])


def _out_proj(mix, w_out, x, mod, mod_row, ln_g, ln_b, layer, tm, alpha):
    t, d = x.shape
    d_mix = mix.shape[1]
    depth = ln_g.shape[0]
    vec = pl.BlockSpec((None, 1, d), lambda m: (layer, 0, 0))
    return pl.pallas_call(
        functools.partial(_out_proj_kernel, alpha=alpha),
        out_shape=jax.ShapeDtypeStruct((t, d), F32),
        grid=(t // tm,),
        in_specs=[pl.BlockSpec((tm, d_mix), lambda m: (m, 0)),
                  pl.BlockSpec((d_mix, d), lambda m: (0, 0), pipeline_mode=pl.Buffered(1)),
                  pl.BlockSpec((tm, d), lambda m: (m, 0)),
                  pl.BlockSpec((1, 1, d), lambda m: (mod_row(m, tm, 2), 0, 0)),
                  vec, vec],
        out_specs=pl.BlockSpec((tm, d), lambda m: (m, 0)),
        compiler_params=_cparams("parallel"),
        name="out_proj_ln",
    )(mix, w_out, x, mod, ln_g.reshape(depth, 1, d), ln_b.reshape(depth, 1, d))


def _ffn_kernel(x_ref, sh2_ref, sc2_ref, wg_ref, wu_ref, wd_ref, g2_ref, lng_ref, lnb_ref, o_ref, h_ref, *, alpha):
    f = pl.program_id(1)

    @pl.when(f == 0)
    def _():
        h_ref[...] = (x_ref[...] * (1.0 + sc2_ref[0]) + sh2_ref[0]).astype(BF16)
        o_ref[...] = jnp.zeros_like(o_ref)

    h = h_ref[...]
    a = _dot(h, wg_ref[...])
    u = _dot(h, wu_ref[...])
    o_ref[...] += _dot((_silu(a) * u).astype(BF16), wd_ref[...])

    @pl.when(f == pl.num_programs(1) - 1)
    def _():
        y = alpha * x_ref[...] + g2_ref[0] * o_ref[...]
        o_ref[...] = _layer_norm(y, lng_ref[...], lnb_ref[...])


def _ffn(x1, w_gu, w_down, mod, mod_row, ln_g, ln_b, layer, tm, alpha):
    t, d = x1.shape
    d_ff = w_down.shape[0]
    depth = ln_g.shape[0]
    tf = _pick_tile(d_ff, TF_FFN)
    nf = d_ff // tf
    vec = pl.BlockSpec((None, 1, d), lambda m, f: (layer, 0, 0))
    row = lambda j: pl.BlockSpec((1, 1, d), lambda m, f: (mod_row(m, tm, j), 0, 0))
    return pl.pallas_call(
        functools.partial(_ffn_kernel, alpha=alpha),
        out_shape=jax.ShapeDtypeStruct((t, d), F32),
        grid=(t // tm, nf),
        in_specs=[pl.BlockSpec((tm, d), lambda m, f: (m, 0)),
                  row(3), row(4),
                  pl.BlockSpec((d, tf), lambda m, f: (0, f)),
                  pl.BlockSpec((d, tf), lambda m, f: (0, nf + f)),
                  pl.BlockSpec((tf, d), lambda m, f: (f, 0)),
                  row(5), vec, vec],
        out_specs=pl.BlockSpec((tm, d), lambda m, f: (m, 0)),
        scratch_shapes=[pltpu.VMEM((tm, d), BF16)],
        compiler_params=_cparams("parallel", "arbitrary"),
        name="ffn_ln",
    )(x1, mod, mod, w_gu, w_gu, w_down, mod, ln_g.reshape(depth, 1, d), ln_b.reshape(depth, 1, d))


def _diff_attn_kernel(*refs, rope, cached, n_self, kb, lam_init, pipelined, n_tiles):
    it = iter(refs)
    q_ref, k_ref, v_ref = next(it), next(it), next(it)
    if rope:
        qc_ref, qsa_ref, qsb_ref, kc_ref, ksa_ref, ksb_ref = (next(it) for _ in range(6))
    if cached:
        ck_ref, cv_ref = next(it), next(it)
    lam_ref, g_ref, _, o_ref, kr_ref, vt_ref, m_ref, acc_ref = (next(it) for _ in range(8))
    s_refs = tuple(it)
    nblk = kr_ref.shape[0]
    nself = n_self // kb
    tq = q_ref.shape[0]
    step = pl.program_id(2)
    last = pl.num_programs(2) - 1

    def prepare_keys():
        for j in range(nself):
            rows = slice(j * kb, (j + 1) * kb)
            k = k_ref[rows, :]
            if rope:
                k = _rope(k, kc_ref[rows, :], ksa_ref[rows, :], ksb_ref[rows, :], DA_QK // 4)
            kr_ref[j] = k.astype(BF16)
            vt_ref[j] = v_ref[rows, :].T.astype(BF16)
        if cached:
            for j in range(nblk - nself):
                rows = slice(j * kb, (j + 1) * kb)
                kr_ref[nself + j] = ck_ref[rows, :].astype(BF16)
                vt_ref[nself + j] = cv_ref[rows, :].T.astype(BF16)

    def queries():
        q = q_ref[...]
        if rope:
            q = _rope(q, qc_ref[...], qsa_ref[...], qsb_ref[...], DA_QK // 4)
        q = q * (DA_QK ** -0.5 * LOG2_E)
        lane = lax.broadcasted_iota(jnp.int32, q.shape, 1)
        return (jnp.where(lane < DA_QK, q, 0.0).astype(BF16), jnp.where(lane >= DA_QK, q, 0.0).astype(BF16))

    def fold(x, op):
        return op(x.reshape(kb // 8, 8, tq), axis=0)

    def scores_block(j, qz, buf, ms):
        kblk = kr_ref[j]
        out = []
        for mp in range(2):
            s = _dot_t(kblk, qz[mp])
            buf[mp, j] = s
            out.append(jnp.maximum(ms[mp], fold(s, jnp.max)))
        return tuple(out)

    def values_block(j, buf, ms, ls):
        out = []
        for mp in range(2):
            p = jnp.exp2(buf[mp, j] - ms[mp])
            acc_ref[mp] += _dot(vt_ref[j], p.astype(BF16))
            out.append(ls[mp] + fold(p, jnp.sum))
        return tuple(out)

    neg = jnp.full((8, tq), -jnp.inf, F32)
    zero = jnp.zeros((8, tq), F32)

    def save_max(ms):
        for mp in range(2):
            m_ref[mp] = jnp.max(ms[mp], axis=0, keepdims=True)

    def finish(ls):
        l1, l2 = (jnp.sum(l, axis=0, keepdims=True) for l in ls)
        lp = lam_ref[...]
        lam = (jnp.exp(jnp.sum(lp[0:1] * lp[1:2], axis=-1, keepdims=True))
               - jnp.exp(jnp.sum(lp[2:3] * lp[3:4], axis=-1, keepdims=True)) + lam_init)
        ot = acc_ref[0] * (1.0 / l1) - acc_ref[1] * (lam / l2)
        ms_o = jnp.mean(ot * ot, axis=0, keepdims=True)
        ot = ot * lax.rsqrt(ms_o + RMS_EPS) * g_ref[...] * (1.0 - lam_init)
        o_ref[...] = ot.T.astype(BF16)

    if not pipelined:
        pl.when(step == 0)(prepare_keys)
        qz = queries()
        ms = (neg, neg)
        for j in range(nblk):
            ms = scores_block(j, qz, s_refs[0], ms)
        ms = tuple(jnp.max(m, axis=0, keepdims=True) for m in ms)
        acc_ref[...] = jnp.zeros_like(acc_ref)
        ls = (zero, zero)
        for j in range(nblk):
            ls = values_block(j, s_refs[0], ms, ls)
        finish(ls)
        return

    @pl.when(step == 0)
    def _():
        prepare_keys()
        qz = queries()
        ms = (neg, neg)
        for j in range(nblk):
            ms = scores_block(j, qz, s_refs[0], ms)
        save_max(ms)

    def interior(parity):
        prev = (m_ref[0], m_ref[1])
        qz = queries()
        acc_ref[...] = jnp.zeros_like(acc_ref)
        ms, ls = (neg, neg), (zero, zero)
        for j in range(nblk):
            ms = scores_block(j, qz, s_refs[parity], ms)
            ls = values_block(j, s_refs[1 - parity], prev, ls)
        save_max(ms)
        finish(ls)

    inside = (step > 0) & (step < last)
    pl.when(inside & (step % 2 == 1))(functools.partial(interior, 1))
    pl.when(inside & (step % 2 == 0))(functools.partial(interior, 0))

    @pl.when(step == last)
    def _():
        prev = (m_ref[0], m_ref[1])
        acc_ref[...] = jnp.zeros_like(acc_ref)
        ls = (zero, zero)
        for j in range(nblk):
            ls = values_block(j, s_refs[(n_tiles - 1) % 2], prev, ls)
        finish(ls)


def _diff_attn_seq_kernel(q_ref, k_ref, v_ref, lam_ref, g_ref, mix_ref, o_ref, *, n, heads, lam_init):
    del mix_ref
    lane = lax.broadcasted_iota(jnp.int32, (n, HEAD), 1)
    units = [(slice(sq * n, (sq + 1) * n), slice(hd * HEAD, (hd + 1) * HEAD))
             for sq in range(q_ref.shape[0] // n) for hd in range(heads)]
    scores, vts = [], []
    for rows, cols in units:
        q = q_ref[rows, cols] * (DA_QK ** -0.5 * LOG2_E)
        k = k_ref[rows, cols].astype(BF16)
        scores.append([_dot_t(k, jnp.where(lane < DA_QK, q, 0.0).astype(BF16)),
                       _dot_t(k, jnp.where(lane >= DA_QK, q, 0.0).astype(BF16))])
        vts.append(v_ref[rows, cols].T.astype(BF16))
    probs = []
    for pair in scores:
        parts = []
        for s in pair:
            p = jnp.exp2(s - jnp.max(s, axis=0, keepdims=True))
            parts.append((p.astype(BF16), jnp.sum(p, axis=0, keepdims=True)))
        probs.append(parts)
    accs = [[_dot(vt, pb) for pb, _ in parts] for vt, parts in zip(vts, probs)]
    lp = lam_ref[...]
    lam = (jnp.exp(jnp.sum(lp[0:1] * lp[1:2], axis=-1, keepdims=True))
           - jnp.exp(jnp.sum(lp[2:3] * lp[3:4], axis=-1, keepdims=True)) + lam_init)
    for (rows, cols), ((_, l1), (_, l2)), acc in zip(units, probs, accs):
        ot = acc[0] * (1.0 / l1) - acc[1] * (lam / l2)
        ms_o = jnp.mean(ot * ot, axis=0, keepdims=True)
        ot = ot * lax.rsqrt(ms_o + RMS_EPS) * g_ref[...] * (1.0 - lam_init)
        o_ref[rows, cols] = ot.T.astype(BF16)


def _diff_attn_seq(p, mix, geom, layer, diff_lambda, diff_g):
    n_seq, n = geom["n_seq"], geom["n"]
    heads = geom["da_heads"]
    depth = diff_g.shape[0]
    lam_init = 0.8 - 0.6 * math.exp(-0.3 * layer)
    per_step = math.gcd(n_seq, CTX_SEQ_PER_STEP)
    blk = lambda j: pl.BlockSpec((per_step * n, heads * HEAD), lambda b: (b, j))
    return pl.pallas_call(
        functools.partial(_diff_attn_seq_kernel, n=n, heads=heads, lam_init=lam_init),
        out_shape=jax.ShapeDtypeStruct(mix.shape, mix.dtype),
        grid=(n_seq // per_step,),
        in_specs=[blk(0), blk(1), blk(2),
                  pl.BlockSpec((None, 4, DA_QK), lambda b: (layer, 0, 0)),
                  pl.BlockSpec((None, HEAD, 1), lambda b: (layer, 0, 0)),
                  pl.BlockSpec(memory_space=pl.ANY)],
        out_specs=blk(0),
        input_output_aliases={5: 0},
        compiler_params=_cparams("parallel"),
        name="diff_attention_seq",
    )(p, p, p, diff_lambda, diff_g.reshape(depth, HEAD, 1), mix)


def _diff_attn(p, mix, geom, layer, diff_lambda, diff_g, rope_tabs, cache_k, cache_v):
    if rope_tabs is None and cache_k is None and geom["n"] <= DA_TQ:
        return _diff_attn_seq(p, mix, geom, layer, diff_lambda, diff_g)
    n_seq, n, t = geom["n_seq"], geom["n"], geom["t"]
    heads = geom["da_heads"]
    rope = rope_tabs is not None
    cached = cache_k is not None
    tq = min(n, DA_TQ)
    kb = min(n, DA_KB)
    nq = n // tq
    n_ctx = cache_k.shape[3] if cached else 0
    nblk = (n + n_ctx) // kb
    assert n % kb == 0 and n_ctx % kb == 0
    k_off, v_off = heads, 2 * heads
    depth = diff_g.shape[0]
    lam_init = 0.8 - 0.6 * math.exp(-0.3 * layer)

    pipelined = nq > 1
    q_tile = (lambda i: jnp.minimum(i, nq - 1)) if pipelined else (lambda i: i)
    o_tile = (lambda i: jnp.maximum(i - 1, 0)) if pipelined else (lambda i: i)
    in_specs = [pl.BlockSpec((tq, HEAD), lambda b, h, i: (b * nq + q_tile(i), h)),
                pl.BlockSpec((n, HEAD), lambda b, h, i: (b, k_off + h)),
                pl.BlockSpec((n, HEAD), lambda b, h, i: (b, v_off + h))]
    args = [p, p, p]
    if rope:
        in_specs += [pl.BlockSpec((tq, HEAD), lambda b, h, i: (q_tile(i), 0))] * 3
        in_specs += [pl.BlockSpec((n, HEAD), lambda b, h, i: (0, 0))] * 3
        args += list(rope_tabs) * 2
    if cached:
        spec = pl.BlockSpec((None, None, None, n_ctx, HEAD), lambda b, h, i: (b, layer, h, 0, 0))
        in_specs += [spec, spec]
        args += [cache_k, cache_v]
    in_specs += [pl.BlockSpec((None, 4, DA_QK), lambda b, h, i: (layer, 0, 0)),
                 pl.BlockSpec((None, HEAD, 1), lambda b, h, i: (layer, 0, 0)),
                 pl.BlockSpec(memory_space=pl.ANY)]
    args += [diff_lambda, diff_g.reshape(depth, HEAD, 1), mix]
    return pl.pallas_call(
        functools.partial(_diff_attn_kernel, rope=rope, cached=cached, n_self=n, kb=kb, lam_init=lam_init,
                          pipelined=pipelined, n_tiles=nq),
        out_shape=jax.ShapeDtypeStruct(mix.shape, mix.dtype),
        grid=(n_seq, heads, nq + 1 if pipelined else nq),
        in_specs=in_specs,
        input_output_aliases={len(args) - 1: 0},
        out_specs=pl.BlockSpec((tq, HEAD), lambda b, h, i: (b * nq + o_tile(i), h)),
        scratch_shapes=[pltpu.VMEM((nblk, kb, HEAD), BF16), pltpu.VMEM((nblk, HEAD, kb), BF16),
                        pltpu.VMEM((2, 1, tq), F32), pltpu.VMEM((2, HEAD, tq), F32)]
                       + [pltpu.VMEM((2, nblk, kb, tq), F32)] * (2 if pipelined else 1),
        compiler_params=_cparams("parallel", "parallel", "arbitrary"),
        name="diff_attention",
    )(*args)


def _scan_constants(c):
    levels = int(math.log2(c))
    t = np.arange(c)[:, None]
    s = np.arange(c)[None, :]
    tri = (s <= t).astype(np.float32)
    masks = []
    for j in range(levels):
        m = c >> (j + 1)
        base = (t // (2 * m)) * (2 * m)
        sbase = (s // (2 * m)) * (2 * m)
        masks.append((sbase == base) & ((t - base) >= m) & ((s - sbase) < m))
    masks.append(s == t)
    masks = np.stack(masks).astype(np.float32)
    flip = lambda a: a.reshape(-1, c, c)[:, ::-1, ::-1].reshape(a.shape)
    return (jnp.asarray(tri, BF16), jnp.asarray(flip(tri), BF16),
            jnp.asarray(masks, F32), jnp.asarray(flip(masks), F32))


def _level_exponents(cum, d, chunk):
    row = lax.broadcasted_iota(jnp.int32, cum.shape, 0)
    sub = lax.broadcasted_iota(jnp.int32, (8, HEAD), 0)
    out = []
    m = chunk // 2
    while m >= 1:
        in_second = (row & m) != 0
        q_side = in_second if d == 0 else jnp.logical_not(in_second)
        if m >= 8:
            pieces = []
            for b0 in range(0, chunk, 2 * m):
                r = b0 + m - 1 + d
                bnd = jnp.broadcast_to(cum[r:r + 1, :], (m, HEAD))
                first, second = cum[b0:b0 + m, :], cum[b0 + m:b0 + 2 * m, :]
                pieces += [bnd - first, second - bnd] if d == 0 else [first - bnd, bnd - second]
            x = jnp.concatenate(pieces, axis=0)
        elif m == 1:
            neighbour = pltpu.roll(cum, 1 if d == 0 else chunk - 1, 0)
            x = jnp.where(q_side, cum - neighbour, 0.0)
        else:
            tiles = []
            for v in range(chunk // 8):
                if m == 4:
                    r = 8 * v + 3 + d
                    tiles.append(jnp.broadcast_to(cum[r:r + 1, :], (8, HEAD)))
                else:
                    lo = jnp.broadcast_to(cum[8 * v + 1 + d:8 * v + 2 + d, :], (8, HEAD))
                    hi = jnp.broadcast_to(cum[8 * v + 5 + d:8 * v + 6 + d, :], (8, HEAD))
                    tiles.append(jnp.where(sub < 4, lo, hi))
            bnd = jnp.concatenate(tiles, axis=0)
            x = jnp.where(q_side, cum - bnd, bnd - cum)
        out.append(x)
        m //= 2
    return out


def _hgrn_kernel(*refs, n, chunk, unroll, hs, has_state, emit_state):
    it = iter(refs)
    q_ref, zf_ref, zb_ref, v_ref, hg_ref = (next(it) for _ in range(5))
    loglb_ref, log1m_ref, onem_ref, g_ref = (next(it) for _ in range(4))
    tri_refs = (next(it), next(it))
    mask_refs = (next(it), next(it))
    s0_ref = next(it) if has_state else None
    next(it)
    if emit_state:
        next(it)
    o_ref = next(it)
    st_ref = next(it) if emit_state else None
    obuf_refs = (next(it), next(it))
    s_ref = next(it)

    levels = int(math.log2(chunk))
    nchunks = n // chunk
    z_refs = (zf_ref, zb_ref)

    for d in range(2):
        for hd in range(hs):
            s_ref[d * hs + hd] = s0_ref[d, hd].T if has_state else jnp.zeros((HEAD, HEAD), F32)

    def gates(hd, d, start):
        rows = pl.ds(start, chunk)
        cols = slice(hd * HEAD, (hd + 1) * HEAD)
        z = z_refs[d][rows, cols]
        q = _silu(q_ref[rows, cols])
        v = v_ref[rows, cols].astype(BF16)
        z2 = z * LOG2_E
        e = jnp.exp2(-jnp.abs(z2))
        one_e = 1.0 + e
        log_sig = jnp.minimum(z2, 0.0) - jnp.log2(one_e)
        sig_neg = jnp.where(z >= 0.0, e, 1.0) / one_e
        a = loglb_ref[d:d + 1, cols] * LOG2_E
        b = log1m_ref[d:d + 1, cols] * LOG2_E + log_sig
        g = jnp.maximum(a, b) + jnp.log2(1.0 + jnp.exp2(-jnp.abs(a - b)))
        k = onem_ref[d:d + 1, cols] * sig_neg
        g_hi = g.astype(BF16)
        g_lo = (g - g_hi.astype(F32)).astype(BF16)
        cum = _dot(tri_refs[d][...], jnp.concatenate([g_hi, g_lo], axis=1))
        return q, k, v, cum[:, :HEAD] + cum[:, HEAD:]

    def body(i, carry):
        chains = []
        for u in range(unroll):
            c = i * unroll + u
            for hd in range(hs):
                chains.append((hd, 0, pl.multiple_of(c * chunk, chunk)))
                chains.append((hd, 1, pl.multiple_of((nchunks - 1 - c) * chunk, chunk)))
        work = [gates(hd, d, start) for hd, d, start in chains]
        qk16 = [(q.astype(BF16), k.astype(BF16)) for q, k, _, _ in work]
        atts = [mask_refs[d][levels] * _dot_t(qb, kb) for (_, d, _), (qb, kb) in zip(chains, qk16)]
        expo = [_level_exponents(cum, d, chunk) for (_, d, _), (_, _, _, cum) in zip(chains, work)]
        for j in range(levels):
            for ci, ((_, d, _), (qb, kb)) in enumerate(zip(chains, qk16)):
                fac = jnp.exp2(expo[ci][j]).astype(BF16)
                atts[ci] = atts[ci] + mask_refs[d][j] * _dot_t(qb * fac, kb * fac)
        intra, delta, q_in, decay = [], [], [], []
        for (_, d, _), (q, k, v, cum), att in zip(chains, work, atts):
            last = cum[chunk - 1:chunk] if d == 0 else cum[0:1]
            kt = (k * jnp.exp2(last - cum)).astype(BF16)
            intra.append(_dot(att.astype(BF16), v))
            delta.append(lax.dot_general(v, kt, (((0,), (0,)), ((), ())), preferred_element_type=F32))
            q_in.append((q * jnp.exp2(cum)).astype(BF16))
            decay.append(jnp.exp2(last))
        st = [s_ref[j] for j in range(2 * hs)]
        for ci, (hd, d, start) in enumerate(chains):
            j = d * hs + hd
            obuf_refs[d][pl.ds(start, chunk), hd * HEAD:(hd + 1) * HEAD] = (
                intra[ci] + _dot_t(q_in[ci], st[j].astype(BF16)))
            st[j] = decay[ci] * st[j] + delta[ci]
        for j in range(2 * hs):
            s_ref[j] = st[j]
        return carry

    lax.fori_loop(0, nchunks // unroll, body, 0)

    for hd in range(hs):
        cols = slice(hd * HEAD, (hd + 1) * HEAD)
        o = obuf_refs[0][:, cols] + obuf_refs[1][:, cols]
        o_ref[:, cols] = (_rms_norm(o, g_ref[...]) * _silu(hg_ref[:, cols])).astype(BF16)
        if emit_state:
            for d in range(2):
                st_ref[d, hd] = s_ref[d * hs + hd].T


def _hgrn(p, mix, geom, layer, lb_params, hgrn_g, consts, state, new_state=None):
    n_seq, n, t = geom["n_seq"], geom["n"], geom["t"]
    heads = geom["hg_heads"]
    mix_col = geom["da_heads"]
    base = 3 * geom["da_heads"]
    has_state = state is not None
    emit_state = not has_state
    depth = hgrn_g.shape[0]
    chunk = min(HG_CHUNK, n)
    unroll = math.gcd(n // chunk, HG_UNROLL)
    hs = math.gcd(heads, HG_UNROLL // unroll)
    assert base % hs == 0 and mix_col % hs == 0
    hw = hs * HEAD
    col = lambda j: pl.BlockSpec((n, hw), lambda b, h: (b, (base + j * heads) // hs + h))
    lbspec = pl.BlockSpec((None, 2, hw), lambda b, h: (layer, 0, h))
    const_specs = [pl.BlockSpec(c.shape, lambda b, h, nd=c.ndim: (0,) * nd) for c in consts]
    in_specs = [col(0), col(1), col(2), col(3), col(4), lbspec, lbspec, lbspec,
                pl.BlockSpec((None, 1, HEAD), lambda b, h: (layer, 0, 0))] + const_specs
    args = [p] * 5 + list(lb_params) + [hgrn_g.reshape(depth, 1, HEAD)] + list(consts)
    if has_state:
        in_specs.append(pl.BlockSpec((None, None, 2, hs, HEAD, HEAD), lambda b, h: (b, layer, 0, h, 0, 0)))
        args.append(state)
    in_specs.append(pl.BlockSpec(memory_space=pl.ANY))
    args.append(mix)
    aliases = {len(args) - 1: 0}
    out_shape = [jax.ShapeDtypeStruct(mix.shape, mix.dtype)]
    out_specs = [pl.BlockSpec((n, hw), lambda b, h: (b, mix_col // hs + h))]
    if emit_state:
        in_specs.append(pl.BlockSpec(memory_space=pl.ANY))
        args.append(new_state)
        aliases[len(args) - 1] = 1
        out_shape.append(jax.ShapeDtypeStruct(new_state.shape, new_state.dtype))
        out_specs.append(pl.BlockSpec((None, None, 2, hs, HEAD, HEAD), lambda b, h: (b, layer, 0, h, 0, 0)))
    res = pl.pallas_call(
        functools.partial(_hgrn_kernel, n=n, chunk=chunk, unroll=unroll, hs=hs,
                          has_state=has_state, emit_state=emit_state),
        out_shape=tuple(out_shape),
        grid=(n_seq, heads // hs),
        in_specs=in_specs,
        out_specs=tuple(out_specs),
        scratch_shapes=[pltpu.VMEM((n, hw), F32), pltpu.VMEM((n, hw), F32),
                        pltpu.VMEM((2 * hs, HEAD, HEAD), F32)],
        input_output_aliases=aliases,
        compiler_params=_cparams("parallel", "parallel"),
        name="hgrn2_scan",
    )(*args)
    return res if emit_state else (res[0], None)


def _swa_ctx_kernel(q_ref, k_ref, v_ref, sink_ref, mix_ref, o_ref, *, layer, n, kvh):
    del mix_ref
    lane = lax.broadcasted_iota(jnp.int32, (1, SW_GROUP * n), 1)
    units = [(slice(sq * n, (sq + 1) * n), kv) for sq in range(q_ref.shape[0] // n) for kv in range(kvh)]

    def fold(x, op):
        return op(op(x.reshape(x.shape[0] // 8, 8, x.shape[1]), axis=0), axis=0, keepdims=True)

    sinks = []
    for kv in range(kvh):
        sink = jnp.zeros((1, SW_GROUP * n), F32)
        for g in range(SW_GROUP):
            sink = jnp.where(lane // n == g, sink_ref[layer, kv * SW_GROUP + g], sink)
        sinks.append(sink)
    scores = []
    for rows, kv in units:
        q_all = jnp.concatenate(
            [(q_ref[rows, (kv * SW_GROUP + g) * HEAD:(kv * SW_GROUP + g + 1) * HEAD] * (HEAD ** -0.5)).astype(BF16)
             for g in range(SW_GROUP)], axis=0)
        scores.append(_dot_t(k_ref[rows, kv * HEAD:(kv + 1) * HEAD].astype(BF16), q_all))
    probs = []
    for s, (_, kv) in zip(scores, units):
        m = jnp.maximum(fold(s, jnp.max), sinks[kv])
        p = jnp.exp(s - m)
        inv = 1.0 / (fold(p, jnp.sum) + jnp.exp(sinks[kv] - m))
        probs.append((p * inv).astype(BF16))
    outs = [_dot(v_ref[rows, kv * HEAD:(kv + 1) * HEAD].T.astype(BF16), pb) for (rows, kv), pb in zip(units, probs)]
    for (rows, kv), ot in zip(units, outs):
        for g in range(SW_GROUP):
            hd = kv * SW_GROUP + g
            o_ref[rows, hd * HEAD:(hd + 1) * HEAD] = ot[:, g * n:(g + 1) * n].T.astype(BF16)


def _swa_ctx(p, mix, geom, layer, sink):
    n_seq, n, t = geom["n_seq"], geom["n"], geom["t"]
    kvh = geom["sw_kv"]
    qw = kvh * SW_GROUP * HEAD
    kw = kvh * HEAD
    q_blk = geom["cq_off"] // qw
    k_blk = geom["ck_off"] // kw
    mix_blk = (geom["da_heads"] + geom["hg_heads"]) * HEAD // qw
    assert geom["cq_off"] % qw == 0 and geom["ck_off"] % kw == 0
    rows = math.gcd(n_seq, CTX_SEQ_PER_STEP) * n
    return pl.pallas_call(
        functools.partial(_swa_ctx_kernel, layer=layer, n=n, kvh=kvh),
        out_shape=jax.ShapeDtypeStruct(mix.shape, mix.dtype),
        grid=(t // rows,),
        in_specs=[pl.BlockSpec((rows, qw), lambda b: (b, q_blk)),
                  pl.BlockSpec((rows, kw), lambda b: (b, k_blk)),
                  pl.BlockSpec((rows, kw), lambda b: (b, k_blk + 1)),
                  pl.BlockSpec(memory_space=pltpu.SMEM),
                  pl.BlockSpec(memory_space=pl.ANY)],
        out_specs=pl.BlockSpec((rows, qw), lambda b: (b, mix_blk)),
        input_output_aliases={4: 0},
        compiler_params=_cparams("parallel"),
        name="sink_attention",
    )(p, p, p, sink, mix)


def _band_masks(n):
    r = np.arange(3 * SW_BLOCK)[:, None]
    i = (np.arange(SW_GROUP * SW_BLOCK) % SW_BLOCK)[None, :]
    window = np.abs(SW_BLOCK + i - r) <= SW_WINDOW
    not_before = r >= SW_BLOCK
    not_after = r < 2 * SW_BLOCK
    variants = [window, window & not_before, window & not_after, window & not_before & not_after]
    return jnp.asarray(np.stack(variants).astype(np.float32))


def _swa_lat_kernel(q_ref, k_ref, v_ref, qc_ref, qsa_ref, qsb_ref, kc_ref, ksa_ref, ksb_ref, ck_ref, cv_ref,
                    mask_ref, sink_ref, mix_ref, o_ref, kr_ref, vt_ref, kctx_ref, vctxt_ref, *, layer, n):
    kv = pl.program_id(1)
    step = pl.program_id(2)
    nb = n // SW_BLOCK
    w = HEAD // 4
    band = 3 * SW_BLOCK
    per_step = q_ref.shape[0] // SW_BLOCK

    @pl.when(step == 0)
    def _():
        zeros = jnp.zeros((SW_BLOCK, HEAD), BF16)
        for j in (0, nb + 1):
            kr_ref[j] = zeros
            vt_ref[j] = zeros

        def fill(j, carry):
            rows = pl.ds(pl.multiple_of(j * SW_BLOCK, SW_BLOCK), SW_BLOCK)
            k = _rope(k_ref[rows, :], kc_ref[rows, :], ksa_ref[rows, :], ksb_ref[rows, :], w)
            kr_ref[j + 1] = k.astype(BF16)
            vt_ref[j + 1] = v_ref[rows, :].T.astype(BF16)
            return carry

        lax.fori_loop(0, nb, fill, 0)
        kctx_ref[...] = ck_ref[...].astype(BF16)
        vctxt_ref[...] = cv_ref[...].T.astype(BF16)

    lane = lax.broadcasted_iota(jnp.int32, (1, SW_GROUP * SW_BLOCK), 1)
    sink = jnp.zeros((1, SW_GROUP * SW_BLOCK), F32)
    for g in range(SW_GROUP):
        sink = jnp.where(lane // SW_BLOCK == g, sink_ref[layer, kv * SW_GROUP + g], sink)

    def fold(x, op):
        return op(op(x.reshape(x.shape[0] // 8, 8, x.shape[1]), axis=0), axis=0, keepdims=True)

    blocks = [step * per_step + c for c in range(per_step)]
    rows = [slice(c * SW_BLOCK, (c + 1) * SW_BLOCK) for c in range(per_step)]
    scores = []
    for qb, r in zip(blocks, rows):
        qc, qsa, qsb = qc_ref[r, :], qsa_ref[r, :], qsb_ref[r, :]
        q_all = jnp.concatenate(
            [(_rope(q_ref[r, g * HEAD:(g + 1) * HEAD], qc, qsa, qsb, w) * (HEAD ** -0.5)).astype(BF16)
             for g in range(SW_GROUP)], axis=0)
        k_all = jnp.concatenate([kr_ref[qb], kr_ref[qb + 1], kr_ref[qb + 2], kctx_ref[...]], axis=0)
        scores.append(_dot_t(k_all, q_all))
    probs = []
    for qb, s in zip(blocks, scores):
        variant = jnp.where(qb == 0, 1, 0) + jnp.where(qb == nb - 1, 2, 0)
        s_band = jnp.where(mask_ref[variant] > 0.5, s[:band], NEG_INF)
        s_ctx = s[band:]
        m = jnp.maximum(jnp.maximum(fold(s_band, jnp.max), fold(s_ctx, jnp.max)), sink)
        p_band = jnp.exp(s_band - m)
        p_ctx = jnp.exp(s_ctx - m)
        inv = 1.0 / (fold(p_band, jnp.sum) + fold(p_ctx, jnp.sum) + jnp.exp(sink - m))
        probs.append(jnp.concatenate([p_band * inv, p_ctx * inv], axis=0).astype(BF16))
    outs = []
    for qb, p_all in zip(blocks, probs):
        vt_all = jnp.concatenate([vt_ref[qb], vt_ref[qb + 1], vt_ref[qb + 2], vctxt_ref[...]], axis=1)
        outs.append(_dot(vt_all, p_all))
    for r, ot in zip(rows, outs):
        for g in range(SW_GROUP):
            o_ref[r, g * HEAD:(g + 1) * HEAD] = ot[:, g * SW_BLOCK:(g + 1) * SW_BLOCK].T.astype(BF16)


def _swa_lat(p, mix, geom, layer, sink, rope_tabs, cache_k, cache_v):
    n_seq, n, t = geom["n_seq"], geom["n"], geom["t"]
    kvh = geom["sw_kv"]
    qw = SW_GROUP * HEAD
    q_blk = geom["cq_off"] // qw
    k_blk = geom["ck_off"] // HEAD
    v_blk = k_blk + kvh
    mix_blk = (geom["da_heads"] + geom["hg_heads"]) * HEAD // qw
    nb = n // SW_BLOCK
    per_step = SW_PER_STEP if nb % SW_PER_STEP == 0 else 1
    nsteps = nb // per_step
    tq = per_step * SW_BLOCK
    n_ctx = cache_k.shape[3]
    masks = _band_masks(n)
    cspec = pl.BlockSpec((None, None, None, n_ctx, HEAD), lambda b, kv, i: (b, layer, kv, 0, 0))
    return pl.pallas_call(
        functools.partial(_swa_lat_kernel, layer=layer, n=n),
        out_shape=jax.ShapeDtypeStruct(mix.shape, mix.dtype),
        grid=(n_seq, kvh, nsteps),
        in_specs=[pl.BlockSpec((tq, qw), lambda b, kv, i: (b * nsteps + i, q_blk + kv)),
                  pl.BlockSpec((n, HEAD), lambda b, kv, i: (b, k_blk + kv)),
                  pl.BlockSpec((n, HEAD), lambda b, kv, i: (b, v_blk + kv))]
                 + [pl.BlockSpec((tq, HEAD), lambda b, kv, i: (i, 0))] * 3
                 + [pl.BlockSpec((n, HEAD), lambda b, kv, i: (0, 0))] * 3
                 + [cspec, cspec, pl.BlockSpec(masks.shape, lambda b, kv, i: (0, 0, 0)),
                    pl.BlockSpec(memory_space=pltpu.SMEM), pl.BlockSpec(memory_space=pl.ANY)],
        out_specs=pl.BlockSpec((tq, qw), lambda b, kv, i: (b * nsteps + i, mix_blk + kv)),
        scratch_shapes=[pltpu.VMEM((nb + 2, SW_BLOCK, HEAD), BF16), pltpu.VMEM((nb + 2, HEAD, SW_BLOCK), BF16),
                        pltpu.VMEM((n_ctx, HEAD), BF16), pltpu.VMEM((HEAD, n_ctx), BF16)],
        input_output_aliases={13: 0},
        compiler_params=_cparams("parallel", "parallel", "arbitrary"),
        name="banded_sink_attention",
    )(p, p, p, *rope_tabs, *rope_tabs, cache_k, cache_v, masks, sink, mix)


def _rope_tables(n, half):
    h = half // 2
    pos = jnp.arange(n)
    inv = ROPE_BASE ** (-jnp.arange(h, dtype=F32) / h)
    zero = jnp.zeros((n, h), F32)
    c, sa, sb = [], [], []
    for axis_pos in (pos // GRID_W, pos % GRID_W):
        ang = axis_pos.astype(F32)[:, None] * inv[None, :]
        cos, sin = jnp.cos(ang), jnp.sin(ang)
        c += [cos, cos]
        sa += [-sin, zero]
        sb += [zero, sin]
    reps = HEAD // (2 * half)
    cat = lambda parts: jnp.tile(jnp.concatenate(parts, axis=1), (1, reps))
    return cat(c), cat(sa), cat(sb)


def _geometry(n_seq, n, d_model):
    da_heads = hg_heads = d_model // 512
    sw_heads = d_model // 256
    sw_kv = sw_heads // SW_GROUP
    cq_off = (3 * da_heads + 5 * hg_heads) * HEAD
    return dict(n_seq=n_seq, n=n, t=n_seq * n, da_heads=da_heads, hg_heads=hg_heads, sw_kv=sw_kv,
                cq_off=cq_off, ck_off=cq_off + sw_heads * HEAD, d_mix=(da_heads + hg_heads + sw_heads) * HEAD)


def _token_tiles(n):
    return _pick_tile(n, TM_PROJ), _pick_tile(n, TM_OUT), _pick_tile(n, TM_FFN)


def _project(x, geom, layer, mod, mod_row, w_in, cast_srcs, new_caches):
    if new_caches is None:
        tm = _token_tiles(geom["n"])[0]
        return _in_proj(x, mod, mod_row, w_in, layer, tm, geom["d_mix"], cast_srcs)
    heads, kvh = geom["da_heads"], geom["sw_kv"]
    cols = ((heads * HEAD, heads), (2 * heads * HEAD, heads), (geom["ck_off"], kvh),
            (geom["ck_off"] + kvh * HEAD, kvh))
    return _in_proj(x, mod, mod_row, w_in, layer, geom["n"], geom["d_mix"], cast_srcs, new_caches, cols)


def _mix_and_ffn(x, p, mix, geom, layer, mod, mod_row, wts, params, lb_params, scan_consts, tabs, caches,
                 new_state, alpha):
    w_out, w_gu, w_down = wts
    _, tm_out, tm_ffn = _token_tiles(geom["t"] if caches is None else geom["n"])
    if caches is None:
        mix = _diff_attn(p, mix, geom, layer, params["diff_lambda"], params["diff_norm_g"], None, None, None)
        mix, new_state = _hgrn(p, mix, geom, layer, lb_params, params["hgrn_norm_g"], scan_consts, None, new_state)
        mix = _swa_ctx(p, mix, geom, layer, params["swa_sink"])
    else:
        ck_d, cv_d, ck_s, cv_s, state = caches
        mix = _diff_attn(p, mix, geom, layer, params["diff_lambda"], params["diff_norm_g"], tabs[0], ck_d, cv_d)
        mix, _ = _hgrn(p, mix, geom, layer, lb_params, params["hgrn_norm_g"], scan_consts, state)
        mix = _swa_lat(p, mix, geom, layer, params["swa_sink"], tabs[1], ck_s, cv_s)
    x1 = _out_proj(mix, w_out, x, mod, mod_row, params["ln1_g"], params["ln1_b"], layer, tm_out, alpha)
    y = _ffn(x1, w_gu, w_down, mod, mod_row, params["ln2_g"], params["ln2_b"], layer, tm_ffn, alpha)
    return y, new_state


def kernel(x_prompt, x_sample, cache_diff_k, cache_diff_v, cache_swa_k, cache_swa_v, state_hgrn, c, c_ctx, w_mod,
           b_mod, w_in, w_out, diff_lambda, diff_norm_g, hgrn_lb_logits, hgrn_norm_g, swa_sink, ln1_g, ln1_b, ln2_g,
           ln2_b, w_gate_up, w_down):
    batch, seq, d = x_prompt.shape
    dec_batch, dec_seq, _ = x_sample.shape
    depth = w_mod.shape[0]
    alpha = (2 * depth) ** 0.25
    geom_c = _geometry(batch, seq, d)
    geom_l = _geometry(dec_batch, dec_seq, d)
    assert 1 + dec_batch <= MOD_ROWS

    cond = jnp.zeros((MOD_ROWS, d), F32).at[0].set(c_ctx).at[1:1 + dec_batch].set(c)
    mod = _modulation(cond, w_mod, b_mod).reshape(depth * MOD_ROWS * 6, 1, d)
    lb_params = _lb_params(hgrn_lb_logits)
    params = dict(diff_lambda=diff_lambda, diff_norm_g=diff_norm_g, hgrn_norm_g=hgrn_norm_g, swa_sink=swa_sink,
                  ln1_g=ln1_g, ln1_b=ln1_b, ln2_g=ln2_g, ln2_b=ln2_b)
    w_in16, w_out16 = w_in[0].astype(BF16), w_out[0].astype(BF16)
    tabs = (_rope_tables(dec_seq, DA_QK // 2), _rope_tables(dec_seq, HEAD // 2))
    consts_c = _scan_constants(min(HG_CHUNK, seq))
    consts_l = _scan_constants(min(HG_CHUNK, dec_seq))

    y_p = x_prompt.reshape(batch * seq, d)
    y_s = x_sample.reshape(dec_batch * dec_seq, d)
    heads, kvh = geom_c["da_heads"], geom_c["sw_kv"]
    new_kv = tuple(jnp.zeros((batch, depth, nh, seq, HEAD), F32) for nh in (heads, heads, kvh, kvh))
    new_state = jnp.zeros((batch, depth, 2, geom_c["hg_heads"], HEAD, HEAD), F32)
    caches = (cache_diff_k, cache_diff_v, cache_swa_k, cache_swa_v, state_hgrn)
    for l in range(depth):
        row_c = lambda m, tm, j, l=l: (l * MOD_ROWS) * 6 + j
        row_l = lambda m, tm, j, l=l: (l * MOD_ROWS + 1 + (m * tm) // dec_seq) * 6 + j
        more = l + 1 < depth
        casts_c = ((w_down, l),) + (((w_out, l + 1),) if more else ())
        casts_l = ((w_gate_up, l),) + (((w_in, l + 1),) if more else ())
        p_c, mix_c, cast_c, new_kv = _project(y_p, geom_c, l, mod, row_c, w_in16, casts_c, new_kv)
        p_l, mix_l, cast_l, _ = _project(y_s, geom_l, l, mod, row_l, w_in16, casts_l, None)
        wts = (w_out16, cast_l[0], cast_c[0])
        y_p, new_state = _mix_and_ffn(y_p, p_c, mix_c, geom_c, l, mod, row_c, wts, params, lb_params, consts_c,
                                      None, None, new_state, alpha)
        y_s, _ = _mix_and_ffn(y_s, p_l, mix_l, geom_l, l, mod, row_l, wts, params, lb_params, consts_l, tabs,
                              caches, None, alpha)
        if more:
            w_out16, w_in16 = cast_c[1], cast_l[1]
    return (y_p.reshape(batch, seq, d), y_s.reshape(dec_batch, dec_seq, d)) + new_kv + (new_state,)
```
